```python
import math
import jax
import jax.numpy as jnp
from jax import lax
import numpy as np

D_MODEL = 1024
BATCH = 8
SEQ = 8192
DEPTH = 2

CHUNK = 64
EPS = 1e-6
PLE_DIM = 256
FFN_DIM = 2816
N_BRANCH = 4
BRANCH_WIDTH = 512
S5_GROUP_CH = 16
S5_GROUPS = BRANCH_WIDTH // S5_GROUP_CH
S5_STATE = 64
LRU_HEADS = 8
LRU_HEAD_DIM = BRANCH_WIDTH // LRU_HEADS
LRU_CONV = 4
LRU_C = 8.0
M2_HEAD_DIM = 64
M2_HEADS = BRANCH_WIDTH // M2_HEAD_DIM
M2_GROUPS = 2
M2_STATE = 128
M2_CONV = 4
M2_CONV_DIM = BRANCH_WIDTH + 2 * M2_GROUPS * M2_STATE
GDN_HEAD_DIM = 128
GDN_HEADS = BRANCH_WIDTH // GDN_HEAD_DIM
GDN_CONV = 4
IN_SPLITS = (BRANCH_WIDTH, BRANCH_WIDTH, BRANCH_WIDTH, BRANCH_WIDTH, M2_CONV_DIM, M2_HEADS, 3 * BRANCH_WIDTH, BRANCH_WIDTH, GDN_HEADS, GDN_HEADS)
IN_WIDTH = 4 * BRANCH_WIDTH + M2_CONV_DIM + M2_HEADS + 3 * BRANCH_WIDTH + BRANCH_WIDTH + 2 * GDN_HEADS

kernel_name = "hybrid_parallel_ssm_lru_ssd_gdn_trunk"


def _split(t, sizes):
    out, start = [], 0
    for s in sizes:
        out.append(t[..., start:start + s])
        start += s
    return out


def rmsnorm(x, g):
    xf = x.astype(jnp.float32)
    y = xf * lax.rsqrt(jnp.mean(xf * xf, axis=-1, keepdims=True) + EPS)
    return (y * g.astype(jnp.float32)).astype(x.dtype)


def _l2norm(x):
    return x * lax.rsqrt(jnp.sum(x * x, axis=-1, keepdims=True) + EPS)


def swiglu(h, w_in, w_out):
    gate, up = jnp.split(h @ w_in, 2, axis=-1)
    return (jax.nn.silu(gate) * up) @ w_out


def causal_depthwise_conv(x, w, b=None):
    k, c = w.shape
    y = lax.conv_general_dilated(x, w[:, None, :].astype(x.dtype), window_strides=(1,), padding=[(k - 1, 0)], dimension_numbers=('NWC', 'WIO', 'NWC'), feature_group_count=c)
    return y if b is None else y + b.astype(y.dtype)


def _linear_combine(l, r):
    return (l[0] * r[0], r[0] * l[1] + r[1])


def _complex_combine(l, r):
    a1r, a1i, b1r, b1i = l
    a2r, a2i, b2r, b2i = r
    return (a2r * a1r - a2i * a1i, a2r * a1i + a2i * a1r, a2r * b1r - a2i * b1i + b2r, a2r * b1i + a2i * b1r + b2i)


def _decay_matrix(cs):
    n = cs.shape[-1]
    causal = jnp.tril(jnp.ones((n, n), dtype=bool))
    diff = cs[..., :, None] - cs[..., None, :]
    return jnp.where(causal, jnp.exp(jnp.where(causal, diff, 0.0)), 0.0)


def s5_mixer(u, log_step, a_re, a_im, b_re, b_im, c_re, c_im, d_skip, w_glu, b_glu):
    f32 = jnp.float32
    dtype = u.dtype
    bsz, s, _ = u.shape
    ug = u.astype(f32).reshape(bsz, s, S5_GROUPS, S5_GROUP_CH)
    a_re, a_im = a_re.astype(f32), a_im.astype(f32)
    b_re, b_im = b_re.astype(f32), b_im.astype(f32)
    step = jnp.exp(log_step.astype(f32))[:, None]
    mag = jnp.exp(a_re * step)
    ab_re, ab_im = mag * jnp.cos(a_im * step), mag * jnp.sin(a_im * step)
    den = a_re * a_re + a_im * a_im
    num_re = ab_re - 1.0
    f_re = (num_re * a_re + ab_im * a_im) / den
    f_im = (ab_im * a_re - num_re * a_im) / den
    bb_re = f_re[..., None] * b_re - f_im[..., None] * b_im
    bb_im = f_re[..., None] * b_im + f_im[..., None] * b_re
    bu_re = jnp.einsum('bsgc,gpc->bsgp', ug, bb_re)
    bu_im = jnp.einsum('bsgc,gpc->bsgp', ug, bb_im)
    ar = jnp.broadcast_to(ab_re, bu_re.shape)
    ai = jnp.broadcast_to(ab_im, bu_im.shape)
    _, _, h_re, h_im = lax.associative_scan(_complex_combine, (ar, ai, bu_re, bu_im), axis=1)
    y = jnp.einsum('bsgp,gcp->bsgc', h_re, c_re.astype(f32)) - jnp.einsum('bsgp,gcp->bsgc', h_im, c_im.astype(f32)) + d_skip.astype(f32) * ug
    z = jax.nn.gelu(y.reshape(bsz, s, BRANCH_WIDTH))
    return (z * jax.nn.sigmoid(z @ w_glu.astype(f32) + b_glu.astype(f32))).astype(dtype)


def rglru_mixer(xb, gate, conv_w, conv_b, w_r, b_r, w_i, b_i, lam):
    f32 = jnp.float32
    dtype = xb.dtype
    bsz, s, _ = xb.shape
    xc = causal_depthwise_conv(xb, conv_w, conv_b).astype(f32)
    xh = xc.reshape(bsz, s, LRU_HEADS, LRU_HEAD_DIM)
    r = jax.nn.sigmoid(jnp.einsum('bshi,hij->bshj', xh, w_r.astype(f32)).reshape(bsz, s, BRANCH_WIDTH) + b_r.astype(f32))
    i_g = jax.nn.sigmoid(jnp.einsum('bshi,hij->bshj', xh, w_i.astype(f32)).reshape(bsz, s, BRANCH_WIDTH) + b_i.astype(f32))
    log_a = -LRU_C * r * jax.nn.softplus(-lam.astype(f32))
    a = jnp.exp(log_a)
    inp = jnp.sqrt(-jnp.expm1(2.0 * log_a)) * (i_g * xc)
    _, h = lax.associative_scan(_linear_combine, (a, inp), axis=1)
    return (h * jax.nn.gelu(gate.astype(f32))).astype(dtype)


def ssd_chunked(x, a, bm, cm):
    bsz, s, h, pdim = x.shape
    n = bm.shape[-1]
    c = s // CHUNK
    xc = x.reshape(bsz, c, CHUNK, h, pdim)
    bc = bm.reshape(bsz, c, CHUNK, h, n)
    cc = cm.reshape(bsz, c, CHUNK, h, n)
    a_cs = jnp.cumsum(a.reshape(bsz, c, CHUNK, h).transpose(0, 3, 1, 2), axis=-1)
    scores = jnp.einsum('bclhn,bcshn->bhcls', cc, bc) * _decay_matrix(a_cs)
    y_diag = jnp.einsum('bhcls,bcshp->bclhp', scores, xc)
    decay_to_end = jnp.exp(a_cs[..., -1:] - a_cs)
    states = jnp.einsum('bclhn,bhcl,bclhp->bchpn', bc, decay_to_end, xc)
    chunk_decay = jnp.exp(a_cs[..., -1])

    def step(carry, inp):
        st, dec = inp
        return carry * dec[..., None, None] + st, carry

    init = jnp.zeros((bsz, h, pdim, n), x.dtype)
    _, prev = lax.scan(step, init, (jnp.moveaxis(states, 1, 0), jnp.moveaxis(chunk_decay, 2, 0)))
    prev = jnp.moveaxis(prev, 0, 1)
    y_off = jnp.einsum('bclhn,bchpn,bhcl->bclhp', cc, prev, jnp.exp(a_cs))
    return (y_diag + y_off).reshape(bsz, s, h, pdim)


def mamba2_mixer(z, xbc, dt_raw, conv_w, conv_b, dt_bias, a_log, d_skip, norm_g):
    f32 = jnp.float32
    dtype = z.dtype
    bsz, s, _ = z.shape
    xbc = jax.nn.silu(causal_depthwise_conv(xbc, conv_w, conv_b)).astype(f32)
    xs, bm, cm = _split(xbc, (BRANCH_WIDTH, M2_GROUPS * M2_STATE, M2_GROUPS * M2_STATE))
    xh = xs.reshape(bsz, s, M2_HEADS, M2_HEAD_DIM)
    rep = M2_HEADS // M2_GROUPS
    bm = jnp.repeat(bm.reshape(bsz, s, M2_GROUPS, M2_STATE), rep, axis=2)
    cm = jnp.repeat(cm.reshape(bsz, s, M2_GROUPS, M2_STATE), rep, axis=2)
    dt = jax.nn.softplus(dt_raw.astype(f32) + dt_bias.astype(f32))
    a = -jnp.exp(a_log.astype(f32))
    y = ssd_chunked(xh * dt[..., None], dt * a, bm, cm)
    y = y + d_skip.astype(f32)[:, None] * xh
    y = y.reshape(bsz, s, BRANCH_WIDTH) * jax.nn.silu(z.astype(f32))
    return rmsnorm(y, norm_g).astype(dtype)


def chunk_gated_delta_rule(q, k, v, g, beta):
    bsz, s, h, dk = q.shape
    dv = v.shape[-1]
    c = s // CHUNK

    def to_chunks(t):
        return t.reshape(bsz, c, CHUNK, h, -1).transpose(0, 3, 1, 2, 4)

    q, k, v = to_chunks(q), to_chunks(k), to_chunks(v)
    g_cs = jnp.cumsum(g.reshape(bsz, c, CHUNK, h).transpose(0, 3, 1, 2), axis=-1)
    beta = beta.reshape(bsz, c, CHUNK, h).transpose(0, 3, 1, 2)[..., None]
    decay = _decay_matrix(g_cs)
    kb = k * beta
    strict = jnp.tril(jnp.ones((CHUNK, CHUNK), dtype=bool), -1)
    m = jnp.where(strict, jnp.einsum('bhcld,bhcsd->bhcls', kb, k) * decay, 0.0)
    eye = jnp.eye(CHUNK, dtype=m.dtype)
    rhs = jnp.concatenate([v * beta, kb * jnp.exp(g_cs)[..., None]], axis=-1)
    sol = lax.linalg.triangular_solve(m + eye, rhs, left_side=True, lower=True, unit_diagonal=True)
    u, w = sol[..., :dv], sol[..., dv:]
    qk = jnp.einsum('bhcld,bhcsd->bhcls', q, k) * decay
    q_dec = q * jnp.exp(g_cs)[..., None]
    k_dec = k * jnp.exp(g_cs[..., -1:] - g_cs)[..., None]
    chunk_decay = jnp.exp(g_cs[..., -1])

    def step(state, inp):
        q_i, k_i, w_i, u_i, qk_i, dec_i = inp
        v_new = u_i - jnp.einsum('bhld,bhde->bhle', w_i, state)
        o_i = jnp.einsum('bhld,bhde->bhle', q_i, state) + jnp.einsum('bhls,bhse->bhle', qk_i, v_new)
        state = state * dec_i[..., None, None] + jnp.einsum('bhld,bhle->bhde', k_i, v_new)
        return state, o_i

    xs = (jnp.moveaxis(q_dec, 2, 0), jnp.moveaxis(k_dec, 2, 0), jnp.moveaxis(w, 2, 0), jnp.moveaxis(u, 2, 0), jnp.moveaxis(qk, 2, 0), jnp.moveaxis(chunk_decay, 2, 0))
    _, o = lax.scan(step, jnp.zeros((bsz, h, dk, dv), q.dtype), xs)
    return o.transpose(1, 0, 3, 2, 4).reshape(bsz, s, h, dv)


def gated_deltanet_mixer(qkv, gate, beta_raw, a_raw, conv_w, dt_bias, a_log, norm_g):
    f32 = jnp.float32
    dtype = qkv.dtype
    bsz, s, _ = qkv.shape
    qkv = jax.nn.silu(causal_depthwise_conv(qkv, conv_w)).astype(f32)
    q, k, v = _split(qkv, (BRANCH_WIDTH, BRANCH_WIDTH, BRANCH_WIDTH))
    shape = (bsz, s, GDN_HEADS, GDN_HEAD_DIM)
    q = _l2norm(q.reshape(shape)) * (GDN_HEAD_DIM ** -0.5)
    k = _l2norm(k.reshape(shape))
    v = v.reshape(shape)
    beta = jax.nn.sigmoid(beta_raw.astype(f32))
    g = -jnp.exp(a_log.astype(f32)) * jax.nn.softplus(a_raw.astype(f32) + dt_bias.astype(f32))
    o = chunk_gated_delta_rule(q, k, v, g, beta)
    o = rmsnorm(o, norm_g) * jax.nn.silu(gate.astype(f32).reshape(shape))
    return o.reshape(bsz, s, BRANCH_WIDTH).astype(dtype)


def _fwd_setup_inputs(seed: int = 0) -> dict:
    key = jax.random.key(seed)
    keys = iter(jax.random.split(key, 64))
    f32 = jnp.float32

    def normal(shape, scale):
        return scale * jax.random.normal(next(keys), shape, f32)

    def gain(shape):
        return 1.0 + 0.01 * jax.random.normal(next(keys), shape, f32)

    def uniform(shape, lo, hi):
        return jax.random.uniform(next(keys), shape, f32, lo, hi)

    def dt_bias(shape):
        dt = jnp.exp(uniform(shape, math.log(1e-3), math.log(1e-1)))
        return dt + jnp.log(-jnp.expm1(-dt))

    L, D, W = DEPTH, D_MODEL, BRANCH_WIDTH
    n_idx = jnp.arange(S5_STATE, dtype=f32)
    a_pow = uniform((L, W), 0.9, 0.999) ** (1.0 / LRU_C)
    return {
        'x': normal((BATCH, SEQ, D), 1.0),
        'p': normal((DEPTH, BATCH, SEQ, PLE_DIM), 1.0),
        'ffn1_norm': gain((L, D)),
        'ffn1_w_in': normal((L, D, 2 * FFN_DIM), D ** -0.5),
        'ffn1_w_out': normal((L, FFN_DIM, D), FFN_DIM ** -0.5),
        'mix_norm': gain((L, D)),
        'w_in': normal((L, D, IN_WIDTH), D ** -0.5),
        'w_gate': normal((L, D, N_BRANCH * D), D ** -0.5),
        'b_gate': normal((L, N_BRANCH * D), 0.01),
        's5_log_step': uniform((L, S5_GROUPS), math.log(1e-3), math.log(1e-1)),
        's5_a_re': -0.5 + normal((L, S5_GROUPS, S5_STATE), 0.01),
        's5_a_im': math.pi * n_idx + normal((L, S5_GROUPS, S5_STATE), 0.01),
        's5_b_re': normal((L, S5_GROUPS, S5_STATE, S5_GROUP_CH), (2 * S5_GROUP_CH) ** -0.5),
        's5_b_im': normal((L, S5_GROUPS, S5_STATE, S5_GROUP_CH), (2 * S5_GROUP_CH) ** -0.5),
        's5_c_re': normal((L, S5_GROUPS, S5_GROUP_CH, S5_STATE), (2 * S5_STATE) ** -0.5),
        's5_c_im': normal((L, S5_GROUPS, S5_GROUP_CH, S5_STATE), (2 * S5_STATE) ** -0.5),
        's5_d': normal((L, S5_GROUPS, S5_GROUP_CH), 1.0),
        's5_w_glu': normal((L, W, W), W ** -0.5),
        's5_b_glu': normal((L, W), 0.01),
        'lru_conv_w': normal((L, LRU_CONV, W), LRU_CONV ** -0.5),
        'lru_conv_b': normal((L, W), 0.01),
        'lru_w_r': normal((L, LRU_HEADS, LRU_HEAD_DIM, LRU_HEAD_DIM), LRU_HEAD_DIM ** -0.5),
        'lru_b_r': normal((L, W), 0.01),
        'lru_w_i': normal((L, LRU_HEADS, LRU_HEAD_DIM, LRU_HEAD_DIM), LRU_HEAD_DIM ** -0.5),
        'lru_b_i': normal((L, W), 0.01),
        'lru_lambda': jnp.log(a_pow) - jnp.log1p(-a_pow),
        'm2_conv_w': normal((L, M2_CONV, M2_CONV_DIM), M2_CONV ** -0.5),
        'm2_conv_b': normal((L, M2_CONV_DIM), 0.01),
        'm2_dt_bias': dt_bias((L, M2_HEADS)),
        'm2_a_log': jnp.log(uniform((L, M2_HEADS), 1.0, 16.0)),
        'm2_d': gain((L, M2_HEADS)),
        'm2_norm': gain((L, W)),
        'gdn_conv_w': normal((L, GDN_CONV, 3 * W), GDN_CONV ** -0.5),
        'gdn_dt_bias': dt_bias((L, GDN_HEADS)),
        'gdn_a_log': jnp.log(uniform((L, GDN_HEADS), 1.0, 16.0)),
        'gdn_norm': gain((L, GDN_HEAD_DIM)),
        'w_branch': normal((L, N_BRANCH, W, D), W ** -0.5),
        'w_out': normal((L, D, D), D ** -0.5),
        'ffn2_norm': gain((L, D)),
        'ffn2_w_in': normal((L, D, 2 * FFN_DIM), D ** -0.5),
        'ffn2_w_out': normal((L, FFN_DIM, D), FFN_DIM ** -0.5),
        'ple_norm': gain((L, D)),
        'ple_w_gate': normal((L, D, D), D ** -0.5),
        'ple_w_proj': normal((L, PLE_DIM, D), PLE_DIM ** -0.5),
        'final_norm': gain((D,)),
    }


def _fwd_reference(x, p, ffn1_norm, ffn1_w_in, ffn1_w_out, mix_norm, w_in, w_gate, b_gate, s5_log_step, s5_a_re, s5_a_im, s5_b_re, s5_b_im, s5_c_re, s5_c_im, s5_d, s5_w_glu, s5_b_glu, lru_conv_w, lru_conv_b, lru_w_r, lru_b_r, lru_w_i, lru_b_i, lru_lambda, m2_conv_w, m2_conv_b, m2_dt_bias, m2_a_log, m2_d, m2_norm, gdn_conv_w, gdn_dt_bias, gdn_a_log, gdn_norm, w_branch, w_out, ffn2_norm, ffn2_w_in, ffn2_w_out, ple_norm, ple_w_gate, ple_w_proj, final_norm):
    bsz, s, d = x.shape
    h = x
    for i in range(DEPTH):
        h = h + 0.5 * swiglu(rmsnorm(h, ffn1_norm[i]), ffn1_w_in[i], ffn1_w_out[i])
        u = rmsnorm(h, mix_norm[i])
        s5_u, lru_x, lru_g, m2_z, m2_xbc, m2_dt, gdn_qkv, gdn_g, gdn_b, gdn_a = _split(u @ w_in[i], IN_SPLITS)
        y_a = s5_mixer(s5_u, s5_log_step[i], s5_a_re[i], s5_a_im[i], s5_b_re[i], s5_b_im[i], s5_c_re[i], s5_c_im[i], s5_d[i], s5_w_glu[i], s5_b_glu[i])
        y_b = rglru_mixer(lru_x, lru_g, lru_conv_w[i], lru_conv_b[i], lru_w_r[i], lru_b_r[i], lru_w_i[i], lru_b_i[i], lru_lambda[i])
        y_c = mamba2_mixer(m2_z, m2_xbc, m2_dt, m2_conv_w[i], m2_conv_b[i], m2_dt_bias[i], m2_a_log[i], m2_d[i], m2_norm[i])
        y_d = gated_deltanet_mixer(gdn_qkv, gdn_g, gdn_b, gdn_a, gdn_conv_w[i], gdn_dt_bias[i], gdn_a_log[i], gdn_norm[i])
        ys = jnp.stack([y_a, y_b, y_c, y_d], axis=2)
        yb = jnp.einsum('bsnc,ncd->bsnd', ys, w_branch[i])
        gates = jax.nn.sigmoid(u @ w_gate[i] + b_gate[i]).reshape(bsz, s, N_BRANCH, d)
        h = h + jnp.einsum('bsnd,bsnd->bsd', gates, yb) @ w_out[i]
        h = h + 0.5 * swiglu(rmsnorm(h, ffn2_norm[i]), ffn2_w_in[i], ffn2_w_out[i])
        h = h + jax.nn.sigmoid(rmsnorm(h, ple_norm[i]) @ ple_w_gate[i]) * (p[i] @ ple_w_proj[i])
    return rmsnorm(h, final_norm)


import jax as _jax
import jax.numpy as _jnp

TWIN_FORMAT = 'train_step'
FWD_PARAMS = ['x', 'p', 'ffn1_norm', 'ffn1_w_in', 'ffn1_w_out', 'mix_norm', 'w_in', 'w_gate', 'b_gate', 's5_log_step', 's5_a_re', 's5_a_im', 's5_b_re', 's5_b_im', 's5_c_re', 's5_c_im', 's5_d', 's5_w_glu', 's5_b_glu', 'lru_conv_w', 'lru_conv_b', 'lru_w_r', 'lru_b_r', 'lru_w_i', 'lru_b_i', 'lru_lambda', 'm2_conv_w', 'm2_conv_b', 'm2_dt_bias', 'm2_a_log', 'm2_d', 'm2_norm', 'gdn_conv_w', 'gdn_dt_bias', 'gdn_a_log', 'gdn_norm', 'w_branch', 'w_out', 'ffn2_norm', 'ffn2_w_in', 'ffn2_w_out', 'ple_norm', 'ple_w_gate', 'ple_w_proj', 'final_norm']
TWIN_WEIGHTS = ['ffn1_norm', 'ffn1_w_in', 'ffn1_w_out', 'mix_norm', 'w_in', 'w_gate', 'b_gate', 's5_log_step', 's5_a_re', 's5_a_im', 's5_b_re', 's5_b_im', 's5_c_re', 's5_c_im', 's5_d', 's5_w_glu', 's5_b_glu', 'lru_conv_w', 'lru_conv_b', 'lru_w_r', 'lru_b_r', 'lru_w_i', 'lru_b_i', 'lru_lambda', 'm2_conv_w', 'm2_conv_b', 'm2_dt_bias', 'm2_a_log', 'm2_d', 'm2_norm', 'gdn_conv_w', 'gdn_dt_bias', 'gdn_a_log', 'gdn_norm', 'w_branch', 'w_out', 'ffn2_norm', 'ffn2_w_in', 'ffn2_w_out', 'ple_norm', 'ple_w_gate', 'ple_w_proj', 'final_norm']
TWIN_DIFF_INPUT = 'x'
TWIN_INPUTS = ['x', 'p', 'ffn1_norm', 'ffn1_w_in', 'ffn1_w_out', 'mix_norm', 'w_in', 'w_gate', 'b_gate', 's5_log_step', 's5_a_re', 's5_a_im', 's5_b_re', 's5_b_im', 's5_c_re', 's5_c_im', 's5_d', 's5_w_glu', 's5_b_glu', 'lru_conv_w', 'lru_conv_b', 'lru_w_r', 'lru_b_r', 'lru_w_i', 'lru_b_i', 'lru_lambda', 'm2_conv_w', 'm2_conv_b', 'm2_dt_bias', 'm2_a_log', 'm2_d', 'm2_norm', 'gdn_conv_w', 'gdn_dt_bias', 'gdn_a_log', 'gdn_norm', 'w_branch', 'w_out', 'ffn2_norm', 'ffn2_w_in', 'ffn2_w_out', 'ple_norm', 'ple_w_gate', 'ple_w_proj', 'final_norm', 'loss_target', 'm_ffn1_norm', 'm_ffn1_w_in', 'm_ffn1_w_out', 'm_mix_norm', 'm_w_in', 'm_w_gate', 'm_b_gate', 'm_s5_log_step', 'm_s5_a_re', 'm_s5_a_im', 'm_s5_b_re', 'm_s5_b_im', 'm_s5_c_re', 'm_s5_c_im', 'm_s5_d', 'm_s5_w_glu', 'm_s5_b_glu', 'm_lru_conv_w', 'm_lru_conv_b', 'm_lru_w_r', 'm_lru_b_r', 'm_lru_w_i', 'm_lru_b_i', 'm_lru_lambda', 'm_m2_conv_w', 'm_m2_conv_b', 'm_m2_dt_bias', 'm_m2_a_log', 'm_m2_d', 'm_m2_norm', 'm_gdn_conv_w', 'm_gdn_dt_bias', 'm_gdn_a_log', 'm_gdn_norm', 'm_w_branch', 'm_w_out', 'm_ffn2_norm', 'm_ffn2_w_in', 'm_ffn2_w_out', 'm_ple_norm', 'm_ple_w_gate', 'm_ple_w_proj', 'm_final_norm', 'v_ffn1_norm', 'v_ffn1_w_in', 'v_ffn1_w_out', 'v_mix_norm', 'v_w_in', 'v_w_gate', 'v_b_gate', 'v_s5_log_step', 'v_s5_a_re', 'v_s5_a_im', 'v_s5_b_re', 'v_s5_b_im', 'v_s5_c_re', 'v_s5_c_im', 'v_s5_d', 'v_s5_w_glu', 'v_s5_b_glu', 'v_lru_conv_w', 'v_lru_conv_b', 'v_lru_w_r', 'v_lru_b_r', 'v_lru_w_i', 'v_lru_b_i', 'v_lru_lambda', 'v_m2_conv_w', 'v_m2_conv_b', 'v_m2_dt_bias', 'v_m2_a_log', 'v_m2_d', 'v_m2_norm', 'v_gdn_conv_w', 'v_gdn_dt_bias', 'v_gdn_a_log', 'v_gdn_norm', 'v_w_branch', 'v_w_out', 'v_ffn2_norm', 'v_ffn2_w_in', 'v_ffn2_w_out', 'v_ple_norm', 'v_ple_w_gate', 'v_ple_w_proj', 'v_final_norm']
TWIN_OUTPUTS = ['loss', 'grad_x', 'grad_ffn1_norm', 'grad_ffn1_w_in', 'grad_ffn1_w_out', 'grad_mix_norm', 'grad_w_in', 'grad_w_gate', 'grad_b_gate', 'grad_s5_log_step', 'grad_s5_a_re', 'grad_s5_a_im', 'grad_s5_b_re', 'grad_s5_b_im', 'grad_s5_c_re', 'grad_s5_c_im', 'grad_s5_d', 'grad_s5_w_glu', 'grad_s5_b_glu', 'grad_lru_conv_w', 'grad_lru_conv_b', 'grad_lru_w_r', 'grad_lru_b_r', 'grad_lru_w_i', 'grad_lru_b_i', 'grad_lru_lambda', 'grad_m2_conv_w', 'grad_m2_conv_b', 'grad_m2_dt_bias', 'grad_m2_a_log', 'grad_m2_d', 'grad_m2_norm', 'grad_gdn_conv_w', 'grad_gdn_dt_bias', 'grad_gdn_a_log', 'grad_gdn_norm', 'grad_w_branch', 'grad_w_out', 'grad_ffn2_norm', 'grad_ffn2_w_in', 'grad_ffn2_w_out', 'grad_ple_norm', 'grad_ple_w_gate', 'grad_ple_w_proj', 'grad_final_norm', 'delta_ffn1_norm', 'delta_ffn1_w_in', 'delta_ffn1_w_out', 'delta_mix_norm', 'delta_w_in', 'delta_w_gate', 'delta_b_gate', 'delta_s5_log_step', 'delta_s5_a_re', 'delta_s5_a_im', 'delta_s5_b_re', 'delta_s5_b_im', 'delta_s5_c_re', 'delta_s5_c_im', 'delta_s5_d', 'delta_s5_w_glu', 'delta_s5_b_glu', 'delta_lru_conv_w', 'delta_lru_conv_b', 'delta_lru_w_r', 'delta_lru_b_r', 'delta_lru_w_i', 'delta_lru_b_i', 'delta_lru_lambda', 'delta_m2_conv_w', 'delta_m2_conv_b', 'delta_m2_dt_bias', 'delta_m2_a_log', 'delta_m2_d', 'delta_m2_norm', 'delta_gdn_conv_w', 'delta_gdn_dt_bias', 'delta_gdn_a_log', 'delta_gdn_norm', 'delta_w_branch', 'delta_w_out', 'delta_ffn2_norm', 'delta_ffn2_w_in', 'delta_ffn2_w_out', 'delta_ple_norm', 'delta_ple_w_gate', 'delta_ple_w_proj', 'delta_final_norm', 'new_m_ffn1_norm', 'new_m_ffn1_w_in', 'new_m_ffn1_w_out', 'new_m_mix_norm', 'new_m_w_in', 'new_m_w_gate', 'new_m_b_gate', 'new_m_s5_log_step', 'new_m_s5_a_re', 'new_m_s5_a_im', 'new_m_s5_b_re', 'new_m_s5_b_im', 'new_m_s5_c_re', 'new_m_s5_c_im', 'new_m_s5_d', 'new_m_s5_w_glu', 'new_m_s5_b_glu', 'new_m_lru_conv_w', 'new_m_lru_conv_b', 'new_m_lru_w_r', 'new_m_lru_b_r', 'new_m_lru_w_i', 'new_m_lru_b_i', 'new_m_lru_lambda', 'new_m_m2_conv_w', 'new_m_m2_conv_b', 'new_m_m2_dt_bias', 'new_m_m2_a_log', 'new_m_m2_d', 'new_m_m2_norm', 'new_m_gdn_conv_w', 'new_m_gdn_dt_bias', 'new_m_gdn_a_log', 'new_m_gdn_norm', 'new_m_w_branch', 'new_m_w_out', 'new_m_ffn2_norm', 'new_m_ffn2_w_in', 'new_m_ffn2_w_out', 'new_m_ple_norm', 'new_m_ple_w_gate', 'new_m_ple_w_proj', 'new_m_final_norm', 'new_v_ffn1_norm', 'new_v_ffn1_w_in', 'new_v_ffn1_w_out', 'new_v_mix_norm', 'new_v_w_in', 'new_v_w_gate', 'new_v_b_gate', 'new_v_s5_log_step', 'new_v_s5_a_re', 'new_v_s5_a_im', 'new_v_s5_b_re', 'new_v_s5_b_im', 'new_v_s5_c_re', 'new_v_s5_c_im', 'new_v_s5_d', 'new_v_s5_w_glu', 'new_v_s5_b_glu', 'new_v_lru_conv_w', 'new_v_lru_conv_b', 'new_v_lru_w_r', 'new_v_lru_b_r', 'new_v_lru_w_i', 'new_v_lru_b_i', 'new_v_lru_lambda', 'new_v_m2_conv_w', 'new_v_m2_conv_b', 'new_v_m2_dt_bias', 'new_v_m2_a_log', 'new_v_m2_d', 'new_v_m2_norm', 'new_v_gdn_conv_w', 'new_v_gdn_dt_bias', 'new_v_gdn_a_log', 'new_v_gdn_norm', 'new_v_w_branch', 'new_v_w_out', 'new_v_ffn2_norm', 'new_v_ffn2_w_in', 'new_v_ffn2_w_out', 'new_v_ple_norm', 'new_v_ple_w_gate', 'new_v_ple_w_proj', 'new_v_final_norm']
TWIN_LEAF_KINDS = {'loss': 'loss', 'grad_x': 'grad_x', 'grad_ffn1_norm': 'grad_w', 'grad_ffn1_w_in': 'grad_w', 'grad_ffn1_w_out': 'grad_w', 'grad_mix_norm': 'grad_w', 'grad_w_in': 'grad_w', 'grad_w_gate': 'grad_w', 'grad_b_gate': 'grad_w', 'grad_s5_log_step': 'grad_w', 'grad_s5_a_re': 'grad_w', 'grad_s5_a_im': 'grad_w', 'grad_s5_b_re': 'grad_w', 'grad_s5_b_im': 'grad_w', 'grad_s5_c_re': 'grad_w', 'grad_s5_c_im': 'grad_w', 'grad_s5_d': 'grad_w', 'grad_s5_w_glu': 'grad_w', 'grad_s5_b_glu': 'grad_w', 'grad_lru_conv_w': 'grad_w', 'grad_lru_conv_b': 'grad_w', 'grad_lru_w_r': 'grad_w', 'grad_lru_b_r': 'grad_w', 'grad_lru_w_i': 'grad_w', 'grad_lru_b_i': 'grad_w', 'grad_lru_lambda': 'grad_w', 'grad_m2_conv_w': 'grad_w', 'grad_m2_conv_b': 'grad_w', 'grad_m2_dt_bias': 'grad_w', 'grad_m2_a_log': 'grad_w', 'grad_m2_d': 'grad_w', 'grad_m2_norm': 'grad_w', 'grad_gdn_conv_w': 'grad_w', 'grad_gdn_dt_bias': 'grad_w', 'grad_gdn_a_log': 'grad_w', 'grad_gdn_norm': 'grad_w', 'grad_w_branch': 'grad_w', 'grad_w_out': 'grad_w', 'grad_ffn2_norm': 'grad_w', 'grad_ffn2_w_in': 'grad_w', 'grad_ffn2_w_out': 'grad_w', 'grad_ple_norm': 'grad_w', 'grad_ple_w_gate': 'grad_w', 'grad_ple_w_proj': 'grad_w', 'grad_final_norm': 'grad_w', 'delta_ffn1_norm': 'delta_w', 'delta_ffn1_w_in': 'delta_w', 'delta_ffn1_w_out': 'delta_w', 'delta_mix_norm': 'delta_w', 'delta_w_in': 'delta_w', 'delta_w_gate': 'delta_w', 'delta_b_gate': 'delta_w', 'delta_s5_log_step': 'delta_w', 'delta_s5_a_re': 'delta_w', 'delta_s5_a_im': 'delta_w', 'delta_s5_b_re': 'delta_w', 'delta_s5_b_im': 'delta_w', 'delta_s5_c_re': 'delta_w', 'delta_s5_c_im': 'delta_w', 'delta_s5_d': 'delta_w', 'delta_s5_w_glu': 'delta_w', 'delta_s5_b_glu': 'delta_w', 'delta_lru_conv_w': 'delta_w', 'delta_lru_conv_b': 'delta_w', 'delta_lru_w_r': 'delta_w', 'delta_lru_b_r': 'delta_w', 'delta_lru_w_i': 'delta_w', 'delta_lru_b_i': 'delta_w', 'delta_lru_lambda': 'delta_w', 'delta_m2_conv_w': 'delta_w', 'delta_m2_conv_b': 'delta_w', 'delta_m2_dt_bias': 'delta_w', 'delta_m2_a_log': 'delta_w', 'delta_m2_d': 'delta_w', 'delta_m2_norm': 'delta_w', 'delta_gdn_conv_w': 'delta_w', 'delta_gdn_dt_bias': 'delta_w', 'delta_gdn_a_log': 'delta_w', 'delta_gdn_norm': 'delta_w', 'delta_w_branch': 'delta_w', 'delta_w_out': 'delta_w', 'delta_ffn2_norm': 'delta_w', 'delta_ffn2_w_in': 'delta_w', 'delta_ffn2_w_out': 'delta_w', 'delta_ple_norm': 'delta_w', 'delta_ple_w_gate': 'delta_w', 'delta_ple_w_proj': 'delta_w', 'delta_final_norm': 'delta_w', 'new_m_ffn1_norm': 'new_m', 'new_m_ffn1_w_in': 'new_m', 'new_m_ffn1_w_out': 'new_m', 'new_m_mix_norm': 'new_m', 'new_m_w_in': 'new_m', 'new_m_w_gate': 'new_m', 'new_m_b_gate': 'new_m', 'new_m_s5_log_step': 'new_m', 'new_m_s5_a_re': 'new_m', 'new_m_s5_a_im': 'new_m', 'new_m_s5_b_re': 'new_m', 'new_m_s5_b_im': 'new_m', 'new_m_s5_c_re': 'new_m', 'new_m_s5_c_im': 'new_m', 'new_m_s5_d': 'new_m', 'new_m_s5_w_glu': 'new_m', 'new_m_s5_b_glu': 'new_m', 'new_m_lru_conv_w': 'new_m', 'new_m_lru_conv_b': 'new_m', 'new_m_lru_w_r': 'new_m', 'new_m_lru_b_r': 'new_m', 'new_m_lru_w_i': 'new_m', 'new_m_lru_b_i': 'new_m', 'new_m_lru_lambda': 'new_m', 'new_m_m2_conv_w': 'new_m', 'new_m_m2_conv_b': 'new_m', 'new_m_m2_dt_bias': 'new_m', 'new_m_m2_a_log': 'new_m', 'new_m_m2_d': 'new_m', 'new_m_m2_norm': 'new_m', 'new_m_gdn_conv_w': 'new_m', 'new_m_gdn_dt_bias': 'new_m', 'new_m_gdn_a_log': 'new_m', 'new_m_gdn_norm': 'new_m', 'new_m_w_branch': 'new_m', 'new_m_w_out': 'new_m', 'new_m_ffn2_norm': 'new_m', 'new_m_ffn2_w_in': 'new_m', 'new_m_ffn2_w_out': 'new_m', 'new_m_ple_norm': 'new_m', 'new_m_ple_w_gate': 'new_m', 'new_m_ple_w_proj': 'new_m', 'new_m_final_norm': 'new_m', 'new_v_ffn1_norm': 'new_v', 'new_v_ffn1_w_in': 'new_v', 'new_v_ffn1_w_out': 'new_v', 'new_v_mix_norm': 'new_v', 'new_v_w_in': 'new_v', 'new_v_w_gate': 'new_v', 'new_v_b_gate': 'new_v', 'new_v_s5_log_step': 'new_v', 'new_v_s5_a_re': 'new_v', 'new_v_s5_a_im': 'new_v', 'new_v_s5_b_re': 'new_v', 'new_v_s5_b_im': 'new_v', 'new_v_s5_c_re': 'new_v', 'new_v_s5_c_im': 'new_v', 'new_v_s5_d': 'new_v', 'new_v_s5_w_glu': 'new_v', 'new_v_s5_b_glu': 'new_v', 'new_v_lru_conv_w': 'new_v', 'new_v_lru_conv_b': 'new_v', 'new_v_lru_w_r': 'new_v', 'new_v_lru_b_r': 'new_v', 'new_v_lru_w_i': 'new_v', 'new_v_lru_b_i': 'new_v', 'new_v_lru_lambda': 'new_v', 'new_v_m2_conv_w': 'new_v', 'new_v_m2_conv_b': 'new_v', 'new_v_m2_dt_bias': 'new_v', 'new_v_m2_a_log': 'new_v', 'new_v_m2_d': 'new_v', 'new_v_m2_norm': 'new_v', 'new_v_gdn_conv_w': 'new_v', 'new_v_gdn_dt_bias': 'new_v', 'new_v_gdn_a_log': 'new_v', 'new_v_gdn_norm': 'new_v', 'new_v_w_branch': 'new_v', 'new_v_w_out': 'new_v', 'new_v_ffn2_norm': 'new_v', 'new_v_ffn2_w_in': 'new_v', 'new_v_ffn2_w_out': 'new_v', 'new_v_ple_norm': 'new_v', 'new_v_ple_w_gate': 'new_v', 'new_v_ple_w_proj': 'new_v', 'new_v_final_norm': 'new_v'}


def _forward(args):
    return _fwd_reference(*[args[k] for k in FWD_PARAMS])


def _output_shape():
    def fwd():
        inp = _fwd_setup_inputs(0)
        return _fwd_reference(*[inp[k] for k in FWD_PARAMS])
    out = _jax.eval_shape(fwd)
    return out.shape, out.dtype

N_MICROBATCH = 1
ADAM_LR = 0.001
ADAM_B1 = 0.9
ADAM_B2 = 0.999
ADAM_EPS = 1e-08
ADAM_WD = 0.01
ADAM_STEP = 10
PER_EXAMPLE_BATCH_AXIS = {'x': 0, 'p': 1, 'loss_target': 0}
SHARED_INPUTS = []
_WEIGHT_DTYPES = {'ffn1_norm': _jnp.float32, 'ffn1_w_in': _jnp.float32, 'ffn1_w_out': _jnp.float32, 'mix_norm': _jnp.float32, 'w_in': _jnp.float32, 'w_gate': _jnp.float32, 'b_gate': _jnp.float32, 's5_log_step': _jnp.float32, 's5_a_re': _jnp.float32, 's5_a_im': _jnp.float32, 's5_b_re': _jnp.float32, 's5_b_im': _jnp.float32, 's5_c_re': _jnp.float32, 's5_c_im': _jnp.float32, 's5_d': _jnp.float32, 's5_w_glu': _jnp.float32, 's5_b_glu': _jnp.float32, 'lru_conv_w': _jnp.float32, 'lru_conv_b': _jnp.float32, 'lru_w_r': _jnp.float32, 'lru_b_r': _jnp.float32, 'lru_w_i': _jnp.float32, 'lru_b_i': _jnp.float32, 'lru_lambda': _jnp.float32, 'm2_conv_w': _jnp.float32, 'm2_conv_b': _jnp.float32, 'm2_dt_bias': _jnp.float32, 'm2_a_log': _jnp.float32, 'm2_d': _jnp.float32, 'm2_norm': _jnp.float32, 'gdn_conv_w': _jnp.float32, 'gdn_dt_bias': _jnp.float32, 'gdn_a_log': _jnp.float32, 'gdn_norm': _jnp.float32, 'w_branch': _jnp.float32, 'w_out': _jnp.float32, 'ffn2_norm': _jnp.float32, 'ffn2_w_in': _jnp.float32, 'ffn2_w_out': _jnp.float32, 'ple_norm': _jnp.float32, 'ple_w_gate': _jnp.float32, 'ple_w_proj': _jnp.float32, 'final_norm': _jnp.float32}
MOMENT_SCALE = {'ffn1_norm': 1.140365e-01, 'ffn1_w_in': 4.718487e-02, 'ffn1_w_out': 7.683314e-02, 'mix_norm': 2.060706e-01, 'w_in': 8.827516e-02, 'w_gate': 2.590764e-02, 'b_gate': 2.802060e-02, 's5_log_step': 2.590008e+00, 's5_a_re': 2.637742e-03, 's5_a_im': 2.705820e-03, 's5_b_re': 1.733281e-03, 's5_b_im': 1.640425e-03, 's5_c_re': 3.481672e-03, 's5_c_im': 3.539105e-03, 's5_d': 6.506666e-02, 's5_w_glu': 1.486655e-02, 's5_b_glu': 2.821551e-02, 'lru_conv_w': 8.886724e-02, 'lru_conv_b': 8.410557e-01, 'lru_w_r': 2.739227e-02, 'lru_b_r': 2.302981e-02, 'lru_w_i': 5.058163e-02, 'lru_b_i': 2.753453e-02, 'lru_lambda': 4.405147e-02, 'm2_conv_w': 1.110523e-01, 'm2_conv_b': 1.536448e-01, 'm2_dt_bias': 2.749376e-01, 'm2_a_log': 3.214060e-01, 'm2_d': 8.615848e-01, 'm2_norm': 1.443283e-01, 'gdn_conv_w': 6.359870e-02, 'gdn_dt_bias': 4.596542e-01, 'gdn_a_log': 5.799312e-01, 'gdn_norm': 1.676520e-01, 'w_branch': 6.915290e-02, 'w_out': 1.380539e-01, 'ffn2_norm': 7.809312e-02, 'ffn2_w_in': 3.265442e-02, 'ffn2_w_out': 5.326994e-02, 'ple_norm': 3.829652e-02, 'ple_w_gate': 3.740255e-02, 'ple_w_proj': 9.544484e-02, 'final_norm': 6.398211e+01}


def _to_microbatches(a, axis):
    t = _jnp.moveaxis(a, axis, 0)
    t = t.reshape((N_MICROBATCH, t.shape[0] // N_MICROBATCH) + t.shape[1:])
    return _jnp.moveaxis(t, 1, axis + 1)


def setup_inputs(seed: int = 0) -> dict:
    inp = _fwd_setup_inputs(seed)
    key = _jax.random.fold_in(_jax.random.key(seed), 7919)
    shape, _ = _output_shape()
    out = dict(inp)
    out["loss_target"] = _jax.random.normal(_jax.random.fold_in(key, 0), shape, _jnp.float32)
    for i, name in enumerate(TWIN_WEIGHTS):
        w = inp[name].astype(_jnp.float32)
        if MOMENT_SCALE is None:
            s = _jnp.sqrt(_jnp.mean(_jnp.square(w)) + 1e-30)
        else:
            s = MOMENT_SCALE[name]
        km, kv = _jax.random.split(_jax.random.fold_in(key, i + 1))
        out[name] = w
        out["m_" + name] = s * _jax.random.normal(km, w.shape, _jnp.float32)
        out["v_" + name] = (s * s) * _jax.random.uniform(kv, w.shape, _jnp.float32, 0.5, 1.5)
    if N_MICROBATCH > 1:
        for name, axis in PER_EXAMPLE_BATCH_AXIS.items():
            out[name] = _to_microbatches(out[name], axis)
    return {'x': out['x'], 'p': out['p'], 'ffn1_norm': out['ffn1_norm'], 'ffn1_w_in': out['ffn1_w_in'], 'ffn1_w_out': out['ffn1_w_out'], 'mix_norm': out['mix_norm'], 'w_in': out['w_in'], 'w_gate': out['w_gate'], 'b_gate': out['b_gate'], 's5_log_step': out['s5_log_step'], 's5_a_re': out['s5_a_re'], 's5_a_im': out['s5_a_im'], 's5_b_re': out['s5_b_re'], 's5_b_im': out['s5_b_im'], 's5_c_re': out['s5_c_re'], 's5_c_im': out['s5_c_im'], 's5_d': out['s5_d'], 's5_w_glu': out['s5_w_glu'], 's5_b_glu': out['s5_b_glu'], 'lru_conv_w': out['lru_conv_w'], 'lru_conv_b': out['lru_conv_b'], 'lru_w_r': out['lru_w_r'], 'lru_b_r': out['lru_b_r'], 'lru_w_i': out['lru_w_i'], 'lru_b_i': out['lru_b_i'], 'lru_lambda': out['lru_lambda'], 'm2_conv_w': out['m2_conv_w'], 'm2_conv_b': out['m2_conv_b'], 'm2_dt_bias': out['m2_dt_bias'], 'm2_a_log': out['m2_a_log'], 'm2_d': out['m2_d'], 'm2_norm': out['m2_norm'], 'gdn_conv_w': out['gdn_conv_w'], 'gdn_dt_bias': out['gdn_dt_bias'], 'gdn_a_log': out['gdn_a_log'], 'gdn_norm': out['gdn_norm'], 'w_branch': out['w_branch'], 'w_out': out['w_out'], 'ffn2_norm': out['ffn2_norm'], 'ffn2_w_in': out['ffn2_w_in'], 'ffn2_w_out': out['ffn2_w_out'], 'ple_norm': out['ple_norm'], 'ple_w_gate': out['ple_w_gate'], 'ple_w_proj': out['ple_w_proj'], 'final_norm': out['final_norm'], 'loss_target': out['loss_target'], 'm_ffn1_norm': out['m_ffn1_norm'], 'm_ffn1_w_in': out['m_ffn1_w_in'], 'm_ffn1_w_out': out['m_ffn1_w_out'], 'm_mix_norm': out['m_mix_norm'], 'm_w_in': out['m_w_in'], 'm_w_gate': out['m_w_gate'], 'm_b_gate': out['m_b_gate'], 'm_s5_log_step': out['m_s5_log_step'], 'm_s5_a_re': out['m_s5_a_re'], 'm_s5_a_im': out['m_s5_a_im'], 'm_s5_b_re': out['m_s5_b_re'], 'm_s5_b_im': out['m_s5_b_im'], 'm_s5_c_re': out['m_s5_c_re'], 'm_s5_c_im': out['m_s5_c_im'], 'm_s5_d': out['m_s5_d'], 'm_s5_w_glu': out['m_s5_w_glu'], 'm_s5_b_glu': out['m_s5_b_glu'], 'm_lru_conv_w': out['m_lru_conv_w'], 'm_lru_conv_b': out['m_lru_conv_b'], 'm_lru_w_r': out['m_lru_w_r'], 'm_lru_b_r': out['m_lru_b_r'], 'm_lru_w_i': out['m_lru_w_i'], 'm_lru_b_i': out['m_lru_b_i'], 'm_lru_lambda': out['m_lru_lambda'], 'm_m2_conv_w': out['m_m2_conv_w'], 'm_m2_conv_b': out['m_m2_conv_b'], 'm_m2_dt_bias': out['m_m2_dt_bias'], 'm_m2_a_log': out['m_m2_a_log'], 'm_m2_d': out['m_m2_d'], 'm_m2_norm': out['m_m2_norm'], 'm_gdn_conv_w': out['m_gdn_conv_w'], 'm_gdn_dt_bias': out['m_gdn_dt_bias'], 'm_gdn_a_log': out['m_gdn_a_log'], 'm_gdn_norm': out['m_gdn_norm'], 'm_w_branch': out['m_w_branch'], 'm_w_out': out['m_w_out'], 'm_ffn2_norm': out['m_ffn2_norm'], 'm_ffn2_w_in': out['m_ffn2_w_in'], 'm_ffn2_w_out': out['m_ffn2_w_out'], 'm_ple_norm': out['m_ple_norm'], 'm_ple_w_gate': out['m_ple_w_gate'], 'm_ple_w_proj': out['m_ple_w_proj'], 'm_final_norm': out['m_final_norm'], 'v_ffn1_norm': out['v_ffn1_norm'], 'v_ffn1_w_in': out['v_ffn1_w_in'], 'v_ffn1_w_out': out['v_ffn1_w_out'], 'v_mix_norm': out['v_mix_norm'], 'v_w_in': out['v_w_in'], 'v_w_gate': out['v_w_gate'], 'v_b_gate': out['v_b_gate'], 'v_s5_log_step': out['v_s5_log_step'], 'v_s5_a_re': out['v_s5_a_re'], 'v_s5_a_im': out['v_s5_a_im'], 'v_s5_b_re': out['v_s5_b_re'], 'v_s5_b_im': out['v_s5_b_im'], 'v_s5_c_re': out['v_s5_c_re'], 'v_s5_c_im': out['v_s5_c_im'], 'v_s5_d': out['v_s5_d'], 'v_s5_w_glu': out['v_s5_w_glu'], 'v_s5_b_glu': out['v_s5_b_glu'], 'v_lru_conv_w': out['v_lru_conv_w'], 'v_lru_conv_b': out['v_lru_conv_b'], 'v_lru_w_r': out['v_lru_w_r'], 'v_lru_b_r': out['v_lru_b_r'], 'v_lru_w_i': out['v_lru_w_i'], 'v_lru_b_i': out['v_lru_b_i'], 'v_lru_lambda': out['v_lru_lambda'], 'v_m2_conv_w': out['v_m2_conv_w'], 'v_m2_conv_b': out['v_m2_conv_b'], 'v_m2_dt_bias': out['v_m2_dt_bias'], 'v_m2_a_log': out['v_m2_a_log'], 'v_m2_d': out['v_m2_d'], 'v_m2_norm': out['v_m2_norm'], 'v_gdn_conv_w': out['v_gdn_conv_w'], 'v_gdn_dt_bias': out['v_gdn_dt_bias'], 'v_gdn_a_log': out['v_gdn_a_log'], 'v_gdn_norm': out['v_gdn_norm'], 'v_w_branch': out['v_w_branch'], 'v_w_out': out['v_w_out'], 'v_ffn2_norm': out['v_ffn2_norm'], 'v_ffn2_w_in': out['v_ffn2_w_in'], 'v_ffn2_w_out': out['v_ffn2_w_out'], 'v_ple_norm': out['v_ple_norm'], 'v_ple_w_gate': out['v_ple_w_gate'], 'v_ple_w_proj': out['v_ple_w_proj'], 'v_final_norm': out['v_final_norm']}


def _loss(weights, diff, rest, loss_target):
    with _jax.named_scope("forward"):
        args = {**rest, TWIN_DIFF_INPUT: diff, **{k: w.astype(_WEIGHT_DTYPES[k]) for k, w in weights.items()}}
        y = _forward(args)
    with _jax.named_scope("loss_head"):
        err = _jnp.square(y.astype(_jnp.float32) - loss_target)
        return 0.5 * _jnp.sum(_jnp.mean(err, axis=-1)) if err.ndim else 0.5 * err


def _adamw(w, g, m, v):
    m = ADAM_B1 * m + (1.0 - ADAM_B1) * g
    v = ADAM_B2 * v + (1.0 - ADAM_B2) * _jnp.square(g)
    m_hat = m / (1.0 - ADAM_B1 ** ADAM_STEP)
    v_hat = v / (1.0 - ADAM_B2 ** ADAM_STEP)
    delta = -ADAM_LR * (m_hat / (_jnp.sqrt(v_hat) + ADAM_EPS) + ADAM_WD * w)
    return delta, m, v


def reference(x, p, ffn1_norm, ffn1_w_in, ffn1_w_out, mix_norm, w_in, w_gate, b_gate, s5_log_step, s5_a_re, s5_a_im, s5_b_re, s5_b_im, s5_c_re, s5_c_im, s5_d, s5_w_glu, s5_b_glu, lru_conv_w, lru_conv_b, lru_w_r, lru_b_r, lru_w_i, lru_b_i, lru_lambda, m2_conv_w, m2_conv_b, m2_dt_bias, m2_a_log, m2_d, m2_norm, gdn_conv_w, gdn_dt_bias, gdn_a_log, gdn_norm, w_branch, w_out, ffn2_norm, ffn2_w_in, ffn2_w_out, ple_norm, ple_w_gate, ple_w_proj, final_norm, loss_target, m_ffn1_norm, m_ffn1_w_in, m_ffn1_w_out, m_mix_norm, m_w_in, m_w_gate, m_b_gate, m_s5_log_step, m_s5_a_re, m_s5_a_im, m_s5_b_re, m_s5_b_im, m_s5_c_re, m_s5_c_im, m_s5_d, m_s5_w_glu, m_s5_b_glu, m_lru_conv_w, m_lru_conv_b, m_lru_w_r, m_lru_b_r, m_lru_w_i, m_lru_b_i, m_lru_lambda, m_m2_conv_w, m_m2_conv_b, m_m2_dt_bias, m_m2_a_log, m_m2_d, m_m2_norm, m_gdn_conv_w, m_gdn_dt_bias, m_gdn_a_log, m_gdn_norm, m_w_branch, m_w_out, m_ffn2_norm, m_ffn2_w_in, m_ffn2_w_out, m_ple_norm, m_ple_w_gate, m_ple_w_proj, m_final_norm, v_ffn1_norm, v_ffn1_w_in, v_ffn1_w_out, v_mix_norm, v_w_in, v_w_gate, v_b_gate, v_s5_log_step, v_s5_a_re, v_s5_a_im, v_s5_b_re, v_s5_b_im, v_s5_c_re, v_s5_c_im, v_s5_d, v_s5_w_glu, v_s5_b_glu, v_lru_conv_w, v_lru_conv_b, v_lru_w_r, v_lru_b_r, v_lru_w_i, v_lru_b_i, v_lru_lambda, v_m2_conv_w, v_m2_conv_b, v_m2_dt_bias, v_m2_a_log, v_m2_d, v_m2_norm, v_gdn_conv_w, v_gdn_dt_bias, v_gdn_a_log, v_gdn_norm, v_w_branch, v_w_out, v_ffn2_norm, v_ffn2_w_in, v_ffn2_w_out, v_ple_norm, v_ple_w_gate, v_ple_w_proj, v_final_norm):
    given = dict(x=x, p=p, ffn1_norm=ffn1_norm, ffn1_w_in=ffn1_w_in, ffn1_w_out=ffn1_w_out, mix_norm=mix_norm, w_in=w_in, w_gate=w_gate, b_gate=b_gate, s5_log_step=s5_log_step, s5_a_re=s5_a_re, s5_a_im=s5_a_im, s5_b_re=s5_b_re, s5_b_im=s5_b_im, s5_c_re=s5_c_re, s5_c_im=s5_c_im, s5_d=s5_d, s5_w_glu=s5_w_glu, s5_b_glu=s5_b_glu, lru_conv_w=lru_conv_w, lru_conv_b=lru_conv_b, lru_w_r=lru_w_r, lru_b_r=lru_b_r, lru_w_i=lru_w_i, lru_b_i=lru_b_i, lru_lambda=lru_lambda, m2_conv_w=m2_conv_w, m2_conv_b=m2_conv_b, m2_dt_bias=m2_dt_bias, m2_a_log=m2_a_log, m2_d=m2_d, m2_norm=m2_norm, gdn_conv_w=gdn_conv_w, gdn_dt_bias=gdn_dt_bias, gdn_a_log=gdn_a_log, gdn_norm=gdn_norm, w_branch=w_branch, w_out=w_out, ffn2_norm=ffn2_norm, ffn2_w_in=ffn2_w_in, ffn2_w_out=ffn2_w_out, ple_norm=ple_norm, ple_w_gate=ple_w_gate, ple_w_proj=ple_w_proj, final_norm=final_norm, loss_target=loss_target, m_ffn1_norm=m_ffn1_norm, m_ffn1_w_in=m_ffn1_w_in, m_ffn1_w_out=m_ffn1_w_out, m_mix_norm=m_mix_norm, m_w_in=m_w_in, m_w_gate=m_w_gate, m_b_gate=m_b_gate, m_s5_log_step=m_s5_log_step, m_s5_a_re=m_s5_a_re, m_s5_a_im=m_s5_a_im, m_s5_b_re=m_s5_b_re, m_s5_b_im=m_s5_b_im, m_s5_c_re=m_s5_c_re, m_s5_c_im=m_s5_c_im, m_s5_d=m_s5_d, m_s5_w_glu=m_s5_w_glu, m_s5_b_glu=m_s5_b_glu, m_lru_conv_w=m_lru_conv_w, m_lru_conv_b=m_lru_conv_b, m_lru_w_r=m_lru_w_r, m_lru_b_r=m_lru_b_r, m_lru_w_i=m_lru_w_i, m_lru_b_i=m_lru_b_i, m_lru_lambda=m_lru_lambda, m_m2_conv_w=m_m2_conv_w, m_m2_conv_b=m_m2_conv_b, m_m2_dt_bias=m_m2_dt_bias, m_m2_a_log=m_m2_a_log, m_m2_d=m_m2_d, m_m2_norm=m_m2_norm, m_gdn_conv_w=m_gdn_conv_w, m_gdn_dt_bias=m_gdn_dt_bias, m_gdn_a_log=m_gdn_a_log, m_gdn_norm=m_gdn_norm, m_w_branch=m_w_branch, m_w_out=m_w_out, m_ffn2_norm=m_ffn2_norm, m_ffn2_w_in=m_ffn2_w_in, m_ffn2_w_out=m_ffn2_w_out, m_ple_norm=m_ple_norm, m_ple_w_gate=m_ple_w_gate, m_ple_w_proj=m_ple_w_proj, m_final_norm=m_final_norm, v_ffn1_norm=v_ffn1_norm, v_ffn1_w_in=v_ffn1_w_in, v_ffn1_w_out=v_ffn1_w_out, v_mix_norm=v_mix_norm, v_w_in=v_w_in, v_w_gate=v_w_gate, v_b_gate=v_b_gate, v_s5_log_step=v_s5_log_step, v_s5_a_re=v_s5_a_re, v_s5_a_im=v_s5_a_im, v_s5_b_re=v_s5_b_re, v_s5_b_im=v_s5_b_im, v_s5_c_re=v_s5_c_re, v_s5_c_im=v_s5_c_im, v_s5_d=v_s5_d, v_s5_w_glu=v_s5_w_glu, v_s5_b_glu=v_s5_b_glu, v_lru_conv_w=v_lru_conv_w, v_lru_conv_b=v_lru_conv_b, v_lru_w_r=v_lru_w_r, v_lru_b_r=v_lru_b_r, v_lru_w_i=v_lru_w_i, v_lru_b_i=v_lru_b_i, v_lru_lambda=v_lru_lambda, v_m2_conv_w=v_m2_conv_w, v_m2_conv_b=v_m2_conv_b, v_m2_dt_bias=v_m2_dt_bias, v_m2_a_log=v_m2_a_log, v_m2_d=v_m2_d, v_m2_norm=v_m2_norm, v_gdn_conv_w=v_gdn_conv_w, v_gdn_dt_bias=v_gdn_dt_bias, v_gdn_a_log=v_gdn_a_log, v_gdn_norm=v_gdn_norm, v_w_branch=v_w_branch, v_w_out=v_w_out, v_ffn2_norm=v_ffn2_norm, v_ffn2_w_in=v_ffn2_w_in, v_ffn2_w_out=v_ffn2_w_out, v_ple_norm=v_ple_norm, v_ple_w_gate=v_ple_w_gate, v_ple_w_proj=v_ple_w_proj, v_final_norm=v_final_norm)
    weights = {n: given[n] for n in TWIN_WEIGHTS}
    shared = {n: given[n] for n in SHARED_INPUTS}
    per_example = {n: given[n] for n in ['x', 'p']}
    grad_fn = _jax.value_and_grad(_loss, argnums=(0, 1))

    def one_microbatch(ex, loss_target):
        ex = dict(ex)
        diff = ex.pop(TWIN_DIFF_INPUT)
        return grad_fn(weights, diff, {**shared, **ex}, loss_target)

    if N_MICROBATCH == 1:
        loss, (grad_w, grad_x) = one_microbatch(per_example, given["loss_target"])
    else:
        def body(carry, xs):
            loss_sum, grad_sum = carry
            l_k, (gw_k, gx_k) = one_microbatch(xs[0], xs[1])
            with _jax.named_scope("update"):
                return (loss_sum + l_k, _jax.tree.map(_jnp.add, grad_sum, gw_k)), gx_k

        init = (_jnp.zeros((), _jnp.float32), _jax.tree.map(_jnp.zeros_like, weights))
        (loss, grad_w), grad_x = _jax.lax.scan(body, init, (per_example, given["loss_target"]))
    with _jax.named_scope("update"):
        delta_w, new_m, new_v = {}, {}, {}
        for n in TWIN_WEIGHTS:
            delta_w[n], new_m[n], new_v[n] = _adamw(weights[n], grad_w[n], given["m_" + n], given["v_" + n])
    return (loss, grad_x, *[grad_w[n] for n in TWIN_WEIGHTS], *[delta_w[n] for n in TWIN_WEIGHTS],
            *[new_m[n] for n in TWIN_WEIGHTS], *[new_v[n] for n in TWIN_WEIGHTS])
```

```python
import functools
import math

import jax
import jax.numpy as jnp
from jax import lax
from jax.experimental import pallas as pl
from jax.experimental.pallas import tpu as pltpu

F32, BF16 = jnp.float32, jnp.bfloat16
EPS = 1e-6
D_MODEL = 1024
DEPTH = 2
FFN_DIM = 2816
BRANCH_WIDTH = 512
N_BRANCH = 4
LRU_C = 8.0
S5_GROUPS, S5_GROUP_CH, S5_STATE = 32, 16, 64
S5_W = S5_GROUPS * S5_STATE
LRU_HEADS, LRU_HEAD_DIM = 8, 64
M2_HEADS, M2_HEAD_DIM, M2_GROUPS, M2_STATE = 8, 64, 2, 128
GDN_HEADS, GDN_HEAD_DIM = 4, 128
CHUNK = 128
ADAM_LR, ADAM_B1, ADAM_B2, ADAM_EPS, ADAM_WD, ADAM_STEP = 0.001, 0.9, 0.999, 1e-08, 0.01, 10
VMEM_LIMIT_BYTES = 56 * 1024 * 1024
MESH = pl.DeviceIdType.MESH


def _params(sem=None):
    return pltpu.CompilerParams(vmem_limit_bytes=VMEM_LIMIT_BYTES, dimension_semantics=sem)


def _dot_bf16(a, b, dims):
    return lax.dot_general(a.astype(BF16), b.astype(BF16), (dims, ((), ())), preferred_element_type=F32)


def _dot_f32(a, b, dims):
    return lax.dot_general(a, b, (dims, ((), ())), precision=lax.Precision.HIGHEST, preferred_element_type=F32)


def _make_mm(dot):
    @jax.custom_vjp
    def nn(a, b):
        return dot(a, b, ((1,), (0,)))

    @jax.custom_vjp
    def nt(a, b):
        return dot(a, b, ((1,), (1,)))

    @jax.custom_vjp
    def tn(a, b):
        return dot(a, b, ((0,), (0,)))

    nn.defvjp(lambda a, b: (nn(a, b), (a, b)), lambda r, g: (nt(g, r[1]), tn(r[0], g)))
    nt.defvjp(lambda a, b: (nt(a, b), (a, b)), lambda r, g: (nn(g, r[1]), tn(g, r[0])))
    tn.defvjp(lambda a, b: (tn(a, b), (a, b)), lambda r, g: (nt(r[1], g), nn(r[0], g)))
    return nn, nt, tn


mm, mm_nt, mm_tn = _make_mm(_dot_bf16)
mmh, mmh_nt, mmh_tn = _make_mm(_dot_f32)


def _row_ids(shape):
    return lax.broadcasted_iota(jnp.int32, shape, 0)


def _shift_down(x, d):
    return jnp.where(_row_ids(x.shape) >= d, pltpu.roll(x, d, 0), 0.0)


def _shift_up(x, d):
    n = x.shape[0]
    return jnp.where(_row_ids(x.shape) < n - d, pltpu.roll(x, n - d, 0), 0.0)


def _first_row(x):
    return jnp.sum(jnp.where(_row_ids(x.shape) == 0, x, 0.0), axis=0, keepdims=True)


def last_row(x):
    return jnp.sum(jnp.where(_row_ids(x.shape) == x.shape[0] - 1, x, 0.0), axis=0, keepdims=True)


def pick_row(x, j):
    return jnp.sum(jnp.where(_row_ids(x.shape) == j, x, 0.0), axis=0, keepdims=True)


@jax.custom_vjp
def lin_scan(a, b, h0):
    n = a.shape[0]
    row = _row_ids(a.shape)
    acc_a = a
    acc_b = b + jnp.where(row == 0, a * h0, 0.0)
    d = 1
    while d < n:
        acc_b = acc_a * _shift_down(acc_b, d) + acc_b
        acc_a = acc_a * jnp.where(row >= d, pltpu.roll(acc_a, d, 0), 1.0)
        d *= 2
    return acc_b


def _lin_scan_fwd(a, b, h0):
    h = lin_scan(a, b, h0)
    return h, (a, h, h0)


def _lin_scan_bwd(res, dh):
    a, h, h0 = res
    n = a.shape[0]
    row = _row_ids(a.shape)
    acc_a = _shift_up(a, 1)
    g = dh
    d = 1
    while d < n:
        g = acc_a * _shift_up(g, d) + g
        acc_a = acc_a * jnp.where(row < n - d, pltpu.roll(acc_a, n - d, 0), 1.0)
        d *= 2
    h_prev = _shift_down(h, 1) + jnp.where(row == 0, h0, 0.0)
    return g * h_prev, g, _first_row(a * g)


lin_scan.defvjp(_lin_scan_fwd, _lin_scan_bwd)


def _cscan(br, bi, ar, ai, up):
    n = br.shape[0]
    shift = _shift_up if up else _shift_down
    hr, hi, pr, pi = br, bi, ar, ai
    d = 1
    while d < n:
        sr, si = shift(hr, d), shift(hi, d)
        hr, hi = hr + pr * sr - pi * si, hi + pr * si + pi * sr
        pr, pi = pr * pr - pi * pi, 2.0 * pr * pi
        d *= 2
    return hr, hi


@jax.custom_vjp
def complex_scan(br, bi, ar, ai, h0r, h0i):
    first = _row_ids(br.shape) == 0
    br = br + jnp.where(first, ar * h0r - ai * h0i, 0.0)
    bi = bi + jnp.where(first, ar * h0i + ai * h0r, 0.0)
    return _cscan(br, bi, ar, ai, False)


def _complex_scan_fwd(br, bi, ar, ai, h0r, h0i):
    hr, hi = complex_scan(br, bi, ar, ai, h0r, h0i)
    return (hr, hi), (ar, ai, hr, hi, h0r, h0i)


def _complex_scan_bwd(res, cts):
    ar, ai, hr, hi, h0r, h0i = res
    gr, gi = _cscan(cts[0], cts[1], ar, -ai, True)
    first = _row_ids(hr.shape) == 0
    pr = _shift_down(hr, 1) + jnp.where(first, h0r, 0.0)
    pi = _shift_down(hi, 1) + jnp.where(first, h0i, 0.0)
    d_ar = jnp.sum(gr * pr + gi * pi, axis=0, keepdims=True)
    d_ai = jnp.sum(gi * pr - gr * pi, axis=0, keepdims=True)
    g0r, g0i = _first_row(gr), _first_row(gi)
    return gr, gi, d_ar, d_ai, ar * g0r + ai * g0i, ar * g0i - ai * g0r


complex_scan.defvjp(_complex_scan_fwd, _complex_scan_bwd)

TAIL = 8


@jax.custom_vjp
def tail_rows(x):
    return x[x.shape[0] - TAIL:, :]


tail_rows.defvjp(
    lambda x: (tail_rows(x), x.shape[0]),
    lambda n, g: (jnp.concatenate([jnp.zeros((n - TAIL, g.shape[1]), g.dtype), g], axis=0),),
)


def _make_shift_tail(d):
    @jax.custom_vjp
    def shifted(x, tail):
        n = x.shape[0]
        tpad = jnp.concatenate([tail, jnp.zeros((n - TAIL, x.shape[1]), x.dtype)], axis=0)
        return jnp.where(_row_ids(x.shape) >= d, pltpu.roll(x, d, 0), pltpu.roll(tpad, n + d - TAIL, 0))

    def fwd(x, tail):
        return shifted(x, tail), None

    def bwd(_, g):
        g8 = g[:TAIL, :]
        dtail = jnp.where(_row_ids(g8.shape) >= TAIL - d, pltpu.roll(g8, TAIL - d, 0), 0.0)
        return _shift_up(g, d), dtail

    shifted.defvjp(fwd, bwd)
    return shifted


_SHIFT_TAIL = {d: _make_shift_tail(d) for d in (1, 2, 3)}


def causal_conv4(x, tail, w):
    y = pick_row(w, 3) * x
    for j in range(3):
        y = y + pick_row(w, j) * _SHIFT_TAIL[3 - j](x, tail)
    return y


def rmsnorm(x, g):
    return x * lax.rsqrt(jnp.mean(x * x, axis=-1, keepdims=True) + EPS) * g


def to_row(col):
    n = col.shape[0]
    eye = lax.broadcasted_iota(jnp.int32, (n, n), 0) == lax.broadcasted_iota(jnp.int32, (n, n), 1)
    return jnp.sum(jnp.where(eye, col, 0.0), axis=0, keepdims=True)


def causal_decay(a_col):
    n = a_col.shape[0]
    causal = lax.broadcasted_iota(jnp.int32, (n, n), 0) >= lax.broadcasted_iota(jnp.int32, (n, n), 1)
    cs = jnp.sum(jnp.where(causal, to_row(a_col), 0.0), axis=1, keepdims=True)
    diff = cs - to_row(cs)
    return cs, jnp.where(causal, jnp.exp(jnp.where(causal, diff, 0.0)), 0.0)


class Arg:
    def __init__(self, block, imap, dtype=F32, grad=True, shared=True):
        self.block, self.imap, self.dtype, self.grad, self.shared = block, imap, dtype, grad, shared


def rows(width, nblk, col=lambda g: 0, dtype=F32, grad=True):
    return Arg((nblk, width), lambda g, c: (c, col(g)), dtype, grad)


def head_rows(width, nblk, head=lambda g: g, dtype=F32, grad=True):
    return Arg((None, nblk, width), lambda g, c: (head(g), c, 0), dtype, grad)


def whole(shape, grad=True):
    return Arg(tuple(shape), lambda g, c: (0,) * len(shape), F32, grad, shared=True)


def per_group(shape, idx=lambda g: g, grad=True):
    return Arg((None,) + tuple(shape), lambda g, c: (idx(g),) + (0,) * len(shape), F32, grad, shared=False)


def _bshape(block):
    return tuple(b for b in block if b is not None)


def _spec(arg, nb=None):
    if nb is None:
        return pl.BlockSpec(arg.block, arg.imap)
    return pl.BlockSpec(arg.block, lambda g, c: arg.imap(g, nb - 1 - c))


def blocked_forward(name, fn, groups, nb, tok_args, tok, const_args, consts, out_args, out_shapes, state_shapes, save):
    n_tok, n_const, n_out, n_state = len(tok), len(consts), len(out_args), len(state_shapes)

    def body(*refs):
        tok_refs = refs[:n_tok]
        const_refs = refs[n_tok:n_tok + n_const]
        out_refs = refs[n_tok + n_const:n_tok + n_const + n_out]
        pos = n_tok + n_const + n_out
        save_refs = refs[pos:pos + (n_state if save else 0)]
        state_refs = refs[len(refs) - n_state:] if n_state else ()

        @pl.when(pl.program_id(1) == 0)
        def _():
            for s in state_refs:
                s[...] = jnp.zeros_like(s)

        states = [s[...] for s in state_refs]
        for sr, s in zip(save_refs, states):
            sr[...] = s
        new_states, outs = fn(states, [r[...].astype(F32) for r in tok_refs], [r[...] for r in const_refs])
        for o_ref, o in zip(out_refs, outs):
            o_ref[...] = o.astype(o_ref.dtype)
        for s_ref, s in zip(state_refs, new_states):
            s_ref[...] = s

    out_specs = [_spec(a) for a in out_args]
    out_shape = [jax.ShapeDtypeStruct(s, a.dtype) for s, a in zip(out_shapes, out_args)]
    if save:
        for s in state_shapes:
            out_specs.append(pl.BlockSpec((None, None) + tuple(s), lambda g, c, k=len(s): (g, c) + (0,) * k))
            out_shape.append(jax.ShapeDtypeStruct((groups, nb) + tuple(s), F32))
    res = pl.pallas_call(
        body, name=name, grid=(groups, nb),
        in_specs=[_spec(a) for a in tok_args] + [_spec(a) for a in const_args],
        out_specs=out_specs, out_shape=out_shape,
        scratch_shapes=[pltpu.VMEM(tuple(s), F32) for s in state_shapes],
        compiler_params=_params(("arbitrary", "arbitrary")),
    )(*tok, *consts)
    return list(res[:n_out]), list(res[n_out:])


def blocked_backward(name, fn, groups, nb, tok_args, tok, const_args, consts, out_args, cts, state_shapes, saved):
    n_tok, n_const, n_out, n_state = len(tok), len(consts), len(out_args), len(state_shapes)
    tok_g = [i for i, a in enumerate(tok_args) if a.grad]
    const_g = [i for i, a in enumerate(const_args) if a.grad]

    def body(*refs):
        tok_refs = refs[:n_tok]
        const_refs = refs[n_tok:n_tok + n_const]
        pos = n_tok + n_const
        saved_refs = refs[pos:pos + n_state]
        ct_refs = refs[pos + n_state:pos + n_state + n_out]
        pos += n_state + n_out
        dtok_refs = refs[pos:pos + len(tok_g)]
        dconst_refs = refs[pos + len(tok_g):pos + len(tok_g) + len(const_g)]
        dstate_refs = refs[len(refs) - n_state:] if n_state else ()
        g_id, c_id = pl.program_id(0), pl.program_id(1)

        @pl.when(c_id == 0)
        def _():
            for s in dstate_refs:
                s[...] = jnp.zeros_like(s)
            for r, i in zip(dconst_refs, const_g):
                if not const_args[i].shared:
                    r[...] = jnp.zeros_like(r)

        @pl.when((c_id == 0) & (g_id == 0))
        def _():
            for r, i in zip(dconst_refs, const_g):
                if const_args[i].shared:
                    r[...] = jnp.zeros_like(r)

        tok_vals = [r[...].astype(F32) for r in tok_refs]
        const_vals = [r[...] for r in const_refs]

        def f(states, tok_d, const_d):
            tv, cv = list(tok_vals), list(const_vals)
            for i, v in zip(tok_g, tok_d):
                tv[i] = v
            for i, v in zip(const_g, const_d):
                cv[i] = v
            return fn(states, tv, cv)

        _, vjp = jax.vjp(f, [r[...] for r in saved_refs], [tok_vals[i] for i in tok_g], [const_vals[i] for i in const_g])
        dstates, dtok, dconst = vjp(([r[...] for r in dstate_refs], [r[...].astype(F32) for r in ct_refs]))
        for r, v in zip(dtok_refs, dtok):
            r[...] = v.astype(r.dtype)
        for r, v in zip(dconst_refs, dconst):
            r[...] += v
        for r, v in zip(dstate_refs, dstates):
            r[...] = v

    in_specs = [_spec(a, nb) for a in tok_args] + [_spec(a, nb) for a in const_args]
    for s in state_shapes:
        in_specs.append(pl.BlockSpec((None, None) + tuple(s), lambda g, c, k=len(s): (g, nb - 1 - c) + (0,) * k))
    in_specs += [_spec(a, nb) for a in out_args]
    out_specs = [_spec(tok_args[i], nb) for i in tok_g] + [_spec(const_args[i], nb) for i in const_g]
    out_shape = [jax.ShapeDtypeStruct(tok[i].shape, tok[i].dtype) for i in tok_g]
    out_shape += [jax.ShapeDtypeStruct(consts[i].shape, F32) for i in const_g]
    res = pl.pallas_call(
        body, name=name, grid=(groups, nb), in_specs=in_specs, out_specs=out_specs, out_shape=out_shape,
        scratch_shapes=[pltpu.VMEM(tuple(s), F32) for s in state_shapes],
        compiler_params=_params(("arbitrary", "arbitrary")),
    )(*tok, *consts, *saved, *cts)
    dtok = [None] * n_tok
    dconst = [None] * n_const
    for i, v in zip(tok_g, res[:len(tok_g)]):
        dtok[i] = v
    for i, v in zip(const_g, res[len(tok_g):]):
        dconst[i] = v
    return dtok, dconst


def blocked_op(name, fn, groups, nb, tok_args, const_args, out_args, out_shapes, state_shapes=()):
    state_shapes = tuple(state_shapes)

    @jax.custom_vjp
    def op(tok, consts):
        outs, _ = blocked_forward(name, fn, groups, nb, tok_args, tok, const_args, consts, out_args, out_shapes, state_shapes, False)
        return outs

    def fwd(tok, consts):
        outs, saved = blocked_forward(name, fn, groups, nb, tok_args, tok, const_args, consts, out_args, out_shapes, state_shapes, True)
        return outs, (tok, consts, saved)

    def bwd(res, cts):
        tok, consts, saved = res
        dtok, dconst = blocked_backward(name + "_bwd", fn, groups, nb, tok_args, tok, const_args, consts, out_args, list(cts), state_shapes, saved)
        dtok = [jnp.zeros_like(t) if d is None else d for t, d in zip(tok, dtok)]
        dconst = [jnp.zeros_like(k) if d is None else d for k, d in zip(consts, dconst)]
        return dtok, dconst

    op.defvjp(fwd, bwd)
    return op


def _make_split(sizes):
    offs = [sum(sizes[:i]) for i in range(len(sizes))]

    @jax.custom_vjp
    def split(x):
        return tuple(x[:, o:o + s] for o, s in zip(offs, sizes))

    split.defvjp(lambda x: (split(x), None), lambda _, g: (jnp.concatenate(list(g), axis=1),))
    return split


def _make_join(sizes):
    offs = [sum(sizes[:i]) for i in range(len(sizes))]

    @jax.custom_vjp
    def join(parts):
        return jnp.concatenate(list(parts), axis=1)

    join.defvjp(lambda parts: (join(parts), None), lambda _, g: (tuple(g[:, o:o + s] for o, s in zip(offs, sizes)),))
    return join


def split_cols(x, sizes):
    return _make_split(tuple(sizes))(x)


def join_cols(parts):
    return _make_join(tuple(p.shape[1] for p in parts))(tuple(parts))


def lane_scalar(row, j):
    lane = lax.broadcasted_iota(jnp.int32, row.shape, 1)
    return jnp.sum(jnp.where(lane == j, row, 0.0), axis=1, keepdims=True)


def lane_col(blk, j):
    lane = lax.broadcasted_iota(jnp.int32, blk.shape, 1)
    return jnp.sum(jnp.where(lane == j, blk, 0.0), axis=1, keepdims=True)


LANE = 128
MM_TILE_M, MM_TILE_N, MM_TILE_K = 512, 1024, 2816


def _tile(n, cap, unit):
    if n <= cap:
        return n
    best = None
    for t in range(unit, cap + 1, unit):
        if n % t == 0:
            best = t
    assert best is not None, (n, cap, unit)
    return best


def matmul(a, b, mode="nn", add=None, out_dtype=F32, name="matmul"):
    if mode == "nn":
        (m, k), n = a.shape, b.shape[1]
    elif mode == "nt":
        (m, k), n = a.shape, b.shape[0]
    else:
        (k, m), n = a.shape, b.shape[1]
    tm = _tile(m, MM_TILE_M, LANE if mode == "tn" else 8)
    tn = _tile(n, MM_TILE_N, LANE)
    tk = _tile(k, MM_TILE_K if mode != "tn" else 1024, LANE)
    nk = k // tk
    a_spec = pl.BlockSpec((tk, tm), lambda i, j, l: (l, i)) if mode == "tn" else pl.BlockSpec((tm, tk), lambda i, j, l: (i, l))
    b_spec = pl.BlockSpec((tn, tk), lambda i, j, l: (j, l)) if mode == "nt" else pl.BlockSpec((tk, tn), lambda i, j, l: (l, j))
    dims = {"nn": ((1,), (0,)), "nt": ((1,), (1,)), "tn": ((0,), (0,))}[mode]
    has_add = add is not None

    def body(*refs):
        a_ref, b_ref = refs[0], refs[1]
        add_ref = refs[2] if has_add else None
        o_ref, acc_ref = refs[-2], refs[-1]
        l = pl.program_id(2)

        @pl.when(l == 0)
        def _():
            acc_ref[...] = add_ref[...].astype(F32) if has_add else jnp.zeros_like(acc_ref)

        acc_ref[...] += _dot_bf16(a_ref[...], b_ref[...], dims)

        @pl.when(l == nk - 1)
        def _():
            o_ref[...] = acc_ref[...].astype(o_ref.dtype)

    o_spec = pl.BlockSpec((tm, tn), lambda i, j, l: (i, j))
    return pl.pallas_call(
        body, name=name, grid=(m // tm, n // tn, nk),
        in_specs=[a_spec, b_spec] + ([o_spec] if has_add else []),
        out_specs=o_spec, out_shape=jax.ShapeDtypeStruct((m, n), out_dtype),
        scratch_shapes=[pltpu.VMEM((tm, tn), F32)],
        compiler_params=_params(("parallel", "parallel", "arbitrary")),
    )(*([a, b] + ([add] if has_add else [])))


def linear(name, out_dtype=F32):
    @jax.custom_vjp
    def op(a, w, handle):
        return matmul(a, w, "nn", out_dtype=out_dtype, name=name)

    def fwd(a, w, handle):
        return op(a, w, handle), (a, w)

    def bwd(res, g):
        a, w = res
        da = matmul(g, w, "nt", out_dtype=a.dtype, name=name + "_da")
        dw = matmul(a, g, "tn", out_dtype=F32, name=name + "_dw")
        return da, jnp.zeros_like(w), dw

    op.defvjp(fwd, bwd)
    return op


def multi_linear(name, out_dtypes):
    @jax.custom_vjp
    def op(a, ws, handles):
        return [matmul(a, w, "nn", out_dtype=dt, name=f"{name}{i}") for i, (w, dt) in enumerate(zip(ws, out_dtypes))]

    def fwd(a, ws, handles):
        return op(a, ws, handles), (a, ws)

    def bwd(res, gs):
        a, ws = res
        acc = None
        for i, (g, w) in enumerate(zip(gs, ws)):
            last = i == len(ws) - 1
            acc = matmul(g, w, "nt", add=acc, out_dtype=a.dtype if last else F32, name=f"{name}{i}_da")
        dws = [matmul(a, g, "tn", out_dtype=F32, name=f"{name}{i}_dw") for i, g in enumerate(gs)]
        return acc, [jnp.zeros_like(w) for w in ws], dws

    op.defvjp(fwd, bwd)
    return op


def dense(name, out_dtype=F32):
    @jax.custom_vjp
    def op(a, w):
        return matmul(a, w, "nn", out_dtype=out_dtype, name=name)

    def fwd(a, w):
        return op(a, w), (a, w)

    def bwd(res, g):
        a, w = res
        return (matmul(g, w, "nt", out_dtype=a.dtype, name=name + "_da"),
                matmul(a, g, "tn", out_dtype=w.dtype, name=name + "_dw"))

    op.defvjp(fwd, bwd)
    return op


def to_col(row):
    n = row.shape[1]
    eye = lax.broadcasted_iota(jnp.int32, (n, n), 0) == lax.broadcasted_iota(jnp.int32, (n, n), 1)
    return jnp.sum(jnp.where(eye, row, 0.0), axis=1, keepdims=True)


def norm_fn(states, toks, consts):
    return [], [rmsnorm(toks[0], consts[0])]


def make_addnorm_fn(scale):
    def fn(states, toks, consts):
        h = toks[0] + scale * toks[1]
        return [], [h, rmsnorm(h, consts[0])]

    return fn


def swiglu_fn(states, toks, consts):
    gate, up = split_cols(toks[0], (FFN_DIM, FFN_DIM))
    return [], [jax.nn.silu(gate) * up]


def gate_merge_fn(states, toks, consts):
    gates = split_cols(jax.nn.sigmoid(toks[0] + consts[0]), (D_MODEL,) * N_BRANCH)
    mixed = gates[0] * toks[1]
    for n in range(1, N_BRANCH):
        mixed = mixed + gates[n] * toks[1 + n]
    return [], [mixed]


def ple_fn(states, toks, consts):
    h = toks[0] + jax.nn.sigmoid(toks[1]) * toks[2]
    return [], [h, rmsnorm(h, consts[0])]


def ple_loss_fn(states, toks, consts):
    h = toks[0] + jax.nn.sigmoid(toks[1]) * toks[2]
    err = rmsnorm(h, consts[0]) - toks[3]
    return [], [0.5 * jnp.mean(err * err, axis=-1, keepdims=True)]


def s5_discretise_fn(states, toks, consts):
    log_step, a_re, a_im, b_re, b_im = consts
    step = jnp.exp(log_step)
    mag = jnp.exp(a_re * step)
    ab_re, ab_im = mag * jnp.cos(a_im * step), mag * jnp.sin(a_im * step)
    den = a_re * a_re + a_im * a_im
    num_re = ab_re - 1.0
    f_re = (num_re * a_re + ab_im * a_im) / den
    f_im = (ab_im * a_re - num_re * a_im) / den
    return [], [ab_re, ab_im, f_re * b_re - f_im * b_im, f_re * b_im + f_im * b_re]


def s5_scan_fn(states, toks, consts):
    b_re, b_im = split_cols(toks[0], (S5_W, S5_W))
    h_re, h_im = complex_scan(b_re, b_im, consts[0], consts[1], states[0], states[1])
    return [last_row(h_re), last_row(h_im)], [join_cols([h_re, h_im])]


def s5_glu_fn(states, toks, consts):
    d_skip, w_glu, b_glu = consts
    z = jax.nn.gelu(toks[0] + d_skip * toks[1])
    return [], [z * jax.nn.sigmoid(mm(z, w_glu) + b_glu)]


def lru_fn(states, toks, consts):
    h0, tail = states
    x, gate = toks
    conv_w, conv_b, w_r, b_r, w_i, b_i, lam = consts
    xc = causal_conv4(x, tail, conv_w) + conv_b
    r = jax.nn.sigmoid(mm(xc, w_r) + b_r)
    i_g = jax.nn.sigmoid(mm(xc, w_i) + b_i)
    log_a = -LRU_C * r * jax.nn.softplus(-lam)
    inp = jnp.sqrt(1.0 - jnp.exp(2.0 * log_a)) * (i_g * xc)
    h = lin_scan(jnp.exp(log_a), inp, h0)
    return [last_row(h), tail_rows(x)], [h * jax.nn.gelu(gate)]


def m2_conv_fn(states, toks, consts):
    y = jax.nn.silu(causal_conv4(toks[0], states[0], consts[0]) + consts[1])
    return [tail_rows(toks[0])], list(split_cols(y, (BRANCH_WIDTH, M2_GROUPS * M2_STATE, M2_GROUPS * M2_STATE)))


def gdn_conv_fn(states, toks, consts):
    y = jax.nn.silu(causal_conv4(toks[0], states[0], consts[0]))
    return [tail_rows(toks[0])], list(split_cols(y, (BRANCH_WIDTH,) * 3))


def ssd_fn(states, toks, consts):
    xs, bm, cm, small = toks
    dt_bias, a_log, d_skip = consts
    x_pairs = split_cols(xs, (LANE,) * 4)
    b_g = split_cols(bm, (M2_STATE,) * M2_GROUPS)
    c_g = split_cols(cm, (M2_STATE,) * M2_GROUPS)
    lo = lax.broadcasted_iota(jnp.int32, (1, LANE), 1) < M2_HEAD_DIM
    new_states, y_pairs = [], []
    for g in range(M2_GROUPS):
        scores = mm_nt(c_g[g], b_g[g])
        y_off = split_cols(mm_nt(c_g[g], states[g]), (LANE, LANE))
        to_end, ends = [], []
        for j in range(2):
            pair = 2 * g + j
            x2 = x_pairs[pair]
            dts, css, decays, end = [], [], [], []
            for h in (2 * pair, 2 * pair + 1):
                dt = jax.nn.softplus(lane_col(small, h) + lane_scalar(dt_bias, h))
                a = dt * (-jnp.exp(lane_scalar(a_log, h)))
                cs, decay = causal_decay(a)
                dts.append(dt)
                css.append(cs)
                decays.append(decay)
                end.append(jnp.sum(a, axis=0, keepdims=True))
            xdt = x2 * jnp.where(lo, dts[0], dts[1])
            y = mm(scores * decays[0], jnp.where(lo, xdt, 0.0)) + mm(scores * decays[1], jnp.where(lo, 0.0, xdt))
            cs2 = jnp.where(lo, css[0], css[1])
            end2 = jnp.where(lo, end[0], end[1])
            y = y + y_off[j] * jnp.exp(cs2)
            y = y + jnp.where(lo, lane_scalar(d_skip, 2 * pair), lane_scalar(d_skip, 2 * pair + 1)) * x2
            y_pairs.append(y)
            to_end.append(xdt * jnp.exp(end2 - cs2))
            ends.append(end2)
        chunk_decay = jnp.exp(to_col(join_cols(ends)))
        new_states.append(states[g] * chunk_decay + mm_tn(join_cols(to_end), b_g[g]))
    return new_states, [join_cols(y_pairs)]


def m2_post_fn(states, toks, consts):
    return [], [rmsnorm(toks[0] * jax.nn.silu(toks[1]), consts[0])]


def gdn_head(state, q, k, v, gate, a_raw, b_raw, dt_bias, a_log, norm_g):
    n = q.shape[0]
    qn = q * lax.rsqrt(jnp.sum(q * q, axis=-1, keepdims=True) + EPS) * (GDN_HEAD_DIM ** -0.5)
    kn = k * lax.rsqrt(jnp.sum(k * k, axis=-1, keepdims=True) + EPS)
    beta = jax.nn.sigmoid(b_raw)
    g = -jnp.exp(a_log) * jax.nn.softplus(a_raw + dt_bias)
    cs, decay = causal_decay(g)
    row = lax.broadcasted_iota(jnp.int32, (n, n), 0)
    col = lax.broadcasted_iota(jnp.int32, (n, n), 1)
    kb = kn * beta
    m_neg = -jnp.where(row > col, mm_nt(kb, kn) * decay, 0.0)
    inv = jnp.where(row == col, 1.0, 0.0) + m_neg
    power = m_neg
    d = 2
    while d < n:
        power = mmh(power, power)
        inv = inv + mmh(inv, power)
        d *= 2
    ecs = jnp.exp(cs)
    u = mmh(inv, v * beta)
    w = mmh(inv, kb * ecs)
    qk = mm_nt(qn, kn) * decay
    cs_end = jnp.sum(g, axis=0, keepdims=True)
    v_new = u - mm(w, state)
    o = mm(qn * ecs, state) + mm(qk, v_new)
    new_state = state * jnp.exp(cs_end) + mm_tn(kn * jnp.exp(cs_end - cs), v_new)
    return new_state, rmsnorm(o, norm_g) * jax.nn.silu(gate)


def gdn_fn(states, toks, consts):
    q, k, v, gate, small = toks
    dt_bias, a_log, norm_g = consts
    heads = (GDN_HEAD_DIM,) * GDN_HEADS
    qs, ks, vs, gs = split_cols(q, heads), split_cols(k, heads), split_cols(v, heads), split_cols(gate, heads)
    new_states, ys = [], []
    for h in range(GDN_HEADS):
        st, y = gdn_head(states[h], qs[h], ks[h], vs[h], gs[h], lane_col(small, GDN_HEADS + h), lane_col(small, h),
                         lane_scalar(dt_bias, h), lane_scalar(a_log, h), norm_g)
        new_states.append(st)
        ys.append(y)
    return new_states, [join_cols(ys)]


def adamw_fn(states, toks, consts):
    w, g, m, v = toks
    m = ADAM_B1 * m + (1.0 - ADAM_B1) * g
    v = ADAM_B2 * v + (1.0 - ADAM_B2) * (g * g)
    m_hat = m / (1.0 - ADAM_B1 ** ADAM_STEP)
    v_hat = v / (1.0 - ADAM_B2 ** ADAM_STEP)
    return [], [-ADAM_LR * (m_hat / (jnp.sqrt(v_hat) + ADAM_EPS) + ADAM_WD * w), m, v]


def tok_op(name, fn, t, nblk, tok, consts, outs, states=()):
    nblk = min(nblk, t)
    tok_args = [rows(e[0], nblk, dtype=e[1], grad=e[2] if len(e) > 2 else True) for e in tok]
    const_args = [whole(s) for s in consts]
    out_args = [rows(w, nblk, dtype=dt) for (w, dt) in outs]
    return blocked_op(name, fn, 1, t // nblk, tok_args, const_args, out_args, [(t, w) for (w, _) in outs], states)


def const_op(name, fn, in_shapes, out_shapes):
    return blocked_op(name, fn, 1, 1, [], [whole(s) for s in in_shapes], [whole(s) for s in out_shapes], list(out_shapes))


W = BRANCH_WIDTH
SCAN_ROWS = 128
ROW_BLOCK = 256


def s5_mixer(tag, t, u, p):
    col = (S5_W, 1)
    disc = const_op("s5_disc" + tag, s5_discretise_fn, [col, col, col, (S5_W, 16), (S5_W, 16)], [col, col, (S5_W, 16), (S5_W, 16)])
    ab_re, ab_im, bb_re, bb_im = disc([], [
        jnp.repeat(p["s5_log_step"], S5_STATE).reshape(col), p["s5_a_re"].reshape(col), p["s5_a_im"].reshape(col),
        p["s5_b_re"].reshape(S5_W, S5_GROUP_CH), p["s5_b_im"].reshape(S5_W, S5_GROUP_CH)])
    eye = jnp.eye(S5_GROUPS, dtype=F32)

    def block_in(bb):
        return jnp.einsum("gpc,gh->gchp", bb.reshape(S5_GROUPS, S5_STATE, S5_GROUP_CH), eye).reshape(W, S5_W)

    def block_out(c):
        return jnp.einsum("gcp,gh->gphc", c, eye).reshape(S5_W, W)

    w_b = jnp.concatenate([block_in(bb_re), block_in(bb_im)], axis=1)
    w_c = jnp.concatenate([block_out(p["s5_c_re"]), -block_out(p["s5_c_im"])], axis=0)
    bu = dense("s5_b" + tag)(u, w_b)
    scan = tok_op("s5_scan" + tag, s5_scan_fn, t, SCAN_ROWS, [(2 * S5_W, F32)], [(1, S5_W)] * 2, [(2 * S5_W, F32)], states=[(1, S5_W)] * 2)
    h = scan([bu], [ab_re.reshape(1, S5_W), ab_im.reshape(1, S5_W)])[0]
    yc = dense("s5_c" + tag)(h, w_c)
    glu = tok_op("s5_glu" + tag, s5_glu_fn, t, ROW_BLOCK, [(W, F32), (W, F32)], [(1, W), (W, W), (1, W)], [(W, BF16)])
    return glu([yc, u], [p["s5_d"].reshape(1, W), p["s5_w_glu"], p["s5_b_glu"][None]])[0]


def lru_mixer(tag, t, x, gate, p):
    def block_diag(w):
        return jnp.einsum("hij,hk->hikj", w, jnp.eye(LRU_HEADS, dtype=F32)).reshape(W, W)

    op = tok_op("lru" + tag, lru_fn, t, SCAN_ROWS, [(W, F32), (W, F32)],
                [(4, W), (1, W), (W, W), (1, W), (W, W), (1, W), (1, W)], [(W, BF16)], states=[(1, W), (TAIL, W)])
    return op([x, gate], [p["lru_conv_w"], p["lru_conv_b"][None], block_diag(p["lru_w_r"]), p["lru_b_r"][None],
                          block_diag(p["lru_w_i"]), p["lru_b_i"][None], p["lru_lambda"][None]])[0]


def m2_mixer(tag, t, z, xbc, small, p):
    cw = 2 * W
    conv = tok_op("m2_conv" + tag, m2_conv_fn, t, ROW_BLOCK, [(cw, F32)], [(4, cw), (1, cw)],
                  [(W, F32), (W // 2, F32), (W // 2, F32)], states=[(TAIL, cw)])
    xs, bm, cm = conv([xbc], [p["m2_conv_w"], p["m2_conv_b"][None]])
    ssd = tok_op("ssd" + tag, ssd_fn, t, CHUNK, [(W, F32), (W // 2, F32), (W // 2, F32), (LANE, F32)],
                 [(1, M2_HEADS)] * 3, [(W, F32)], states=[(4 * M2_HEAD_DIM, M2_STATE)] * M2_GROUPS)
    y = ssd([xs, bm, cm, small], [p["m2_dt_bias"][None], p["m2_a_log"][None], p["m2_d"][None]])[0]
    post = tok_op("m2_post" + tag, m2_post_fn, t, ROW_BLOCK, [(W, F32), (W, F32)], [(1, W)], [(W, BF16)])
    return post([y, z], [p["m2_norm"][None]])[0]


def gdn_mixer(tag, t, qkv, gate, small, p):
    conv = tok_op("gdn_conv" + tag, gdn_conv_fn, t, ROW_BLOCK, [(3 * W, F32)], [(4, 3 * W)], [(W, F32)] * 3, states=[(TAIL, 3 * W)])
    q, k, v = conv([qkv], [p["gdn_conv_w"]])
    op = tok_op("gdn" + tag, gdn_fn, t, CHUNK, [(W, F32)] * 4 + [(LANE, F32)], [(1, GDN_HEADS), (1, GDN_HEADS), (1, GDN_HEAD_DIM)],
                [(W, BF16)], states=[(GDN_HEAD_DIM, GDN_HEAD_DIM)] * GDN_HEADS)
    return op([q, k, v, gate, small], [p["gdn_dt_bias"][None], p["gdn_a_log"][None], p["gdn_norm"][None]])[0]


WEIGHTS = ["ffn1_norm", "ffn1_w_in", "ffn1_w_out", "mix_norm", "w_in", "w_gate", "b_gate", "s5_log_step", "s5_a_re",
           "s5_a_im", "s5_b_re", "s5_b_im", "s5_c_re", "s5_c_im", "s5_d", "s5_w_glu", "s5_b_glu", "lru_conv_w",
           "lru_conv_b", "lru_w_r", "lru_b_r", "lru_w_i", "lru_b_i", "lru_lambda", "m2_conv_w", "m2_conv_b", "m2_dt_bias",
           "m2_a_log", "m2_d", "m2_norm", "gdn_conv_w", "gdn_dt_bias", "gdn_a_log", "gdn_norm", "w_branch", "w_out",
           "ffn2_norm", "ffn2_w_in", "ffn2_w_out", "ple_norm", "ple_w_gate", "ple_w_proj", "final_norm"]
N_CHIPS = 4
N_DEV = 8
IN_WIDTH = 5136
SHARDED = {
    "ffn1_w_in": ((D_MODEL, 2 * FFN_DIM // N_CHIPS), 1, False),
    "ffn1_w_out": ((FFN_DIM // N_CHIPS, D_MODEL), 0, False),
    "w_in": ((D_MODEL, IN_WIDTH // N_CHIPS), 1, False),
    "w_gate": ((D_MODEL, N_BRANCH * D_MODEL // N_CHIPS), 1, False),
    "s5_w_glu": ((W // N_CHIPS, W), 0, False),
    "lru_conv_w": ((4, W // N_CHIPS), 1, True),
    "m2_conv_w": ((4, 2 * W // N_CHIPS), 1, True),
    "gdn_conv_w": ((4, 3 * W // N_CHIPS), 1, True),
    "w_branch": ((N_BRANCH, W, D_MODEL // N_CHIPS), 2, False),
    "w_out": ((D_MODEL // N_CHIPS, D_MODEL), 0, False),
    "ffn2_w_in": ((D_MODEL, 2 * FFN_DIM // N_CHIPS), 1, False),
    "ffn2_w_out": ((FFN_DIM // N_CHIPS, D_MODEL), 0, False),
    "ple_w_gate": ((D_MODEL // N_CHIPS, D_MODEL), 0, False),
    "ple_w_proj": ((256, D_MODEL // N_CHIPS), 1, False),
}
SMALL = [n for n in WEIGHTS if n not in SHARDED]
ROW = 1024


def _count(shape):
    return math.prod(shape)


def _round_up(n, unit):
    return -(-n // unit) * unit


N_GATHER = sum(DEPTH * _count(s) * (2 if exact else 1) for s, _, exact in SHARDED.values())
N_GRAD = sum(DEPTH * _count(s) for s, _, _ in SHARDED.values())
GATHER_ROWS = _round_up(-(-N_GATHER // ROW), 32)
GRAD_ROWS = _round_up(-(-N_GRAD // ROW), 16)
GRAD_HALF = GRAD_ROWS // 2
IN_PIECES = [(0, 512), (512, 512), (1024, 512), (1536, 512), (2048, 1024), (3072, 8), (3080, 1536), (4616, 512), (5128, 8)]


def pack_for_gather(a):
    parts = []
    for n, (_, _, exact) in SHARDED.items():
        parts.append((lax.bitcast_convert_type(a[n], BF16) if exact else a[n].astype(BF16)).reshape(-1))
    flat = jnp.concatenate(parts)
    return jnp.pad(flat, (0, GATHER_ROWS * ROW - flat.shape[0])).reshape(GATHER_ROWS, ROW)


def unpack_gathered(buf):
    flat = buf.reshape(N_CHIPS, -1)
    out, off = {}, 0
    for n, (shape, ax, exact) in SHARDED.items():
        cnt = DEPTH * _count(shape) * (2 if exact else 1)
        piece = flat[:, off:off + cnt]
        off += cnt
        if exact:
            w = lax.bitcast_convert_type(piece.reshape(N_CHIPS, DEPTH, *shape, 2), F32)
        else:
            w = piece.reshape(N_CHIPS, DEPTH, *shape)
        full = list(shape)
        full[ax] *= N_CHIPS
        out[n] = jnp.moveaxis(w, 0, ax + 1).reshape(DEPTH, *full)
    return out


def pack_grads(g):
    parts = []
    for n, (shape, ax, _) in SHARDED.items():
        for layer in range(DEPTH):
            full = g[n][layer]
            w = full.reshape(*full.shape[:ax], N_CHIPS, shape[ax], *full.shape[ax + 1:])
            parts.append(jnp.moveaxis(w, ax, 0).reshape(N_CHIPS, -1))
    flat = jnp.concatenate(parts, axis=1)
    return jnp.pad(flat, ((0, 0), (0, GRAD_ROWS * ROW - flat.shape[1]))).reshape(N_CHIPS, GRAD_ROWS, ROW)


def unpack_shard(buf):
    flat = buf.reshape(-1)
    out, off = {}, 0
    for n, (shape, _, _) in SHARDED.items():
        cnt = DEPTH * _count(shape)
        out[n] = flat[off:off + cnt].reshape(DEPTH, *shape)
        off += cnt
    return out


def pack_small(a, prefix, shapes):
    flat = jnp.concatenate([a[prefix + n].reshape(-1) for n in SMALL])
    rows_n = _round_up(-(-flat.shape[0] // ROW), 8)
    return jnp.pad(flat, (0, rows_n * ROW - flat.shape[0])).reshape(rows_n, ROW)


def unpack_small(buf, shapes):
    flat = buf.reshape(-1)
    out, off = {}, 0
    for n in SMALL:
        cnt = _count(shapes[n])
        out[n] = flat[off:off + cnt].reshape(shapes[n])
        off += cnt
    return out


ANY = pl.BlockSpec(memory_space=pl.ANY)


def _position():
    return lax.axis_index("x"), lax.axis_index("y"), lax.axis_index("c")


def _other_chips(x, y):
    return [(1 - x, y), (x, 1 - y), (1 - x, 1 - y)]


def gather_weights(packed):
    r = packed.shape[0]
    half = r // 2

    def body(in_ref, out_ref, send_sems, recv_sems, local_sem):
        x, y, c = _position()
        sibling = (x, y, 1 - c)
        chips = _other_chips(x, y)

        def half_rows(px, py, pc):
            return out_ref.at[2 * px + py, pl.ds(pl.multiple_of(pc * half, 16), half), :]

        def copy(k, block, to, src=None):
            return pltpu.make_async_remote_copy(
                src_ref=half_rows(*block) if src is None else src, dst_ref=half_rows(*block),
                send_sem=send_sems.at[k], recv_sem=recv_sems.at[k], device_id=to, device_id_type=MESH)

        mine = pltpu.make_async_copy(in_ref, out_ref.at[2 * x + y], local_sem)
        mine.start()
        my_half = in_ref.at[pl.ds(pl.multiple_of(c * half, 16), half), :]
        first = [copy(j, (x, y, c), (*chip, c), src=my_half) for j, chip in enumerate(chips)]
        for cp in first:
            cp.start()
        passed = [copy(3 + j, (*chip, c), sibling) for j, chip in enumerate(chips)]
        for j, chip in enumerate(chips):
            copy(j, (*chip, c), (x, y, c)).wait_recv()
            passed[j].start()
        for j, chip in enumerate(chips):
            copy(3 + j, (*chip, 1 - c), (x, y, c)).wait_recv()
        for cp in first + passed:
            cp.wait_send()
        mine.wait()

    return pl.pallas_call(
        body, name="gather_weights", in_specs=[ANY], out_specs=ANY,
        out_shape=jax.ShapeDtypeStruct((N_CHIPS, r, ROW), packed.dtype),
        scratch_shapes=[pltpu.SemaphoreType.DMA((6,)), pltpu.SemaphoreType.DMA((6,)), pltpu.SemaphoreType.DMA],
    )(packed)


def swap_pair_halves(g):
    half = g.shape[1] // 2

    def body(g_ref, land_ref, send_sem, recv_sem):
        x, y, c = _position()
        src = g_ref.at[:, pl.ds(pl.multiple_of((1 - c) * half, 8), half), :]
        cp = pltpu.make_async_remote_copy(src_ref=src, dst_ref=land_ref, send_sem=send_sem, recv_sem=recv_sem,
                                          device_id=(x, y, 1 - c), device_id_type=MESH)
        cp.start()
        cp.wait()

    return pl.pallas_call(
        body, name="swap_pair_halves", in_specs=[ANY], out_specs=ANY,
        out_shape=jax.ShapeDtypeStruct((N_CHIPS, half, ROW), g.dtype),
        scratch_shapes=[pltpu.SemaphoreType.DMA, pltpu.SemaphoreType.DMA],
    )(g)


def exchange_chip_partials(part):
    half = part.shape[1]

    def body(p_ref, land_ref, send_sems, recv_sems):
        x, y, c = _position()
        cps = [pltpu.make_async_remote_copy(src_ref=p_ref.at[2 * px + py], dst_ref=land_ref.at[j], send_sem=send_sems.at[j],
                                            recv_sem=recv_sems.at[j], device_id=(px, py, c), device_id_type=MESH)
               for j, (px, py) in enumerate(_other_chips(x, y))]
        for cp in cps:
            cp.start()
        for cp in cps:
            cp.wait()

    return pl.pallas_call(
        body, name="exchange_chip_partials", in_specs=[ANY], out_specs=ANY,
        out_shape=jax.ShapeDtypeStruct((3, half, ROW), part.dtype),
        scratch_shapes=[pltpu.SemaphoreType.DMA((3,)), pltpu.SemaphoreType.DMA((3,))],
    )(part)


def share_halves(mine):
    half = mine.shape[0]

    def body(r_ref, out_ref, send_sem, recv_sem, local_sem):
        x, y, c = _position()
        my_rows = out_ref.at[pl.ds(pl.multiple_of(c * half, 8), half), :]
        local = pltpu.make_async_copy(r_ref, my_rows, local_sem)
        local.start()
        cp = pltpu.make_async_remote_copy(src_ref=r_ref, dst_ref=my_rows, send_sem=send_sem, recv_sem=recv_sem,
                                          device_id=(x, y, 1 - c), device_id_type=MESH)
        cp.start()
        cp.wait()
        local.wait()

    return pl.pallas_call(
        body, name="share_halves", in_specs=[ANY], out_specs=ANY,
        out_shape=jax.ShapeDtypeStruct((2 * half, ROW), mine.dtype),
        scratch_shapes=[pltpu.SemaphoreType.DMA, pltpu.SemaphoreType.DMA, pltpu.SemaphoreType.DMA],
    )(mine)


def gather_all(block):
    m_per = block.shape[0]

    def body(x_ref, out_ref, send_sems, recv_sems, local_sem):
        x, y, c = _position()
        me, sibling = (x, y, c), (x, y, 1 - c)
        chips = _other_chips(x, y)

        def rows_of(px, py, pc):
            return out_ref.at[pl.ds(pl.multiple_of((4 * px + 2 * py + pc) * m_per, 8), m_per), :]

        def copy(k, blk, to, src=None):
            return pltpu.make_async_remote_copy(
                src_ref=rows_of(*blk) if src is None else src, dst_ref=rows_of(*blk),
                send_sem=send_sems.at[k], recv_sem=recv_sems.at[k], device_id=to, device_id_type=MESH)

        mine = pltpu.make_async_copy(x_ref, rows_of(*me), local_sem)
        mine.start()
        first = [copy(0, me, sibling, src=x_ref)]
        first += [copy(1 + j, me, (*chip, c), src=x_ref) for j, chip in enumerate(chips)]
        for cp in first:
            cp.start()
        passed = [copy(4 + j, (*chip, c), sibling) for j, chip in enumerate(chips)]
        for j, chip in enumerate(chips):
            copy(1 + j, (*chip, c), me).wait_recv()
            passed[j].start()
        copy(0, sibling, me).wait_recv()
        for j, chip in enumerate(chips):
            copy(4 + j, (*chip, 1 - c), me).wait_recv()
        for cp in first + passed:
            cp.wait_send()
        mine.wait()

    return pl.pallas_call(
        body, name="gather_all", out_shape=jax.ShapeDtypeStruct((N_DEV * m_per, ROW), block.dtype),
        in_specs=[pl.BlockSpec(memory_space=pltpu.VMEM)], out_specs=pl.BlockSpec(memory_space=pltpu.VMEM),
        scratch_shapes=[pltpu.SemaphoreType.DMA((7,)), pltpu.SemaphoreType.DMA((7,)), pltpu.SemaphoreType.DMA],
        compiler_params=_params(),
    )(block)


SUM_ROWS = 248


def add_pair_halves(grads, landed, core):
    half = landed.shape[1]
    nblk = half // SUM_ROWS

    def body(c_ref, g_ref, l_ref, o_ref):
        o_ref[...] = g_ref[...] + l_ref[...]

    blk = (None, SUM_ROWS, ROW)
    return pl.pallas_call(
        body, name="add_pair_halves",
        grid_spec=pltpu.PrefetchScalarGridSpec(
            num_scalar_prefetch=1, grid=(N_CHIPS, nblk),
            in_specs=[pl.BlockSpec(blk, lambda s, i, c: (s, c[0] * nblk + i, 0)), pl.BlockSpec(blk, lambda s, i, c: (s, i, 0))],
            out_specs=pl.BlockSpec(blk, lambda s, i, c: (s, i, 0))),
        out_shape=jax.ShapeDtypeStruct(landed.shape, F32), compiler_params=_params(("arbitrary", "arbitrary")),
    )(core, grads, landed)


def add_chip_partials(part, landed, slot):
    half = part.shape[1]
    nblk = half // SUM_ROWS

    def body(s_ref, p_ref, l_ref, o_ref):
        o_ref[...] = ((p_ref[...] + l_ref[0]) + l_ref[1]) + l_ref[2]

    return pl.pallas_call(
        body, name="add_chip_partials",
        grid_spec=pltpu.PrefetchScalarGridSpec(
            num_scalar_prefetch=1, grid=(nblk,),
            in_specs=[pl.BlockSpec((None, SUM_ROWS, ROW), lambda i, s: (s[0], i, 0)), pl.BlockSpec((3, SUM_ROWS, ROW), lambda i, s: (0, i, 0))],
            out_specs=pl.BlockSpec((SUM_ROWS, ROW), lambda i, s: (i, 0))),
        out_shape=jax.ShapeDtypeStruct((half, ROW), F32), compiler_params=_params(("arbitrary",)),
    )(slot, part, landed)


def sum_devices(stacked):
    m = stacked.shape[1]

    def body(s_ref, o_ref):
        acc = s_ref[0]
        for d in range(1, N_DEV):
            acc = acc + s_ref[d]
        o_ref[...] = acc

    return pl.pallas_call(
        body, name="sum_devices", grid=(m // 8,), in_specs=[pl.BlockSpec((N_DEV, 8, ROW), lambda i: (0, i, 0))],
        out_specs=pl.BlockSpec((8, ROW), lambda i: (i, 0)), out_shape=jax.ShapeDtypeStruct((m, ROW), F32),
        compiler_params=_params(("arbitrary",)),
    )(stacked)


def adamw(name, w, g, m, v):
    width = w.shape[-1]
    n_rows = w.size // width
    nblk = _tile(n_rows, 256, 8)
    arg = rows(width, nblk)
    outs, _ = blocked_forward("adamw_" + name, adamw_fn, 1, n_rows // nblk, [arg] * 4, [t.reshape(n_rows, width) for t in (w, g, m, v)],
                              [], [], [arg] * 3, [(n_rows, width)] * 3, (), False)
    return [o.reshape(w.shape) for o in outs]


def split_w_in(w):
    out = []
    for start, n in IN_PIECES:
        piece = w[:, start:start + n]
        out.append(piece if n % LANE == 0 else jnp.pad(piece, ((0, 0), (0, LANE - n))))
    return out


def join_w_in(pieces):
    return jnp.concatenate([p[:, :n] for p, (_, n) in zip(pieces, IN_PIECES)], axis=1)


def trunk_loss(diff, p_emb, target, w16):
    x, small, gw, hd = diff["x"], diff["small"], diff["gw"], diff["hd"]
    t = x.shape[0]
    d = D_MODEL

    def norm_pair(name, fn, h, o, gain):
        op = tok_op(name, fn, t, 512, [(d, F32), (d, F32)], [(1, d)], [(d, F32), (d, BF16)])
        return op([h, o], [gain[None]])

    def ffn(tag, n, which, i):
        z = linear(f"{which}_in{tag}", BF16)(n, w16[which + "_w_in"][i], hd[which + "_w_in"][i])
        act = tok_op(f"{which}_act{tag}", swiglu_fn, t, 128, [(2 * FFN_DIM, BF16)], [], [(FFN_DIM, BF16)])([z], [])[0]
        return linear(f"{which}_out{tag}", F32)(act, w16[which + "_w_out"][i], hd[which + "_w_out"][i])

    h = x
    n = tok_op("norm_in", norm_fn, t, 512, [(d, F32)], [(1, d)], [(d, BF16)])([x], [small["ffn1_norm"][0][None]])[0]
    loss_rows = None
    for i in range(DEPTH):
        tag = str(i)
        p = {k: v[i] for k, v in small.items() if k != "final_norm"}
        p.update({k: v[i] for k, v in gw.items()})
        o = ffn(tag, n, "ffn1", i)
        h, u = norm_pair("mix_norm" + tag, make_addnorm_fn(0.5), h, o, p["mix_norm"])
        ws = split_w_in(w16["w_in"][i]) + [w16["w_gate"][i]]
        proj = multi_linear("in_proj" + tag, [F32] * len(IN_PIECES) + [BF16])(u, ws, hd["in_proj"][i])
        s5_u, lru_x, lru_g, m2_z, m2_xbc, m2_dt, gdn_qkv, gdn_g, gdn_ba, gate_logits = proj
        ys = [s5_mixer(tag, t, s5_u, p), lru_mixer(tag, t, lru_x, lru_g, p),
              m2_mixer(tag, t, m2_z, m2_xbc, m2_dt, p), gdn_mixer(tag, t, gdn_qkv, gdn_g, gdn_ba, p)]
        yb = [linear(f"branch{b}_{tag}", BF16)(y, w16["w_branch"][i][b], hd["w_branch"][i][b]) for b, y in enumerate(ys)]
        merge = tok_op("gate_merge" + tag, gate_merge_fn, t, 128, [(N_BRANCH * d, BF16)] + [(d, BF16)] * N_BRANCH,
                       [(1, N_BRANCH * d)], [(d, BF16)])
        mixed = merge([gate_logits] + yb, [p["b_gate"][None]])[0]
        o = linear("w_out" + tag, F32)(mixed, w16["w_out"][i], hd["w_out"][i])
        h, n = norm_pair("ffn2_norm" + tag, make_addnorm_fn(1.0), h, o, p["ffn2_norm"])
        o = ffn(tag, n, "ffn2", i)
        h, n = norm_pair("ple_norm" + tag, make_addnorm_fn(0.5), h, o, p["ple_norm"])
        pg = linear("ple_gate" + tag, F32)(n, w16["ple_w_gate"][i], hd["ple_w_gate"][i])
        pp = linear("ple_proj" + tag, F32)(p_emb[i], w16["ple_w_proj"][i], hd["ple_w_proj"][i])
        if i + 1 < DEPTH:
            op = tok_op("ple" + tag, ple_fn, t, 512, [(d, F32)] * 3, [(1, d)], [(d, F32), (d, BF16)])
            h, n = op([h, pg, pp], [small["ffn1_norm"][i + 1][None]])
        else:
            op = tok_op("ple_loss", ple_loss_fn, t, 512, [(d, F32)] * 3 + [(d, F32, False)], [(1, d)], [(1, F32)])
            loss_rows = op([h, pg, pp, target], [small["final_norm"][None]])[0]
    return jnp.sum(loss_rows)


def kernel(x, p, ffn1_norm, ffn1_w_in, ffn1_w_out, mix_norm, w_in, w_gate, b_gate, s5_log_step, s5_a_re, s5_a_im, s5_b_re, s5_b_im, s5_c_re, s5_c_im, s5_d, s5_w_glu, s5_b_glu, lru_conv_w, lru_conv_b, lru_w_r, lru_b_r, lru_w_i, lru_b_i, lru_lambda, m2_conv_w, m2_conv_b, m2_dt_bias, m2_a_log, m2_d, m2_norm, gdn_conv_w, gdn_dt_bias, gdn_a_log, gdn_norm, w_branch, w_out, ffn2_norm, ffn2_w_in, ffn2_w_out, ple_norm, ple_w_gate, ple_w_proj, final_norm, loss_target, m_ffn1_norm, m_ffn1_w_in, m_ffn1_w_out, m_mix_norm, m_w_in, m_w_gate, m_b_gate, m_s5_log_step, m_s5_a_re, m_s5_a_im, m_s5_b_re, m_s5_b_im, m_s5_c_re, m_s5_c_im, m_s5_d, m_s5_w_glu, m_s5_b_glu, m_lru_conv_w, m_lru_conv_b, m_lru_w_r, m_lru_b_r, m_lru_w_i, m_lru_b_i, m_lru_lambda, m_m2_conv_w, m_m2_conv_b, m_m2_dt_bias, m_m2_a_log, m_m2_d, m_m2_norm, m_gdn_conv_w, m_gdn_dt_bias, m_gdn_a_log, m_gdn_norm, m_w_branch, m_w_out, m_ffn2_norm, m_ffn2_w_in, m_ffn2_w_out, m_ple_norm, m_ple_w_gate, m_ple_w_proj, m_final_norm, v_ffn1_norm, v_ffn1_w_in, v_ffn1_w_out, v_mix_norm, v_w_in, v_w_gate, v_b_gate, v_s5_log_step, v_s5_a_re, v_s5_a_im, v_s5_b_re, v_s5_b_im, v_s5_c_re, v_s5_c_im, v_s5_d, v_s5_w_glu, v_s5_b_glu, v_lru_conv_w, v_lru_conv_b, v_lru_w_r, v_lru_b_r, v_lru_w_i, v_lru_b_i, v_lru_lambda, v_m2_conv_w, v_m2_conv_b, v_m2_dt_bias, v_m2_a_log, v_m2_d, v_m2_norm, v_gdn_conv_w, v_gdn_dt_bias, v_gdn_a_log, v_gdn_norm, v_w_branch, v_w_out, v_ffn2_norm, v_ffn2_w_in, v_ffn2_w_out, v_ple_norm, v_ple_w_gate, v_ple_w_proj, v_final_norm):
    a = dict(locals())
    t = x.shape[1]
    core = lax.axis_index("c").astype(jnp.int32).reshape(1)
    slot = (2 * lax.axis_index("x") + lax.axis_index("y")).astype(jnp.int32).reshape(1)

    full = unpack_gathered(gather_weights(pack_for_gather(a)))
    exact = [n for n, spec in SHARDED.items() if spec[2]] + ["s5_w_glu"]
    gw = {n: full[n].astype(F32) for n in exact}
    w16 = {n: full[n] for n in SHARDED if n not in exact}

    def zeros_like_layers(n):
        return [jnp.zeros(w16[n].shape[1:], F32) for _ in range(DEPTH)]

    hd = {n: zeros_like_layers(n) for n in w16 if n not in ("w_in", "w_gate", "w_branch")}
    hd["w_branch"] = [[jnp.zeros((W, D_MODEL), F32) for _ in range(N_BRANCH)] for _ in range(DEPTH)]
    hd["in_proj"] = [[jnp.zeros((D_MODEL, max(n, LANE)), F32) for _, n in IN_PIECES] + [jnp.zeros((D_MODEL, N_BRANCH * D_MODEL), F32)]
                     for _ in range(DEPTH)]
    diff = {"x": x.reshape(t, D_MODEL), "small": {n: a[n] for n in SMALL}, "gw": gw, "hd": hd}
    loss_local, vjp = jax.vjp(lambda dd: trunk_loss(dd, p.reshape(DEPTH, t, -1), loss_target.reshape(t, D_MODEL), w16), diff)
    (grads,) = vjp(jnp.ones((), F32))
    loss = lax.psum(loss_local, ("x", "y", "c"))
    grad_x = grads["x"].reshape(x.shape)

    gh = grads["hd"]
    big = {n: gh[n] for n in gh if n not in ("in_proj", "w_branch")}
    big["w_in"] = [join_w_in(gh["in_proj"][i][:-1]) for i in range(DEPTH)]
    big["w_gate"] = [gh["in_proj"][i][-1] for i in range(DEPTH)]
    big["w_branch"] = [jnp.stack(gh["w_branch"][i]) for i in range(DEPTH)]
    for n in exact:
        big[n] = [grads["gw"][n][i] for i in range(DEPTH)]

    packed = pack_grads(big)
    pair = add_pair_halves(packed, swap_pair_halves(packed), core)
    mine = add_chip_partials(pair, exchange_chip_partials(pair), slot)
    g_shard = unpack_shard(share_halves(mine))

    shapes = {n: a[n].shape for n in SMALL}
    gs_local = pack_small({n: grads["small"][n] for n in SMALL}, "", shapes)
    gs = gather_all(gs_local)
    g_small = sum_devices(gs.reshape(N_DEV, gs_local.shape[0], ROW))

    out_small = adamw("small", pack_small(a, "", shapes), g_small, pack_small(a, "m_", shapes), pack_small(a, "v_", shapes))
    g_small, d_small, m_small, v_small = [unpack_small(b, shapes) for b in [g_small] + out_small]
    res = {}
    for n in WEIGHTS:
        if n in SHARDED:
            res[n] = [g_shard[n]] + adamw(n, a[n], g_shard[n], a["m_" + n], a["v_" + n])
        else:
            res[n] = [g_small[n], d_small[n], m_small[n], v_small[n]]
    return (loss, grad_x, *[res[n][0] for n in WEIGHTS], *[res[n][1] for n in WEIGHTS],
            *[res[n][2] for n in WEIGHTS], *[res[n][3] for n in WEIGHTS])
```

```python
import functools
import math

import jax
import jax.numpy as jnp
from jax import lax
from jax.experimental import pallas as pl
from jax.experimental.pallas import tpu as pltpu

F32, BF16 = jnp.float32, jnp.bfloat16
EPS = 1e-6
D_MODEL = 1024
DEPTH = 2
FFN_DIM = 2816
BRANCH_WIDTH = 512
N_BRANCH = 4
LRU_C = 8.0
S5_GROUPS, S5_GROUP_CH, S5_STATE = 32, 16, 64
S5_W = S5_GROUPS * S5_STATE
LRU_HEADS, LRU_HEAD_DIM = 8, 64
M2_HEADS, M2_HEAD_DIM, M2_GROUPS, M2_STATE = 8, 64, 2, 128
GDN_HEADS, GDN_HEAD_DIM = 4, 128
CHUNK = 128
GDN_CHUNK = 64
ADAM_LR, ADAM_B1, ADAM_B2, ADAM_EPS, ADAM_WD, ADAM_STEP = 0.001, 0.9, 0.999, 1e-08, 0.01, 10
VMEM_LIMIT_BYTES = 56 * 1024 * 1024
MESH = pl.DeviceIdType.MESH


def _params(sem=None):
    return pltpu.CompilerParams(vmem_limit_bytes=VMEM_LIMIT_BYTES, dimension_semantics=sem)


def _dot_bf16(a, b, dims):
    return lax.dot_general(a.astype(BF16), b.astype(BF16), (dims, ((), ())), preferred_element_type=F32)


def _dot_f32(a, b, dims):
    return lax.dot_general(a, b, (dims, ((), ())), precision=lax.Precision.HIGHEST, preferred_element_type=F32)


def _make_mm(dot):
    @jax.custom_vjp
    def nn(a, b):
        return dot(a, b, ((1,), (0,)))

    @jax.custom_vjp
    def nt(a, b):
        return dot(a, b, ((1,), (1,)))

    @jax.custom_vjp
    def tn(a, b):
        return dot(a, b, ((0,), (0,)))

    nn.defvjp(lambda a, b: (nn(a, b), (a, b)), lambda r, g: (nt(g, r[1]), tn(r[0], g)))
    nt.defvjp(lambda a, b: (nt(a, b), (a, b)), lambda r, g: (nn(g, r[1]), tn(g, r[0])))
    tn.defvjp(lambda a, b: (tn(a, b), (a, b)), lambda r, g: (nt(r[1], g), nn(r[0], g)))
    return nn, nt, tn


mm, mm_nt, mm_tn = _make_mm(_dot_bf16)
mmh, mmh_nt, mmh_tn = _make_mm(_dot_f32)


def _row_ids(shape):
    return lax.broadcasted_iota(jnp.int32, shape, 0)


def _shift_down(x, d):
    return jnp.where(_row_ids(x.shape) >= d, pltpu.roll(x, d, 0), 0.0)


def _shift_up(x, d):
    n = x.shape[0]
    return jnp.where(_row_ids(x.shape) < n - d, pltpu.roll(x, n - d, 0), 0.0)


def _first_row(x):
    return jnp.sum(jnp.where(_row_ids(x.shape) == 0, x, 0.0), axis=0, keepdims=True)


def last_row(x):
    return jnp.sum(jnp.where(_row_ids(x.shape) == x.shape[0] - 1, x, 0.0), axis=0, keepdims=True)


def pick_row(x, j):
    return jnp.sum(jnp.where(_row_ids(x.shape) == j, x, 0.0), axis=0, keepdims=True)


@jax.custom_vjp
def lin_scan(a, b, h0):
    n = a.shape[0]
    row = _row_ids(a.shape)
    acc_a = a
    acc_b = b + jnp.where(row == 0, a * h0, 0.0)
    d = 1
    while d < n:
        acc_b = acc_a * _shift_down(acc_b, d) + acc_b
        acc_a = acc_a * jnp.where(row >= d, pltpu.roll(acc_a, d, 0), 1.0)
        d *= 2
    return acc_b


def _lin_scan_fwd(a, b, h0):
    h = lin_scan(a, b, h0)
    return h, (a, h, h0)


def _lin_scan_bwd(res, dh):
    a, h, h0 = res
    n = a.shape[0]
    row = _row_ids(a.shape)
    acc_a = _shift_up(a, 1)
    g = dh
    d = 1
    while d < n:
        g = acc_a * _shift_up(g, d) + g
        acc_a = acc_a * jnp.where(row < n - d, pltpu.roll(acc_a, n - d, 0), 1.0)
        d *= 2
    h_prev = _shift_down(h, 1) + jnp.where(row == 0, h0, 0.0)
    return g * h_prev, g, _first_row(a * g)


lin_scan.defvjp(_lin_scan_fwd, _lin_scan_bwd)


def _cscan(br, bi, ar, ai, up):
    n = br.shape[0]
    shift = _shift_up if up else _shift_down
    hr, hi, pr, pi = br, bi, ar, ai
    d = 1
    while d < n:
        sr, si = shift(hr, d), shift(hi, d)
        hr, hi = hr + pr * sr - pi * si, hi + pr * si + pi * sr
        pr, pi = pr * pr - pi * pi, 2.0 * pr * pi
        d *= 2
    return hr, hi


@jax.custom_vjp
def complex_scan(br, bi, ar, ai, h0r, h0i):
    first = _row_ids(br.shape) == 0
    br = br + jnp.where(first, ar * h0r - ai * h0i, 0.0)
    bi = bi + jnp.where(first, ar * h0i + ai * h0r, 0.0)
    return _cscan(br, bi, ar, ai, False)


def _complex_scan_fwd(br, bi, ar, ai, h0r, h0i):
    hr, hi = complex_scan(br, bi, ar, ai, h0r, h0i)
    return (hr, hi), (ar, ai, hr, hi, h0r, h0i)


def _complex_scan_bwd(res, cts):
    ar, ai, hr, hi, h0r, h0i = res
    gr, gi = _cscan(cts[0], cts[1], ar, -ai, True)
    first = _row_ids(hr.shape) == 0
    pr = _shift_down(hr, 1) + jnp.where(first, h0r, 0.0)
    pi = _shift_down(hi, 1) + jnp.where(first, h0i, 0.0)
    d_ar = jnp.sum(gr * pr + gi * pi, axis=0, keepdims=True)
    d_ai = jnp.sum(gi * pr - gr * pi, axis=0, keepdims=True)
    g0r, g0i = _first_row(gr), _first_row(gi)
    return gr, gi, d_ar, d_ai, ar * g0r + ai * g0i, ar * g0i - ai * g0r


complex_scan.defvjp(_complex_scan_fwd, _complex_scan_bwd)

TAIL = 8


@jax.custom_vjp
def tail_rows(x):
    return x[x.shape[0] - TAIL:, :]


tail_rows.defvjp(
    lambda x: (tail_rows(x), x.shape[0]),
    lambda n, g: (jnp.concatenate([jnp.zeros((n - TAIL, g.shape[1]), g.dtype), g], axis=0),),
)


def _make_shift_tail(d):
    @jax.custom_vjp
    def shifted(x, tail):
        n = x.shape[0]
        tpad = jnp.concatenate([tail, jnp.zeros((n - TAIL, x.shape[1]), x.dtype)], axis=0)
        return jnp.where(_row_ids(x.shape) >= d, pltpu.roll(x, d, 0), pltpu.roll(tpad, n + d - TAIL, 0))

    def fwd(x, tail):
        return shifted(x, tail), None

    def bwd(_, g):
        g8 = g[:TAIL, :]
        dtail = jnp.where(_row_ids(g8.shape) >= TAIL - d, pltpu.roll(g8, TAIL - d, 0), 0.0)
        return _shift_up(g, d), dtail

    shifted.defvjp(fwd, bwd)
    return shifted


_SHIFT_TAIL = {d: _make_shift_tail(d) for d in (1, 2, 3)}


def causal_conv4(x, tail, w):
    y = pick_row(w, 3) * x
    for j in range(3):
        y = y + pick_row(w, j) * _SHIFT_TAIL[3 - j](x, tail)
    return y


def rmsnorm(x, g):
    return x * lax.rsqrt(jnp.mean(x * x, axis=-1, keepdims=True) + EPS) * g


def to_row(col):
    n = col.shape[0]
    eye = lax.broadcasted_iota(jnp.int32, (n, n), 0) == lax.broadcasted_iota(jnp.int32, (n, n), 1)
    return jnp.sum(jnp.where(eye, col, 0.0), axis=0, keepdims=True)


def causal_decay(a_col):
    n = a_col.shape[0]
    causal = lax.broadcasted_iota(jnp.int32, (n, n), 0) >= lax.broadcasted_iota(jnp.int32, (n, n), 1)
    cs = jnp.sum(jnp.where(causal, to_row(a_col), 0.0), axis=1, keepdims=True)
    diff = cs - to_row(cs)
    return cs, jnp.where(causal, jnp.exp(jnp.where(causal, diff, 0.0)), 0.0)


class Arg:
    def __init__(self, block, imap, dtype=F32, grad=True, shared=True):
        self.block, self.imap, self.dtype, self.grad, self.shared = block, imap, dtype, grad, shared


def rows(width, nblk, col=lambda g: 0, dtype=F32, grad=True):
    return Arg((nblk, width), lambda g, c: (c, col(g)), dtype, grad)


def head_rows(width, nblk, head=lambda g: g, dtype=F32, grad=True):
    return Arg((None, nblk, width), lambda g, c: (head(g), c, 0), dtype, grad)


def whole(shape, grad=True):
    return Arg(tuple(shape), lambda g, c: (0,) * len(shape), F32, grad, shared=True)


def per_group(shape, idx=lambda g: g, grad=True):
    return Arg((None,) + tuple(shape), lambda g, c: (idx(g),) + (0,) * len(shape), F32, grad, shared=False)


def _bshape(block):
    return tuple(b for b in block if b is not None)


def _spec(arg, nb=None):
    if nb is None:
        return pl.BlockSpec(arg.block, arg.imap)
    return pl.BlockSpec(arg.block, lambda g, c: arg.imap(g, nb - 1 - c))


def blocked_forward(name, fn, groups, nb, tok_args, tok, const_args, consts, out_args, out_shapes, state_shapes, save):
    n_tok, n_const, n_out, n_state = len(tok), len(consts), len(out_args), len(state_shapes)

    def body(*refs):
        tok_refs = refs[:n_tok]
        const_refs = refs[n_tok:n_tok + n_const]
        out_refs = refs[n_tok + n_const:n_tok + n_const + n_out]
        pos = n_tok + n_const + n_out
        save_refs = refs[pos:pos + (n_state if save else 0)]
        state_refs = refs[len(refs) - n_state:] if n_state else ()

        @pl.when(pl.program_id(1) == 0)
        def _():
            for s in state_refs:
                s[...] = jnp.zeros_like(s)

        states = [s[...] for s in state_refs]
        for sr, s in zip(save_refs, states):
            sr[...] = s
        new_states, outs = fn(states, [r[...].astype(F32) for r in tok_refs], [r[...] for r in const_refs])
        for o_ref, o in zip(out_refs, outs):
            o_ref[...] = o.astype(o_ref.dtype)
        for s_ref, s in zip(state_refs, new_states):
            s_ref[...] = s

    out_specs = [_spec(a) for a in out_args]
    out_shape = [jax.ShapeDtypeStruct(s, a.dtype) for s, a in zip(out_shapes, out_args)]
    if save:
        for s in state_shapes:
            out_specs.append(pl.BlockSpec((None, None) + tuple(s), lambda g, c, k=len(s): (g, c) + (0,) * k))
            out_shape.append(jax.ShapeDtypeStruct((groups, nb) + tuple(s), F32))
    res = pl.pallas_call(
        body, name=name, grid=(groups, nb),
        in_specs=[_spec(a) for a in tok_args] + [_spec(a) for a in const_args],
        out_specs=out_specs, out_shape=out_shape,
        scratch_shapes=[pltpu.VMEM(tuple(s), F32) for s in state_shapes],
        compiler_params=_params(("arbitrary", "arbitrary")),
    )(*tok, *consts)
    return list(res[:n_out]), list(res[n_out:])


def blocked_backward(name, fn, groups, nb, tok_args, tok, const_args, consts, out_args, cts, state_shapes, saved):
    n_tok, n_const, n_out, n_state = len(tok), len(consts), len(out_args), len(state_shapes)
    tok_g = [i for i, a in enumerate(tok_args) if a.grad]
    const_g = [i for i, a in enumerate(const_args) if a.grad]

    def body(*refs):
        tok_refs = refs[:n_tok]
        const_refs = refs[n_tok:n_tok + n_const]
        pos = n_tok + n_const
        saved_refs = refs[pos:pos + n_state]
        ct_refs = refs[pos + n_state:pos + n_state + n_out]
        pos += n_state + n_out
        dtok_refs = refs[pos:pos + len(tok_g)]
        dconst_refs = refs[pos + len(tok_g):pos + len(tok_g) + len(const_g)]
        dstate_refs = refs[len(refs) - n_state:] if n_state else ()
        g_id, c_id = pl.program_id(0), pl.program_id(1)

        @pl.when(c_id == 0)
        def _():
            for s in dstate_refs:
                s[...] = jnp.zeros_like(s)
            for r, i in zip(dconst_refs, const_g):
                if not const_args[i].shared:
                    r[...] = jnp.zeros_like(r)

        @pl.when((c_id == 0) & (g_id == 0))
        def _():
            for r, i in zip(dconst_refs, const_g):
                if const_args[i].shared:
                    r[...] = jnp.zeros_like(r)

        tok_vals = [r[...].astype(F32) for r in tok_refs]
        const_vals = [r[...] for r in const_refs]

        def f(states, tok_d, const_d):
            tv, cv = list(tok_vals), list(const_vals)
            for i, v in zip(tok_g, tok_d):
                tv[i] = v
            for i, v in zip(const_g, const_d):
                cv[i] = v
            return fn(states, tv, cv)

        _, vjp = jax.vjp(f, [r[...] for r in saved_refs], [tok_vals[i] for i in tok_g], [const_vals[i] for i in const_g])
        dstates, dtok, dconst = vjp(([r[...] for r in dstate_refs], [r[...].astype(F32) for r in ct_refs]))
        for r, v in zip(dtok_refs, dtok):
            r[...] = v.astype(r.dtype)
        for r, v in zip(dconst_refs, dconst):
            r[...] += v
        for r, v in zip(dstate_refs, dstates):
            r[...] = v

    in_specs = [_spec(a, nb) for a in tok_args] + [_spec(a, nb) for a in const_args]
    for s in state_shapes:
        in_specs.append(pl.BlockSpec((None, None) + tuple(s), lambda g, c, k=len(s): (g, nb - 1 - c) + (0,) * k))
    in_specs += [_spec(a, nb) for a in out_args]
    out_specs = [_spec(tok_args[i], nb) for i in tok_g] + [_spec(const_args[i], nb) for i in const_g]
    out_shape = [jax.ShapeDtypeStruct(tok[i].shape, tok[i].dtype) for i in tok_g]
    out_shape += [jax.ShapeDtypeStruct(consts[i].shape, F32) for i in const_g]
    res = pl.pallas_call(
        body, name=name, grid=(groups, nb), in_specs=in_specs, out_specs=out_specs, out_shape=out_shape,
        scratch_shapes=[pltpu.VMEM(tuple(s), F32) for s in state_shapes],
        compiler_params=_params(("arbitrary", "arbitrary")),
    )(*tok, *consts, *saved, *cts)
    dtok = [None] * n_tok
    dconst = [None] * n_const
    for i, v in zip(tok_g, res[:len(tok_g)]):
        dtok[i] = v
    for i, v in zip(const_g, res[len(tok_g):]):
        dconst[i] = v
    return dtok, dconst


def blocked_op(name, fn, groups, nb, tok_args, const_args, out_args, out_shapes, state_shapes=()):
    state_shapes = tuple(state_shapes)

    @jax.custom_vjp
    def op(tok, consts):
        outs, _ = blocked_forward(name, fn, groups, nb, tok_args, tok, const_args, consts, out_args, out_shapes, state_shapes, False)
        return outs

    def fwd(tok, consts):
        outs, saved = blocked_forward(name, fn, groups, nb, tok_args, tok, const_args, consts, out_args, out_shapes, state_shapes, True)
        return outs, (tok, consts, saved)

    def bwd(res, cts):
        tok, consts, saved = res
        dtok, dconst = blocked_backward(name + "_bwd", fn, groups, nb, tok_args, tok, const_args, consts, out_args, list(cts), state_shapes, saved)
        dtok = [jnp.zeros_like(t) if d is None else d for t, d in zip(tok, dtok)]
        dconst = [jnp.zeros_like(k) if d is None else d for k, d in zip(consts, dconst)]
        return dtok, dconst

    op.defvjp(fwd, bwd)
    return op


def _make_split(sizes):
    offs = [sum(sizes[:i]) for i in range(len(sizes))]

    @jax.custom_vjp
    def split(x):
        return tuple(x[:, o:o + s] for o, s in zip(offs, sizes))

    split.defvjp(lambda x: (split(x), None), lambda _, g: (jnp.concatenate(list(g), axis=1),))
    return split


def _make_join(sizes):
    offs = [sum(sizes[:i]) for i in range(len(sizes))]

    @jax.custom_vjp
    def join(parts):
        return jnp.concatenate(list(parts), axis=1)

    join.defvjp(lambda parts: (join(parts), None), lambda _, g: (tuple(g[:, o:o + s] for o, s in zip(offs, sizes)),))
    return join


def split_cols(x, sizes):
    return _make_split(tuple(sizes))(x)


def join_cols(parts):
    return _make_join(tuple(p.shape[1] for p in parts))(tuple(parts))


def lane_scalar(row, j):
    lane = lax.broadcasted_iota(jnp.int32, row.shape, 1)
    return jnp.sum(jnp.where(lane == j, row, 0.0), axis=1, keepdims=True)


def lane_col(blk, j):
    lane = lax.broadcasted_iota(jnp.int32, blk.shape, 1)
    return jnp.sum(jnp.where(lane == j, blk, 0.0), axis=1, keepdims=True)


LANE = 128
MM_TILE_M, MM_TILE_N, MM_TILE_K = 1024, 1536, 2816
MM_TILE_MT = 1408


def _tile(n, cap, unit):
    if n <= cap:
        return n
    best = None
    for t in range(unit, cap + 1, unit):
        if n % t == 0:
            best = t
    assert best is not None, (n, cap, unit)
    return best


def matmul(a, b, mode="nn", add=None, out_dtype=F32, name="matmul", pre=None, col_shards=0):
    lead = () if pre is None else tuple(pre)
    rows_b, cols_b = b.shape[-2:]
    if mode != "tn" and col_shards:
        cols_b *= col_shards
    if mode == "nn":
        (m, k), n = a.shape, cols_b
    elif mode == "nt":
        (m, k), n = a.shape, rows_b
    else:
        (k, m), n = a.shape, cols_b
    shard_n = n // col_shards if col_shards and mode != "nt" else n
    shard_k = k // col_shards if col_shards and mode == "nt" else k
    tm = _tile(m, MM_TILE_MT, LANE) if mode == "tn" else _tile(m, MM_TILE_M, 8)
    tn = _tile(shard_n, MM_TILE_N, LANE)
    tk = _tile(shard_k, MM_TILE_K if mode != "tn" else MM_TILE_M, LANE)
    nk = k // tk
    qn, qk = shard_n // tn, shard_k // tk
    nolead = (None,) * len(lead)
    a_spec = pl.BlockSpec((tk, tm), lambda i, j, l: (l, i)) if mode == "tn" else pl.BlockSpec((tm, tk), lambda i, j, l: (i, l))
    if mode == "tn":
        b_spec = pl.BlockSpec((tk, tn), lambda i, j, l: (l, j))
    elif mode == "nn" and col_shards:
        b_spec = pl.BlockSpec((None,) + nolead + (tk, tn), lambda i, j, l: (j // qn,) + lead + (l, j % qn))
    elif mode == "nn":
        b_spec = pl.BlockSpec(nolead + (tk, tn), lambda i, j, l: lead + (l, j))
    elif col_shards:
        b_spec = pl.BlockSpec((None,) + nolead + (tn, tk), lambda i, j, l: (l // qk,) + lead + (j, l % qk))
    else:
        b_spec = pl.BlockSpec(nolead + (tn, tk), lambda i, j, l: lead + (j, l))
    dims = {"nn": ((1,), (0,)), "nt": ((1,), (1,)), "tn": ((0,), (0,))}[mode]
    has_add = add is not None

    def body_single(*refs):
        acc = _dot_bf16(refs[0][...], refs[1][...], dims)
        if has_add:
            acc = acc + refs[2][...].astype(F32)
        refs[-1][...] = acc.astype(refs[-1].dtype)

    def body(*refs):
        a_ref, b_ref = refs[0], refs[1]
        add_ref = refs[2] if has_add else None
        o_ref, acc_ref = refs[-2], refs[-1]
        l = pl.program_id(2)

        @pl.when(l == 0)
        def _():
            acc_ref[...] = add_ref[...].astype(F32) if has_add else jnp.zeros_like(acc_ref)

        acc_ref[...] += _dot_bf16(a_ref[...], b_ref[...], dims)

        @pl.when(l == nk - 1)
        def _():
            o_ref[...] = acc_ref[...].astype(o_ref.dtype)

    if mode == "tn" and col_shards:
        assert not has_add
        o_spec = pl.BlockSpec((None, tm, tn), lambda i, j, l: (j // qn, i, j % qn))
        out_shape = jax.ShapeDtypeStruct((col_shards, m, shard_n), out_dtype)
    else:
        o_spec = pl.BlockSpec((tm, tn), lambda i, j, l: (i, j))
        out_shape = jax.ShapeDtypeStruct((m, n), out_dtype)
    return pl.pallas_call(
        body_single if nk == 1 else body, name=name, grid=(m // tm, n // tn, nk),
        in_specs=[a_spec, b_spec] + ([o_spec] if has_add else []), out_specs=o_spec, out_shape=out_shape,
        scratch_shapes=[] if nk == 1 else [pltpu.VMEM((tm, tn), F32)],
        compiler_params=_params(("parallel", "parallel", "arbitrary")),
    )(*([a, b] + ([add] if has_add else [])))


def linear(name, out_dtype=F32, pre=None, col_shards=0):
    @jax.custom_vjp
    def op(a, w, handle):
        return matmul(a, w, "nn", out_dtype=out_dtype, name=name, pre=pre, col_shards=col_shards)

    def fwd(a, w, handle):
        return op(a, w, handle), (a, w)

    def bwd(res, g):
        a, w = res
        da = matmul(g, w, "nt", out_dtype=a.dtype, name=name + "_da", pre=pre, col_shards=col_shards)
        dw = matmul(a, g, "tn", out_dtype=F32, name=name + "_dw", col_shards=col_shards)
        return da, jnp.zeros_like(w), dw

    op.defvjp(fwd, bwd)
    return op


def multi_linear(name, out_dtypes, pres, shards):
    sel = [dict(pre=p, col_shards=s) for p, s in zip(pres, shards)]

    @jax.custom_vjp
    def op(a, ws, handles):
        return [matmul(a, w, "nn", out_dtype=dt, name=f"{name}{i}", **sel[i]) for i, (w, dt) in enumerate(zip(ws, out_dtypes))]

    def fwd(a, ws, handles):
        return op(a, ws, handles), (a, ws)

    def bwd(res, gs):
        a, ws = res
        acc = None
        for i, (g, w) in enumerate(zip(gs, ws)):
            last = i == len(ws) - 1
            acc = matmul(g, w, "nt", add=acc, out_dtype=a.dtype if last else F32, name=f"{name}{i}_da", **sel[i])
        dws = [matmul(a, g, "tn", out_dtype=F32, name=f"{name}{i}_dw", col_shards=shards[i]) for i, g in enumerate(gs)]
        return acc, [jnp.zeros_like(w) for w in ws], dws

    op.defvjp(fwd, bwd)
    return op


def dense(name, out_dtype=F32):
    @jax.custom_vjp
    def op(a, w):
        return matmul(a, w, "nn", out_dtype=out_dtype, name=name)

    def fwd(a, w):
        return op(a, w), (a, w)

    def bwd(res, g):
        a, w = res
        return (matmul(g, w, "nt", out_dtype=a.dtype, name=name + "_da"),
                matmul(a, g, "tn", out_dtype=w.dtype, name=name + "_dw"))

    op.defvjp(fwd, bwd)
    return op


def to_col(row):
    n = row.shape[1]
    eye = lax.broadcasted_iota(jnp.int32, (n, n), 0) == lax.broadcasted_iota(jnp.int32, (n, n), 1)
    return jnp.sum(jnp.where(eye, row, 0.0), axis=1, keepdims=True)


def norm_fn(states, toks, consts):
    return [], [rmsnorm(toks[0], consts[0])]


def make_addnorm_fn(scale):
    def fn(states, toks, consts):
        h = toks[0] + scale * toks[1]
        return [], [h, rmsnorm(h, consts[0])]

    return fn


def swiglu_fn(states, toks, consts):
    gate, up = split_cols(toks[0], (FFN_DIM, FFN_DIM))
    return [], [jax.nn.silu(gate) * up]


def gate_merge_fn(states, toks, consts):
    gates = split_cols(jax.nn.sigmoid(toks[0] + consts[0]), (D_MODEL,) * N_BRANCH)
    mixed = gates[0] * toks[1]
    for n in range(1, N_BRANCH):
        mixed = mixed + gates[n] * toks[1 + n]
    return [], [mixed]


def ple_fn(states, toks, consts):
    h = toks[0] + jax.nn.sigmoid(toks[1]) * toks[2]
    return [], [h, rmsnorm(h, consts[0])]


def ple_loss_fn(states, toks, consts):
    h = toks[0] + jax.nn.sigmoid(toks[1]) * toks[2]
    err = rmsnorm(h, consts[0]) - toks[3]
    return [], [0.5 * jnp.mean(err * err, axis=-1, keepdims=True)]


def s5_discretise_fn(states, toks, consts):
    log_step, a_re, a_im, b_re, b_im = consts
    step = jnp.exp(log_step)
    mag = jnp.exp(a_re * step)
    ab_re, ab_im = mag * jnp.cos(a_im * step), mag * jnp.sin(a_im * step)
    den = a_re * a_re + a_im * a_im
    num_re = ab_re - 1.0
    f_re = (num_re * a_re + ab_im * a_im) / den
    f_im = (ab_im * a_re - num_re * a_im) / den
    return [], [ab_re, ab_im, f_re * b_re - f_im * b_im, f_re * b_im + f_im * b_re]


def s5_scan_fn(states, toks, consts):
    b_re, b_im = split_cols(toks[0], (S5_W, S5_W))
    h_re, h_im = complex_scan(b_re, b_im, consts[0], consts[1], states[0], states[1])
    return [last_row(h_re), last_row(h_im)], [join_cols([h_re, h_im])]


def s5_glu_fn(states, toks, consts):
    d_skip, w_glu, b_glu = consts
    z = jax.nn.gelu(toks[0] + d_skip * toks[1])
    return [], [z * jax.nn.sigmoid(mm(z, w_glu) + b_glu)]


def lru_fn(states, toks, consts):
    h0, tail = states
    x, gate = toks
    conv_w, conv_b, w_r, b_r, w_i, b_i, lam = consts
    xc = causal_conv4(x, tail, conv_w) + conv_b
    r = jax.nn.sigmoid(mm(xc, w_r) + b_r)
    i_g = jax.nn.sigmoid(mm(xc, w_i) + b_i)
    log_a = -LRU_C * r * jax.nn.softplus(-lam)
    inp = jnp.sqrt(1.0 - jnp.exp(2.0 * log_a)) * (i_g * xc)
    h = lin_scan(jnp.exp(log_a), inp, h0)
    return [last_row(h), tail_rows(x)], [h * jax.nn.gelu(gate)]


def m2_conv_fn(states, toks, consts):
    y = jax.nn.silu(causal_conv4(toks[0], states[0], consts[0]) + consts[1])
    return [tail_rows(toks[0])], list(split_cols(y, (BRANCH_WIDTH, M2_GROUPS * M2_STATE, M2_GROUPS * M2_STATE)))


def gdn_conv_fn(states, toks, consts):
    y = jax.nn.silu(causal_conv4(toks[0], states[0], consts[0]))
    return [tail_rows(toks[0])], list(split_cols(y, (BRANCH_WIDTH,) * 3))


def ssd_fn(states, toks, consts):
    xs, bm, cm, small = toks
    dt_bias, a_log, d_skip = consts
    x_pairs = split_cols(xs, (LANE,) * 4)
    b_g = split_cols(bm, (M2_STATE,) * M2_GROUPS)
    c_g = split_cols(cm, (M2_STATE,) * M2_GROUPS)
    lo = lax.broadcasted_iota(jnp.int32, (1, LANE), 1) < M2_HEAD_DIM
    new_states, y_pairs = [], []
    for g in range(M2_GROUPS):
        scores = mm_nt(c_g[g], b_g[g])
        y_off = split_cols(mm_nt(c_g[g], states[g]), (LANE, LANE))
        to_end, ends = [], []
        for j in range(2):
            pair = 2 * g + j
            x2 = x_pairs[pair]
            dts, css, decays, end = [], [], [], []
            for h in (2 * pair, 2 * pair + 1):
                dt = jax.nn.softplus(lane_col(small, h) + lane_scalar(dt_bias, h))
                a = dt * (-jnp.exp(lane_scalar(a_log, h)))
                cs, decay = causal_decay(a)
                dts.append(dt)
                css.append(cs)
                decays.append(decay)
                end.append(jnp.sum(a, axis=0, keepdims=True))
            xdt = x2 * jnp.where(lo, dts[0], dts[1])
            y = mm(scores * decays[0], jnp.where(lo, xdt, 0.0)) + mm(scores * decays[1], jnp.where(lo, 0.0, xdt))
            cs2 = jnp.where(lo, css[0], css[1])
            end2 = jnp.where(lo, end[0], end[1])
            y = y + y_off[j] * jnp.exp(cs2)
            y = y + jnp.where(lo, lane_scalar(d_skip, 2 * pair), lane_scalar(d_skip, 2 * pair + 1)) * x2
            y_pairs.append(y)
            to_end.append(xdt * jnp.exp(end2 - cs2))
            ends.append(end2)
        chunk_decay = jnp.exp(to_col(join_cols(ends)))
        new_states.append(states[g] * chunk_decay + mm_tn(join_cols(to_end), b_g[g]))
    return new_states, [join_cols(y_pairs)]


def m2_post_fn(states, toks, consts):
    return [], [rmsnorm(toks[0] * jax.nn.silu(toks[1]), consts[0])]


def gdn_head(state, q, k, v, gate, a_raw, b_raw, dt_bias, a_log, norm_g):
    n = q.shape[0]
    qn = q * lax.rsqrt(jnp.sum(q * q, axis=-1, keepdims=True) + EPS) * (GDN_HEAD_DIM ** -0.5)
    kn = k * lax.rsqrt(jnp.sum(k * k, axis=-1, keepdims=True) + EPS)
    beta = jax.nn.sigmoid(b_raw)
    g = -jnp.exp(a_log) * jax.nn.softplus(a_raw + dt_bias)
    cs, decay = causal_decay(g)
    row = lax.broadcasted_iota(jnp.int32, (n, n), 0)
    col = lax.broadcasted_iota(jnp.int32, (n, n), 1)
    kb = kn * beta
    m_neg = -jnp.where(row > col, mm_nt(kb, kn) * decay, 0.0)
    inv = jnp.where(row == col, 1.0, 0.0) + m_neg
    power = m_neg
    d = 2
    while d < n:
        power = mmh(power, power)
        inv = inv + mmh(inv, power)
        d *= 2
    ecs = jnp.exp(cs)
    u = mmh(inv, v * beta)
    w = mmh(inv, kb * ecs)
    qk = mm_nt(qn, kn) * decay
    cs_end = jnp.sum(g, axis=0, keepdims=True)
    v_new = u - mm(w, state)
    o = mm(qn * ecs, state) + mm(qk, v_new)
    new_state = state * jnp.exp(cs_end) + mm_tn(kn * jnp.exp(cs_end - cs), v_new)
    return new_state, rmsnorm(o, norm_g) * jax.nn.silu(gate)


def gdn_fn(states, toks, consts):
    q, k, v, gate, small = toks
    dt_bias, a_log, norm_g = consts
    heads = (GDN_HEAD_DIM,) * GDN_HEADS
    qs, ks, vs, gs = split_cols(q, heads), split_cols(k, heads), split_cols(v, heads), split_cols(gate, heads)
    new_states, ys = [], []
    for h in range(GDN_HEADS):
        st, y = gdn_head(states[h], qs[h], ks[h], vs[h], gs[h], lane_col(small, GDN_HEADS + h), lane_col(small, h),
                         lane_scalar(dt_bias, h), lane_scalar(a_log, h), norm_g)
        new_states.append(st)
        ys.append(y)
    return new_states, [join_cols(ys)]


def adamw_fn(states, toks, consts):
    w, g, m, v = toks
    m = ADAM_B1 * m + (1.0 - ADAM_B1) * g
    v = ADAM_B2 * v + (1.0 - ADAM_B2) * (g * g)
    m_hat = m / (1.0 - ADAM_B1 ** ADAM_STEP)
    v_hat = v / (1.0 - ADAM_B2 ** ADAM_STEP)
    return [], [-ADAM_LR * (m_hat / (jnp.sqrt(v_hat) + ADAM_EPS) + ADAM_WD * w), m, v]


def tok_op(name, fn, t, nblk, tok, consts, outs, states=()):
    nblk = min(nblk, t)
    tok_args = [rows(e[0], nblk, dtype=e[1], grad=e[2] if len(e) > 2 else True) for e in tok]
    const_args = [whole(s) for s in consts]
    out_args = [rows(w, nblk, dtype=dt) for (w, dt) in outs]
    return blocked_op(name, fn, 1, t // nblk, tok_args, const_args, out_args, [(t, w) for (w, _) in outs], states)


def const_op(name, fn, in_shapes, out_shapes):
    return blocked_op(name, fn, 1, 1, [], [whole(s) for s in in_shapes], [whole(s) for s in out_shapes], list(out_shapes))


W = BRANCH_WIDTH
SCAN_ROWS = 128
ROW_BLOCK = 256


def s5_mixer(tag, t, u, p):
    col = (S5_W, 1)
    disc = const_op("s5_disc" + tag, s5_discretise_fn, [col, col, col, (S5_W, 16), (S5_W, 16)], [col, col, (S5_W, 16), (S5_W, 16)])
    ab_re, ab_im, bb_re, bb_im = disc([], [
        jnp.repeat(p["s5_log_step"], S5_STATE).reshape(col), p["s5_a_re"].reshape(col), p["s5_a_im"].reshape(col),
        p["s5_b_re"].reshape(S5_W, S5_GROUP_CH), p["s5_b_im"].reshape(S5_W, S5_GROUP_CH)])
    eye = jnp.eye(S5_GROUPS, dtype=F32)

    def block_in(bb):
        return jnp.einsum("gpc,gh->gchp", bb.reshape(S5_GROUPS, S5_STATE, S5_GROUP_CH), eye).reshape(W, S5_W)

    def block_out(c):
        return jnp.einsum("gcp,gh->gphc", c, eye).reshape(S5_W, W)

    w_b = jnp.concatenate([block_in(bb_re), block_in(bb_im)], axis=1)
    w_c = jnp.concatenate([block_out(p["s5_c_re"]), -block_out(p["s5_c_im"])], axis=0)
    bu = dense("s5_b" + tag)(u, w_b)
    scan = tok_op("s5_scan" + tag, s5_scan_fn, t, SCAN_ROWS, [(2 * S5_W, F32)], [(1, S5_W)] * 2, [(2 * S5_W, F32)], states=[(1, S5_W)] * 2)
    h = scan([bu], [ab_re.reshape(1, S5_W), ab_im.reshape(1, S5_W)])[0]
    yc = dense("s5_c" + tag)(h, w_c)
    glu = tok_op("s5_glu" + tag, s5_glu_fn, t, ROW_BLOCK, [(W, F32), (W, F32)], [(1, W), (W, W), (1, W)], [(W, BF16)])
    return glu([yc, u], [p["s5_d"].reshape(1, W), p["s5_w_glu"], p["s5_b_glu"][None]])[0]


def lru_mixer(tag, t, x, gate, p):
    def block_diag(w):
        return jnp.einsum("hij,hk->hikj", w, jnp.eye(LRU_HEADS, dtype=F32)).reshape(W, W)

    op = tok_op("lru" + tag, lru_fn, t, SCAN_ROWS, [(W, F32), (W, F32)],
                [(4, W), (1, W), (W, W), (1, W), (W, W), (1, W), (1, W)], [(W, BF16)], states=[(1, W), (TAIL, W)])
    return op([x, gate], [p["lru_conv_w"], p["lru_conv_b"][None], block_diag(p["lru_w_r"]), p["lru_b_r"][None],
                          block_diag(p["lru_w_i"]), p["lru_b_i"][None], p["lru_lambda"][None]])[0]


def m2_mixer(tag, t, z, xbc, small, p):
    cw = 2 * W
    conv = tok_op("m2_conv" + tag, m2_conv_fn, t, ROW_BLOCK, [(cw, F32)], [(4, cw), (1, cw)],
                  [(W, F32), (W // 2, F32), (W // 2, F32)], states=[(TAIL, cw)])
    xs, bm, cm = conv([xbc], [p["m2_conv_w"], p["m2_conv_b"][None]])
    ssd = tok_op("ssd" + tag, ssd_fn, t, CHUNK, [(W, F32), (W // 2, F32), (W // 2, F32), (LANE, F32)],
                 [(1, M2_HEADS)] * 3, [(W, F32)], states=[(4 * M2_HEAD_DIM, M2_STATE)] * M2_GROUPS)
    y = ssd([xs, bm, cm, small], [p["m2_dt_bias"][None], p["m2_a_log"][None], p["m2_d"][None]])[0]
    post = tok_op("m2_post" + tag, m2_post_fn, t, ROW_BLOCK, [(W, F32), (W, F32)], [(1, W)], [(W, BF16)])
    return post([y, z], [p["m2_norm"][None]])[0]


def gdn_mixer(tag, t, qkv, gate, small, p):
    conv = tok_op("gdn_conv" + tag, gdn_conv_fn, t, ROW_BLOCK, [(3 * W, F32)], [(4, 3 * W)], [(W, F32)] * 3, states=[(TAIL, 3 * W)])
    q, k, v = conv([qkv], [p["gdn_conv_w"]])
    op = tok_op("gdn" + tag, gdn_fn, t, GDN_CHUNK, [(W, F32)] * 4 + [(LANE, F32)], [(1, GDN_HEADS), (1, GDN_HEADS), (1, GDN_HEAD_DIM)],
                [(W, BF16)], states=[(GDN_HEAD_DIM, GDN_HEAD_DIM)] * GDN_HEADS)
    return op([q, k, v, gate, small], [p["gdn_dt_bias"][None], p["gdn_a_log"][None], p["gdn_norm"][None]])[0]


WEIGHTS = ["ffn1_norm", "ffn1_w_in", "ffn1_w_out", "mix_norm", "w_in", "w_gate", "b_gate", "s5_log_step", "s5_a_re",
           "s5_a_im", "s5_b_re", "s5_b_im", "s5_c_re", "s5_c_im", "s5_d", "s5_w_glu", "s5_b_glu", "lru_conv_w",
           "lru_conv_b", "lru_w_r", "lru_b_r", "lru_w_i", "lru_b_i", "lru_lambda", "m2_conv_w", "m2_conv_b", "m2_dt_bias",
           "m2_a_log", "m2_d", "m2_norm", "gdn_conv_w", "gdn_dt_bias", "gdn_a_log", "gdn_norm", "w_branch", "w_out",
           "ffn2_norm", "ffn2_w_in", "ffn2_w_out", "ple_norm", "ple_w_gate", "ple_w_proj", "final_norm"]
N_CHIPS = 4
N_DEV = 8
IN_WIDTH = 5136
SHARDED = {
    "ffn1_w_in": ((D_MODEL, 2 * FFN_DIM // N_CHIPS), 1, False),
    "ffn1_w_out": ((FFN_DIM // N_CHIPS, D_MODEL), 0, False),
    "w_in": ((D_MODEL, IN_WIDTH // N_CHIPS), 1, False),
    "w_gate": ((D_MODEL, N_BRANCH * D_MODEL // N_CHIPS), 1, False),
    "s5_w_glu": ((W // N_CHIPS, W), 0, False),
    "lru_conv_w": ((4, W // N_CHIPS), 1, True),
    "m2_conv_w": ((4, 2 * W // N_CHIPS), 1, True),
    "gdn_conv_w": ((4, 3 * W // N_CHIPS), 1, True),
    "w_branch": ((N_BRANCH, W, D_MODEL // N_CHIPS), 2, False),
    "w_out": ((D_MODEL // N_CHIPS, D_MODEL), 0, False),
    "ffn2_w_in": ((D_MODEL, 2 * FFN_DIM // N_CHIPS), 1, False),
    "ffn2_w_out": ((FFN_DIM // N_CHIPS, D_MODEL), 0, False),
    "ple_w_gate": ((D_MODEL // N_CHIPS, D_MODEL), 0, False),
    "ple_w_proj": ((256, D_MODEL // N_CHIPS), 1, False),
}
SMALL = [n for n in WEIGHTS if n not in SHARDED]
ROW = 1024


def _count(shape):
    return math.prod(shape)


def _round_up(n, unit):
    return -(-n // unit) * unit


N_GATHER = sum(DEPTH * _count(s) * (2 if exact else 1) for s, _, exact in SHARDED.values())
N_GRAD = sum(DEPTH * _count(s) for s, _, _ in SHARDED.values())
GATHER_ROWS = _round_up(-(-N_GATHER // ROW), 32)
GRAD_ROWS = _round_up(-(-N_GRAD // ROW), 32)
GRAD_HALF = GRAD_ROWS // 2
IN_PIECES = [(0, 512), (512, 512), (1024, 512), (1536, 512), (2048, 1024), (3072, 8), (3080, 1536), (4616, 512), (5128, 8)]


COL_SHARDED = ("ffn1_w_in", "ffn2_w_in", "w_gate", "ple_w_proj", "w_branch")
IN_SHARD = IN_WIDTH // N_CHIPS


def _w_in_cuts():
    cuts = {0, IN_SHARD}
    for s in range(N_CHIPS):
        for start, _ in IN_PIECES:
            if s * IN_SHARD < start < (s + 1) * IN_SHARD:
                cuts.add(start - s * IN_SHARD)
    return sorted(cuts)


IN_CUTS = _w_in_cuts()
IN_BLOCKS = list(zip(IN_CUTS[:-1], IN_CUTS[1:]))


def _piece_of(col):
    for k, (start, n) in enumerate(IN_PIECES):
        if start <= col < start + n:
            return k, col - start
    raise ValueError(col)


def cut_w_in(w):
    return jnp.concatenate([w[:, :, lo:hi].reshape(-1) for lo, hi in IN_BLOCKS])


def uncut_w_in(flat):
    blocks, off = [], 0
    for lo, hi in IN_BLOCKS:
        cnt = DEPTH * D_MODEL * (hi - lo)
        blocks.append(flat[off:off + cnt].reshape(DEPTH, D_MODEL, hi - lo))
        off += cnt
    return jnp.concatenate(blocks, axis=2)


def pack_for_gather(a):
    parts = []
    for n, (_, _, exact) in SHARDED.items():
        w = cut_w_in(a[n]) if n == "w_in" else a[n]
        parts.append((lax.bitcast_convert_type(w, BF16) if exact else w.astype(BF16)).reshape(-1))
    flat = jnp.concatenate(parts)
    return jnp.pad(flat, (0, GATHER_ROWS * ROW - flat.shape[0])).reshape(GATHER_ROWS, ROW)


def unpack_gathered(buf):
    flat = buf.reshape(N_CHIPS, -1)
    out, off = {}, 0
    for n, (shape, ax, exact) in SHARDED.items():
        cnt = DEPTH * _count(shape) * (2 if exact else 1)
        piece = flat[:, off:off + cnt]
        off += cnt
        if n == "w_in":
            cols = [[[] for _ in IN_PIECES] for _ in range(DEPTH)]
            for s in range(N_CHIPS):
                o = 0
                for lo, hi in IN_BLOCKS:
                    c = DEPTH * D_MODEL * (hi - lo)
                    blk = piece[s, o:o + c].reshape(DEPTH, D_MODEL, hi - lo)
                    o += c
                    k, _ = _piece_of(s * IN_SHARD + lo)
                    for layer in range(DEPTH):
                        cols[layer][k].append(blk[layer])
            out[n] = [[_pad_lanes(jnp.concatenate(c, axis=1)) for c in cols[layer]] for layer in range(DEPTH)]
        elif n in COL_SHARDED:
            out[n] = piece.reshape(N_CHIPS, DEPTH, *shape)
        else:
            if exact:
                w = lax.bitcast_convert_type(piece.reshape(N_CHIPS, DEPTH, *shape, 2), F32)
            else:
                w = piece.reshape(N_CHIPS, DEPTH, *shape)
            full = list(shape)
            full[ax] *= N_CHIPS
            out[n] = jnp.moveaxis(w, 0, ax + 1).reshape(DEPTH, *full)
    return out


def _pad_lanes(w):
    n = w.shape[1]
    return w if n % LANE == 0 else jnp.pad(w, ((0, 0), (0, LANE - n % LANE)))


def shard_w_in_grads(pieces):
    shards = []
    for s in range(N_CHIPS):
        parts = []
        for lo, hi in IN_BLOCKS:
            k, dst = _piece_of(s * IN_SHARD + lo)
            parts.append(jnp.stack([pieces[layer][k][:, dst:dst + hi - lo] for layer in range(DEPTH)]).reshape(-1))
        shards.append(jnp.concatenate(parts))
    return jnp.stack(shards)


def pack_grads(g):
    flat = jnp.concatenate([g[n] for n in SHARDED], axis=1)
    return jnp.pad(flat, ((0, 0), (0, GRAD_ROWS * ROW - flat.shape[1]))).reshape(N_CHIPS, GRAD_ROWS, ROW)


def shard_full_grads(n, per_layer):
    shape, ax, _ = SHARDED[n]
    parts = []
    for full in per_layer:
        w = full.reshape(*full.shape[:ax], N_CHIPS, shape[ax], *full.shape[ax + 1:])
        parts.append(jnp.moveaxis(w, ax, 0).reshape(N_CHIPS, -1))
    return jnp.concatenate(parts, axis=1)


def unpack_shard(buf):
    flat = buf.reshape(-1)
    out, off = {}, 0
    for n, (shape, _, _) in SHARDED.items():
        cnt = DEPTH * _count(shape)
        piece = flat[off:off + cnt]
        out[n] = uncut_w_in(piece) if n == "w_in" else piece.reshape(DEPTH, *shape)
        off += cnt
    return out


def pack_small(a, prefix, shapes):
    flat = jnp.concatenate([a[prefix + n].reshape(-1) for n in SMALL])
    rows_n = _round_up(-(-flat.shape[0] // ROW), 8)
    return jnp.pad(flat, (0, rows_n * ROW - flat.shape[0])).reshape(rows_n, ROW)


def unpack_small(buf, shapes):
    flat = buf.reshape(-1)
    out, off = {}, 0
    for n in SMALL:
        cnt = _count(shapes[n])
        out[n] = flat[off:off + cnt].reshape(shapes[n])
        off += cnt
    return out


ANY = pl.BlockSpec(memory_space=pl.ANY)


def _position():
    return lax.axis_index("x"), lax.axis_index("y"), lax.axis_index("c")


def _other_chips(x, y):
    return [(1 - x, y), (x, 1 - y), (1 - x, 1 - y)]


PLACE_ROWS = 592


def gather_weights(packed, slot):
    r = packed.shape[0]
    half = r // 2
    nblk = r // PLACE_ROWS

    def place(s_ref, in_ref, o_ref):
        o_ref[...] = in_ref[...]

    slots = pl.pallas_call(
        place, name="place_shard",
        grid_spec=pltpu.PrefetchScalarGridSpec(
            num_scalar_prefetch=1, grid=(nblk,), in_specs=[pl.BlockSpec((PLACE_ROWS, ROW), lambda i, s: (i, 0))],
            out_specs=pl.BlockSpec((None, PLACE_ROWS, ROW), lambda i, s: (s[0], i, 0))),
        out_shape=jax.ShapeDtypeStruct((N_CHIPS, r, ROW), packed.dtype), compiler_params=_params(("arbitrary",)),
    )(slot, packed)

    def body(in_ref, out_ref, send_sems, recv_sems):
        x, y, c = _position()
        sibling = (x, y, 1 - c)
        chips = _other_chips(x, y)

        def half_rows(px, py, pc):
            return out_ref.at[2 * px + py, pl.ds(pl.multiple_of(pc * half, 16), half), :]

        def copy(k, block, to):
            return pltpu.make_async_remote_copy(
                src_ref=half_rows(*block), dst_ref=half_rows(*block),
                send_sem=send_sems.at[k], recv_sem=recv_sems.at[k], device_id=to, device_id_type=MESH)

        first = [copy(j, (x, y, c), (*chip, c)) for j, chip in enumerate(chips)]
        for cp in first:
            cp.start()
        passed = [copy(3 + j, (*chip, c), sibling) for j, chip in enumerate(chips)]
        for j, chip in enumerate(chips):
            copy(j, (*chip, c), (x, y, c)).wait_recv()
            passed[j].start()
        for j, chip in enumerate(chips):
            copy(3 + j, (*chip, 1 - c), (x, y, c)).wait_recv()
        for cp in first + passed:
            cp.wait_send()

    return pl.pallas_call(
        body, name="gather_weights", in_specs=[ANY], out_specs=ANY, input_output_aliases={0: 0},
        out_shape=jax.ShapeDtypeStruct((N_CHIPS, r, ROW), packed.dtype),
        scratch_shapes=[pltpu.SemaphoreType.DMA((6,)), pltpu.SemaphoreType.DMA((6,))],
    )(slots)


def swap_pair_halves(g):
    half = g.shape[1] // 2

    def body(g_ref, land_ref, send_sem, recv_sem):
        x, y, c = _position()
        src = g_ref.at[:, pl.ds(pl.multiple_of((1 - c) * half, 8), half), :]
        cp = pltpu.make_async_remote_copy(src_ref=src, dst_ref=land_ref, send_sem=send_sem, recv_sem=recv_sem,
                                          device_id=(x, y, 1 - c), device_id_type=MESH)
        cp.start()
        cp.wait()

    return pl.pallas_call(
        body, name="swap_pair_halves", in_specs=[ANY], out_specs=ANY,
        out_shape=jax.ShapeDtypeStruct((N_CHIPS, half, ROW), g.dtype),
        scratch_shapes=[pltpu.SemaphoreType.DMA, pltpu.SemaphoreType.DMA],
    )(g)


def exchange_chip_partials(part):
    half = part.shape[1]

    def body(p_ref, land_ref, send_sems, recv_sems):
        x, y, c = _position()
        cps = [pltpu.make_async_remote_copy(src_ref=p_ref.at[2 * px + py], dst_ref=land_ref.at[j], send_sem=send_sems.at[j],
                                            recv_sem=recv_sems.at[j], device_id=(px, py, c), device_id_type=MESH)
               for j, (px, py) in enumerate(_other_chips(x, y))]
        for cp in cps:
            cp.start()
        for cp in cps:
            cp.wait()

    return pl.pallas_call(
        body, name="exchange_chip_partials", in_specs=[ANY], out_specs=ANY,
        out_shape=jax.ShapeDtypeStruct((3, half, ROW), part.dtype),
        scratch_shapes=[pltpu.SemaphoreType.DMA((3,)), pltpu.SemaphoreType.DMA((3,))],
    )(part)


def share_halves(both):
    half = both.shape[0] // 2

    def body(in_ref, out_ref, send_sem, recv_sem):
        x, y, c = _position()
        my_rows = out_ref.at[pl.ds(pl.multiple_of(c * half, 8), half), :]
        cp = pltpu.make_async_remote_copy(src_ref=my_rows, dst_ref=my_rows, send_sem=send_sem, recv_sem=recv_sem,
                                          device_id=(x, y, 1 - c), device_id_type=MESH)
        cp.start()
        cp.wait()

    return pl.pallas_call(
        body, name="share_halves", in_specs=[ANY], out_specs=ANY, input_output_aliases={0: 0},
        out_shape=jax.ShapeDtypeStruct(both.shape, both.dtype),
        scratch_shapes=[pltpu.SemaphoreType.DMA, pltpu.SemaphoreType.DMA],
    )(both)


def gather_all(block):
    m_per = block.shape[0]

    def body(x_ref, out_ref, send_sems, recv_sems, local_sem):
        x, y, c = _position()
        me, sibling = (x, y, c), (x, y, 1 - c)
        chips = _other_chips(x, y)

        def rows_of(px, py, pc):
            return out_ref.at[pl.ds(pl.multiple_of((4 * px + 2 * py + pc) * m_per, 8), m_per), :]

        def copy(k, blk, to, src=None):
            return pltpu.make_async_remote_copy(
                src_ref=rows_of(*blk) if src is None else src, dst_ref=rows_of(*blk),
                send_sem=send_sems.at[k], recv_sem=recv_sems.at[k], device_id=to, device_id_type=MESH)

        mine = pltpu.make_async_copy(x_ref, rows_of(*me), local_sem)
        mine.start()
        first = [copy(0, me, sibling, src=x_ref)]
        first += [copy(1 + j, me, (*chip, c), src=x_ref) for j, chip in enumerate(chips)]
        for cp in first:
            cp.start()
        passed = [copy(4 + j, (*chip, c), sibling) for j, chip in enumerate(chips)]
        for j, chip in enumerate(chips):
            copy(1 + j, (*chip, c), me).wait_recv()
            passed[j].start()
        copy(0, sibling, me).wait_recv()
        for j, chip in enumerate(chips):
            copy(4 + j, (*chip, 1 - c), me).wait_recv()
        for cp in first + passed:
            cp.wait_send()
        mine.wait()

    return pl.pallas_call(
        body, name="gather_all", out_shape=jax.ShapeDtypeStruct((N_DEV * m_per, ROW), block.dtype),
        in_specs=[pl.BlockSpec(memory_space=pltpu.VMEM)], out_specs=pl.BlockSpec(memory_space=pltpu.VMEM),
        scratch_shapes=[pltpu.SemaphoreType.DMA((7,)), pltpu.SemaphoreType.DMA((7,)), pltpu.SemaphoreType.DMA],
        compiler_params=_params(),
    )(block)


SUM_ROWS = 592


def add_pair_halves(grads, landed, core):
    half = landed.shape[1]
    nblk = half // SUM_ROWS

    def body(c_ref, g_ref, l_ref, o_ref, o16_ref):
        acc = g_ref[...] + l_ref[...]
        o_ref[...] = acc
        o16_ref[...] = acc.astype(BF16)

    blk = (None, SUM_ROWS, ROW)
    o_spec = pl.BlockSpec(blk, lambda s, i, c: (s, i, 0))
    return pl.pallas_call(
        body, name="add_pair_halves",
        grid_spec=pltpu.PrefetchScalarGridSpec(
            num_scalar_prefetch=1, grid=(N_CHIPS, nblk),
            in_specs=[pl.BlockSpec(blk, lambda s, i, c: (s, c[0] * nblk + i, 0)), o_spec], out_specs=[o_spec, o_spec]),
        out_shape=[jax.ShapeDtypeStruct(landed.shape, F32), jax.ShapeDtypeStruct(landed.shape, BF16)],
        compiler_params=_params(("arbitrary", "arbitrary")),
    )(core, grads, landed)


def add_chip_partials(part, landed, where):
    half = part.shape[1]
    nblk = half // SUM_ROWS

    def body(w_ref, p_ref, l_ref, o_ref):
        o_ref[...] = ((p_ref[...] + l_ref[0].astype(F32)) + l_ref[1].astype(F32)) + l_ref[2].astype(F32)

    return pl.pallas_call(
        body, name="add_chip_partials",
        grid_spec=pltpu.PrefetchScalarGridSpec(
            num_scalar_prefetch=1, grid=(nblk,),
            in_specs=[pl.BlockSpec((None, SUM_ROWS, ROW), lambda i, w: (w[0], i, 0)), pl.BlockSpec((3, SUM_ROWS, ROW), lambda i, w: (0, i, 0))],
            out_specs=pl.BlockSpec((SUM_ROWS, ROW), lambda i, w: (w[1] * nblk + i, 0))),
        out_shape=jax.ShapeDtypeStruct((2 * half, ROW), F32), compiler_params=_params(("arbitrary",)),
    )(where, part, landed)


def sum_devices(stacked):
    m = stacked.shape[1]

    def body(s_ref, o_ref):
        acc = s_ref[0]
        for d in range(1, N_DEV):
            acc = acc + s_ref[d]
        o_ref[...] = acc

    return pl.pallas_call(
        body, name="sum_devices", grid=(m // 8,), in_specs=[pl.BlockSpec((N_DEV, 8, ROW), lambda i: (0, i, 0))],
        out_specs=pl.BlockSpec((8, ROW), lambda i: (i, 0)), out_shape=jax.ShapeDtypeStruct((m, ROW), F32),
        compiler_params=_params(("arbitrary",)),
    )(stacked)


def adamw(name, w, g, m, v):
    width = w.shape[-1]
    n_rows = w.size // width
    nblk = _tile(n_rows, 256, 8)
    arg = rows(width, nblk)
    outs, _ = blocked_forward("adamw_" + name, adamw_fn, 1, n_rows // nblk, [arg] * 4, [t.reshape(n_rows, width) for t in (w, g, m, v)],
                              [], [], [arg] * 3, [(n_rows, width)] * 3, (), False)
    return [o.reshape(w.shape) for o in outs]


def trunk_loss(diff, p_emb, target, w16):
    x, small, gw, hd = diff["x"], diff["small"], diff["gw"], diff["hd"]
    t = x.shape[0]
    d = D_MODEL

    def norm_pair(name, fn, h, o, gain):
        op = tok_op(name, fn, t, 512, [(d, F32), (d, F32)], [(1, d)], [(d, F32), (d, BF16)])
        return op([h, o], [gain[None]])

    def ffn(tag, n, which, i):
        z = linear(f"{which}_in{tag}", BF16, (i,), N_CHIPS)(n, w16[which + "_w_in"], hd[which + "_w_in"][i])
        act = tok_op(f"{which}_act{tag}", swiglu_fn, t, 128, [(2 * FFN_DIM, BF16)], [], [(FFN_DIM, BF16)])([z], [])[0]
        return linear(f"{which}_out{tag}", F32, (i,))(act, w16[which + "_w_out"], hd[which + "_w_out"][i])

    h = x
    n = tok_op("norm_in", norm_fn, t, 512, [(d, F32)], [(1, d)], [(d, BF16)])([x], [small["ffn1_norm"][0][None]])[0]
    loss_rows = None
    for i in range(DEPTH):
        tag = str(i)
        p = {k: v[i] for k, v in small.items() if k != "final_norm"}
        p.update({k: v[i] for k, v in gw.items()})
        o = ffn(tag, n, "ffn1", i)
        h, u = norm_pair("mix_norm" + tag, make_addnorm_fn(0.5), h, o, p["mix_norm"])
        n_in = len(IN_PIECES)
        in_proj = multi_linear("in_proj" + tag, [F32] * n_in + [BF16], [None] * n_in + [(i,)], [0] * n_in + [N_CHIPS])
        proj = in_proj(u, w16["w_in"][i] + [w16["w_gate"]], hd["in_proj"][i])
        s5_u, lru_x, lru_g, m2_z, m2_xbc, m2_dt, gdn_qkv, gdn_g, gdn_ba, gate_logits = proj
        ys = [s5_mixer(tag, t, s5_u, p), lru_mixer(tag, t, lru_x, lru_g, p),
              m2_mixer(tag, t, m2_z, m2_xbc, m2_dt, p), gdn_mixer(tag, t, gdn_qkv, gdn_g, gdn_ba, p)]
        yb = [linear(f"branch{b}_{tag}", BF16, (i, b), N_CHIPS)(y, w16["w_branch"], hd["w_branch"][i][b]) for b, y in enumerate(ys)]
        merge = tok_op("gate_merge" + tag, gate_merge_fn, t, 128, [(N_BRANCH * d, BF16)] + [(d, BF16)] * N_BRANCH,
                       [(1, N_BRANCH * d)], [(d, BF16)])
        mixed = merge([gate_logits] + yb, [p["b_gate"][None]])[0]
        o = linear("w_out" + tag, F32, (i,))(mixed, w16["w_out"], hd["w_out"][i])
        h, n = norm_pair("ffn2_norm" + tag, make_addnorm_fn(1.0), h, o, p["ffn2_norm"])
        o = ffn(tag, n, "ffn2", i)
        h, n = norm_pair("ple_norm" + tag, make_addnorm_fn(0.5), h, o, p["ple_norm"])
        pg = linear("ple_gate" + tag, F32, (i,))(n, w16["ple_w_gate"], hd["ple_w_gate"][i])
        pp = linear("ple_proj" + tag, F32, (i,), N_CHIPS)(p_emb[i], w16["ple_w_proj"], hd["ple_w_proj"][i])
        if i + 1 < DEPTH:
            op = tok_op("ple" + tag, ple_fn, t, 512, [(d, F32)] * 3, [(1, d)], [(d, F32), (d, BF16)])
            h, n = op([h, pg, pp], [small["ffn1_norm"][i + 1][None]])
        else:
            op = tok_op("ple_loss", ple_loss_fn, t, 512, [(d, F32)] * 3 + [(d, F32, False)], [(1, d)], [(1, F32)])
            loss_rows = op([h, pg, pp, target], [small["final_norm"][None]])[0]
    return jnp.sum(loss_rows)


def kernel(x, p, ffn1_norm, ffn1_w_in, ffn1_w_out, mix_norm, w_in, w_gate, b_gate, s5_log_step, s5_a_re, s5_a_im, s5_b_re, s5_b_im, s5_c_re, s5_c_im, s5_d, s5_w_glu, s5_b_glu, lru_conv_w, lru_conv_b, lru_w_r, lru_b_r, lru_w_i, lru_b_i, lru_lambda, m2_conv_w, m2_conv_b, m2_dt_bias, m2_a_log, m2_d, m2_norm, gdn_conv_w, gdn_dt_bias, gdn_a_log, gdn_norm, w_branch, w_out, ffn2_norm, ffn2_w_in, ffn2_w_out, ple_norm, ple_w_gate, ple_w_proj, final_norm, loss_target, m_ffn1_norm, m_ffn1_w_in, m_ffn1_w_out, m_mix_norm, m_w_in, m_w_gate, m_b_gate, m_s5_log_step, m_s5_a_re, m_s5_a_im, m_s5_b_re, m_s5_b_im, m_s5_c_re, m_s5_c_im, m_s5_d, m_s5_w_glu, m_s5_b_glu, m_lru_conv_w, m_lru_conv_b, m_lru_w_r, m_lru_b_r, m_lru_w_i, m_lru_b_i, m_lru_lambda, m_m2_conv_w, m_m2_conv_b, m_m2_dt_bias, m_m2_a_log, m_m2_d, m_m2_norm, m_gdn_conv_w, m_gdn_dt_bias, m_gdn_a_log, m_gdn_norm, m_w_branch, m_w_out, m_ffn2_norm, m_ffn2_w_in, m_ffn2_w_out, m_ple_norm, m_ple_w_gate, m_ple_w_proj, m_final_norm, v_ffn1_norm, v_ffn1_w_in, v_ffn1_w_out, v_mix_norm, v_w_in, v_w_gate, v_b_gate, v_s5_log_step, v_s5_a_re, v_s5_a_im, v_s5_b_re, v_s5_b_im, v_s5_c_re, v_s5_c_im, v_s5_d, v_s5_w_glu, v_s5_b_glu, v_lru_conv_w, v_lru_conv_b, v_lru_w_r, v_lru_b_r, v_lru_w_i, v_lru_b_i, v_lru_lambda, v_m2_conv_w, v_m2_conv_b, v_m2_dt_bias, v_m2_a_log, v_m2_d, v_m2_norm, v_gdn_conv_w, v_gdn_dt_bias, v_gdn_a_log, v_gdn_norm, v_w_branch, v_w_out, v_ffn2_norm, v_ffn2_w_in, v_ffn2_w_out, v_ple_norm, v_ple_w_gate, v_ple_w_proj, v_final_norm):
    a = dict(locals())
    t = x.shape[1]
    core = lax.axis_index("c").astype(jnp.int32).reshape(1)
    slot = (2 * lax.axis_index("x") + lax.axis_index("y")).astype(jnp.int32).reshape(1)

    full = unpack_gathered(gather_weights(pack_for_gather(a), slot))
    exact = [n for n, spec in SHARDED.items() if spec[2]] + ["s5_w_glu"]
    gw = {n: full[n].astype(F32) for n in exact}
    w16 = {n: full[n] for n in SHARDED if n not in exact}

    def handle(n):
        shape = SHARDED[n][0]
        return jnp.zeros((N_CHIPS, *shape) if n in COL_SHARDED else (N_CHIPS * shape[0], *shape[1:]), F32)

    hd = {n: [handle(n) for _ in range(DEPTH)] for n in w16 if n not in ("w_in", "w_gate", "w_branch")}
    hd["w_branch"] = [[jnp.zeros((N_CHIPS, W, D_MODEL // N_CHIPS), F32) for _ in range(N_BRANCH)] for _ in range(DEPTH)]
    hd["in_proj"] = [[jnp.zeros((D_MODEL, _round_up(n, LANE)), F32) for _, n in IN_PIECES] + [handle("w_gate")] for _ in range(DEPTH)]
    diff = {"x": x.reshape(t, D_MODEL), "small": {n: a[n] for n in SMALL}, "gw": gw, "hd": hd}
    loss_local, vjp = jax.vjp(lambda dd: trunk_loss(dd, p.reshape(DEPTH, t, -1), loss_target.reshape(t, D_MODEL), w16), diff)
    (grads,) = vjp(jnp.ones((), F32))
    loss = lax.psum(loss_local, ("x", "y", "c"))
    grad_x = grads["x"].reshape(x.shape)

    gh = grads["hd"]
    big = {n: jnp.concatenate([g.reshape(N_CHIPS, -1) for g in gh[n]], axis=1) for n in gh if n not in ("in_proj", "w_branch")}
    big["w_in"] = shard_w_in_grads([gh["in_proj"][i][:-1] for i in range(DEPTH)])
    big["w_gate"] = jnp.concatenate([gh["in_proj"][i][-1].reshape(N_CHIPS, -1) for i in range(DEPTH)], axis=1)
    big["w_branch"] = jnp.concatenate([g.reshape(N_CHIPS, -1) for i in range(DEPTH) for g in gh["w_branch"][i]], axis=1)
    for n in exact:
        big[n] = shard_full_grads(n, [grads["gw"][n][i] for i in range(DEPTH)])

    packed = pack_grads(big)
    pair, pair16 = add_pair_halves(packed, swap_pair_halves(packed), core)
    mine = add_chip_partials(pair, exchange_chip_partials(pair16), jnp.concatenate([slot, core]))
    g_shard = unpack_shard(share_halves(mine))

    shapes = {n: a[n].shape for n in SMALL}
    gs_local = pack_small({n: grads["small"][n] for n in SMALL}, "", shapes)
    gs = gather_all(gs_local)
    g_small = sum_devices(gs.reshape(N_DEV, gs_local.shape[0], ROW))

    out_small = adamw("small", pack_small(a, "", shapes), g_small, pack_small(a, "m_", shapes), pack_small(a, "v_", shapes))
    g_small, d_small, m_small, v_small = [unpack_small(b, shapes) for b in [g_small] + out_small]
    res = {}
    for n in WEIGHTS:
        if n in SHARDED:
            res[n] = [g_shard[n]] + adamw(n, a[n], g_shard[n], a["m_" + n], a["v_" + n])
        else:
            res[n] = [g_small[n], d_small[n], m_small[n], v_small[n]]
    return (loss, grad_x, *[res[n][0] for n in WEIGHTS], *[res[n][1] for n in WEIGHTS],
            *[res[n][2] for n in WEIGHTS], *[res[n][3] for n in WEIGHTS])
```

```python
import functools
import math

import jax
import jax.numpy as jnp
from jax import lax
from jax.experimental import pallas as pl
from jax.experimental.pallas import tpu as pltpu

F32, BF16 = jnp.float32, jnp.bfloat16
EPS = 1e-6
D_MODEL = 1024
DEPTH = 2
FFN_DIM = 2816
BRANCH_WIDTH = 512
N_BRANCH = 4
LRU_C = 8.0
S5_GROUPS, S5_GROUP_CH, S5_STATE = 32, 16, 64
S5_W = S5_GROUPS * S5_STATE
LRU_HEADS, LRU_HEAD_DIM = 8, 64
M2_HEADS, M2_HEAD_DIM, M2_GROUPS, M2_STATE = 8, 64, 2, 128
GDN_HEADS, GDN_HEAD_DIM = 4, 128
CHUNK = 128
ADAM_LR, ADAM_B1, ADAM_B2, ADAM_EPS, ADAM_WD, ADAM_STEP = 0.001, 0.9, 0.999, 1e-08, 0.01, 10
VMEM_LIMIT_BYTES = 56 * 1024 * 1024
MESH = pl.DeviceIdType.MESH


def _params(sem=None):
    return pltpu.CompilerParams(vmem_limit_bytes=VMEM_LIMIT_BYTES, dimension_semantics=sem)


def _dot_bf16(a, b, dims):
    return lax.dot_general(a.astype(BF16), b.astype(BF16), (dims, ((), ())), preferred_element_type=F32)


def _make_mm(dot):
    @jax.custom_vjp
    def nn(a, b):
        return dot(a, b, ((1,), (0,)))

    @jax.custom_vjp
    def nt(a, b):
        return dot(a, b, ((1,), (1,)))

    @jax.custom_vjp
    def tn(a, b):
        return dot(a, b, ((0,), (0,)))

    nn.defvjp(lambda a, b: (nn(a, b), (a, b)), lambda r, g: (nt(g, r[1]), tn(r[0], g)))
    nt.defvjp(lambda a, b: (nt(a, b), (a, b)), lambda r, g: (nn(g, r[1]), tn(g, r[0])))
    tn.defvjp(lambda a, b: (tn(a, b), (a, b)), lambda r, g: (nt(r[1], g), nn(r[0], g)))
    return nn, nt, tn


def _dot_bf16x3(a, b, dims):
    a_hi, b_hi = a.astype(BF16), b.astype(BF16)
    a_lo = (a - a_hi.astype(F32)).astype(BF16)
    b_lo = (b - b_hi.astype(F32)).astype(BF16)

    def dot(p, q):
        return lax.dot_general(p, q, (dims, ((), ())), preferred_element_type=F32)

    return dot(a_hi, b_hi) + (dot(a_hi, b_lo) + dot(a_lo, b_hi))


mm, mm_nt, mm_tn = _make_mm(_dot_bf16)
mmh, mmh_nt, mmh_tn = _make_mm(_dot_bf16x3)


def _row_ids(shape):
    return lax.broadcasted_iota(jnp.int32, shape, 0)


def _shift_down(x, d):
    return jnp.where(_row_ids(x.shape) >= d, pltpu.roll(x, d, 0), 0.0)


def _shift_up(x, d):
    n = x.shape[0]
    return jnp.where(_row_ids(x.shape) < n - d, pltpu.roll(x, n - d, 0), 0.0)


def _first_row(x):
    return jnp.sum(jnp.where(_row_ids(x.shape) == 0, x, 0.0), axis=0, keepdims=True)


def last_row(x):
    return jnp.sum(jnp.where(_row_ids(x.shape) == x.shape[0] - 1, x, 0.0), axis=0, keepdims=True)


def pick_row(x, j):
    return jnp.sum(jnp.where(_row_ids(x.shape) == j, x, 0.0), axis=0, keepdims=True)


@jax.custom_vjp
def lin_scan(a, b, h0):
    n = a.shape[0]
    row = _row_ids(a.shape)
    acc_a = a
    acc_b = b + jnp.where(row == 0, a * h0, 0.0)
    d = 1
    while d < n:
        acc_b = acc_a * _shift_down(acc_b, d) + acc_b
        acc_a = acc_a * jnp.where(row >= d, pltpu.roll(acc_a, d, 0), 1.0)
        d *= 2
    return acc_b


def _lin_scan_fwd(a, b, h0):
    h = lin_scan(a, b, h0)
    return h, (a, h, h0)


def _lin_scan_bwd(res, dh):
    a, h, h0 = res
    n = a.shape[0]
    row = _row_ids(a.shape)
    acc_a = _shift_up(a, 1)
    g = dh
    d = 1
    while d < n:
        g = acc_a * _shift_up(g, d) + g
        acc_a = acc_a * jnp.where(row < n - d, pltpu.roll(acc_a, n - d, 0), 1.0)
        d *= 2
    h_prev = _shift_down(h, 1) + jnp.where(row == 0, h0, 0.0)
    return g * h_prev, g, _first_row(a * g)


lin_scan.defvjp(_lin_scan_fwd, _lin_scan_bwd)


def _cscan(br, bi, ar, ai, up):
    n = br.shape[0]
    shift = _shift_up if up else _shift_down
    hr, hi, pr, pi = br, bi, ar, ai
    d = 1
    while d < n:
        sr, si = shift(hr, d), shift(hi, d)
        hr, hi = hr + pr * sr - pi * si, hi + pr * si + pi * sr
        pr, pi = pr * pr - pi * pi, 2.0 * pr * pi
        d *= 2
    return hr, hi


@jax.custom_vjp
def complex_scan(br, bi, ar, ai, h0r, h0i):
    first = _row_ids(br.shape) == 0
    br = br + jnp.where(first, ar * h0r - ai * h0i, 0.0)
    bi = bi + jnp.where(first, ar * h0i + ai * h0r, 0.0)
    return _cscan(br, bi, ar, ai, False)


def _complex_scan_fwd(br, bi, ar, ai, h0r, h0i):
    hr, hi = complex_scan(br, bi, ar, ai, h0r, h0i)
    return (hr, hi), (ar, ai, hr, hi, h0r, h0i)


def _complex_scan_bwd(res, cts):
    ar, ai, hr, hi, h0r, h0i = res
    gr, gi = _cscan(cts[0], cts[1], ar, -ai, True)
    first = _row_ids(hr.shape) == 0
    pr = _shift_down(hr, 1) + jnp.where(first, h0r, 0.0)
    pi = _shift_down(hi, 1) + jnp.where(first, h0i, 0.0)
    d_ar = jnp.sum(gr * pr + gi * pi, axis=0, keepdims=True)
    d_ai = jnp.sum(gi * pr - gr * pi, axis=0, keepdims=True)
    g0r, g0i = _first_row(gr), _first_row(gi)
    return gr, gi, d_ar, d_ai, ar * g0r + ai * g0i, ar * g0i - ai * g0r


complex_scan.defvjp(_complex_scan_fwd, _complex_scan_bwd)

TAIL = 8


@jax.custom_vjp
def tail_rows(x):
    return x[x.shape[0] - TAIL:, :]


tail_rows.defvjp(
    lambda x: (tail_rows(x), x.shape[0]),
    lambda n, g: (jnp.concatenate([jnp.zeros((n - TAIL, g.shape[1]), g.dtype), g], axis=0),),
)


def _make_shift_tail(d):
    @jax.custom_vjp
    def shifted(x, tail):
        n = x.shape[0]
        tpad = jnp.concatenate([tail, jnp.zeros((n - TAIL, x.shape[1]), x.dtype)], axis=0)
        return jnp.where(_row_ids(x.shape) >= d, pltpu.roll(x, d, 0), pltpu.roll(tpad, n + d - TAIL, 0))

    def fwd(x, tail):
        return shifted(x, tail), None

    def bwd(_, g):
        g8 = g[:TAIL, :]
        dtail = jnp.where(_row_ids(g8.shape) >= TAIL - d, pltpu.roll(g8, TAIL - d, 0), 0.0)
        return _shift_up(g, d), dtail

    shifted.defvjp(fwd, bwd)
    return shifted


_SHIFT_TAIL = {d: _make_shift_tail(d) for d in (1, 2, 3)}


def causal_conv4(x, tail, w):
    y = pick_row(w, 3) * x
    for j in range(3):
        y = y + pick_row(w, j) * _SHIFT_TAIL[3 - j](x, tail)
    return y


def rmsnorm(x, g):
    return x * lax.rsqrt(jnp.mean(x * x, axis=-1, keepdims=True) + EPS) * g


def to_row(col):
    n = col.shape[0]
    eye = lax.broadcasted_iota(jnp.int32, (n, n), 0) == lax.broadcasted_iota(jnp.int32, (n, n), 1)
    return jnp.sum(jnp.where(eye, col, 0.0), axis=0, keepdims=True)


def causal_decay(a_col):
    n = a_col.shape[0]
    causal = lax.broadcasted_iota(jnp.int32, (n, n), 0) >= lax.broadcasted_iota(jnp.int32, (n, n), 1)
    cs = jnp.sum(jnp.where(causal, to_row(a_col), 0.0), axis=1, keepdims=True)
    diff = cs - to_row(cs)
    return cs, jnp.where(causal, jnp.exp(jnp.where(causal, diff, 0.0)), 0.0)


class Arg:
    def __init__(self, block, imap, dtype=F32, grad=True, shared=True):
        self.block, self.imap, self.dtype, self.grad, self.shared = block, imap, dtype, grad, shared


def rows(width, nblk, col=lambda g: 0, dtype=F32, grad=True):
    return Arg((nblk, width), lambda g, c: (c, col(g)), dtype, grad)


def head_rows(width, nblk, head=lambda g: g, dtype=F32, grad=True):
    return Arg((None, nblk, width), lambda g, c: (head(g), c, 0), dtype, grad)


def whole(shape, grad=True):
    return Arg(tuple(shape), lambda g, c: (0,) * len(shape), F32, grad, shared=True)


def per_group(shape, idx=lambda g: g, grad=True):
    return Arg((None,) + tuple(shape), lambda g, c: (idx(g),) + (0,) * len(shape), F32, grad, shared=False)


def _bshape(block):
    return tuple(b for b in block if b is not None)


def _spec(arg, nb=None):
    if nb is None:
        return pl.BlockSpec(arg.block, arg.imap)
    return pl.BlockSpec(arg.block, lambda g, c: arg.imap(g, nb - 1 - c))


def blocked_forward(name, fn, groups, nb, tok_args, tok, const_args, consts, out_args, out_shapes, state_shapes, save,
                    reverse=False):
    n_tok, n_const, n_out, n_state = len(tok), len(consts), len(out_args), len(state_shapes)
    walk = nb if reverse else None

    def body(*refs):
        tok_refs = refs[:n_tok]
        const_refs = refs[n_tok:n_tok + n_const]
        out_refs = refs[n_tok + n_const:n_tok + n_const + n_out]
        pos = n_tok + n_const + n_out
        save_refs = refs[pos:pos + (n_state if save else 0)]
        state_refs = refs[len(refs) - n_state:] if n_state else ()

        @pl.when(pl.program_id(1) == 0)
        def _():
            for s in state_refs:
                s[...] = jnp.zeros_like(s)

        states = [s[...] for s in state_refs]
        for sr, s in zip(save_refs, states):
            sr[...] = s
        new_states, outs = fn(states, [r[...].astype(F32) for r in tok_refs], [r[...] for r in const_refs])
        for o_ref, o in zip(out_refs, outs):
            o_ref[...] = o.astype(o_ref.dtype)
        for s_ref, s in zip(state_refs, new_states):
            s_ref[...] = s

    out_specs = [_spec(a, walk) for a in out_args]
    out_shape = [jax.ShapeDtypeStruct(s, a.dtype) for s, a in zip(out_shapes, out_args)]
    if save:
        assert not reverse
        for s in state_shapes:
            out_specs.append(pl.BlockSpec((None, None) + tuple(s), lambda g, c, k=len(s): (g, c) + (0,) * k))
            out_shape.append(jax.ShapeDtypeStruct((groups, nb) + tuple(s), F32))
    res = pl.pallas_call(
        body, name=name, grid=(groups, nb),
        in_specs=[_spec(a, walk) for a in tok_args] + [_spec(a, walk) for a in const_args],
        out_specs=out_specs, out_shape=out_shape,
        scratch_shapes=[pltpu.VMEM(tuple(s), F32) for s in state_shapes],
        compiler_params=_params(("arbitrary", "arbitrary")),
    )(*tok, *consts)
    return list(res[:n_out]), list(res[n_out:])


def blocked_backward(name, fn, groups, nb, tok_args, tok, const_args, consts, out_args, cts, state_shapes, saved):
    n_tok, n_const, n_out, n_state = len(tok), len(consts), len(out_args), len(state_shapes)
    tok_g = [i for i, a in enumerate(tok_args) if a.grad]
    const_g = [i for i, a in enumerate(const_args) if a.grad]

    def body(*refs):
        tok_refs = refs[:n_tok]
        const_refs = refs[n_tok:n_tok + n_const]
        pos = n_tok + n_const
        saved_refs = refs[pos:pos + n_state]
        ct_refs = refs[pos + n_state:pos + n_state + n_out]
        pos += n_state + n_out
        dtok_refs = refs[pos:pos + len(tok_g)]
        dconst_refs = refs[pos + len(tok_g):pos + len(tok_g) + len(const_g)]
        dstate_refs = refs[len(refs) - n_state:] if n_state else ()
        g_id, c_id = pl.program_id(0), pl.program_id(1)

        @pl.when(c_id == 0)
        def _():
            for s in dstate_refs:
                s[...] = jnp.zeros_like(s)
            for r, i in zip(dconst_refs, const_g):
                if not const_args[i].shared:
                    r[...] = jnp.zeros_like(r)

        @pl.when((c_id == 0) & (g_id == 0))
        def _():
            for r, i in zip(dconst_refs, const_g):
                if const_args[i].shared:
                    r[...] = jnp.zeros_like(r)

        tok_vals = [r[...].astype(F32) for r in tok_refs]
        const_vals = [r[...] for r in const_refs]

        def f(states, tok_d, const_d):
            tv, cv = list(tok_vals), list(const_vals)
            for i, v in zip(tok_g, tok_d):
                tv[i] = v
            for i, v in zip(const_g, const_d):
                cv[i] = v
            return fn(states, tv, cv)

        _, vjp = jax.vjp(f, [r[...] for r in saved_refs], [tok_vals[i] for i in tok_g], [const_vals[i] for i in const_g])
        dstates, dtok, dconst = vjp(([r[...] for r in dstate_refs], [r[...].astype(F32) for r in ct_refs]))
        for r, v in zip(dtok_refs, dtok):
            r[...] = v.astype(r.dtype)
        for r, v in zip(dconst_refs, dconst):
            r[...] += v
        for r, v in zip(dstate_refs, dstates):
            r[...] = v

    in_specs = [_spec(a, nb) for a in tok_args] + [_spec(a, nb) for a in const_args]
    for s in state_shapes:
        in_specs.append(pl.BlockSpec((None, None) + tuple(s), lambda g, c, k=len(s): (g, nb - 1 - c) + (0,) * k))
    in_specs += [_spec(a, nb) for a in out_args]
    out_specs = [_spec(tok_args[i], nb) for i in tok_g] + [_spec(const_args[i], nb) for i in const_g]
    out_shape = [jax.ShapeDtypeStruct(tok[i].shape, tok[i].dtype) for i in tok_g]
    out_shape += [jax.ShapeDtypeStruct(consts[i].shape, F32) for i in const_g]
    res = pl.pallas_call(
        body, name=name, grid=(groups, nb), in_specs=in_specs, out_specs=out_specs, out_shape=out_shape,
        scratch_shapes=[pltpu.VMEM(tuple(s), F32) for s in state_shapes],
        compiler_params=_params(("arbitrary", "arbitrary")),
    )(*tok, *consts, *saved, *cts)
    dtok = [None] * n_tok
    dconst = [None] * n_const
    for i, v in zip(tok_g, res[:len(tok_g)]):
        dtok[i] = v
    for i, v in zip(const_g, res[len(tok_g):]):
        dconst[i] = v
    return dtok, dconst


def blocked_op(name, fn, groups, nb, tok_args, const_args, out_args, out_shapes, state_shapes=()):
    state_shapes = tuple(state_shapes)

    @jax.custom_vjp
    def op(tok, consts):
        outs, _ = blocked_forward(name, fn, groups, nb, tok_args, tok, const_args, consts, out_args, out_shapes, state_shapes, False)
        return outs

    def fwd(tok, consts):
        outs, saved = blocked_forward(name, fn, groups, nb, tok_args, tok, const_args, consts, out_args, out_shapes, state_shapes, True)
        return outs, (tok, consts, saved)

    def bwd(res, cts):
        tok, consts, saved = res
        dtok, dconst = blocked_backward(name + "_bwd", fn, groups, nb, tok_args, tok, const_args, consts, out_args, list(cts), state_shapes, saved)
        dtok = [jnp.zeros_like(t) if d is None else d for t, d in zip(tok, dtok)]
        dconst = [jnp.zeros_like(k) if d is None else d for k, d in zip(consts, dconst)]
        return dtok, dconst

    op.defvjp(fwd, bwd)
    return op


def _make_split(sizes):
    offs = [sum(sizes[:i]) for i in range(len(sizes))]

    @jax.custom_vjp
    def split(x):
        return tuple(x[:, o:o + s] for o, s in zip(offs, sizes))

    split.defvjp(lambda x: (split(x), None), lambda _, g: (jnp.concatenate(list(g), axis=1),))
    return split


def _make_join(sizes):
    offs = [sum(sizes[:i]) for i in range(len(sizes))]

    @jax.custom_vjp
    def join(parts):
        return jnp.concatenate(list(parts), axis=1)

    join.defvjp(lambda parts: (join(parts), None), lambda _, g: (tuple(g[:, o:o + s] for o, s in zip(offs, sizes)),))
    return join


def split_cols(x, sizes):
    return _make_split(tuple(sizes))(x)


def join_cols(parts):
    return _make_join(tuple(p.shape[1] for p in parts))(tuple(parts))


def lane_scalar(row, j):
    lane = lax.broadcasted_iota(jnp.int32, row.shape, 1)
    return jnp.sum(jnp.where(lane == j, row, 0.0), axis=1, keepdims=True)


def lane_col(blk, j):
    lane = lax.broadcasted_iota(jnp.int32, blk.shape, 1)
    return jnp.sum(jnp.where(lane == j, blk, 0.0), axis=1, keepdims=True)


LANE = 128
MM_TILE_M, MM_TILE_N, MM_TILE_K = 1024, 1536, 2816
MM_TILE_MT = 1408


def _tile(n, cap, unit):
    if n <= cap:
        return n
    best = None
    for t in range(unit, cap + 1, unit):
        if n % t == 0:
            best = t
    assert best is not None, (n, cap, unit)
    return best


def matmul(a, b, mode="nn", add=None, out_dtype=F32, name="matmul", pre=None, col_shards=0):
    lead = () if pre is None else tuple(pre)
    rows_b, cols_b = b.shape[-2:]
    if mode != "tn" and col_shards:
        cols_b *= col_shards
    if mode == "nn":
        (m, k), n = a.shape, cols_b
    elif mode == "nt":
        (m, k), n = a.shape, rows_b
    else:
        (k, m), n = a.shape, cols_b
    shard_n = n // col_shards if col_shards and mode != "nt" else n
    shard_k = k // col_shards if col_shards and mode == "nt" else k
    tm = _tile(m, MM_TILE_MT, LANE) if mode == "tn" else _tile(m, MM_TILE_M, 8)
    tn = _tile(shard_n, MM_TILE_N, LANE)
    tk = _tile(shard_k, MM_TILE_K if mode != "tn" else MM_TILE_M, LANE)
    nk = k // tk
    qn, qk = shard_n // tn, shard_k // tk
    nolead = (None,) * len(lead)
    a_spec = pl.BlockSpec((tk, tm), lambda i, j, l: (l, i)) if mode == "tn" else pl.BlockSpec((tm, tk), lambda i, j, l: (i, l))
    if mode == "tn":
        b_spec = pl.BlockSpec((tk, tn), lambda i, j, l: (l, j))
    elif mode == "nn" and col_shards:
        b_spec = pl.BlockSpec((None,) + nolead + (tk, tn), lambda i, j, l: (j // qn,) + lead + (l, j % qn))
    elif mode == "nn":
        b_spec = pl.BlockSpec(nolead + (tk, tn), lambda i, j, l: lead + (l, j))
    elif col_shards:
        b_spec = pl.BlockSpec((None,) + nolead + (tn, tk), lambda i, j, l: (l // qk,) + lead + (j, l % qk))
    else:
        b_spec = pl.BlockSpec(nolead + (tn, tk), lambda i, j, l: lead + (j, l))
    dims = {"nn": ((1,), (0,)), "nt": ((1,), (1,)), "tn": ((0,), (0,))}[mode]
    has_add = add is not None

    def body_single(*refs):
        acc = _dot_bf16(refs[0][...], refs[1][...], dims)
        if has_add:
            acc = acc + refs[2][...].astype(F32)
        refs[-1][...] = acc.astype(refs[-1].dtype)

    def body(*refs):
        a_ref, b_ref = refs[0], refs[1]
        add_ref = refs[2] if has_add else None
        o_ref, acc_ref = refs[-2], refs[-1]
        l = pl.program_id(2)

        @pl.when(l == 0)
        def _():
            acc_ref[...] = add_ref[...].astype(F32) if has_add else jnp.zeros_like(acc_ref)

        acc_ref[...] += _dot_bf16(a_ref[...], b_ref[...], dims)

        @pl.when(l == nk - 1)
        def _():
            o_ref[...] = acc_ref[...].astype(o_ref.dtype)

    if mode == "tn" and col_shards:
        assert not has_add
        o_spec = pl.BlockSpec((None, tm, tn), lambda i, j, l: (j // qn, i, j % qn))
        out_shape = jax.ShapeDtypeStruct((col_shards, m, shard_n), out_dtype)
    else:
        o_spec = pl.BlockSpec((tm, tn), lambda i, j, l: (i, j))
        out_shape = jax.ShapeDtypeStruct((m, n), out_dtype)
    return pl.pallas_call(
        body_single if nk == 1 else body, name=name, grid=(m // tm, n // tn, nk),
        in_specs=[a_spec, b_spec] + ([o_spec] if has_add else []), out_specs=o_spec, out_shape=out_shape,
        scratch_shapes=[] if nk == 1 else [pltpu.VMEM((tm, tn), F32)],
        compiler_params=_params(("parallel", "parallel", "arbitrary")),
    )(*([a, b] + ([add] if has_add else [])))


def linear(name, out_dtype=F32, pre=None, col_shards=0):
    @jax.custom_vjp
    def op(a, w, handle):
        return matmul(a, w, "nn", out_dtype=out_dtype, name=name, pre=pre, col_shards=col_shards)

    def fwd(a, w, handle):
        return op(a, w, handle), (a, w)

    def bwd(res, g):
        a, w = res
        da = matmul(g, w, "nt", out_dtype=a.dtype, name=name + "_da", pre=pre, col_shards=col_shards)
        dw = matmul(a, g, "tn", out_dtype=F32, name=name + "_dw", col_shards=col_shards)
        return da, jnp.zeros_like(w), dw

    op.defvjp(fwd, bwd)
    return op


def multi_linear(name, out_dtypes, pres, shards):
    sel = [dict(pre=p, col_shards=s) for p, s in zip(pres, shards)]

    @jax.custom_vjp
    def op(a, ws, handles):
        return [matmul(a, w, "nn", out_dtype=dt, name=f"{name}{i}", **sel[i]) for i, (w, dt) in enumerate(zip(ws, out_dtypes))]

    def fwd(a, ws, handles):
        return op(a, ws, handles), (a, ws)

    def bwd(res, gs):
        a, ws = res
        acc = None
        for i, (g, w) in enumerate(zip(gs, ws)):
            last = i == len(ws) - 1
            acc = matmul(g, w, "nt", add=acc, out_dtype=a.dtype if last else F32, name=f"{name}{i}_da", **sel[i])
        dws = [matmul(a, g, "tn", out_dtype=F32, name=f"{name}{i}_dw", col_shards=shards[i]) for i, g in enumerate(gs)]
        return acc, [jnp.zeros_like(w) for w in ws], dws

    op.defvjp(fwd, bwd)
    return op


def dense(name, out_dtype=F32):
    @jax.custom_vjp
    def op(a, w):
        return matmul(a, w, "nn", out_dtype=out_dtype, name=name)

    def fwd(a, w):
        return op(a, w), (a, w)

    def bwd(res, g):
        a, w = res
        return (matmul(g, w, "nt", out_dtype=a.dtype, name=name + "_da"),
                matmul(a, g, "tn", out_dtype=w.dtype, name=name + "_dw"))

    op.defvjp(fwd, bwd)
    return op


def to_col(row):
    n = row.shape[1]
    eye = lax.broadcasted_iota(jnp.int32, (n, n), 0) == lax.broadcasted_iota(jnp.int32, (n, n), 1)
    return jnp.sum(jnp.where(eye, row, 0.0), axis=1, keepdims=True)


def norm_fn(states, toks, consts):
    return [], [rmsnorm(toks[0], consts[0])]


def make_addnorm_fn(scale):
    def fn(states, toks, consts):
        h = toks[0] + scale * toks[1]
        return [], [h, rmsnorm(h, consts[0])]

    return fn


def swiglu_fn(states, toks, consts):
    gate, up = split_cols(toks[0], (FFN_DIM, FFN_DIM))
    return [], [jax.nn.silu(gate) * up]


def gate_merge_fn(states, toks, consts):
    gates = split_cols(jax.nn.sigmoid(toks[0] + consts[0]), (D_MODEL,) * N_BRANCH)
    mixed = gates[0] * toks[1]
    for n in range(1, N_BRANCH):
        mixed = mixed + gates[n] * toks[1 + n]
    return [], [mixed]


def ple_fn(states, toks, consts):
    h = toks[0] + jax.nn.sigmoid(toks[1]) * toks[2]
    return [], [h, rmsnorm(h, consts[0])]


def ple_loss_fn(states, toks, consts):
    h = toks[0] + jax.nn.sigmoid(toks[1]) * toks[2]
    err = rmsnorm(h, consts[0]) - toks[3]
    return [], [0.5 * jnp.mean(err * err, axis=-1, keepdims=True)]


def s5_discretise_fn(states, toks, consts):
    log_step, a_re, a_im, b_re, b_im = consts
    step = jnp.exp(log_step)
    mag = jnp.exp(a_re * step)
    ab_re, ab_im = mag * jnp.cos(a_im * step), mag * jnp.sin(a_im * step)
    den = a_re * a_re + a_im * a_im
    num_re = ab_re - 1.0
    f_re = (num_re * a_re + ab_im * a_im) / den
    f_im = (ab_im * a_re - num_re * a_im) / den
    return [], [ab_re, ab_im, f_re * b_re - f_im * b_im, f_re * b_im + f_im * b_re]


def s5_scan_fn(states, toks, consts):
    b_re, b_im = split_cols(toks[0], (S5_W, S5_W))
    h_re, h_im = complex_scan(b_re, b_im, consts[0], consts[1], states[0], states[1])
    return [last_row(h_re), last_row(h_im)], [join_cols([h_re, h_im])]


def s5_scan_bwd_fn(states, toks, consts):
    g_re, g_im, acc_re, acc_im = states
    h, ct, h0_re, h0_im = toks
    a_re, a_im = consts
    h_re, h_im = split_cols(h, (S5_W, S5_W))
    c_re, c_im = split_cols(ct, (S5_W, S5_W))
    last = _row_ids(c_re.shape) == c_re.shape[0] - 1
    cts = (c_re + jnp.where(last, g_re, 0.0), c_im + jnp.where(last, g_im, 0.0))
    d_re, d_im, da_re, da_im, d0_re, d0_im = _complex_scan_bwd((a_re, a_im, h_re, h_im, h0_re, h0_im), cts)
    acc_re, acc_im = acc_re + da_re, acc_im + da_im
    return [d0_re, d0_im, acc_re, acc_im], [join_cols([d_re, d_im]), acc_re, acc_im]


def s5_scan(name, t, bu, a_re, a_im):
    nblk = min(SCAN_ROWS, t)
    nb = t // nblk
    wide, row = rows(2 * S5_W, nblk), whole((1, S5_W))
    entry = Arg((None, None, 1, S5_W), lambda g, c: (0, c, 0, 0))
    state = [(1, S5_W)] * 2

    def run(bu, a_re, a_im, save):
        return blocked_forward(name, s5_scan_fn, 1, nb, [wide], [bu], [row, row], [a_re, a_im], [wide], [(t, 2 * S5_W)], state, save)

    @jax.custom_vjp
    def op(bu, a_re, a_im):
        return run(bu, a_re, a_im, False)[0][0]

    def fwd(bu, a_re, a_im):
        outs, saved = run(bu, a_re, a_im, True)
        return outs[0], (outs[0], saved, a_re, a_im)

    def bwd(res, ct):
        h, saved, a_re, a_im = res
        outs, _ = blocked_forward(name + "_bwd", s5_scan_bwd_fn, 1, nb, [wide, wide, entry, entry], [h, ct] + saved, [row, row],
                                  [a_re, a_im], [wide, row, row], [(t, 2 * S5_W), (1, S5_W), (1, S5_W)], state * 2, False, reverse=True)
        return tuple(outs)

    op.defvjp(fwd, bwd)
    return op(bu, a_re, a_im)


def s5_glu_fn(states, toks, consts):
    d_skip, w_glu, b_glu = consts
    z = jax.nn.gelu(toks[0] + d_skip * toks[1])
    return [], [z * jax.nn.sigmoid(mm(z, w_glu) + b_glu)]


def lru_fn(states, toks, consts):
    h0, tail = states
    x, gate = toks
    conv_w, conv_b, w_r, b_r, w_i, b_i, lam = consts
    xc = causal_conv4(x, tail, conv_w) + conv_b
    r = jax.nn.sigmoid(mm(xc, w_r) + b_r)
    i_g = jax.nn.sigmoid(mm(xc, w_i) + b_i)
    log_a = -LRU_C * r * jax.nn.softplus(-lam)
    inp = jnp.sqrt(1.0 - jnp.exp(2.0 * log_a)) * (i_g * xc)
    h = lin_scan(jnp.exp(log_a), inp, h0)
    return [last_row(h), tail_rows(x)], [h * jax.nn.gelu(gate)]


def m2_conv_fn(states, toks, consts):
    y = jax.nn.silu(causal_conv4(toks[0], states[0], consts[0]) + consts[1])
    return [tail_rows(toks[0])], list(split_cols(y, (BRANCH_WIDTH, M2_GROUPS * M2_STATE, M2_GROUPS * M2_STATE)))


def gdn_conv_fn(states, toks, consts):
    y = jax.nn.silu(causal_conv4(toks[0], states[0], consts[0]))
    return [tail_rows(toks[0])], list(split_cols(y, (BRANCH_WIDTH,) * 3))


def ssd_fn(states, toks, consts):
    xs, bm, cm, small = toks
    dt_bias, a_log, d_skip = consts
    x_pairs = split_cols(xs, (LANE,) * 4)
    b_g = split_cols(bm, (M2_STATE,) * M2_GROUPS)
    c_g = split_cols(cm, (M2_STATE,) * M2_GROUPS)
    lo = lax.broadcasted_iota(jnp.int32, (1, LANE), 1) < M2_HEAD_DIM
    new_states, y_pairs = [], []
    for g in range(M2_GROUPS):
        scores = mm_nt(c_g[g], b_g[g])
        y_off = split_cols(mm_nt(c_g[g], states[g]), (LANE, LANE))
        to_end, ends = [], []
        for j in range(2):
            pair = 2 * g + j
            x2 = x_pairs[pair]
            dts, css, decays, end = [], [], [], []
            for h in (2 * pair, 2 * pair + 1):
                dt = jax.nn.softplus(lane_col(small, h) + lane_scalar(dt_bias, h))
                a = dt * (-jnp.exp(lane_scalar(a_log, h)))
                cs, decay = causal_decay(a)
                dts.append(dt)
                css.append(cs)
                decays.append(decay)
                end.append(jnp.sum(a, axis=0, keepdims=True))
            xdt = x2 * jnp.where(lo, dts[0], dts[1])
            y = mm(scores * decays[0], jnp.where(lo, xdt, 0.0)) + mm(scores * decays[1], jnp.where(lo, 0.0, xdt))
            cs2 = jnp.where(lo, css[0], css[1])
            end2 = jnp.where(lo, end[0], end[1])
            y = y + y_off[j] * jnp.exp(cs2)
            y = y + jnp.where(lo, lane_scalar(d_skip, 2 * pair), lane_scalar(d_skip, 2 * pair + 1)) * x2
            y_pairs.append(y)
            to_end.append(xdt * jnp.exp(end2 - cs2))
            ends.append(end2)
        chunk_decay = jnp.exp(to_col(join_cols(ends)))
        new_states.append(states[g] * chunk_decay + mm_tn(join_cols(to_end), b_g[g]))
    return new_states, [join_cols(y_pairs)]


def m2_post_fn(states, toks, consts):
    return [], [rmsnorm(toks[0] * jax.nn.silu(toks[1]), consts[0])]


@jax.custom_vjp
def nilpotent_inverse(n_mat):
    size = n_mat.shape[0]
    eye = lax.broadcasted_iota(jnp.int32, n_mat.shape, 0) == lax.broadcasted_iota(jnp.int32, n_mat.shape, 1)
    inv = jnp.where(eye, 1.0, 0.0) + n_mat
    power = n_mat
    d = 2
    while d < size:
        power = _dot_bf16x3(power, power, ((1,), (0,)))
        inv = inv + _dot_bf16x3(inv, power, ((1,), (0,)))
        d *= 2
    return inv


def _nilpotent_inverse_fwd(n_mat):
    inv = nilpotent_inverse(n_mat)
    return inv, inv


def _nilpotent_inverse_bwd(inv, g):
    return (_dot_bf16x3(inv, _dot_bf16x3(g, inv, ((1,), (1,))), ((0,), (0,))),)


nilpotent_inverse.defvjp(_nilpotent_inverse_fwd, _nilpotent_inverse_bwd)


def gdn_head(state, q, k, v, gate, a_raw, b_raw, dt_bias, a_log, norm_g):
    n = q.shape[0]
    qn = q * lax.rsqrt(jnp.sum(q * q, axis=-1, keepdims=True) + EPS) * (GDN_HEAD_DIM ** -0.5)
    kn = k * lax.rsqrt(jnp.sum(k * k, axis=-1, keepdims=True) + EPS)
    beta = jax.nn.sigmoid(b_raw)
    g = -jnp.exp(a_log) * jax.nn.softplus(a_raw + dt_bias)
    cs, decay = causal_decay(g)
    row = lax.broadcasted_iota(jnp.int32, (n, n), 0)
    col = lax.broadcasted_iota(jnp.int32, (n, n), 1)
    kb = kn * beta
    inv = nilpotent_inverse(-jnp.where(row > col, mm_nt(kb, kn) * decay, 0.0))
    ecs = jnp.exp(cs)
    u = mmh(inv, v * beta)
    w = mmh(inv, kb * ecs)
    qk = mm_nt(qn, kn) * decay
    cs_end = jnp.sum(g, axis=0, keepdims=True)
    v_new = u - mm(w, state)
    o = mm(qn * ecs, state) + mm(qk, v_new)
    new_state = state * jnp.exp(cs_end) + mm_tn(kn * jnp.exp(cs_end - cs), v_new)
    return new_state, rmsnorm(o, norm_g) * jax.nn.silu(gate)


def gdn_fn(states, toks, consts):
    q, k, v, gate, small = toks
    dt_bias, a_log, norm_g = consts
    heads = (GDN_HEAD_DIM,) * GDN_HEADS
    qs, ks, vs, gs = split_cols(q, heads), split_cols(k, heads), split_cols(v, heads), split_cols(gate, heads)
    new_states, ys = [], []
    for h in range(GDN_HEADS):
        st, y = gdn_head(states[h], qs[h], ks[h], vs[h], gs[h], lane_col(small, GDN_HEADS + h), lane_col(small, h),
                         lane_scalar(dt_bias, h), lane_scalar(a_log, h), norm_g)
        new_states.append(st)
        ys.append(y)
    return new_states, [join_cols(ys)]


def adamw_fn(states, toks, consts):
    w, g, m, v = toks
    m = ADAM_B1 * m + (1.0 - ADAM_B1) * g
    v = ADAM_B2 * v + (1.0 - ADAM_B2) * (g * g)
    m_hat = m / (1.0 - ADAM_B1 ** ADAM_STEP)
    v_hat = v / (1.0 - ADAM_B2 ** ADAM_STEP)
    return [], [-ADAM_LR * (m_hat / (jnp.sqrt(v_hat) + ADAM_EPS) + ADAM_WD * w), m, v]


def tok_op(name, fn, t, nblk, tok, consts, outs, states=()):
    nblk = min(nblk, t)
    tok_args = [rows(e[0], nblk, dtype=e[1], grad=e[2] if len(e) > 2 else True) for e in tok]
    const_args = [whole(s) for s in consts]
    out_args = [rows(w, nblk, dtype=dt) for (w, dt) in outs]
    return blocked_op(name, fn, 1, t // nblk, tok_args, const_args, out_args, [(t, w) for (w, _) in outs], states)


def const_op(name, fn, in_shapes, out_shapes):
    return blocked_op(name, fn, 1, 1, [], [whole(s) for s in in_shapes], [whole(s) for s in out_shapes], list(out_shapes))


W = BRANCH_WIDTH
SCAN_ROWS = 128
ROW_BLOCK = 256


def s5_mixer(tag, t, u, p):
    col = (S5_W, 1)
    disc = const_op("s5_disc" + tag, s5_discretise_fn, [col, col, col, (S5_W, 16), (S5_W, 16)], [col, col, (S5_W, 16), (S5_W, 16)])
    ab_re, ab_im, bb_re, bb_im = disc([], [
        jnp.repeat(p["s5_log_step"], S5_STATE).reshape(col), p["s5_a_re"].reshape(col), p["s5_a_im"].reshape(col),
        p["s5_b_re"].reshape(S5_W, S5_GROUP_CH), p["s5_b_im"].reshape(S5_W, S5_GROUP_CH)])
    eye = jnp.eye(S5_GROUPS, dtype=F32)

    def block_in(bb):
        return jnp.einsum("gpc,gh->gchp", bb.reshape(S5_GROUPS, S5_STATE, S5_GROUP_CH), eye).reshape(W, S5_W)

    def block_out(c):
        return jnp.einsum("gcp,gh->gphc", c, eye).reshape(S5_W, W)

    w_b = jnp.concatenate([block_in(bb_re), block_in(bb_im)], axis=1)
    w_c = jnp.concatenate([block_out(p["s5_c_re"]), -block_out(p["s5_c_im"])], axis=0)
    bu = dense("s5_b" + tag)(u, w_b)
    h = s5_scan("s5_scan" + tag, t, bu, ab_re.reshape(1, S5_W), ab_im.reshape(1, S5_W))
    yc = dense("s5_c" + tag)(h, w_c)
    glu = tok_op("s5_glu" + tag, s5_glu_fn, t, ROW_BLOCK, [(W, F32), (W, F32)], [(1, W), (W, W), (1, W)], [(W, BF16)])
    return glu([yc, u], [p["s5_d"].reshape(1, W), p["s5_w_glu"], p["s5_b_glu"][None]])[0]


def lru_mixer(tag, t, x, gate, p):
    def block_diag(w):
        return jnp.einsum("hij,hk->hikj", w, jnp.eye(LRU_HEADS, dtype=F32)).reshape(W, W)

    op = tok_op("lru" + tag, lru_fn, t, SCAN_ROWS, [(W, F32), (W, F32)],
                [(4, W), (1, W), (W, W), (1, W), (W, W), (1, W), (1, W)], [(W, BF16)], states=[(1, W), (TAIL, W)])
    return op([x, gate], [p["lru_conv_w"], p["lru_conv_b"][None], block_diag(p["lru_w_r"]), p["lru_b_r"][None],
                          block_diag(p["lru_w_i"]), p["lru_b_i"][None], p["lru_lambda"][None]])[0]


def m2_mixer(tag, t, z, xbc, small, p):
    cw = 2 * W
    conv = tok_op("m2_conv" + tag, m2_conv_fn, t, ROW_BLOCK, [(cw, F32)], [(4, cw), (1, cw)],
                  [(W, F32), (W // 2, F32), (W // 2, F32)], states=[(TAIL, cw)])
    xs, bm, cm = conv([xbc], [p["m2_conv_w"], p["m2_conv_b"][None]])
    ssd = tok_op("ssd" + tag, ssd_fn, t, CHUNK, [(W, F32), (W // 2, F32), (W // 2, F32), (LANE, F32)],
                 [(1, M2_HEADS)] * 3, [(W, F32)], states=[(4 * M2_HEAD_DIM, M2_STATE)] * M2_GROUPS)
    y = ssd([xs, bm, cm, small], [p["m2_dt_bias"][None], p["m2_a_log"][None], p["m2_d"][None]])[0]
    post = tok_op("m2_post" + tag, m2_post_fn, t, ROW_BLOCK, [(W, F32), (W, F32)], [(1, W)], [(W, BF16)])
    return post([y, z], [p["m2_norm"][None]])[0]


def gdn_mixer(tag, t, qkv, gate, small, p):
    conv = tok_op("gdn_conv" + tag, gdn_conv_fn, t, ROW_BLOCK, [(3 * W, F32)], [(4, 3 * W)], [(W, F32)] * 3, states=[(TAIL, 3 * W)])
    q, k, v = conv([qkv], [p["gdn_conv_w"]])
    op = tok_op("gdn" + tag, gdn_fn, t, CHUNK, [(W, F32)] * 4 + [(LANE, F32)], [(1, GDN_HEADS), (1, GDN_HEADS), (1, GDN_HEAD_DIM)],
                [(W, BF16)], states=[(GDN_HEAD_DIM, GDN_HEAD_DIM)] * GDN_HEADS)
    return op([q, k, v, gate, small], [p["gdn_dt_bias"][None], p["gdn_a_log"][None], p["gdn_norm"][None]])[0]


WEIGHTS = ["ffn1_norm", "ffn1_w_in", "ffn1_w_out", "mix_norm", "w_in", "w_gate", "b_gate", "s5_log_step", "s5_a_re",
           "s5_a_im", "s5_b_re", "s5_b_im", "s5_c_re", "s5_c_im", "s5_d", "s5_w_glu", "s5_b_glu", "lru_conv_w",
           "lru_conv_b", "lru_w_r", "lru_b_r", "lru_w_i", "lru_b_i", "lru_lambda", "m2_conv_w", "m2_conv_b", "m2_dt_bias",
           "m2_a_log", "m2_d", "m2_norm", "gdn_conv_w", "gdn_dt_bias", "gdn_a_log", "gdn_norm", "w_branch", "w_out",
           "ffn2_norm", "ffn2_w_in", "ffn2_w_out", "ple_norm", "ple_w_gate", "ple_w_proj", "final_norm"]
N_CHIPS = 4
N_DEV = 8
IN_WIDTH = 5136
SHARDED = {
    "ffn1_w_in": ((D_MODEL, 2 * FFN_DIM // N_CHIPS), 1, False),
    "ffn1_w_out": ((FFN_DIM // N_CHIPS, D_MODEL), 0, False),
    "w_in": ((D_MODEL, IN_WIDTH // N_CHIPS), 1, False),
    "w_gate": ((D_MODEL, N_BRANCH * D_MODEL // N_CHIPS), 1, False),
    "s5_w_glu": ((W // N_CHIPS, W), 0, False),
    "lru_conv_w": ((4, W // N_CHIPS), 1, True),
    "m2_conv_w": ((4, 2 * W // N_CHIPS), 1, True),
    "gdn_conv_w": ((4, 3 * W // N_CHIPS), 1, True),
    "w_branch": ((N_BRANCH, W, D_MODEL // N_CHIPS), 2, False),
    "w_out": ((D_MODEL // N_CHIPS, D_MODEL), 0, False),
    "ffn2_w_in": ((D_MODEL, 2 * FFN_DIM // N_CHIPS), 1, False),
    "ffn2_w_out": ((FFN_DIM // N_CHIPS, D_MODEL), 0, False),
    "ple_w_gate": ((D_MODEL // N_CHIPS, D_MODEL), 0, False),
    "ple_w_proj": ((256, D_MODEL // N_CHIPS), 1, False),
}
SMALL = [n for n in WEIGHTS if n not in SHARDED]
ROW = 1024


def _count(shape):
    return math.prod(shape)


def _round_up(n, unit):
    return -(-n // unit) * unit


N_GATHER = sum(DEPTH * _count(s) * (2 if exact else 1) for s, _, exact in SHARDED.values())
N_GRAD = sum(DEPTH * _count(s) for s, _, _ in SHARDED.values())
GATHER_ROWS = _round_up(-(-N_GATHER // ROW), 32)
GRAD_ROWS = _round_up(-(-N_GRAD // ROW), 32)
GRAD_HALF = GRAD_ROWS // 2
IN_PIECES = [(0, 512), (512, 512), (1024, 512), (1536, 512), (2048, 1024), (3072, 8), (3080, 1536), (4616, 512), (5128, 8)]


COL_SHARDED = ("ffn1_w_in", "ffn2_w_in", "w_gate", "ple_w_proj", "w_branch")
IN_SHARD = IN_WIDTH // N_CHIPS


def _w_in_cuts():
    cuts = {0, IN_SHARD}
    for s in range(N_CHIPS):
        for start, _ in IN_PIECES:
            if s * IN_SHARD < start < (s + 1) * IN_SHARD:
                cuts.add(start - s * IN_SHARD)
    return sorted(cuts)


IN_CUTS = _w_in_cuts()
IN_BLOCKS = list(zip(IN_CUTS[:-1], IN_CUTS[1:]))


def _piece_of(col):
    for k, (start, n) in enumerate(IN_PIECES):
        if start <= col < start + n:
            return k, col - start
    raise ValueError(col)


def cut_w_in(w):
    return jnp.concatenate([w[:, :, lo:hi].reshape(-1) for lo, hi in IN_BLOCKS])


def uncut_w_in(flat):
    blocks, off = [], 0
    for lo, hi in IN_BLOCKS:
        cnt = DEPTH * D_MODEL * (hi - lo)
        blocks.append(flat[off:off + cnt].reshape(DEPTH, D_MODEL, hi - lo))
        off += cnt
    return jnp.concatenate(blocks, axis=2)


def pack_for_gather(a):
    parts = []
    for n, (_, _, exact) in SHARDED.items():
        w = cut_w_in(a[n]) if n == "w_in" else a[n]
        parts.append((lax.bitcast_convert_type(w, BF16) if exact else w.astype(BF16)).reshape(-1))
    flat = jnp.concatenate(parts)
    return jnp.pad(flat, (0, GATHER_ROWS * ROW - flat.shape[0])).reshape(GATHER_ROWS, ROW)


def unpack_gathered(buf):
    flat = buf.reshape(N_CHIPS, -1)
    out, off = {}, 0
    for n, (shape, ax, exact) in SHARDED.items():
        cnt = DEPTH * _count(shape) * (2 if exact else 1)
        piece = flat[:, off:off + cnt]
        off += cnt
        if n == "w_in":
            cols = [[[] for _ in IN_PIECES] for _ in range(DEPTH)]
            for s in range(N_CHIPS):
                o = 0
                for lo, hi in IN_BLOCKS:
                    c = DEPTH * D_MODEL * (hi - lo)
                    blk = piece[s, o:o + c].reshape(DEPTH, D_MODEL, hi - lo)
                    o += c
                    k, _ = _piece_of(s * IN_SHARD + lo)
                    for layer in range(DEPTH):
                        cols[layer][k].append(blk[layer])
            out[n] = [[_pad_lanes(jnp.concatenate(c, axis=1)) for c in cols[layer]] for layer in range(DEPTH)]
        elif n in COL_SHARDED:
            out[n] = piece.reshape(N_CHIPS, DEPTH, *shape)
        else:
            if exact:
                w = lax.bitcast_convert_type(piece.reshape(N_CHIPS, DEPTH, *shape, 2), F32)
            else:
                w = piece.reshape(N_CHIPS, DEPTH, *shape)
            full = list(shape)
            full[ax] *= N_CHIPS
            out[n] = jnp.moveaxis(w, 0, ax + 1).reshape(DEPTH, *full)
    return out


def _pad_lanes(w):
    n = w.shape[1]
    return w if n % LANE == 0 else jnp.pad(w, ((0, 0), (0, LANE - n % LANE)))


def shard_w_in_grads(pieces):
    shards = []
    for s in range(N_CHIPS):
        parts = []
        for lo, hi in IN_BLOCKS:
            k, dst = _piece_of(s * IN_SHARD + lo)
            parts.append(jnp.stack([pieces[layer][k][:, dst:dst + hi - lo] for layer in range(DEPTH)]).reshape(-1))
        shards.append(jnp.concatenate(parts))
    return jnp.stack(shards)


def pack_grads(g):
    tail = jnp.zeros((N_CHIPS, GRAD_ROWS * ROW - N_GRAD), F32)
    return jnp.concatenate([g[n] for n in SHARDED] + [tail], axis=1).reshape(N_CHIPS, GRAD_ROWS, ROW)


def shard_full_grads(n, per_layer):
    shape, ax, _ = SHARDED[n]
    parts = []
    for full in per_layer:
        w = full.reshape(*full.shape[:ax], N_CHIPS, shape[ax], *full.shape[ax + 1:])
        parts.append(jnp.moveaxis(w, ax, 0).reshape(N_CHIPS, -1))
    return jnp.concatenate(parts, axis=1)


def unpack_shard(buf):
    flat = buf.reshape(-1)
    out, off = {}, 0
    for n, (shape, _, _) in SHARDED.items():
        cnt = DEPTH * _count(shape)
        piece = flat[off:off + cnt]
        out[n] = uncut_w_in(piece) if n == "w_in" else piece.reshape(DEPTH, *shape)
        off += cnt
    return out


def pack_small(a, prefix, shapes):
    flat = jnp.concatenate([a[prefix + n].reshape(-1) for n in SMALL])
    rows_n = _round_up(-(-flat.shape[0] // ROW), 8)
    return jnp.pad(flat, (0, rows_n * ROW - flat.shape[0])).reshape(rows_n, ROW)


def unpack_small(buf, shapes):
    flat = buf.reshape(-1)
    out, off = {}, 0
    for n in SMALL:
        cnt = _count(shapes[n])
        out[n] = flat[off:off + cnt].reshape(shapes[n])
        off += cnt
    return out


ANY = pl.BlockSpec(memory_space=pl.ANY)


def _position():
    return lax.axis_index("x"), lax.axis_index("y"), lax.axis_index("c")


def _other_chips(x, y):
    return [(1 - x, y), (x, 1 - y), (1 - x, 1 - y)]


PLACE_ROWS = 592


def gather_weights(packed, slot):
    r = packed.shape[0]
    half = r // 2
    nblk = r // PLACE_ROWS

    def place(s_ref, in_ref, o_ref):
        o_ref[...] = in_ref[...]

    slots = pl.pallas_call(
        place, name="place_shard",
        grid_spec=pltpu.PrefetchScalarGridSpec(
            num_scalar_prefetch=1, grid=(nblk,), in_specs=[pl.BlockSpec((PLACE_ROWS, ROW), lambda i, s: (i, 0))],
            out_specs=pl.BlockSpec((None, PLACE_ROWS, ROW), lambda i, s: (s[0], i, 0))),
        out_shape=jax.ShapeDtypeStruct((N_CHIPS, r, ROW), packed.dtype), compiler_params=_params(("arbitrary",)),
    )(slot, packed)

    def body(in_ref, out_ref, send_sems, recv_sems):
        x, y, c = _position()
        sibling = (x, y, 1 - c)
        chips = _other_chips(x, y)

        def half_rows(px, py, pc):
            return out_ref.at[2 * px + py, pl.ds(pl.multiple_of(pc * half, 16), half), :]

        def copy(k, block, to):
            return pltpu.make_async_remote_copy(
                src_ref=half_rows(*block), dst_ref=half_rows(*block),
                send_sem=send_sems.at[k], recv_sem=recv_sems.at[k], device_id=to, device_id_type=MESH)

        first = [copy(j, (x, y, c), (*chip, c)) for j, chip in enumerate(chips)]
        for cp in first:
            cp.start()
        passed = [copy(3 + j, (*chip, c), sibling) for j, chip in enumerate(chips)]
        for j, chip in enumerate(chips):
            copy(j, (*chip, c), (x, y, c)).wait_recv()
            passed[j].start()
        for j, chip in enumerate(chips):
            copy(3 + j, (*chip, 1 - c), (x, y, c)).wait_recv()
        for cp in first + passed:
            cp.wait_send()

    return pl.pallas_call(
        body, name="gather_weights", in_specs=[ANY], out_specs=ANY, input_output_aliases={0: 0},
        out_shape=jax.ShapeDtypeStruct((N_CHIPS, r, ROW), packed.dtype),
        scratch_shapes=[pltpu.SemaphoreType.DMA((6,)), pltpu.SemaphoreType.DMA((6,))],
    )(slots)


def swap_pair_halves(g):
    half = g.shape[1] // 2

    def body(g_ref, land_ref, send_sem, recv_sem):
        x, y, c = _position()
        src = g_ref.at[:, pl.ds(pl.multiple_of((1 - c) * half, 8), half), :]
        cp = pltpu.make_async_remote_copy(src_ref=src, dst_ref=land_ref, send_sem=send_sem, recv_sem=recv_sem,
                                          device_id=(x, y, 1 - c), device_id_type=MESH)
        cp.start()
        cp.wait()

    return pl.pallas_call(
        body, name="swap_pair_halves", in_specs=[ANY], out_specs=ANY,
        out_shape=jax.ShapeDtypeStruct((N_CHIPS, half, ROW), g.dtype),
        scratch_shapes=[pltpu.SemaphoreType.DMA, pltpu.SemaphoreType.DMA],
    )(g)


def exchange_chip_partials(part):
    half = part.shape[1]

    def body(p_ref, land_ref, send_sems, recv_sems):
        x, y, c = _position()
        cps = [pltpu.make_async_remote_copy(src_ref=p_ref.at[2 * px + py], dst_ref=land_ref.at[j], send_sem=send_sems.at[j],
                                            recv_sem=recv_sems.at[j], device_id=(px, py, c), device_id_type=MESH)
               for j, (px, py) in enumerate(_other_chips(x, y))]
        for cp in cps:
            cp.start()
        for cp in cps:
            cp.wait()

    return pl.pallas_call(
        body, name="exchange_chip_partials", in_specs=[ANY], out_specs=ANY,
        out_shape=jax.ShapeDtypeStruct((3, half, ROW), part.dtype),
        scratch_shapes=[pltpu.SemaphoreType.DMA((3,)), pltpu.SemaphoreType.DMA((3,))],
    )(part)


def share_halves(both):
    half = both.shape[0] // 2

    def body(in_ref, out_ref, send_sem, recv_sem):
        x, y, c = _position()
        my_rows = out_ref.at[pl.ds(pl.multiple_of(c * half, 8), half), :]
        cp = pltpu.make_async_remote_copy(src_ref=my_rows, dst_ref=my_rows, send_sem=send_sem, recv_sem=recv_sem,
                                          device_id=(x, y, 1 - c), device_id_type=MESH)
        cp.start()
        cp.wait()

    return pl.pallas_call(
        body, name="share_halves", in_specs=[ANY], out_specs=ANY, input_output_aliases={0: 0},
        out_shape=jax.ShapeDtypeStruct(both.shape, both.dtype),
        scratch_shapes=[pltpu.SemaphoreType.DMA, pltpu.SemaphoreType.DMA],
    )(both)


def gather_all(block):
    m_per = block.shape[0]

    def body(x_ref, out_ref, send_sems, recv_sems, local_sem):
        x, y, c = _position()
        me, sibling = (x, y, c), (x, y, 1 - c)
        chips = _other_chips(x, y)

        def rows_of(px, py, pc):
            return out_ref.at[pl.ds(pl.multiple_of((4 * px + 2 * py + pc) * m_per, 8), m_per), :]

        def copy(k, blk, to, src=None):
            return pltpu.make_async_remote_copy(
                src_ref=rows_of(*blk) if src is None else src, dst_ref=rows_of(*blk),
                send_sem=send_sems.at[k], recv_sem=recv_sems.at[k], device_id=to, device_id_type=MESH)

        mine = pltpu.make_async_copy(x_ref, rows_of(*me), local_sem)
        mine.start()
        first = [copy(0, me, sibling, src=x_ref)]
        first += [copy(1 + j, me, (*chip, c), src=x_ref) for j, chip in enumerate(chips)]
        for cp in first:
            cp.start()
        passed = [copy(4 + j, (*chip, c), sibling) for j, chip in enumerate(chips)]
        for j, chip in enumerate(chips):
            copy(1 + j, (*chip, c), me).wait_recv()
            passed[j].start()
        copy(0, sibling, me).wait_recv()
        for j, chip in enumerate(chips):
            copy(4 + j, (*chip, 1 - c), me).wait_recv()
        for cp in first + passed:
            cp.wait_send()
        mine.wait()

    return pl.pallas_call(
        body, name="gather_all", out_shape=jax.ShapeDtypeStruct((N_DEV * m_per, ROW), block.dtype),
        in_specs=[pl.BlockSpec(memory_space=pltpu.VMEM)], out_specs=pl.BlockSpec(memory_space=pltpu.VMEM),
        scratch_shapes=[pltpu.SemaphoreType.DMA((7,)), pltpu.SemaphoreType.DMA((7,)), pltpu.SemaphoreType.DMA],
        compiler_params=_params(),
    )(block)


SUM_ROWS = 592


def add_pair_halves(grads, landed, core):
    half = landed.shape[1]
    nblk = half // SUM_ROWS

    def body(c_ref, g_ref, l_ref, o_ref, o16_ref):
        acc = g_ref[...] + l_ref[...]
        o_ref[...] = acc
        o16_ref[...] = acc.astype(BF16)

    blk = (None, SUM_ROWS, ROW)
    o_spec = pl.BlockSpec(blk, lambda s, i, c: (s, i, 0))
    return pl.pallas_call(
        body, name="add_pair_halves",
        grid_spec=pltpu.PrefetchScalarGridSpec(
            num_scalar_prefetch=1, grid=(N_CHIPS, nblk),
            in_specs=[pl.BlockSpec(blk, lambda s, i, c: (s, c[0] * nblk + i, 0)), o_spec], out_specs=[o_spec, o_spec]),
        out_shape=[jax.ShapeDtypeStruct(landed.shape, F32), jax.ShapeDtypeStruct(landed.shape, BF16)],
        compiler_params=_params(("arbitrary", "arbitrary")),
    )(core, grads, landed)


def add_chip_partials(part, landed, slot, core):
    half = part.shape[1]
    nblk = half // SUM_ROWS

    def body(s_ref, c_ref, p_ref, l_ref, o_ref):
        o_ref[...] = ((p_ref[...] + l_ref[0].astype(F32)) + l_ref[1].astype(F32)) + l_ref[2].astype(F32)

    return pl.pallas_call(
        body, name="add_chip_partials",
        grid_spec=pltpu.PrefetchScalarGridSpec(
            num_scalar_prefetch=2, grid=(nblk,),
            in_specs=[pl.BlockSpec((None, SUM_ROWS, ROW), lambda i, s, c: (s[0], i, 0)), pl.BlockSpec((3, SUM_ROWS, ROW), lambda i, s, c: (0, i, 0))],
            out_specs=pl.BlockSpec((SUM_ROWS, ROW), lambda i, s, c: (c[0] * nblk + i, 0))),
        out_shape=jax.ShapeDtypeStruct((2 * half, ROW), F32), compiler_params=_params(("arbitrary",)),
    )(slot, core, part, landed)


def sum_devices(stacked):
    m = stacked.shape[1]

    def body(s_ref, o_ref):
        acc = s_ref[0]
        for d in range(1, N_DEV):
            acc = acc + s_ref[d]
        o_ref[...] = acc

    return pl.pallas_call(
        body, name="sum_devices", grid=(m // 8,), in_specs=[pl.BlockSpec((N_DEV, 8, ROW), lambda i: (0, i, 0))],
        out_specs=pl.BlockSpec((8, ROW), lambda i: (i, 0)), out_shape=jax.ShapeDtypeStruct((m, ROW), F32),
        compiler_params=_params(("arbitrary",)),
    )(stacked)


def adamw(name, w, g, m, v):
    width = w.shape[-1]
    n_rows = w.size // width
    nblk = _tile(n_rows, 256, 8)
    arg = rows(width, nblk)
    outs, _ = blocked_forward("adamw_" + name, adamw_fn, 1, n_rows // nblk, [arg] * 4, [t.reshape(n_rows, width) for t in (w, g, m, v)],
                              [], [], [arg] * 3, [(n_rows, width)] * 3, (), False)
    return [o.reshape(w.shape) for o in outs]


def trunk_loss(diff, p_emb, target, w16):
    x, small, gw, hd = diff["x"], diff["small"], diff["gw"], diff["hd"]
    t = x.shape[0]
    d = D_MODEL

    def norm_pair(name, fn, h, o, gain):
        op = tok_op(name, fn, t, 512, [(d, F32), (d, F32)], [(1, d)], [(d, F32), (d, BF16)])
        return op([h, o], [gain[None]])

    def ffn(tag, n, which, i):
        z = linear(f"{which}_in{tag}", BF16, (i,), N_CHIPS)(n, w16[which + "_w_in"], hd[which + "_w_in"][i])
        act = tok_op(f"{which}_act{tag}", swiglu_fn, t, 128, [(2 * FFN_DIM, BF16)], [], [(FFN_DIM, BF16)])([z], [])[0]
        return linear(f"{which}_out{tag}", F32, (i,))(act, w16[which + "_w_out"], hd[which + "_w_out"][i])

    h = x
    n = tok_op("norm_in", norm_fn, t, 512, [(d, F32)], [(1, d)], [(d, BF16)])([x], [small["ffn1_norm"][0][None]])[0]
    loss_rows = None
    for i in range(DEPTH):
        tag = str(i)
        p = {k: v[i] for k, v in small.items() if k != "final_norm"}
        p.update({k: v[i] for k, v in gw.items()})
        o = ffn(tag, n, "ffn1", i)
        h, u = norm_pair("mix_norm" + tag, make_addnorm_fn(0.5), h, o, p["mix_norm"])
        n_in = len(IN_PIECES)
        in_proj = multi_linear("in_proj" + tag, [F32] * n_in + [BF16], [None] * n_in + [(i,)], [0] * n_in + [N_CHIPS])
        proj = in_proj(u, w16["w_in"][i] + [w16["w_gate"]], hd["in_proj"][i])
        s5_u, lru_x, lru_g, m2_z, m2_xbc, m2_dt, gdn_qkv, gdn_g, gdn_ba, gate_logits = proj
        ys = [s5_mixer(tag, t, s5_u, p), lru_mixer(tag, t, lru_x, lru_g, p),
              m2_mixer(tag, t, m2_z, m2_xbc, m2_dt, p), gdn_mixer(tag, t, gdn_qkv, gdn_g, gdn_ba, p)]
        yb = [linear(f"branch{b}_{tag}", BF16, (i, b), N_CHIPS)(y, w16["w_branch"], hd["w_branch"][i][b]) for b, y in enumerate(ys)]
        merge = tok_op("gate_merge" + tag, gate_merge_fn, t, 128, [(N_BRANCH * d, BF16)] + [(d, BF16)] * N_BRANCH,
                       [(1, N_BRANCH * d)], [(d, BF16)])
        mixed = merge([gate_logits] + yb, [p["b_gate"][None]])[0]
        o = linear("w_out" + tag, F32, (i,))(mixed, w16["w_out"], hd["w_out"][i])
        h, n = norm_pair("ffn2_norm" + tag, make_addnorm_fn(1.0), h, o, p["ffn2_norm"])
        o = ffn(tag, n, "ffn2", i)
        h, n = norm_pair("ple_norm" + tag, make_addnorm_fn(0.5), h, o, p["ple_norm"])
        pg = linear("ple_gate" + tag, F32, (i,))(n, w16["ple_w_gate"], hd["ple_w_gate"][i])
        pp = linear("ple_proj" + tag, F32, (i,), N_CHIPS)(p_emb[i], w16["ple_w_proj"], hd["ple_w_proj"][i])
        if i + 1 < DEPTH:
            op = tok_op("ple" + tag, ple_fn, t, 512, [(d, F32)] * 3, [(1, d)], [(d, F32), (d, BF16)])
            h, n = op([h, pg, pp], [small["ffn1_norm"][i + 1][None]])
        else:
            op = tok_op("ple_loss", ple_loss_fn, t, 512, [(d, F32)] * 3 + [(d, F32, False)], [(1, d)], [(1, F32)])
            loss_rows = op([h, pg, pp, target], [small["final_norm"][None]])[0]
    return jnp.sum(loss_rows)


def kernel(x, p, ffn1_norm, ffn1_w_in, ffn1_w_out, mix_norm, w_in, w_gate, b_gate, s5_log_step, s5_a_re, s5_a_im, s5_b_re, s5_b_im, s5_c_re, s5_c_im, s5_d, s5_w_glu, s5_b_glu, lru_conv_w, lru_conv_b, lru_w_r, lru_b_r, lru_w_i, lru_b_i, lru_lambda, m2_conv_w, m2_conv_b, m2_dt_bias, m2_a_log, m2_d, m2_norm, gdn_conv_w, gdn_dt_bias, gdn_a_log, gdn_norm, w_branch, w_out, ffn2_norm, ffn2_w_in, ffn2_w_out, ple_norm, ple_w_gate, ple_w_proj, final_norm, loss_target, m_ffn1_norm, m_ffn1_w_in, m_ffn1_w_out, m_mix_norm, m_w_in, m_w_gate, m_b_gate, m_s5_log_step, m_s5_a_re, m_s5_a_im, m_s5_b_re, m_s5_b_im, m_s5_c_re, m_s5_c_im, m_s5_d, m_s5_w_glu, m_s5_b_glu, m_lru_conv_w, m_lru_conv_b, m_lru_w_r, m_lru_b_r, m_lru_w_i, m_lru_b_i, m_lru_lambda, m_m2_conv_w, m_m2_conv_b, m_m2_dt_bias, m_m2_a_log, m_m2_d, m_m2_norm, m_gdn_conv_w, m_gdn_dt_bias, m_gdn_a_log, m_gdn_norm, m_w_branch, m_w_out, m_ffn2_norm, m_ffn2_w_in, m_ffn2_w_out, m_ple_norm, m_ple_w_gate, m_ple_w_proj, m_final_norm, v_ffn1_norm, v_ffn1_w_in, v_ffn1_w_out, v_mix_norm, v_w_in, v_w_gate, v_b_gate, v_s5_log_step, v_s5_a_re, v_s5_a_im, v_s5_b_re, v_s5_b_im, v_s5_c_re, v_s5_c_im, v_s5_d, v_s5_w_glu, v_s5_b_glu, v_lru_conv_w, v_lru_conv_b, v_lru_w_r, v_lru_b_r, v_lru_w_i, v_lru_b_i, v_lru_lambda, v_m2_conv_w, v_m2_conv_b, v_m2_dt_bias, v_m2_a_log, v_m2_d, v_m2_norm, v_gdn_conv_w, v_gdn_dt_bias, v_gdn_a_log, v_gdn_norm, v_w_branch, v_w_out, v_ffn2_norm, v_ffn2_w_in, v_ffn2_w_out, v_ple_norm, v_ple_w_gate, v_ple_w_proj, v_final_norm):
    a = dict(locals())
    t = x.shape[1]
    core = lax.axis_index("c").astype(jnp.int32).reshape(1)
    slot = (2 * lax.axis_index("x") + lax.axis_index("y")).astype(jnp.int32).reshape(1)

    full = unpack_gathered(gather_weights(pack_for_gather(a), slot))
    exact = [n for n, spec in SHARDED.items() if spec[2]] + ["s5_w_glu"]
    gw = {n: full[n].astype(F32) for n in exact}
    w16 = {n: full[n] for n in SHARDED if n not in exact}

    def handle(n):
        shape = SHARDED[n][0]
        return jnp.zeros((N_CHIPS, *shape) if n in COL_SHARDED else (N_CHIPS * shape[0], *shape[1:]), F32)

    hd = {n: [handle(n) for _ in range(DEPTH)] for n in w16 if n not in ("w_in", "w_gate", "w_branch")}
    hd["w_branch"] = [[jnp.zeros((N_CHIPS, W, D_MODEL // N_CHIPS), F32) for _ in range(N_BRANCH)] for _ in range(DEPTH)]
    hd["in_proj"] = [[jnp.zeros((D_MODEL, _round_up(n, LANE)), F32) for _, n in IN_PIECES] + [handle("w_gate")] for _ in range(DEPTH)]
    diff = {"x": x.reshape(t, D_MODEL), "small": {n: a[n] for n in SMALL}, "gw": gw, "hd": hd}
    loss_local, vjp = jax.vjp(lambda dd: trunk_loss(dd, p.reshape(DEPTH, t, -1), loss_target.reshape(t, D_MODEL), w16), diff)
    (grads,) = vjp(jnp.ones((), F32))
    loss = lax.psum(loss_local, ("x", "y", "c"))
    grad_x = grads["x"].reshape(x.shape)

    gh = grads["hd"]
    big = {n: jnp.concatenate([g.reshape(N_CHIPS, -1) for g in gh[n]], axis=1) for n in gh if n not in ("in_proj", "w_branch")}
    big["w_in"] = shard_w_in_grads([gh["in_proj"][i][:-1] for i in range(DEPTH)])
    big["w_gate"] = jnp.concatenate([gh["in_proj"][i][-1].reshape(N_CHIPS, -1) for i in range(DEPTH)], axis=1)
    big["w_branch"] = jnp.concatenate([g.reshape(N_CHIPS, -1) for i in range(DEPTH) for g in gh["w_branch"][i]], axis=1)
    for n in exact:
        big[n] = shard_full_grads(n, [grads["gw"][n][i] for i in range(DEPTH)])

    packed = pack_grads(big)
    pair, pair16 = add_pair_halves(packed, swap_pair_halves(packed), core)
    mine = add_chip_partials(pair, exchange_chip_partials(pair16), slot, core)
    g_shard = unpack_shard(share_halves(mine))

    shapes = {n: a[n].shape for n in SMALL}
    gs_local = pack_small({n: grads["small"][n] for n in SMALL}, "", shapes)
    gs = gather_all(gs_local)
    g_small = sum_devices(gs.reshape(N_DEV, gs_local.shape[0], ROW))

    g_all = {**unpack_small(g_small, shapes), **g_shard}
    res = {n: [g_all[n]] + adamw(n, a[n], g_all[n], a["m_" + n], a["v_" + n]) for n in WEIGHTS}
    return (loss, grad_x, *[res[n][0] for n in WEIGHTS], *[res[n][1] for n in WEIGHTS],
            *[res[n][2] for n in WEIGHTS], *[res[n][3] for n in WEIGHTS])
```

```python
import functools
import math

import jax
import jax.numpy as jnp
from jax import lax
from jax.experimental import pallas as pl
from jax.experimental.pallas import tpu as pltpu

F32, BF16 = jnp.float32, jnp.bfloat16
ACT = F32
EPS = 1e-6
D_MODEL = 1024
DEPTH = 2
FFN_DIM = 2816
BRANCH_WIDTH = 512
N_BRANCH = 4
LRU_C = 8.0
S5_GROUPS, S5_GROUP_CH, S5_STATE = 32, 16, 64
S5_W = S5_GROUPS * S5_STATE
LRU_HEADS, LRU_HEAD_DIM = 8, 64
M2_HEADS, M2_HEAD_DIM, M2_GROUPS, M2_STATE = 8, 64, 2, 128
GDN_HEADS, GDN_HEAD_DIM = 4, 128
CHUNK = 128
ADAM_LR, ADAM_B1, ADAM_B2, ADAM_EPS, ADAM_WD, ADAM_STEP = 0.001, 0.9, 0.999, 1e-08, 0.01, 10
VMEM_LIMIT_BYTES = 56 * 1024 * 1024
MESH = pl.DeviceIdType.MESH


def _params(sem=None):
    return pltpu.CompilerParams(vmem_limit_bytes=VMEM_LIMIT_BYTES, dimension_semantics=sem)


def _dot_bf16(a, b, dims):
    return lax.dot_general(a.astype(BF16), b.astype(BF16), (dims, ((), ())), preferred_element_type=F32)


def _make_mm(dot):
    @jax.custom_vjp
    def nn(a, b):
        return dot(a, b, ((1,), (0,)))

    @jax.custom_vjp
    def nt(a, b):
        return dot(a, b, ((1,), (1,)))

    @jax.custom_vjp
    def tn(a, b):
        return dot(a, b, ((0,), (0,)))

    nn.defvjp(lambda a, b: (nn(a, b), (a, b)), lambda r, g: (nt(g, r[1]), tn(r[0], g)))
    nt.defvjp(lambda a, b: (nt(a, b), (a, b)), lambda r, g: (nn(g, r[1]), tn(g, r[0])))
    tn.defvjp(lambda a, b: (tn(a, b), (a, b)), lambda r, g: (nt(r[1], g), nn(r[0], g)))
    return nn, nt, tn


def _dot_bf16x3(a, b, dims):
    a_hi, b_hi = a.astype(BF16), b.astype(BF16)
    a_lo = (a - a_hi.astype(F32)).astype(BF16)
    b_lo = (b - b_hi.astype(F32)).astype(BF16)

    def dot(p, q):
        return lax.dot_general(p, q, (dims, ((), ())), preferred_element_type=F32)

    return dot(a_hi, b_hi) + (dot(a_hi, b_lo) + dot(a_lo, b_hi))


mm, mm_nt, mm_tn = _make_mm(_dot_bf16)
mmh, mmh_nt, mmh_tn = _make_mm(_dot_bf16x3)


def _row_ids(shape):
    return lax.broadcasted_iota(jnp.int32, shape, 0)


def _shift_down(x, d):
    return jnp.where(_row_ids(x.shape) >= d, pltpu.roll(x, d, 0), 0.0)


def _shift_up(x, d):
    n = x.shape[0]
    return jnp.where(_row_ids(x.shape) < n - d, pltpu.roll(x, n - d, 0), 0.0)


def _first_row(x):
    return jnp.sum(jnp.where(_row_ids(x.shape) == 0, x, 0.0), axis=0, keepdims=True)


def last_row(x):
    return jnp.sum(jnp.where(_row_ids(x.shape) == x.shape[0] - 1, x, 0.0), axis=0, keepdims=True)


def pick_row(x, j):
    return jnp.sum(jnp.where(_row_ids(x.shape) == j, x, 0.0), axis=0, keepdims=True)


@jax.custom_vjp
def lin_scan(a, b, h0):
    n = a.shape[0]
    row = _row_ids(a.shape)
    acc_a = a
    acc_b = b + jnp.where(row == 0, a * h0, 0.0)
    d = 1
    while d < n:
        acc_b = acc_a * _shift_down(acc_b, d) + acc_b
        acc_a = acc_a * jnp.where(row >= d, pltpu.roll(acc_a, d, 0), 1.0)
        d *= 2
    return acc_b


def _lin_scan_fwd(a, b, h0):
    h = lin_scan(a, b, h0)
    return h, (a, h, h0)


def _lin_scan_bwd(res, dh):
    a, h, h0 = res
    n = a.shape[0]
    row = _row_ids(a.shape)
    acc_a = _shift_up(a, 1)
    g = dh
    d = 1
    while d < n:
        g = acc_a * _shift_up(g, d) + g
        acc_a = acc_a * jnp.where(row < n - d, pltpu.roll(acc_a, n - d, 0), 1.0)
        d *= 2
    h_prev = _shift_down(h, 1) + jnp.where(row == 0, h0, 0.0)
    return g * h_prev, g, _first_row(a * g)


lin_scan.defvjp(_lin_scan_fwd, _lin_scan_bwd)


def _cscan(br, bi, ar, ai, up):
    n = br.shape[0]
    shift = _shift_up if up else _shift_down
    hr, hi, pr, pi = br, bi, ar, ai
    d = 1
    while d < n:
        sr, si = shift(hr, d), shift(hi, d)
        hr, hi = hr + pr * sr - pi * si, hi + pr * si + pi * sr
        pr, pi = pr * pr - pi * pi, 2.0 * pr * pi
        d *= 2
    return hr, hi


@jax.custom_vjp
def complex_scan(br, bi, ar, ai, h0r, h0i):
    first = _row_ids(br.shape) == 0
    br = br + jnp.where(first, ar * h0r - ai * h0i, 0.0)
    bi = bi + jnp.where(first, ar * h0i + ai * h0r, 0.0)
    return _cscan(br, bi, ar, ai, False)


def _complex_scan_fwd(br, bi, ar, ai, h0r, h0i):
    hr, hi = complex_scan(br, bi, ar, ai, h0r, h0i)
    return (hr, hi), (ar, ai, hr, hi, h0r, h0i)


def _complex_scan_bwd(res, cts):
    ar, ai, hr, hi, h0r, h0i = res
    gr, gi = _cscan(cts[0], cts[1], ar, -ai, True)
    first = _row_ids(hr.shape) == 0
    pr = _shift_down(hr, 1) + jnp.where(first, h0r, 0.0)
    pi = _shift_down(hi, 1) + jnp.where(first, h0i, 0.0)
    d_ar = jnp.sum(gr * pr + gi * pi, axis=0, keepdims=True)
    d_ai = jnp.sum(gi * pr - gr * pi, axis=0, keepdims=True)
    g0r, g0i = _first_row(gr), _first_row(gi)
    return gr, gi, d_ar, d_ai, ar * g0r + ai * g0i, ar * g0i - ai * g0r


complex_scan.defvjp(_complex_scan_fwd, _complex_scan_bwd)

TAIL = 8


@jax.custom_vjp
def tail_rows(x):
    return x[x.shape[0] - TAIL:, :]


tail_rows.defvjp(
    lambda x: (tail_rows(x), x.shape[0]),
    lambda n, g: (jnp.concatenate([jnp.zeros((n - TAIL, g.shape[1]), g.dtype), g], axis=0),),
)


def _make_shift_tail(d):
    @jax.custom_vjp
    def shifted(x, tail):
        n = x.shape[0]
        tpad = jnp.concatenate([tail, jnp.zeros((n - TAIL, x.shape[1]), x.dtype)], axis=0)
        return jnp.where(_row_ids(x.shape) >= d, pltpu.roll(x, d, 0), pltpu.roll(tpad, n + d - TAIL, 0))

    def fwd(x, tail):
        return shifted(x, tail), None

    def bwd(_, g):
        g8 = g[:TAIL, :]
        dtail = jnp.where(_row_ids(g8.shape) >= TAIL - d, pltpu.roll(g8, TAIL - d, 0), 0.0)
        return _shift_up(g, d), dtail

    shifted.defvjp(fwd, bwd)
    return shifted


_SHIFT_TAIL = {d: _make_shift_tail(d) for d in (1, 2, 3)}


def causal_conv4(x, tail, w):
    y = pick_row(w, 3) * x
    for j in range(3):
        y = y + pick_row(w, j) * _SHIFT_TAIL[3 - j](x, tail)
    return y


def rmsnorm(x, g):
    return x * lax.rsqrt(jnp.mean(x * x, axis=-1, keepdims=True) + EPS) * g


def to_row(col):
    n = col.shape[0]
    eye = lax.broadcasted_iota(jnp.int32, (n, n), 0) == lax.broadcasted_iota(jnp.int32, (n, n), 1)
    return jnp.sum(jnp.where(eye, col, 0.0), axis=0, keepdims=True)


def causal_decay(a_col):
    n = a_col.shape[0]
    causal = lax.broadcasted_iota(jnp.int32, (n, n), 0) >= lax.broadcasted_iota(jnp.int32, (n, n), 1)
    cs = jnp.sum(jnp.where(causal, to_row(a_col), 0.0), axis=1, keepdims=True)
    diff = cs - to_row(cs)
    return cs, jnp.where(causal, jnp.exp(jnp.where(causal, diff, 0.0)), 0.0)


class Arg:
    def __init__(self, block, imap, dtype=F32, grad=True, shared=True):
        self.block, self.imap, self.dtype, self.grad, self.shared = block, imap, dtype, grad, shared


def rows(width, nblk, col=lambda g: 0, dtype=F32, grad=True):
    return Arg((nblk, width), lambda g, c: (c, col(g)), dtype, grad)


def head_rows(width, nblk, head=lambda g: g, dtype=F32, grad=True):
    return Arg((None, nblk, width), lambda g, c: (head(g), c, 0), dtype, grad)


def whole(shape, grad=True):
    return Arg(tuple(shape), lambda g, c: (0,) * len(shape), F32, grad, shared=True)


def per_group(shape, idx=lambda g: g, grad=True):
    return Arg((None,) + tuple(shape), lambda g, c: (idx(g),) + (0,) * len(shape), F32, grad, shared=False)


def _bshape(block):
    return tuple(b for b in block if b is not None)


def _spec(arg, nb=None):
    if nb is None:
        return pl.BlockSpec(arg.block, arg.imap)
    return pl.BlockSpec(arg.block, lambda g, c: arg.imap(g, nb - 1 - c))


def blocked_forward(name, fn, groups, nb, tok_args, tok, const_args, consts, out_args, out_shapes, state_shapes, save,
                    reverse=False):
    n_tok, n_const, n_out, n_state = len(tok), len(consts), len(out_args), len(state_shapes)
    walk = nb if reverse else None

    def body(*refs):
        tok_refs = refs[:n_tok]
        const_refs = refs[n_tok:n_tok + n_const]
        out_refs = refs[n_tok + n_const:n_tok + n_const + n_out]
        pos = n_tok + n_const + n_out
        save_refs = refs[pos:pos + (n_state if save else 0)]
        state_refs = refs[len(refs) - n_state:] if n_state else ()

        @pl.when(pl.program_id(1) == 0)
        def _():
            for s in state_refs:
                s[...] = jnp.zeros_like(s)

        states = [s[...] for s in state_refs]
        for sr, s in zip(save_refs, states):
            sr[...] = s
        new_states, outs = fn(states, [r[...].astype(F32) for r in tok_refs], [r[...] for r in const_refs])
        for o_ref, o in zip(out_refs, outs):
            o_ref[...] = o.astype(o_ref.dtype)
        for s_ref, s in zip(state_refs, new_states):
            s_ref[...] = s

    out_specs = [_spec(a, walk) for a in out_args]
    out_shape = [jax.ShapeDtypeStruct(s, a.dtype) for s, a in zip(out_shapes, out_args)]
    if save:
        assert not reverse
        for s in state_shapes:
            out_specs.append(pl.BlockSpec((None, None) + tuple(s), lambda g, c, k=len(s): (g, c) + (0,) * k))
            out_shape.append(jax.ShapeDtypeStruct((groups, nb) + tuple(s), F32))
    res = pl.pallas_call(
        body, name=name, grid=(groups, nb),
        in_specs=[_spec(a, walk) for a in tok_args] + [_spec(a, walk) for a in const_args],
        out_specs=out_specs, out_shape=out_shape,
        scratch_shapes=[pltpu.VMEM(tuple(s), F32) for s in state_shapes],
        compiler_params=_params(("arbitrary", "arbitrary")),
    )(*tok, *consts)
    return list(res[:n_out]), list(res[n_out:])


def blocked_backward(name, fn, groups, nb, tok_args, tok, const_args, consts, out_args, cts, state_shapes, saved):
    n_tok, n_const, n_out, n_state = len(tok), len(consts), len(out_args), len(state_shapes)
    tok_g = [i for i, a in enumerate(tok_args) if a.grad]
    const_g = [i for i, a in enumerate(const_args) if a.grad]

    def body(*refs):
        tok_refs = refs[:n_tok]
        const_refs = refs[n_tok:n_tok + n_const]
        pos = n_tok + n_const
        saved_refs = refs[pos:pos + n_state]
        ct_refs = refs[pos + n_state:pos + n_state + n_out]
        pos += n_state + n_out
        dtok_refs = refs[pos:pos + len(tok_g)]
        dconst_refs = refs[pos + len(tok_g):pos + len(tok_g) + len(const_g)]
        dstate_refs = refs[len(refs) - n_state:] if n_state else ()
        g_id, c_id = pl.program_id(0), pl.program_id(1)

        @pl.when(c_id == 0)
        def _():
            for s in dstate_refs:
                s[...] = jnp.zeros_like(s)
            for r, i in zip(dconst_refs, const_g):
                if not const_args[i].shared:
                    r[...] = jnp.zeros_like(r)

        @pl.when((c_id == 0) & (g_id == 0))
        def _():
            for r, i in zip(dconst_refs, const_g):
                if const_args[i].shared:
                    r[...] = jnp.zeros_like(r)

        tok_vals = [r[...].astype(F32) for r in tok_refs]
        const_vals = [r[...] for r in const_refs]

        def f(states, tok_d, const_d):
            tv, cv = list(tok_vals), list(const_vals)
            for i, v in zip(tok_g, tok_d):
                tv[i] = v
            for i, v in zip(const_g, const_d):
                cv[i] = v
            return fn(states, tv, cv)

        _, vjp = jax.vjp(f, [r[...] for r in saved_refs], [tok_vals[i] for i in tok_g], [const_vals[i] for i in const_g])
        dstates, dtok, dconst = vjp(([r[...] for r in dstate_refs], [r[...].astype(F32) for r in ct_refs]))
        for r, v in zip(dtok_refs, dtok):
            r[...] = v.astype(r.dtype)
        for r, v in zip(dconst_refs, dconst):
            r[...] += v
        for r, v in zip(dstate_refs, dstates):
            r[...] = v

    in_specs = [_spec(a, nb) for a in tok_args] + [_spec(a, nb) for a in const_args]
    for s in state_shapes:
        in_specs.append(pl.BlockSpec((None, None) + tuple(s), lambda g, c, k=len(s): (g, nb - 1 - c) + (0,) * k))
    in_specs += [_spec(a, nb) for a in out_args]
    out_specs = [_spec(tok_args[i], nb) for i in tok_g] + [_spec(const_args[i], nb) for i in const_g]
    out_shape = [jax.ShapeDtypeStruct(tok[i].shape, tok[i].dtype) for i in tok_g]
    out_shape += [jax.ShapeDtypeStruct(consts[i].shape, F32) for i in const_g]
    res = pl.pallas_call(
        body, name=name, grid=(groups, nb), in_specs=in_specs, out_specs=out_specs, out_shape=out_shape,
        scratch_shapes=[pltpu.VMEM(tuple(s), F32) for s in state_shapes],
        compiler_params=_params(("arbitrary", "arbitrary")),
    )(*tok, *consts, *saved, *cts)
    dtok = [None] * n_tok
    dconst = [None] * n_const
    for i, v in zip(tok_g, res[:len(tok_g)]):
        dtok[i] = v
    for i, v in zip(const_g, res[len(tok_g):]):
        dconst[i] = v
    return dtok, dconst


def blocked_op(name, fn, groups, nb, tok_args, const_args, out_args, out_shapes, state_shapes=()):
    state_shapes = tuple(state_shapes)

    @jax.custom_vjp
    def op(tok, consts):
        outs, _ = blocked_forward(name, fn, groups, nb, tok_args, tok, const_args, consts, out_args, out_shapes, state_shapes, False)
        return outs

    def fwd(tok, consts):
        outs, saved = blocked_forward(name, fn, groups, nb, tok_args, tok, const_args, consts, out_args, out_shapes, state_shapes, True)
        return outs, (tok, consts, saved)

    def bwd(res, cts):
        tok, consts, saved = res
        dtok, dconst = blocked_backward(name + "_bwd", fn, groups, nb, tok_args, tok, const_args, consts, out_args, list(cts), state_shapes, saved)
        dtok = [jnp.zeros_like(t) if d is None else d for t, d in zip(tok, dtok)]
        dconst = [jnp.zeros_like(k) if d is None else d for k, d in zip(consts, dconst)]
        return dtok, dconst

    op.defvjp(fwd, bwd)
    return op


def _make_split(sizes):
    offs = [sum(sizes[:i]) for i in range(len(sizes))]

    @jax.custom_vjp
    def split(x):
        return tuple(x[:, o:o + s] for o, s in zip(offs, sizes))

    split.defvjp(lambda x: (split(x), None), lambda _, g: (jnp.concatenate(list(g), axis=1),))
    return split


def _make_join(sizes):
    offs = [sum(sizes[:i]) for i in range(len(sizes))]

    @jax.custom_vjp
    def join(parts):
        return jnp.concatenate(list(parts), axis=1)

    join.defvjp(lambda parts: (join(parts), None), lambda _, g: (tuple(g[:, o:o + s] for o, s in zip(offs, sizes)),))
    return join


def split_cols(x, sizes):
    return _make_split(tuple(sizes))(x)


def join_cols(parts):
    return _make_join(tuple(p.shape[1] for p in parts))(tuple(parts))


def lane_scalar(row, j):
    lane = lax.broadcasted_iota(jnp.int32, row.shape, 1)
    return jnp.sum(jnp.where(lane == j, row, 0.0), axis=1, keepdims=True)


def lane_col(blk, j):
    lane = lax.broadcasted_iota(jnp.int32, blk.shape, 1)
    return jnp.sum(jnp.where(lane == j, blk, 0.0), axis=1, keepdims=True)


LANE = 128
MM_TILE_M, MM_TILE_N, MM_TILE_K = 1024, 1536, 1408
MM_TILE_MT = 1408


def _tile(n, cap, unit):
    if n <= cap:
        return n
    best = None
    for t in range(unit, cap + 1, unit):
        if n % t == 0:
            best = t
    assert best is not None, (n, cap, unit)
    return best


def matmul(a, b, mode="nn", add=None, out_dtype=F32, name="matmul", pre=None, col_shards=0):
    lead = () if pre is None else tuple(pre)
    rows_b, cols_b = b.shape[-2:]
    if mode != "tn" and col_shards:
        cols_b *= col_shards
    if mode == "nn":
        (m, k), n = a.shape, cols_b
    elif mode == "nt":
        (m, k), n = a.shape, rows_b
    else:
        (k, m), n = a.shape, cols_b
    shard_n = n // col_shards if col_shards and mode != "nt" else n
    shard_k = k // col_shards if col_shards and mode == "nt" else k
    tm = _tile(m, MM_TILE_MT, LANE) if mode == "tn" else _tile(m, MM_TILE_M, 8)
    tn = _tile(shard_n, MM_TILE_N, LANE)
    tk = _tile(shard_k, MM_TILE_K if mode != "tn" else MM_TILE_M, LANE)
    nk = k // tk
    qn, qk = shard_n // tn, shard_k // tk
    nolead = (None,) * len(lead)
    a_spec = pl.BlockSpec((tk, tm), lambda i, j, l: (l, i)) if mode == "tn" else pl.BlockSpec((tm, tk), lambda i, j, l: (i, l))
    if mode == "tn":
        b_spec = pl.BlockSpec((tk, tn), lambda i, j, l: (l, j))
    elif mode == "nn" and col_shards:
        b_spec = pl.BlockSpec((None,) + nolead + (tk, tn), lambda i, j, l: (j // qn,) + lead + (l, j % qn))
    elif mode == "nn":
        b_spec = pl.BlockSpec(nolead + (tk, tn), lambda i, j, l: lead + (l, j))
    elif col_shards:
        b_spec = pl.BlockSpec((None,) + nolead + (tn, tk), lambda i, j, l: (l // qk,) + lead + (j, l % qk))
    else:
        b_spec = pl.BlockSpec(nolead + (tn, tk), lambda i, j, l: lead + (j, l))
    dims = {"nn": ((1,), (0,)), "nt": ((1,), (1,)), "tn": ((0,), (0,))}[mode]
    has_add = add is not None

    def body_single(*refs):
        acc = _dot_bf16(refs[0][...], refs[1][...], dims)
        if has_add:
            acc = acc + refs[2][...].astype(F32)
        refs[-1][...] = acc.astype(refs[-1].dtype)

    def body(*refs):
        a_ref, b_ref = refs[0], refs[1]
        add_ref = refs[2] if has_add else None
        o_ref, acc_ref = refs[-2], refs[-1]
        l = pl.program_id(2)

        @pl.when(l == 0)
        def _():
            acc_ref[...] = add_ref[...].astype(F32) if has_add else jnp.zeros_like(acc_ref)

        acc_ref[...] += _dot_bf16(a_ref[...], b_ref[...], dims)

        @pl.when(l == nk - 1)
        def _():
            o_ref[...] = acc_ref[...].astype(o_ref.dtype)

    if mode == "tn" and col_shards:
        assert not has_add
        o_spec = pl.BlockSpec((None, tm, tn), lambda i, j, l: (j // qn, i, j % qn))
        out_shape = jax.ShapeDtypeStruct((col_shards, m, shard_n), out_dtype)
    else:
        o_spec = pl.BlockSpec((tm, tn), lambda i, j, l: (i, j))
        out_shape = jax.ShapeDtypeStruct((m, n), out_dtype)
    return pl.pallas_call(
        body_single if nk == 1 else body, name=name, grid=(m // tm, n // tn, nk),
        in_specs=[a_spec, b_spec] + ([o_spec] if has_add else []), out_specs=o_spec, out_shape=out_shape,
        scratch_shapes=[] if nk == 1 else [pltpu.VMEM((tm, tn), F32)],
        compiler_params=_params(("parallel", "parallel", "arbitrary")),
    )(*([a, b] + ([add] if has_add else [])))


def linear(name, out_dtype=F32, pre=None, col_shards=0):
    @jax.custom_vjp
    def op(a, w, handle):
        return matmul(a, w, "nn", out_dtype=out_dtype, name=name, pre=pre, col_shards=col_shards)

    def fwd(a, w, handle):
        return op(a, w, handle), (a, w)

    def bwd(res, g):
        a, w = res
        da = matmul(g, w, "nt", out_dtype=a.dtype, name=name + "_da", pre=pre, col_shards=col_shards)
        dw = matmul(a, g, "tn", out_dtype=F32, name=name + "_dw", col_shards=col_shards)
        return da, jnp.zeros_like(w), dw

    op.defvjp(fwd, bwd)
    return op


def multi_linear(name, out_dtypes, pres, shards):
    sel = [dict(pre=p, col_shards=s) for p, s in zip(pres, shards)]

    @jax.custom_vjp
    def op(a, ws, handles):
        return [matmul(a, w, "nn", out_dtype=dt, name=f"{name}{i}", **sel[i]) for i, (w, dt) in enumerate(zip(ws, out_dtypes))]

    def fwd(a, ws, handles):
        return op(a, ws, handles), (a, ws)

    def bwd(res, gs):
        a, ws = res
        acc = None
        for i, (g, w) in enumerate(zip(gs, ws)):
            last = i == len(ws) - 1
            acc = matmul(g, w, "nt", add=acc, out_dtype=a.dtype if last else F32, name=f"{name}{i}_da", **sel[i])
        dws = [matmul(a, g, "tn", out_dtype=F32, name=f"{name}{i}_dw", col_shards=shards[i]) for i, g in enumerate(gs)]
        return acc, [jnp.zeros_like(w) for w in ws], dws

    op.defvjp(fwd, bwd)
    return op


def dense(name, out_dtype=F32):
    @jax.custom_vjp
    def op(a, w):
        return matmul(a, w, "nn", out_dtype=out_dtype, name=name)

    def fwd(a, w):
        return op(a, w), (a, w)

    def bwd(res, g):
        a, w = res
        return (matmul(g, w, "nt", out_dtype=a.dtype, name=name + "_da"),
                matmul(a, g, "tn", out_dtype=w.dtype, name=name + "_dw"))

    op.defvjp(fwd, bwd)
    return op


def to_col(row):
    n = row.shape[1]
    eye = lax.broadcasted_iota(jnp.int32, (n, n), 0) == lax.broadcasted_iota(jnp.int32, (n, n), 1)
    return jnp.sum(jnp.where(eye, row, 0.0), axis=1, keepdims=True)


def norm_fn(states, toks, consts):
    return [], [rmsnorm(toks[0], consts[0])]


def make_addnorm_fn(scale):
    def fn(states, toks, consts):
        h = toks[0] + scale * toks[1]
        return [], [h, rmsnorm(h, consts[0])]

    return fn


def swiglu_fn(states, toks, consts):
    gate, up = split_cols(toks[0], (FFN_DIM, FFN_DIM))
    return [], [jax.nn.silu(gate) * up]


def gate_merge_fn(states, toks, consts):
    gates = split_cols(jax.nn.sigmoid(toks[0] + consts[0]), (D_MODEL,) * N_BRANCH)
    mixed = gates[0] * toks[1]
    for n in range(1, N_BRANCH):
        mixed = mixed + gates[n] * toks[1 + n]
    return [], [mixed]


def ple_fn(states, toks, consts):
    h = toks[0] + jax.nn.sigmoid(toks[1]) * toks[2]
    return [], [h, rmsnorm(h, consts[0])]


def ple_loss_fn(states, toks, consts):
    h = toks[0] + jax.nn.sigmoid(toks[1]) * toks[2]
    err = rmsnorm(h, consts[0]) - toks[3]
    return [], [0.5 * jnp.mean(err * err, axis=-1, keepdims=True)]


def s5_discretise_fn(states, toks, consts):
    log_step, a_re, a_im, b_re, b_im = consts
    step = jnp.exp(log_step)
    mag = jnp.exp(a_re * step)
    ab_re, ab_im = mag * jnp.cos(a_im * step), mag * jnp.sin(a_im * step)
    den = a_re * a_re + a_im * a_im
    num_re = ab_re - 1.0
    f_re = (num_re * a_re + ab_im * a_im) / den
    f_im = (ab_im * a_re - num_re * a_im) / den
    return [], [ab_re, ab_im, f_re * b_re - f_im * b_im, f_re * b_im + f_im * b_re]


def s5_scan_fn(states, toks, consts):
    b_re, b_im = split_cols(toks[0], (S5_W, S5_W))
    h_re, h_im = complex_scan(b_re, b_im, consts[0], consts[1], states[0], states[1])
    return [last_row(h_re), last_row(h_im)], [join_cols([h_re, h_im])]


def s5_scan_bwd_fn(states, toks, consts):
    g_re, g_im, acc_re, acc_im = states
    h, ct, h0_re, h0_im = toks
    a_re, a_im = consts
    h_re, h_im = split_cols(h, (S5_W, S5_W))
    c_re, c_im = split_cols(ct, (S5_W, S5_W))
    last = _row_ids(c_re.shape) == c_re.shape[0] - 1
    cts = (c_re + jnp.where(last, g_re, 0.0), c_im + jnp.where(last, g_im, 0.0))
    d_re, d_im, da_re, da_im, d0_re, d0_im = _complex_scan_bwd((a_re, a_im, h_re, h_im, h0_re, h0_im), cts)
    acc_re, acc_im = acc_re + da_re, acc_im + da_im
    return [d0_re, d0_im, acc_re, acc_im], [join_cols([d_re, d_im]), acc_re, acc_im]


def s5_scan(name, t, bu, a_re, a_im):
    nblk = min(SCAN_ROWS, t)
    nb = t // nblk
    wide, row = rows(2 * S5_W, nblk), whole((1, S5_W))
    entry = Arg((None, None, 1, S5_W), lambda g, c: (0, c, 0, 0))
    state = [(1, S5_W)] * 2

    def run(bu, a_re, a_im, save):
        return blocked_forward(name, s5_scan_fn, 1, nb, [wide], [bu], [row, row], [a_re, a_im], [wide], [(t, 2 * S5_W)], state, save)

    @jax.custom_vjp
    def op(bu, a_re, a_im):
        return run(bu, a_re, a_im, False)[0][0]

    def fwd(bu, a_re, a_im):
        outs, saved = run(bu, a_re, a_im, True)
        return outs[0], (outs[0], saved, a_re, a_im)

    def bwd(res, ct):
        h, saved, a_re, a_im = res
        outs, _ = blocked_forward(name + "_bwd", s5_scan_bwd_fn, 1, nb, [wide, wide, entry, entry], [h, ct] + saved, [row, row],
                                  [a_re, a_im], [wide, row, row], [(t, 2 * S5_W), (1, S5_W), (1, S5_W)], state * 2, False, reverse=True)
        return tuple(outs)

    op.defvjp(fwd, bwd)
    return op(bu, a_re, a_im)


def s5_glu_fn(states, toks, consts):
    d_skip, w_glu, b_glu = consts
    z = jax.nn.gelu(toks[0] + d_skip * toks[1])
    return [], [z * jax.nn.sigmoid(mm(z, w_glu) + b_glu)]


def lru_fn(states, toks, consts):
    h0, tail = states
    x, gate = toks
    conv_w, conv_b, w_r, b_r, w_i, b_i, lam = consts
    xc = causal_conv4(x, tail, conv_w) + conv_b
    r = jax.nn.sigmoid(mm(xc, w_r) + b_r)
    i_g = jax.nn.sigmoid(mm(xc, w_i) + b_i)
    log_a = -LRU_C * r * jax.nn.softplus(-lam)
    inp = jnp.sqrt(1.0 - jnp.exp(2.0 * log_a)) * (i_g * xc)
    h = lin_scan(jnp.exp(log_a), inp, h0)
    return [last_row(h), tail_rows(x)], [h * jax.nn.gelu(gate)]


def m2_conv_fn(states, toks, consts):
    y = jax.nn.silu(causal_conv4(toks[0], states[0], consts[0]) + consts[1])
    return [tail_rows(toks[0])], list(split_cols(y, (BRANCH_WIDTH, M2_GROUPS * M2_STATE, M2_GROUPS * M2_STATE)))


def gdn_conv_fn(states, toks, consts):
    y = jax.nn.silu(causal_conv4(toks[0], states[0], consts[0]))
    return [tail_rows(toks[0])], list(split_cols(y, (BRANCH_WIDTH,) * 3))


def ssd_fn(states, toks, consts):
    xs, bm, cm, small = toks
    dt_bias, a_log, d_skip = consts
    x_pairs = split_cols(xs, (LANE,) * 4)
    b_g = split_cols(bm, (M2_STATE,) * M2_GROUPS)
    c_g = split_cols(cm, (M2_STATE,) * M2_GROUPS)
    lo = lax.broadcasted_iota(jnp.int32, (1, LANE), 1) < M2_HEAD_DIM
    new_states, y_pairs = [], []
    for g in range(M2_GROUPS):
        scores = mm_nt(c_g[g], b_g[g])
        y_off = split_cols(mm_nt(c_g[g], states[g]), (LANE, LANE))
        to_end, ends = [], []
        for j in range(2):
            pair = 2 * g + j
            x2 = x_pairs[pair]
            dts, css, decays, end = [], [], [], []
            for h in (2 * pair, 2 * pair + 1):
                dt = jax.nn.softplus(lane_col(small, h) + lane_scalar(dt_bias, h))
                a = dt * (-jnp.exp(lane_scalar(a_log, h)))
                cs, decay = causal_decay(a)
                dts.append(dt)
                css.append(cs)
                decays.append(decay)
                end.append(jnp.sum(a, axis=0, keepdims=True))
            xdt = x2 * jnp.where(lo, dts[0], dts[1])
            y = mm(scores * decays[0], jnp.where(lo, xdt, 0.0)) + mm(scores * decays[1], jnp.where(lo, 0.0, xdt))
            cs2 = jnp.where(lo, css[0], css[1])
            end2 = jnp.where(lo, end[0], end[1])
            y = y + y_off[j] * jnp.exp(cs2)
            y = y + jnp.where(lo, lane_scalar(d_skip, 2 * pair), lane_scalar(d_skip, 2 * pair + 1)) * x2
            y_pairs.append(y)
            to_end.append(xdt * jnp.exp(end2 - cs2))
            ends.append(end2)
        chunk_decay = jnp.exp(to_col(join_cols(ends)))
        new_states.append(states[g] * chunk_decay + mm_tn(join_cols(to_end), b_g[g]))
    return new_states, [join_cols(y_pairs)]


def m2_post_fn(states, toks, consts):
    return [], [rmsnorm(toks[0] * jax.nn.silu(toks[1]), consts[0])]


@jax.custom_vjp
def nilpotent_inverse(n_mat):
    size = n_mat.shape[0]
    eye = lax.broadcasted_iota(jnp.int32, n_mat.shape, 0) == lax.broadcasted_iota(jnp.int32, n_mat.shape, 1)
    inv = jnp.where(eye, 1.0, 0.0) + n_mat
    power = n_mat
    d = 2
    while d < size:
        power = _dot_bf16x3(power, power, ((1,), (0,)))
        inv = inv + _dot_bf16x3(inv, power, ((1,), (0,)))
        d *= 2
    return inv


def _nilpotent_inverse_fwd(n_mat):
    inv = nilpotent_inverse(n_mat)
    return inv, inv


def _nilpotent_inverse_bwd(inv, g):
    return (_dot_bf16x3(inv, _dot_bf16x3(g, inv, ((1,), (1,))), ((0,), (0,))),)


nilpotent_inverse.defvjp(_nilpotent_inverse_fwd, _nilpotent_inverse_bwd)


def gdn_head(state, q, k, v, gate, a_raw, b_raw, dt_bias, a_log, norm_g):
    n = q.shape[0]
    qn = q * lax.rsqrt(jnp.sum(q * q, axis=-1, keepdims=True) + EPS) * (GDN_HEAD_DIM ** -0.5)
    kn = k * lax.rsqrt(jnp.sum(k * k, axis=-1, keepdims=True) + EPS)
    beta = jax.nn.sigmoid(b_raw)
    g = -jnp.exp(a_log) * jax.nn.softplus(a_raw + dt_bias)
    cs, decay = causal_decay(g)
    row = lax.broadcasted_iota(jnp.int32, (n, n), 0)
    col = lax.broadcasted_iota(jnp.int32, (n, n), 1)
    kb = kn * beta
    inv = nilpotent_inverse(-jnp.where(row > col, mm_nt(kb, kn) * decay, 0.0))
    ecs = jnp.exp(cs)
    u = mmh(inv, v * beta)
    w = mmh(inv, kb * ecs)
    qk = mm_nt(qn, kn) * decay
    cs_end = jnp.sum(g, axis=0, keepdims=True)
    v_new = u - mm(w, state)
    o = mm(qn * ecs, state) + mm(qk, v_new)
    new_state = state * jnp.exp(cs_end) + mm_tn(kn * jnp.exp(cs_end - cs), v_new)
    return new_state, rmsnorm(o, norm_g) * jax.nn.silu(gate)


def gdn_fn(states, toks, consts):
    q, k, v, gate, small = toks
    dt_bias, a_log, norm_g = consts
    heads = (GDN_HEAD_DIM,) * GDN_HEADS
    qs, ks, vs, gs = split_cols(q, heads), split_cols(k, heads), split_cols(v, heads), split_cols(gate, heads)
    new_states, ys = [], []
    for h in range(GDN_HEADS):
        st, y = gdn_head(states[h], qs[h], ks[h], vs[h], gs[h], lane_col(small, GDN_HEADS + h), lane_col(small, h),
                         lane_scalar(dt_bias, h), lane_scalar(a_log, h), norm_g)
        new_states.append(st)
        ys.append(y)
    return new_states, [join_cols(ys)]


def adamw_fn(states, toks, consts):
    w, g, m, v = toks
    m = ADAM_B1 * m + (1.0 - ADAM_B1) * g
    v = ADAM_B2 * v + (1.0 - ADAM_B2) * (g * g)
    m_hat = m / (1.0 - ADAM_B1 ** ADAM_STEP)
    v_hat = v / (1.0 - ADAM_B2 ** ADAM_STEP)
    return [], [-ADAM_LR * (m_hat / (jnp.sqrt(v_hat) + ADAM_EPS) + ADAM_WD * w), m, v]


def tok_op(name, fn, t, nblk, tok, consts, outs, states=()):
    nblk = min(nblk, t)
    tok_args = [rows(e[0], nblk, dtype=e[1], grad=e[2] if len(e) > 2 else True) for e in tok]
    const_args = [whole(s) for s in consts]
    out_args = [rows(w, nblk, dtype=dt) for (w, dt) in outs]
    return blocked_op(name, fn, 1, t // nblk, tok_args, const_args, out_args, [(t, w) for (w, _) in outs], states)


def const_op(name, fn, in_shapes, out_shapes):
    return blocked_op(name, fn, 1, 1, [], [whole(s) for s in in_shapes], [whole(s) for s in out_shapes], list(out_shapes))


W = BRANCH_WIDTH
SCAN_ROWS = 128
ROW_BLOCK = 512


def s5_mixer(tag, t, u, p):
    col = (S5_W, 1)
    disc = const_op("s5_disc" + tag, s5_discretise_fn, [col, col, col, (S5_W, 16), (S5_W, 16)], [col, col, (S5_W, 16), (S5_W, 16)])
    ab_re, ab_im, bb_re, bb_im = disc([], [
        jnp.repeat(p["s5_log_step"], S5_STATE).reshape(col), p["s5_a_re"].reshape(col), p["s5_a_im"].reshape(col),
        p["s5_b_re"].reshape(S5_W, S5_GROUP_CH), p["s5_b_im"].reshape(S5_W, S5_GROUP_CH)])
    eye = jnp.eye(S5_GROUPS, dtype=F32)

    def block_in(bb):
        return jnp.einsum("gpc,gh->gchp", bb.reshape(S5_GROUPS, S5_STATE, S5_GROUP_CH), eye).reshape(W, S5_W)

    def block_out(c):
        return jnp.einsum("gcp,gh->gphc", c, eye).reshape(S5_W, W)

    w_b = jnp.concatenate([block_in(bb_re), block_in(bb_im)], axis=1)
    w_c = jnp.concatenate([block_out(p["s5_c_re"]), -block_out(p["s5_c_im"])], axis=0)
    bu = dense("s5_b" + tag)(u, w_b)
    h = s5_scan("s5_scan" + tag, t, bu, ab_re.reshape(1, S5_W), ab_im.reshape(1, S5_W))
    yc = dense("s5_c" + tag)(h, w_c)
    glu = tok_op("s5_glu" + tag, s5_glu_fn, t, ROW_BLOCK, [(W, F32), (W, F32)], [(1, W), (W, W), (1, W)], [(W, ACT)])
    return glu([yc, u], [p["s5_d"].reshape(1, W), p["s5_w_glu"], p["s5_b_glu"][None]])[0]


def lru_mixer(tag, t, x, gate, p):
    def block_diag(w):
        return jnp.einsum("hij,hk->hikj", w, jnp.eye(LRU_HEADS, dtype=F32)).reshape(W, W)

    op = tok_op("lru" + tag, lru_fn, t, SCAN_ROWS, [(W, F32), (W, F32)],
                [(4, W), (1, W), (W, W), (1, W), (W, W), (1, W), (1, W)], [(W, ACT)], states=[(1, W), (TAIL, W)])
    return op([x, gate], [p["lru_conv_w"], p["lru_conv_b"][None], block_diag(p["lru_w_r"]), p["lru_b_r"][None],
                          block_diag(p["lru_w_i"]), p["lru_b_i"][None], p["lru_lambda"][None]])[0]


def m2_mixer(tag, t, z, xbc, small, p):
    cw = 2 * W
    conv = tok_op("m2_conv" + tag, m2_conv_fn, t, ROW_BLOCK, [(cw, F32)], [(4, cw), (1, cw)],
                  [(W, F32), (W // 2, F32), (W // 2, F32)], states=[(TAIL, cw)])
    xs, bm, cm = conv([xbc], [p["m2_conv_w"], p["m2_conv_b"][None]])
    ssd = tok_op("ssd" + tag, ssd_fn, t, CHUNK, [(W, F32), (W // 2, F32), (W // 2, F32), (LANE, F32)],
                 [(1, M2_HEADS)] * 3, [(W, F32)], states=[(4 * M2_HEAD_DIM, M2_STATE)] * M2_GROUPS)
    y = ssd([xs, bm, cm, small], [p["m2_dt_bias"][None], p["m2_a_log"][None], p["m2_d"][None]])[0]
    post = tok_op("m2_post" + tag, m2_post_fn, t, ROW_BLOCK, [(W, F32), (W, F32)], [(1, W)], [(W, ACT)])
    return post([y, z], [p["m2_norm"][None]])[0]


def gdn_mixer(tag, t, qkv, gate, small, p):
    conv = tok_op("gdn_conv" + tag, gdn_conv_fn, t, ROW_BLOCK, [(3 * W, F32)], [(4, 3 * W)], [(W, F32)] * 3, states=[(TAIL, 3 * W)])
    q, k, v = conv([qkv], [p["gdn_conv_w"]])
    op = tok_op("gdn" + tag, gdn_fn, t, CHUNK, [(W, F32)] * 4 + [(LANE, F32)], [(1, GDN_HEADS), (1, GDN_HEADS), (1, GDN_HEAD_DIM)],
                [(W, ACT)], states=[(GDN_HEAD_DIM, GDN_HEAD_DIM)] * GDN_HEADS)
    return op([q, k, v, gate, small], [p["gdn_dt_bias"][None], p["gdn_a_log"][None], p["gdn_norm"][None]])[0]


WEIGHTS = ["ffn1_norm", "ffn1_w_in", "ffn1_w_out", "mix_norm", "w_in", "w_gate", "b_gate", "s5_log_step", "s5_a_re",
           "s5_a_im", "s5_b_re", "s5_b_im", "s5_c_re", "s5_c_im", "s5_d", "s5_w_glu", "s5_b_glu", "lru_conv_w",
           "lru_conv_b", "lru_w_r", "lru_b_r", "lru_w_i", "lru_b_i", "lru_lambda", "m2_conv_w", "m2_conv_b", "m2_dt_bias",
           "m2_a_log", "m2_d", "m2_norm", "gdn_conv_w", "gdn_dt_bias", "gdn_a_log", "gdn_norm", "w_branch", "w_out",
           "ffn2_norm", "ffn2_w_in", "ffn2_w_out", "ple_norm", "ple_w_gate", "ple_w_proj", "final_norm"]
N_CHIPS = 4
N_DEV = 8
IN_WIDTH = 5136
SHARDED = {
    "ffn1_w_in": ((D_MODEL, 2 * FFN_DIM // N_CHIPS), 1, False),
    "ffn1_w_out": ((FFN_DIM // N_CHIPS, D_MODEL), 0, False),
    "w_in": ((D_MODEL, IN_WIDTH // N_CHIPS), 1, False),
    "w_gate": ((D_MODEL, N_BRANCH * D_MODEL // N_CHIPS), 1, False),
    "s5_w_glu": ((W // N_CHIPS, W), 0, False),
    "lru_conv_w": ((4, W // N_CHIPS), 1, True),
    "m2_conv_w": ((4, 2 * W // N_CHIPS), 1, True),
    "gdn_conv_w": ((4, 3 * W // N_CHIPS), 1, True),
    "w_branch": ((N_BRANCH, W, D_MODEL // N_CHIPS), 2, False),
    "w_out": ((D_MODEL // N_CHIPS, D_MODEL), 0, False),
    "ffn2_w_in": ((D_MODEL, 2 * FFN_DIM // N_CHIPS), 1, False),
    "ffn2_w_out": ((FFN_DIM // N_CHIPS, D_MODEL), 0, False),
    "ple_w_gate": ((D_MODEL // N_CHIPS, D_MODEL), 0, False),
    "ple_w_proj": ((256, D_MODEL // N_CHIPS), 1, False),
}
SMALL = [n for n in WEIGHTS if n not in SHARDED]
ROW = 1024


def _count(shape):
    return math.prod(shape)


def _round_up(n, unit):
    return -(-n // unit) * unit


N_GATHER = sum(DEPTH * _count(s) * (2 if exact else 1) for s, _, exact in SHARDED.values())
N_GRAD = sum(DEPTH * _count(s) for s, _, _ in SHARDED.values())
GATHER_ROWS = _round_up(-(-N_GATHER // ROW), 32)
GRAD_ROWS = _round_up(-(-N_GRAD // ROW), 32)
GRAD_HALF = GRAD_ROWS // 2
IN_PIECES = [(0, 512), (512, 512), (1024, 512), (1536, 512), (2048, 1024), (3072, 8), (3080, 1536), (4616, 512), (5128, 8)]


COL_SHARDED = ("ffn1_w_in", "ffn2_w_in", "w_gate", "ple_w_proj", "w_branch")
IN_SHARD = IN_WIDTH // N_CHIPS


def _w_in_cuts():
    cuts = {0, IN_SHARD}
    for s in range(N_CHIPS):
        for start, _ in IN_PIECES:
            if s * IN_SHARD < start < (s + 1) * IN_SHARD:
                cuts.add(start - s * IN_SHARD)
    return sorted(cuts)


IN_CUTS = _w_in_cuts()
IN_BLOCKS = list(zip(IN_CUTS[:-1], IN_CUTS[1:]))


def _piece_of(col):
    for k, (start, n) in enumerate(IN_PIECES):
        if start <= col < start + n:
            return k, col - start
    raise ValueError(col)


def cut_w_in(w):
    return jnp.concatenate([w[:, :, lo:hi].reshape(-1) for lo, hi in IN_BLOCKS])


def uncut_w_in(flat):
    blocks, off = [], 0
    for lo, hi in IN_BLOCKS:
        cnt = DEPTH * D_MODEL * (hi - lo)
        blocks.append(flat[off:off + cnt].reshape(DEPTH, D_MODEL, hi - lo))
        off += cnt
    return jnp.concatenate(blocks, axis=2)


def pack_for_gather(a):
    parts = []
    for n, (_, _, exact) in SHARDED.items():
        w = cut_w_in(a[n]) if n == "w_in" else a[n]
        parts.append((lax.bitcast_convert_type(w, BF16) if exact else w.astype(BF16)).reshape(-1))
    flat = jnp.concatenate(parts)
    return jnp.pad(flat, (0, GATHER_ROWS * ROW - flat.shape[0])).reshape(GATHER_ROWS, ROW)


def unpack_gathered(buf):
    flat16 = buf.reshape(N_CHIPS, -1)
    flat = buf.astype(ACT).reshape(N_CHIPS, -1)
    out, off = {}, 0
    for n, (shape, ax, exact) in SHARDED.items():
        cnt = DEPTH * _count(shape) * (2 if exact else 1)
        piece = (flat16 if exact else flat)[:, off:off + cnt]
        off += cnt
        if n == "w_in":
            cols = [[[] for _ in IN_PIECES] for _ in range(DEPTH)]
            for s in range(N_CHIPS):
                o = 0
                for lo, hi in IN_BLOCKS:
                    c = DEPTH * D_MODEL * (hi - lo)
                    blk = piece[s, o:o + c].reshape(DEPTH, D_MODEL, hi - lo)
                    o += c
                    k, _ = _piece_of(s * IN_SHARD + lo)
                    for layer in range(DEPTH):
                        cols[layer][k].append(blk[layer])
            out[n] = [[_pad_lanes(jnp.concatenate(c, axis=1)) for c in cols[layer]] for layer in range(DEPTH)]
        elif n in COL_SHARDED:
            out[n] = piece.reshape(N_CHIPS, DEPTH, *shape)
        else:
            if exact:
                w = lax.bitcast_convert_type(piece.reshape(N_CHIPS, DEPTH, *shape, 2), F32)
            else:
                w = piece.reshape(N_CHIPS, DEPTH, *shape)
            full = list(shape)
            full[ax] *= N_CHIPS
            out[n] = jnp.moveaxis(w, 0, ax + 1).reshape(DEPTH, *full)
    return out


def _pad_lanes(w):
    n = w.shape[1]
    return w if n % LANE == 0 else jnp.pad(w, ((0, 0), (0, LANE - n % LANE)))


def shard_w_in_grads(pieces):
    shards = []
    for s in range(N_CHIPS):
        parts = []
        for lo, hi in IN_BLOCKS:
            k, dst = _piece_of(s * IN_SHARD + lo)
            parts.append(jnp.stack([pieces[layer][k][:, dst:dst + hi - lo] for layer in range(DEPTH)]).reshape(-1))
        shards.append(jnp.concatenate(parts))
    return jnp.stack(shards)


def pack_grads(g):
    tail = jnp.zeros((N_CHIPS, GRAD_ROWS * ROW - N_GRAD), F32)
    return jnp.concatenate([g[n] for n in SHARDED] + [tail], axis=1).reshape(N_CHIPS, GRAD_ROWS, ROW)


def shard_full_grads(n, per_layer):
    shape, ax, _ = SHARDED[n]
    parts = []
    for full in per_layer:
        w = full.reshape(*full.shape[:ax], N_CHIPS, shape[ax], *full.shape[ax + 1:])
        parts.append(jnp.moveaxis(w, ax, 0).reshape(N_CHIPS, -1))
    return jnp.concatenate(parts, axis=1)


def unpack_shard(buf):
    flat = buf.reshape(-1)
    out, off = {}, 0
    for n, (shape, _, _) in SHARDED.items():
        cnt = DEPTH * _count(shape)
        piece = flat[off:off + cnt]
        out[n] = uncut_w_in(piece) if n == "w_in" else piece.reshape(DEPTH, *shape)
        off += cnt
    return out


def pack_small(a, prefix, shapes):
    flat = jnp.concatenate([a[prefix + n].reshape(-1) for n in SMALL])
    rows_n = _round_up(-(-flat.shape[0] // ROW), 8)
    return jnp.pad(flat, (0, rows_n * ROW - flat.shape[0])).reshape(rows_n, ROW)


def unpack_small(buf, shapes):
    flat = buf.reshape(-1)
    out, off = {}, 0
    for n in SMALL:
        cnt = _count(shapes[n])
        out[n] = flat[off:off + cnt].reshape(shapes[n])
        off += cnt
    return out


ANY = pl.BlockSpec(memory_space=pl.ANY)


def _position():
    return lax.axis_index("x"), lax.axis_index("y"), lax.axis_index("c")


def _other_chips(x, y):
    return [(1 - x, y), (x, 1 - y), (1 - x, 1 - y)]


PLACE_ROWS = 592


def gather_weights(packed, slot):
    r = packed.shape[0]
    half = r // 2
    nblk = r // PLACE_ROWS

    def place(s_ref, in_ref, o_ref):
        o_ref[...] = in_ref[...]

    slots = pl.pallas_call(
        place, name="place_shard",
        grid_spec=pltpu.PrefetchScalarGridSpec(
            num_scalar_prefetch=1, grid=(nblk,), in_specs=[pl.BlockSpec((PLACE_ROWS, ROW), lambda i, s: (i, 0))],
            out_specs=pl.BlockSpec((None, PLACE_ROWS, ROW), lambda i, s: (s[0], i, 0))),
        out_shape=jax.ShapeDtypeStruct((N_CHIPS, r, ROW), packed.dtype), compiler_params=_params(("arbitrary",)),
    )(slot, packed)

    def body(in_ref, out_ref, send_sems, recv_sems):
        x, y, c = _position()
        sibling = (x, y, 1 - c)
        chips = _other_chips(x, y)

        def half_rows(px, py, pc):
            return out_ref.at[2 * px + py, pl.ds(pl.multiple_of(pc * half, 16), half), :]

        def copy(k, block, to):
            return pltpu.make_async_remote_copy(
                src_ref=half_rows(*block), dst_ref=half_rows(*block),
                send_sem=send_sems.at[k], recv_sem=recv_sems.at[k], device_id=to, device_id_type=MESH)

        first = [copy(j, (x, y, c), (*chip, c)) for j, chip in enumerate(chips)]
        for cp in first:
            cp.start()
        passed = [copy(3 + j, (*chip, c), sibling) for j, chip in enumerate(chips)]
        for j, chip in enumerate(chips):
            copy(j, (*chip, c), (x, y, c)).wait_recv()
            passed[j].start()
        for j, chip in enumerate(chips):
            copy(3 + j, (*chip, 1 - c), (x, y, c)).wait_recv()
        for cp in first + passed:
            cp.wait_send()

    return pl.pallas_call(
        body, name="gather_weights", in_specs=[ANY], out_specs=ANY, input_output_aliases={0: 0},
        out_shape=jax.ShapeDtypeStruct((N_CHIPS, r, ROW), packed.dtype),
        scratch_shapes=[pltpu.SemaphoreType.DMA((6,)), pltpu.SemaphoreType.DMA((6,))],
    )(slots)


def swap_pair_halves(g):
    half = g.shape[1] // 2

    def body(g_ref, land_ref, send_sem, recv_sem):
        x, y, c = _position()
        src = g_ref.at[:, pl.ds(pl.multiple_of((1 - c) * half, 8), half), :]
        cp = pltpu.make_async_remote_copy(src_ref=src, dst_ref=land_ref, send_sem=send_sem, recv_sem=recv_sem,
                                          device_id=(x, y, 1 - c), device_id_type=MESH)
        cp.start()
        cp.wait()

    return pl.pallas_call(
        body, name="swap_pair_halves", in_specs=[ANY], out_specs=ANY,
        out_shape=jax.ShapeDtypeStruct((N_CHIPS, half, ROW), g.dtype),
        scratch_shapes=[pltpu.SemaphoreType.DMA, pltpu.SemaphoreType.DMA],
    )(g)


def exchange_chip_partials(part):
    half = part.shape[1]

    def body(p_ref, land_ref, send_sems, recv_sems):
        x, y, c = _position()
        cps = [pltpu.make_async_remote_copy(src_ref=p_ref.at[2 * px + py], dst_ref=land_ref.at[j], send_sem=send_sems.at[j],
                                            recv_sem=recv_sems.at[j], device_id=(px, py, c), device_id_type=MESH)
               for j, (px, py) in enumerate(_other_chips(x, y))]
        for cp in cps:
            cp.start()
        for cp in cps:
            cp.wait()

    return pl.pallas_call(
        body, name="exchange_chip_partials", in_specs=[ANY], out_specs=ANY,
        out_shape=jax.ShapeDtypeStruct((3, half, ROW), part.dtype),
        scratch_shapes=[pltpu.SemaphoreType.DMA((3,)), pltpu.SemaphoreType.DMA((3,))],
    )(part)


def share_halves(both):
    half = both.shape[0] // 2

    def body(in_ref, out_ref, send_sem, recv_sem):
        x, y, c = _position()
        my_rows = out_ref.at[pl.ds(pl.multiple_of(c * half, 8), half), :]
        cp = pltpu.make_async_remote_copy(src_ref=my_rows, dst_ref=my_rows, send_sem=send_sem, recv_sem=recv_sem,
                                          device_id=(x, y, 1 - c), device_id_type=MESH)
        cp.start()
        cp.wait()

    return pl.pallas_call(
        body, name="share_halves", in_specs=[ANY], out_specs=ANY, input_output_aliases={0: 0},
        out_shape=jax.ShapeDtypeStruct(both.shape, both.dtype),
        scratch_shapes=[pltpu.SemaphoreType.DMA, pltpu.SemaphoreType.DMA],
    )(both)


def gather_all(block):
    m_per = block.shape[0]

    def body(x_ref, out_ref, send_sems, recv_sems, local_sem):
        x, y, c = _position()
        me, sibling = (x, y, c), (x, y, 1 - c)
        chips = _other_chips(x, y)

        def rows_of(px, py, pc):
            return out_ref.at[pl.ds(pl.multiple_of((4 * px + 2 * py + pc) * m_per, 8), m_per), :]

        def copy(k, blk, to, src=None):
            return pltpu.make_async_remote_copy(
                src_ref=rows_of(*blk) if src is None else src, dst_ref=rows_of(*blk),
                send_sem=send_sems.at[k], recv_sem=recv_sems.at[k], device_id=to, device_id_type=MESH)

        mine = pltpu.make_async_copy(x_ref, rows_of(*me), local_sem)
        mine.start()
        first = [copy(0, me, sibling, src=x_ref)]
        first += [copy(1 + j, me, (*chip, c), src=x_ref) for j, chip in enumerate(chips)]
        for cp in first:
            cp.start()
        passed = [copy(4 + j, (*chip, c), sibling) for j, chip in enumerate(chips)]
        for j, chip in enumerate(chips):
            copy(1 + j, (*chip, c), me).wait_recv()
            passed[j].start()
        copy(0, sibling, me).wait_recv()
        for j, chip in enumerate(chips):
            copy(4 + j, (*chip, 1 - c), me).wait_recv()
        for cp in first + passed:
            cp.wait_send()
        mine.wait()

    return pl.pallas_call(
        body, name="gather_all", out_shape=jax.ShapeDtypeStruct((N_DEV * m_per, ROW), block.dtype),
        in_specs=[pl.BlockSpec(memory_space=pltpu.VMEM)], out_specs=pl.BlockSpec(memory_space=pltpu.VMEM),
        scratch_shapes=[pltpu.SemaphoreType.DMA((7,)), pltpu.SemaphoreType.DMA((7,)), pltpu.SemaphoreType.DMA],
        compiler_params=_params(),
    )(block)


SUM_ROWS = 592


def add_pair_halves(grads, landed, core):
    half = landed.shape[1]
    nblk = half // SUM_ROWS

    def body(c_ref, g_ref, l_ref, o_ref, o16_ref):
        acc = g_ref[...] + l_ref[...]
        o_ref[...] = acc
        o16_ref[...] = acc.astype(BF16)

    blk = (None, SUM_ROWS, ROW)
    o_spec = pl.BlockSpec(blk, lambda s, i, c: (s, i, 0))
    return pl.pallas_call(
        body, name="add_pair_halves",
        grid_spec=pltpu.PrefetchScalarGridSpec(
            num_scalar_prefetch=1, grid=(N_CHIPS, nblk),
            in_specs=[pl.BlockSpec(blk, lambda s, i, c: (s, c[0] * nblk + i, 0)), o_spec], out_specs=[o_spec, o_spec]),
        out_shape=[jax.ShapeDtypeStruct(landed.shape, F32), jax.ShapeDtypeStruct(landed.shape, BF16)],
        compiler_params=_params(("arbitrary", "arbitrary")),
    )(core, grads, landed)


def add_chip_partials(part, landed, slot, core):
    half = part.shape[1]
    nblk = half // SUM_ROWS

    def body(s_ref, c_ref, p_ref, l_ref, o_ref):
        o_ref[...] = ((p_ref[...] + l_ref[0].astype(F32)) + l_ref[1].astype(F32)) + l_ref[2].astype(F32)

    return pl.pallas_call(
        body, name="add_chip_partials",
        grid_spec=pltpu.PrefetchScalarGridSpec(
            num_scalar_prefetch=2, grid=(nblk,),
            in_specs=[pl.BlockSpec((None, SUM_ROWS, ROW), lambda i, s, c: (s[0], i, 0)), pl.BlockSpec((3, SUM_ROWS, ROW), lambda i, s, c: (0, i, 0))],
            out_specs=pl.BlockSpec((SUM_ROWS, ROW), lambda i, s, c: (c[0] * nblk + i, 0))),
        out_shape=jax.ShapeDtypeStruct((2 * half, ROW), F32), compiler_params=_params(("arbitrary",)),
    )(slot, core, part, landed)


def sum_devices(stacked):
    m = stacked.shape[1]

    def body(s_ref, o_ref):
        acc = s_ref[0]
        for d in range(1, N_DEV):
            acc = acc + s_ref[d]
        o_ref[...] = acc

    return pl.pallas_call(
        body, name="sum_devices", grid=(m // 8,), in_specs=[pl.BlockSpec((N_DEV, 8, ROW), lambda i: (0, i, 0))],
        out_specs=pl.BlockSpec((8, ROW), lambda i: (i, 0)), out_shape=jax.ShapeDtypeStruct((m, ROW), F32),
        compiler_params=_params(("arbitrary",)),
    )(stacked)


def adamw(name, w, g, m, v):
    width = w.shape[-1]
    n_rows = w.size // width
    nblk = _tile(n_rows, 256, 8)
    arg = rows(width, nblk)
    outs, _ = blocked_forward("adamw_" + name, adamw_fn, 1, n_rows // nblk, [arg] * 4, [t.reshape(n_rows, width) for t in (w, g, m, v)],
                              [], [], [arg] * 3, [(n_rows, width)] * 3, (), False)
    return [o.reshape(w.shape) for o in outs]


def trunk_loss(diff, p_emb, target, wts):
    x, small, gw, hd = diff["x"], diff["small"], diff["gw"], diff["hd"]
    t = x.shape[0]
    d = D_MODEL

    def norm_pair(name, fn, h, o, gain):
        op = tok_op(name, fn, t, 512, [(d, F32), (d, F32)], [(1, d)], [(d, F32), (d, ACT)])
        return op([h, o], [gain[None]])

    def ffn(tag, n, which, i):
        z = linear(f"{which}_in{tag}", ACT, (i,), N_CHIPS)(n, wts[which + "_w_in"], hd[which + "_w_in"][i])
        act = tok_op(f"{which}_act{tag}", swiglu_fn, t, 128, [(2 * FFN_DIM, ACT)], [], [(FFN_DIM, ACT)])([z], [])[0]
        return linear(f"{which}_out{tag}", F32, (i,))(act, wts[which + "_w_out"], hd[which + "_w_out"][i])

    h = x
    n = tok_op("norm_in", norm_fn, t, 512, [(d, F32)], [(1, d)], [(d, ACT)])([x], [small["ffn1_norm"][0][None]])[0]
    loss_rows = None
    for i in range(DEPTH):
        tag = str(i)
        p = {k: v[i] for k, v in small.items() if k != "final_norm"}
        p.update({k: v[i] for k, v in gw.items()})
        o = ffn(tag, n, "ffn1", i)
        h, u = norm_pair("mix_norm" + tag, make_addnorm_fn(0.5), h, o, p["mix_norm"])
        n_in = len(IN_PIECES)
        in_proj = multi_linear("in_proj" + tag, [F32] * n_in + [ACT], [None] * n_in + [(i,)], [0] * n_in + [N_CHIPS])
        proj = in_proj(u, wts["w_in"][i] + [wts["w_gate"]], hd["in_proj"][i])
        s5_u, lru_x, lru_g, m2_z, m2_xbc, m2_dt, gdn_qkv, gdn_g, gdn_ba, gate_logits = proj
        ys = [s5_mixer(tag, t, s5_u, p), lru_mixer(tag, t, lru_x, lru_g, p),
              m2_mixer(tag, t, m2_z, m2_xbc, m2_dt, p), gdn_mixer(tag, t, gdn_qkv, gdn_g, gdn_ba, p)]
        yb = [linear(f"branch{b}_{tag}", ACT, (i, b), N_CHIPS)(y, wts["w_branch"], hd["w_branch"][i][b]) for b, y in enumerate(ys)]
        merge = tok_op("gate_merge" + tag, gate_merge_fn, t, 128, [(N_BRANCH * d, ACT)] + [(d, ACT)] * N_BRANCH,
                       [(1, N_BRANCH * d)], [(d, ACT)])
        mixed = merge([gate_logits] + yb, [p["b_gate"][None]])[0]
        o = linear("w_out" + tag, F32, (i,))(mixed, wts["w_out"], hd["w_out"][i])
        h, n = norm_pair("ffn2_norm" + tag, make_addnorm_fn(1.0), h, o, p["ffn2_norm"])
        o = ffn(tag, n, "ffn2", i)
        h, n = norm_pair("ple_norm" + tag, make_addnorm_fn(0.5), h, o, p["ple_norm"])
        pg = linear("ple_gate" + tag, F32, (i,))(n, wts["ple_w_gate"], hd["ple_w_gate"][i])
        pp = linear("ple_proj" + tag, F32, (i,), N_CHIPS)(p_emb[i], wts["ple_w_proj"], hd["ple_w_proj"][i])
        if i + 1 < DEPTH:
            op = tok_op("ple" + tag, ple_fn, t, 512, [(d, F32)] * 3, [(1, d)], [(d, F32), (d, ACT)])
            h, n = op([h, pg, pp], [small["ffn1_norm"][i + 1][None]])
        else:
            op = tok_op("ple_loss", ple_loss_fn, t, 512, [(d, F32)] * 3 + [(d, F32, False)], [(1, d)], [(1, F32)])
            loss_rows = op([h, pg, pp, target], [small["final_norm"][None]])[0]
    return jnp.sum(loss_rows)


def kernel(x, p, ffn1_norm, ffn1_w_in, ffn1_w_out, mix_norm, w_in, w_gate, b_gate, s5_log_step, s5_a_re, s5_a_im, s5_b_re, s5_b_im, s5_c_re, s5_c_im, s5_d, s5_w_glu, s5_b_glu, lru_conv_w, lru_conv_b, lru_w_r, lru_b_r, lru_w_i, lru_b_i, lru_lambda, m2_conv_w, m2_conv_b, m2_dt_bias, m2_a_log, m2_d, m2_norm, gdn_conv_w, gdn_dt_bias, gdn_a_log, gdn_norm, w_branch, w_out, ffn2_norm, ffn2_w_in, ffn2_w_out, ple_norm, ple_w_gate, ple_w_proj, final_norm, loss_target, m_ffn1_norm, m_ffn1_w_in, m_ffn1_w_out, m_mix_norm, m_w_in, m_w_gate, m_b_gate, m_s5_log_step, m_s5_a_re, m_s5_a_im, m_s5_b_re, m_s5_b_im, m_s5_c_re, m_s5_c_im, m_s5_d, m_s5_w_glu, m_s5_b_glu, m_lru_conv_w, m_lru_conv_b, m_lru_w_r, m_lru_b_r, m_lru_w_i, m_lru_b_i, m_lru_lambda, m_m2_conv_w, m_m2_conv_b, m_m2_dt_bias, m_m2_a_log, m_m2_d, m_m2_norm, m_gdn_conv_w, m_gdn_dt_bias, m_gdn_a_log, m_gdn_norm, m_w_branch, m_w_out, m_ffn2_norm, m_ffn2_w_in, m_ffn2_w_out, m_ple_norm, m_ple_w_gate, m_ple_w_proj, m_final_norm, v_ffn1_norm, v_ffn1_w_in, v_ffn1_w_out, v_mix_norm, v_w_in, v_w_gate, v_b_gate, v_s5_log_step, v_s5_a_re, v_s5_a_im, v_s5_b_re, v_s5_b_im, v_s5_c_re, v_s5_c_im, v_s5_d, v_s5_w_glu, v_s5_b_glu, v_lru_conv_w, v_lru_conv_b, v_lru_w_r, v_lru_b_r, v_lru_w_i, v_lru_b_i, v_lru_lambda, v_m2_conv_w, v_m2_conv_b, v_m2_dt_bias, v_m2_a_log, v_m2_d, v_m2_norm, v_gdn_conv_w, v_gdn_dt_bias, v_gdn_a_log, v_gdn_norm, v_w_branch, v_w_out, v_ffn2_norm, v_ffn2_w_in, v_ffn2_w_out, v_ple_norm, v_ple_w_gate, v_ple_w_proj, v_final_norm):
    a = dict(locals())
    t = x.shape[1]
    core = lax.axis_index("c").astype(jnp.int32).reshape(1)
    slot = (2 * lax.axis_index("x") + lax.axis_index("y")).astype(jnp.int32).reshape(1)

    full = unpack_gathered(gather_weights(pack_for_gather(a), slot))
    exact = [n for n, spec in SHARDED.items() if spec[2]] + ["s5_w_glu"]
    gw = {n: full[n].astype(F32) for n in exact}
    wts = {n: full[n] for n in SHARDED if n not in exact}

    def handle(n):
        shape = SHARDED[n][0]
        return jnp.zeros((N_CHIPS, *shape) if n in COL_SHARDED else (N_CHIPS * shape[0], *shape[1:]), F32)

    hd = {n: [handle(n) for _ in range(DEPTH)] for n in wts if n not in ("w_in", "w_gate", "w_branch")}
    hd["w_branch"] = [[jnp.zeros((N_CHIPS, W, D_MODEL // N_CHIPS), F32) for _ in range(N_BRANCH)] for _ in range(DEPTH)]
    hd["in_proj"] = [[jnp.zeros((D_MODEL, _round_up(n, LANE)), F32) for _, n in IN_PIECES] + [handle("w_gate")] for _ in range(DEPTH)]
    diff = {"x": x.reshape(t, D_MODEL), "small": {n: a[n] for n in SMALL}, "gw": gw, "hd": hd}
    loss_local, vjp = jax.vjp(lambda dd: trunk_loss(dd, p.reshape(DEPTH, t, -1), loss_target.reshape(t, D_MODEL), wts), diff)
    (grads,) = vjp(jnp.ones((), F32))
    loss = lax.psum(loss_local, ("x", "y", "c"))
    grad_x = grads["x"].reshape(x.shape)

    gh = grads["hd"]
    big = {n: jnp.concatenate([g.reshape(N_CHIPS, -1) for g in gh[n]], axis=1) for n in gh if n not in ("in_proj", "w_branch")}
    big["w_in"] = shard_w_in_grads([gh["in_proj"][i][:-1] for i in range(DEPTH)])
    big["w_gate"] = jnp.concatenate([gh["in_proj"][i][-1].reshape(N_CHIPS, -1) for i in range(DEPTH)], axis=1)
    big["w_branch"] = jnp.concatenate([g.reshape(N_CHIPS, -1) for i in range(DEPTH) for g in gh["w_branch"][i]], axis=1)
    for n in exact:
        big[n] = shard_full_grads(n, [grads["gw"][n][i] for i in range(DEPTH)])

    packed = pack_grads(big)
    pair, pair16 = add_pair_halves(packed, swap_pair_halves(packed), core)
    mine = add_chip_partials(pair, exchange_chip_partials(pair16), slot, core)
    g_shard = unpack_shard(share_halves(mine))

    shapes = {n: a[n].shape for n in SMALL}
    gs_local = pack_small({n: grads["small"][n] for n in SMALL}, "", shapes)
    gs = gather_all(gs_local)
    g_small = sum_devices(gs.reshape(N_DEV, gs_local.shape[0], ROW))

    g_all = {**unpack_small(g_small, shapes), **g_shard}
    res = {n: [g_all[n]] + adamw(n, a[n], g_all[n], a["m_" + n], a["v_" + n]) for n in WEIGHTS}
    return (loss, grad_x, *[res[n][0] for n in WEIGHTS], *[res[n][1] for n in WEIGHTS],
            *[res[n][2] for n in WEIGHTS], *[res[n][3] for n in WEIGHTS])
```

```python
import functools
import math

import jax
import jax.numpy as jnp
from jax import lax
from jax.experimental import pallas as pl
from jax.experimental.pallas import tpu as pltpu

F32, BF16 = jnp.float32, jnp.bfloat16
ACT = BF16
EPS = 1e-6
D_MODEL = 1024
DEPTH = 2
FFN_DIM = 2816
BRANCH_WIDTH = 512
N_BRANCH = 4
LRU_C = 8.0
S5_GROUPS, S5_GROUP_CH, S5_STATE = 32, 16, 64
S5_W = S5_GROUPS * S5_STATE
LRU_HEADS, LRU_HEAD_DIM = 8, 64
M2_HEADS, M2_HEAD_DIM, M2_GROUPS, M2_STATE = 8, 64, 2, 128
GDN_HEADS, GDN_HEAD_DIM = 4, 128
CHUNK = 128
ADAM_LR, ADAM_B1, ADAM_B2, ADAM_EPS, ADAM_WD, ADAM_STEP = 0.001, 0.9, 0.999, 1e-08, 0.01, 10
VMEM_LIMIT_BYTES = 56 * 1024 * 1024
MESH = pl.DeviceIdType.MESH


def _params(sem=None):
    return pltpu.CompilerParams(vmem_limit_bytes=VMEM_LIMIT_BYTES, dimension_semantics=sem)


def _dot_bf16(a, b, dims):
    return lax.dot_general(a.astype(BF16), b.astype(BF16), (dims, ((), ())), preferred_element_type=F32)


def _make_mm(dot):
    @jax.custom_vjp
    def nn(a, b):
        return dot(a, b, ((1,), (0,)))

    @jax.custom_vjp
    def nt(a, b):
        return dot(a, b, ((1,), (1,)))

    @jax.custom_vjp
    def tn(a, b):
        return dot(a, b, ((0,), (0,)))

    nn.defvjp(lambda a, b: (nn(a, b), (a, b)), lambda r, g: (nt(g, r[1]), tn(r[0], g)))
    nt.defvjp(lambda a, b: (nt(a, b), (a, b)), lambda r, g: (nn(g, r[1]), tn(g, r[0])))
    tn.defvjp(lambda a, b: (tn(a, b), (a, b)), lambda r, g: (nt(r[1], g), nn(r[0], g)))
    return nn, nt, tn


def _dot_bf16x3(a, b, dims):
    a_hi, b_hi = a.astype(BF16), b.astype(BF16)
    a_lo = (a - a_hi.astype(F32)).astype(BF16)
    b_lo = (b - b_hi.astype(F32)).astype(BF16)

    def dot(p, q):
        return lax.dot_general(p, q, (dims, ((), ())), preferred_element_type=F32)

    return dot(a_hi, b_hi) + (dot(a_hi, b_lo) + dot(a_lo, b_hi))


mm, mm_nt, mm_tn = _make_mm(_dot_bf16)
mmh, mmh_nt, mmh_tn = _make_mm(_dot_bf16x3)


def _row_ids(shape):
    return lax.broadcasted_iota(jnp.int32, shape, 0)


def _shift_down(x, d):
    return jnp.where(_row_ids(x.shape) >= d, pltpu.roll(x, d, 0), 0.0)


def _shift_up(x, d):
    n = x.shape[0]
    return jnp.where(_row_ids(x.shape) < n - d, pltpu.roll(x, n - d, 0), 0.0)


def _first_row(x):
    return jnp.sum(jnp.where(_row_ids(x.shape) == 0, x, 0.0), axis=0, keepdims=True)


def last_row(x):
    return jnp.sum(jnp.where(_row_ids(x.shape) == x.shape[0] - 1, x, 0.0), axis=0, keepdims=True)


def pick_row(x, j):
    return jnp.sum(jnp.where(_row_ids(x.shape) == j, x, 0.0), axis=0, keepdims=True)


@jax.custom_vjp
def lin_scan(a, b, h0):
    n = a.shape[0]
    row = _row_ids(a.shape)
    acc_a = a
    acc_b = b + jnp.where(row == 0, a * h0, 0.0)
    d = 1
    while d < n:
        acc_b = acc_a * _shift_down(acc_b, d) + acc_b
        acc_a = acc_a * jnp.where(row >= d, pltpu.roll(acc_a, d, 0), 1.0)
        d *= 2
    return acc_b


def _lin_scan_fwd(a, b, h0):
    h = lin_scan(a, b, h0)
    return h, (a, h, h0)


def _lin_scan_bwd(res, dh):
    a, h, h0 = res
    n = a.shape[0]
    row = _row_ids(a.shape)
    acc_a = _shift_up(a, 1)
    g = dh
    d = 1
    while d < n:
        g = acc_a * _shift_up(g, d) + g
        acc_a = acc_a * jnp.where(row < n - d, pltpu.roll(acc_a, n - d, 0), 1.0)
        d *= 2
    h_prev = _shift_down(h, 1) + jnp.where(row == 0, h0, 0.0)
    return g * h_prev, g, _first_row(a * g)


lin_scan.defvjp(_lin_scan_fwd, _lin_scan_bwd)


def _cscan(br, bi, ar, ai, up):
    n = br.shape[0]
    shift = _shift_up if up else _shift_down
    hr, hi, pr, pi = br, bi, ar, ai
    d = 1
    while d < n:
        sr, si = shift(hr, d), shift(hi, d)
        hr, hi = hr + pr * sr - pi * si, hi + pr * si + pi * sr
        pr, pi = pr * pr - pi * pi, 2.0 * pr * pi
        d *= 2
    return hr, hi


@jax.custom_vjp
def complex_scan(br, bi, ar, ai, h0r, h0i):
    first = _row_ids(br.shape) == 0
    br = br + jnp.where(first, ar * h0r - ai * h0i, 0.0)
    bi = bi + jnp.where(first, ar * h0i + ai * h0r, 0.0)
    return _cscan(br, bi, ar, ai, False)


def _complex_scan_fwd(br, bi, ar, ai, h0r, h0i):
    hr, hi = complex_scan(br, bi, ar, ai, h0r, h0i)
    return (hr, hi), (ar, ai, hr, hi, h0r, h0i)


def _complex_scan_bwd(res, cts):
    ar, ai, hr, hi, h0r, h0i = res
    gr, gi = _cscan(cts[0], cts[1], ar, -ai, True)
    first = _row_ids(hr.shape) == 0
    pr = _shift_down(hr, 1) + jnp.where(first, h0r, 0.0)
    pi = _shift_down(hi, 1) + jnp.where(first, h0i, 0.0)
    d_ar = jnp.sum(gr * pr + gi * pi, axis=0, keepdims=True)
    d_ai = jnp.sum(gi * pr - gr * pi, axis=0, keepdims=True)
    g0r, g0i = _first_row(gr), _first_row(gi)
    return gr, gi, d_ar, d_ai, ar * g0r + ai * g0i, ar * g0i - ai * g0r


complex_scan.defvjp(_complex_scan_fwd, _complex_scan_bwd)

TAIL = 8


@jax.custom_vjp
def tail_rows(x):
    return x[x.shape[0] - TAIL:, :]


tail_rows.defvjp(
    lambda x: (tail_rows(x), x.shape[0]),
    lambda n, g: (jnp.concatenate([jnp.zeros((n - TAIL, g.shape[1]), g.dtype), g], axis=0),),
)


def _make_shift_tail(d):
    @jax.custom_vjp
    def shifted(x, tail):
        n = x.shape[0]
        tpad = jnp.concatenate([tail, jnp.zeros((n - TAIL, x.shape[1]), x.dtype)], axis=0)
        return jnp.where(_row_ids(x.shape) >= d, pltpu.roll(x, d, 0), pltpu.roll(tpad, n + d - TAIL, 0))

    def fwd(x, tail):
        return shifted(x, tail), None

    def bwd(_, g):
        g8 = g[:TAIL, :]
        dtail = jnp.where(_row_ids(g8.shape) >= TAIL - d, pltpu.roll(g8, TAIL - d, 0), 0.0)
        return _shift_up(g, d), dtail

    shifted.defvjp(fwd, bwd)
    return shifted


_SHIFT_TAIL = {d: _make_shift_tail(d) for d in (1, 2, 3)}


def causal_conv4(x, tail, w):
    y = pick_row(w, 3) * x
    for j in range(3):
        y = y + pick_row(w, j) * _SHIFT_TAIL[3 - j](x, tail)
    return y


def rmsnorm(x, g):
    return x * lax.rsqrt(jnp.mean(x * x, axis=-1, keepdims=True) + EPS) * g


def to_row(col):
    n = col.shape[0]
    eye = lax.broadcasted_iota(jnp.int32, (n, n), 0) == lax.broadcasted_iota(jnp.int32, (n, n), 1)
    return jnp.sum(jnp.where(eye, col, 0.0), axis=0, keepdims=True)


def causal_decay(a_col):
    n = a_col.shape[0]
    causal = lax.broadcasted_iota(jnp.int32, (n, n), 0) >= lax.broadcasted_iota(jnp.int32, (n, n), 1)
    cs = jnp.sum(jnp.where(causal, to_row(a_col), 0.0), axis=1, keepdims=True)
    diff = cs - to_row(cs)
    return cs, jnp.where(causal, jnp.exp(jnp.where(causal, diff, 0.0)), 0.0)


class Arg:
    def __init__(self, block, imap, dtype=F32, grad=True, shared=True):
        self.block, self.imap, self.dtype, self.grad, self.shared = block, imap, dtype, grad, shared


def rows(width, nblk, col=lambda g: 0, dtype=F32, grad=True):
    return Arg((nblk, width), lambda g, c: (c, col(g)), dtype, grad)


def head_rows(width, nblk, head=lambda g: g, dtype=F32, grad=True):
    return Arg((None, nblk, width), lambda g, c: (head(g), c, 0), dtype, grad)


def whole(shape, grad=True):
    return Arg(tuple(shape), lambda g, c: (0,) * len(shape), F32, grad, shared=True)


def per_group(shape, idx=lambda g: g, grad=True):
    return Arg((None,) + tuple(shape), lambda g, c: (idx(g),) + (0,) * len(shape), F32, grad, shared=False)


def _bshape(block):
    return tuple(b for b in block if b is not None)


def _spec(arg, nb=None):
    if nb is None:
        return pl.BlockSpec(arg.block, arg.imap)
    return pl.BlockSpec(arg.block, lambda g, c: arg.imap(g, nb - 1 - c))


def blocked_forward(name, fn, groups, nb, tok_args, tok, const_args, consts, out_args, out_shapes, state_shapes, save,
                    reverse=False):
    n_tok, n_const, n_out, n_state = len(tok), len(consts), len(out_args), len(state_shapes)
    walk = nb if reverse else None

    def body(*refs):
        tok_refs = refs[:n_tok]
        const_refs = refs[n_tok:n_tok + n_const]
        out_refs = refs[n_tok + n_const:n_tok + n_const + n_out]
        pos = n_tok + n_const + n_out
        save_refs = refs[pos:pos + (n_state if save else 0)]
        state_refs = refs[len(refs) - n_state:] if n_state else ()

        @pl.when(pl.program_id(1) == 0)
        def _():
            for s in state_refs:
                s[...] = jnp.zeros_like(s)

        states = [s[...] for s in state_refs]
        for sr, s in zip(save_refs, states):
            sr[...] = s
        new_states, outs = fn(states, [r[...].astype(F32) for r in tok_refs], [r[...] for r in const_refs])
        for o_ref, o in zip(out_refs, outs):
            o_ref[...] = o.astype(o_ref.dtype)
        for s_ref, s in zip(state_refs, new_states):
            s_ref[...] = s

    out_specs = [_spec(a, walk) for a in out_args]
    out_shape = [jax.ShapeDtypeStruct(s, a.dtype) for s, a in zip(out_shapes, out_args)]
    if save:
        assert not reverse
        for s in state_shapes:
            out_specs.append(pl.BlockSpec((None, None) + tuple(s), lambda g, c, k=len(s): (g, c) + (0,) * k))
            out_shape.append(jax.ShapeDtypeStruct((groups, nb) + tuple(s), F32))
    res = pl.pallas_call(
        body, name=name, grid=(groups, nb),
        in_specs=[_spec(a, walk) for a in tok_args] + [_spec(a, walk) for a in const_args],
        out_specs=out_specs, out_shape=out_shape,
        scratch_shapes=[pltpu.VMEM(tuple(s), F32) for s in state_shapes],
        compiler_params=_params(("arbitrary", "arbitrary")),
    )(*tok, *consts)
    return list(res[:n_out]), list(res[n_out:])


def blocked_backward(name, fn, groups, nb, tok_args, tok, const_args, consts, out_args, cts, state_shapes, saved):
    n_tok, n_const, n_out, n_state = len(tok), len(consts), len(out_args), len(state_shapes)
    tok_g = [i for i, a in enumerate(tok_args) if a.grad]
    const_g = [i for i, a in enumerate(const_args) if a.grad]

    def body(*refs):
        tok_refs = refs[:n_tok]
        const_refs = refs[n_tok:n_tok + n_const]
        pos = n_tok + n_const
        saved_refs = refs[pos:pos + n_state]
        ct_refs = refs[pos + n_state:pos + n_state + n_out]
        pos += n_state + n_out
        dtok_refs = refs[pos:pos + len(tok_g)]
        dconst_refs = refs[pos + len(tok_g):pos + len(tok_g) + len(const_g)]
        dstate_refs = refs[len(refs) - n_state:] if n_state else ()
        g_id, c_id = pl.program_id(0), pl.program_id(1)

        @pl.when(c_id == 0)
        def _():
            for s in dstate_refs:
                s[...] = jnp.zeros_like(s)
            for r, i in zip(dconst_refs, const_g):
                if not const_args[i].shared:
                    r[...] = jnp.zeros_like(r)

        @pl.when((c_id == 0) & (g_id == 0))
        def _():
            for r, i in zip(dconst_refs, const_g):
                if const_args[i].shared:
                    r[...] = jnp.zeros_like(r)

        tok_vals = [r[...].astype(F32) for r in tok_refs]
        const_vals = [r[...] for r in const_refs]

        def f(states, tok_d, const_d):
            tv, cv = list(tok_vals), list(const_vals)
            for i, v in zip(tok_g, tok_d):
                tv[i] = v
            for i, v in zip(const_g, const_d):
                cv[i] = v
            return fn(states, tv, cv)

        _, vjp = jax.vjp(f, [r[...] for r in saved_refs], [tok_vals[i] for i in tok_g], [const_vals[i] for i in const_g])
        dstates, dtok, dconst = vjp(([r[...] for r in dstate_refs], [r[...].astype(F32) for r in ct_refs]))
        for r, v in zip(dtok_refs, dtok):
            r[...] = v.astype(r.dtype)
        for r, v in zip(dconst_refs, dconst):
            r[...] += v
        for r, v in zip(dstate_refs, dstates):
            r[...] = v

    in_specs = [_spec(a, nb) for a in tok_args] + [_spec(a, nb) for a in const_args]
    for s in state_shapes:
        in_specs.append(pl.BlockSpec((None, None) + tuple(s), lambda g, c, k=len(s): (g, nb - 1 - c) + (0,) * k))
    in_specs += [_spec(a, nb) for a in out_args]
    out_specs = [_spec(tok_args[i], nb) for i in tok_g] + [_spec(const_args[i], nb) for i in const_g]
    out_shape = [jax.ShapeDtypeStruct(tok[i].shape, tok[i].dtype) for i in tok_g]
    out_shape += [jax.ShapeDtypeStruct(consts[i].shape, F32) for i in const_g]
    res = pl.pallas_call(
        body, name=name, grid=(groups, nb), in_specs=in_specs, out_specs=out_specs, out_shape=out_shape,
        scratch_shapes=[pltpu.VMEM(tuple(s), F32) for s in state_shapes],
        compiler_params=_params(("arbitrary", "arbitrary")),
    )(*tok, *consts, *saved, *cts)
    dtok = [None] * n_tok
    dconst = [None] * n_const
    for i, v in zip(tok_g, res[:len(tok_g)]):
        dtok[i] = v
    for i, v in zip(const_g, res[len(tok_g):]):
        dconst[i] = v
    return dtok, dconst


def blocked_op(name, fn, groups, nb, tok_args, const_args, out_args, out_shapes, state_shapes=()):
    state_shapes = tuple(state_shapes)

    @jax.custom_vjp
    def op(tok, consts):
        outs, _ = blocked_forward(name, fn, groups, nb, tok_args, tok, const_args, consts, out_args, out_shapes, state_shapes, False)
        return outs

    def fwd(tok, consts):
        outs, saved = blocked_forward(name, fn, groups, nb, tok_args, tok, const_args, consts, out_args, out_shapes, state_shapes, True)
        return outs, (tok, consts, saved)

    def bwd(res, cts):
        tok, consts, saved = res
        dtok, dconst = blocked_backward(name + "_bwd", fn, groups, nb, tok_args, tok, const_args, consts, out_args, list(cts), state_shapes, saved)
        dtok = [jnp.zeros_like(t) if d is None else d for t, d in zip(tok, dtok)]
        dconst = [jnp.zeros_like(k) if d is None else d for k, d in zip(consts, dconst)]
        return dtok, dconst

    op.defvjp(fwd, bwd)
    return op


def _make_split(sizes):
    offs = [sum(sizes[:i]) for i in range(len(sizes))]

    @jax.custom_vjp
    def split(x):
        return tuple(x[:, o:o + s] for o, s in zip(offs, sizes))

    split.defvjp(lambda x: (split(x), None), lambda _, g: (jnp.concatenate(list(g), axis=1),))
    return split


def _make_join(sizes):
    offs = [sum(sizes[:i]) for i in range(len(sizes))]

    @jax.custom_vjp
    def join(parts):
        return jnp.concatenate(list(parts), axis=1)

    join.defvjp(lambda parts: (join(parts), None), lambda _, g: (tuple(g[:, o:o + s] for o, s in zip(offs, sizes)),))
    return join


def split_cols(x, sizes):
    return _make_split(tuple(sizes))(x)


def join_cols(parts):
    return _make_join(tuple(p.shape[1] for p in parts))(tuple(parts))


def lane_scalar(row, j):
    lane = lax.broadcasted_iota(jnp.int32, row.shape, 1)
    return jnp.sum(jnp.where(lane == j, row, 0.0), axis=1, keepdims=True)


def lane_col(blk, j):
    lane = lax.broadcasted_iota(jnp.int32, blk.shape, 1)
    return jnp.sum(jnp.where(lane == j, blk, 0.0), axis=1, keepdims=True)


LANE = 128
MM_ROWS = 512
MM_TILE_M, MM_TILE_N = 1024, 1536
MM_TILE_MT = 1408


def _tile(n, cap, unit):
    if n <= cap:
        return n
    best = None
    for t in range(unit, cap + 1, unit):
        if n % t == 0:
            best = t
    assert best is not None, (n, cap, unit)
    return best


def _matmul_resident(a, b, mode, add, out_dtype, name, lead, shards):
    m, k = a.shape
    rows_b, cols_b = b.shape[-2:]
    n = cols_b * max(shards, 1) if mode == "nn" else rows_b
    tm = _tile(m, MM_ROWS, 8)
    nolead = (None,) * len(lead)
    if shards:
        w_spec = pl.BlockSpec((shards,) + nolead + (rows_b, cols_b), lambda i: (0,) + lead + (0, 0))
    else:
        w_spec = pl.BlockSpec(nolead + (rows_b, cols_b), lambda i: lead + (0, 0))
    has_add = add is not None
    dims = ((1,), (0,)) if mode == "nn" else ((1,), (1,))

    def body(*refs):
        a_ref, w_ref, o_ref = refs[0], refs[1], refs[-1]
        if shards and mode == "nn":
            lhs = a_ref[...].astype(BF16)
            for s in range(shards):
                cols = slice(s * cols_b, (s + 1) * cols_b)
                part = _dot_bf16(lhs, w_ref[s], dims)
                if has_add:
                    part = part + refs[2][:, cols].astype(F32)
                o_ref[:, cols] = part.astype(o_ref.dtype)
            return
        if shards:
            acc = _dot_bf16(a_ref[:, 0:cols_b], w_ref[0], dims)
            for s in range(1, shards):
                acc = acc + _dot_bf16(a_ref[:, s * cols_b:(s + 1) * cols_b], w_ref[s], dims)
        else:
            acc = _dot_bf16(a_ref[...], w_ref[...], dims)
        if has_add:
            acc = acc + refs[2][...].astype(F32)
        o_ref[...] = acc.astype(o_ref.dtype)

    o_spec = pl.BlockSpec((tm, n), lambda i: (i, 0))
    return pl.pallas_call(
        body, name=name, grid=(m // tm,),
        in_specs=[pl.BlockSpec((tm, k), lambda i: (i, 0)), w_spec] + ([o_spec] if has_add else []),
        out_specs=o_spec, out_shape=jax.ShapeDtypeStruct((m, n), out_dtype), compiler_params=_params(("parallel",)),
    )(*([a, b] + ([add] if has_add else [])))


def matmul(a, b, mode="nn", add=None, out_dtype=F32, name="matmul", pre=None, col_shards=0):
    if mode != "tn":
        return _matmul_resident(a, b, mode, add, out_dtype, name, () if pre is None else tuple(pre), col_shards)
    assert add is None and pre is None
    (k, m), n = a.shape, b.shape[1]
    shard_n = n // col_shards if col_shards else n
    tm, tn, tk = _tile(m, MM_TILE_MT, LANE), _tile(shard_n, MM_TILE_N, LANE), _tile(k, MM_TILE_M, LANE)
    nk, qn = k // tk, shard_n // tn

    def body(a_ref, b_ref, o_ref, acc_ref):
        l = pl.program_id(2)

        @pl.when(l == 0)
        def _():
            acc_ref[...] = jnp.zeros_like(acc_ref)

        acc_ref[...] += _dot_bf16(a_ref[...], b_ref[...], ((0,), (0,)))

        @pl.when(l == nk - 1)
        def _():
            o_ref[...] = acc_ref[...].astype(o_ref.dtype)

    if col_shards:
        o_spec = pl.BlockSpec((None, tm, tn), lambda i, j, l: (j // qn, i, j % qn))
        out_shape = jax.ShapeDtypeStruct((col_shards, m, shard_n), out_dtype)
    else:
        o_spec = pl.BlockSpec((tm, tn), lambda i, j, l: (i, j))
        out_shape = jax.ShapeDtypeStruct((m, n), out_dtype)
    return pl.pallas_call(
        body, name=name, grid=(m // tm, n // tn, nk),
        in_specs=[pl.BlockSpec((tk, tm), lambda i, j, l: (l, i)), pl.BlockSpec((tk, tn), lambda i, j, l: (l, j))],
        out_specs=o_spec, out_shape=out_shape, scratch_shapes=[pltpu.VMEM((tm, tn), F32)],
        compiler_params=_params(("parallel", "parallel", "arbitrary")),
    )(a, b)


def linear(name, out_dtype=F32, pre=None, col_shards=0):
    @jax.custom_vjp
    def op(a, w, handle):
        return matmul(a, w, "nn", out_dtype=out_dtype, name=name, pre=pre, col_shards=col_shards)

    def fwd(a, w, handle):
        return op(a, w, handle), (a, w)

    def bwd(res, g):
        a, w = res
        da = matmul(g, w, "nt", out_dtype=a.dtype, name=name + "_da", pre=pre, col_shards=col_shards)
        dw = matmul(a, g, "tn", out_dtype=F32, name=name + "_dw", col_shards=col_shards)
        return da, jnp.zeros_like(w), dw

    op.defvjp(fwd, bwd)
    return op


def multi_linear(name, out_dtypes, pres, shards):
    sel = [dict(pre=p, col_shards=s) for p, s in zip(pres, shards)]

    @jax.custom_vjp
    def op(a, ws, handles):
        return [matmul(a, w, "nn", out_dtype=dt, name=f"{name}{i}", **sel[i]) for i, (w, dt) in enumerate(zip(ws, out_dtypes))]

    def fwd(a, ws, handles):
        return op(a, ws, handles), (a, ws)

    def bwd(res, gs):
        a, ws = res
        acc = None
        for i, (g, w) in enumerate(zip(gs, ws)):
            last = i == len(ws) - 1
            acc = matmul(g, w, "nt", add=acc, out_dtype=a.dtype if last else F32, name=f"{name}{i}_da", **sel[i])
        dws = [matmul(a, g, "tn", out_dtype=F32, name=f"{name}{i}_dw", col_shards=shards[i]) for i, g in enumerate(gs)]
        return acc, [jnp.zeros_like(w) for w in ws], dws

    op.defvjp(fwd, bwd)
    return op


def dense(name, out_dtype=F32):
    @jax.custom_vjp
    def op(a, w):
        return matmul(a, w, "nn", out_dtype=out_dtype, name=name)

    def fwd(a, w):
        return op(a, w), (a, w)

    def bwd(res, g):
        a, w = res
        return (matmul(g, w, "nt", out_dtype=a.dtype, name=name + "_da"),
                matmul(a, g, "tn", out_dtype=w.dtype, name=name + "_dw"))

    op.defvjp(fwd, bwd)
    return op


def to_col(row):
    n = row.shape[1]
    eye = lax.broadcasted_iota(jnp.int32, (n, n), 0) == lax.broadcasted_iota(jnp.int32, (n, n), 1)
    return jnp.sum(jnp.where(eye, row, 0.0), axis=1, keepdims=True)


def norm_fn(states, toks, consts):
    return [], [rmsnorm(toks[0], consts[0])]


def make_addnorm_fn(scale):
    def fn(states, toks, consts):
        h = toks[0] + scale * toks[1]
        return [], [h, rmsnorm(h, consts[0])]

    return fn


def swiglu_fn(states, toks, consts):
    gate, up = split_cols(toks[0], (FFN_DIM, FFN_DIM))
    return [], [jax.nn.silu(gate) * up]


def gate_merge_fn(states, toks, consts):
    gates = split_cols(jax.nn.sigmoid(toks[0] + consts[0]), (D_MODEL,) * N_BRANCH)
    mixed = gates[0] * toks[1]
    for n in range(1, N_BRANCH):
        mixed = mixed + gates[n] * toks[1 + n]
    return [], [mixed]


def ple_fn(states, toks, consts):
    h = toks[0] + jax.nn.sigmoid(toks[1]) * toks[2]
    return [], [h, rmsnorm(h, consts[0])]


def ple_loss_fn(states, toks, consts):
    h = toks[0] + jax.nn.sigmoid(toks[1]) * toks[2]
    err = rmsnorm(h, consts[0]) - toks[3]
    return [], [0.5 * jnp.mean(err * err, axis=-1, keepdims=True)]


def s5_discretise_fn(states, toks, consts):
    log_step, a_re, a_im, b_re, b_im = consts
    step = jnp.exp(log_step)
    mag = jnp.exp(a_re * step)
    ab_re, ab_im = mag * jnp.cos(a_im * step), mag * jnp.sin(a_im * step)
    den = a_re * a_re + a_im * a_im
    num_re = ab_re - 1.0
    f_re = (num_re * a_re + ab_im * a_im) / den
    f_im = (ab_im * a_re - num_re * a_im) / den
    return [], [ab_re, ab_im, f_re * b_re - f_im * b_im, f_re * b_im + f_im * b_re]


def s5_scan_fn(states, toks, consts):
    b_re, b_im = split_cols(toks[0], (S5_W, S5_W))
    h_re, h_im = complex_scan(b_re, b_im, consts[0], consts[1], states[0], states[1])
    return [last_row(h_re), last_row(h_im)], [join_cols([h_re, h_im])]


def s5_scan_bwd_fn(states, toks, consts):
    g_re, g_im, acc_re, acc_im = states
    h, ct, h0_re, h0_im = toks
    a_re, a_im = consts
    h_re, h_im = split_cols(h, (S5_W, S5_W))
    c_re, c_im = split_cols(ct, (S5_W, S5_W))
    last = _row_ids(c_re.shape) == c_re.shape[0] - 1
    cts = (c_re + jnp.where(last, g_re, 0.0), c_im + jnp.where(last, g_im, 0.0))
    d_re, d_im, da_re, da_im, d0_re, d0_im = _complex_scan_bwd((a_re, a_im, h_re, h_im, h0_re, h0_im), cts)
    acc_re, acc_im = acc_re + da_re, acc_im + da_im
    return [d0_re, d0_im, acc_re, acc_im], [join_cols([d_re, d_im]), acc_re, acc_im]


def s5_scan(name, t, bu, a_re, a_im):
    nblk = min(SCAN_ROWS, t)
    nb = t // nblk
    wide, row = rows(2 * S5_W, nblk, dtype=ACT), whole((1, S5_W))
    entry = Arg((None, None, 1, S5_W), lambda g, c: (0, c, 0, 0))
    state = [(1, S5_W)] * 2

    def run(bu, a_re, a_im, save):
        return blocked_forward(name, s5_scan_fn, 1, nb, [wide], [bu], [row, row], [a_re, a_im], [wide], [(t, 2 * S5_W)], state, save)

    @jax.custom_vjp
    def op(bu, a_re, a_im):
        return run(bu, a_re, a_im, False)[0][0]

    def fwd(bu, a_re, a_im):
        outs, saved = run(bu, a_re, a_im, True)
        return outs[0], (outs[0], saved, a_re, a_im)

    def bwd(res, ct):
        h, saved, a_re, a_im = res
        outs, _ = blocked_forward(name + "_bwd", s5_scan_bwd_fn, 1, nb, [wide, wide, entry, entry], [h, ct] + saved, [row, row],
                                  [a_re, a_im], [wide, row, row], [(t, 2 * S5_W), (1, S5_W), (1, S5_W)], state * 2, False, reverse=True)
        return tuple(outs)

    op.defvjp(fwd, bwd)
    return op(bu, a_re, a_im)


def s5_glu_fn(states, toks, consts):
    d_skip, w_glu, b_glu = consts
    z = jax.nn.gelu(toks[0] + d_skip * toks[1])
    return [], [z * jax.nn.sigmoid(mm(z, w_glu) + b_glu)]


def lru_fn(states, toks, consts):
    h0, tail = states
    x, gate = toks
    conv_w, conv_b, w_r, b_r, w_i, b_i, lam = consts
    xc = causal_conv4(x, tail, conv_w) + conv_b
    r = jax.nn.sigmoid(mm(xc, w_r) + b_r)
    i_g = jax.nn.sigmoid(mm(xc, w_i) + b_i)
    log_a = -LRU_C * r * jax.nn.softplus(-lam)
    inp = jnp.sqrt(1.0 - jnp.exp(2.0 * log_a)) * (i_g * xc)
    h = lin_scan(jnp.exp(log_a), inp, h0)
    return [last_row(h), tail_rows(x)], [h * jax.nn.gelu(gate)]


def m2_conv_fn(states, toks, consts):
    y = jax.nn.silu(causal_conv4(toks[0], states[0], consts[0]) + consts[1])
    return [tail_rows(toks[0])], list(split_cols(y, (BRANCH_WIDTH, M2_GROUPS * M2_STATE, M2_GROUPS * M2_STATE)))


def gdn_conv_fn(states, toks, consts):
    y = jax.nn.silu(causal_conv4(toks[0], states[0], consts[0]))
    return [tail_rows(toks[0])], list(split_cols(y, (BRANCH_WIDTH,) * 3))


def ssd_fn(states, toks, consts):
    xs, bm, cm, small = toks
    dt_bias, a_log, d_skip = consts
    x_pairs = split_cols(xs, (LANE,) * 4)
    b_g = split_cols(bm, (M2_STATE,) * M2_GROUPS)
    c_g = split_cols(cm, (M2_STATE,) * M2_GROUPS)
    lo = lax.broadcasted_iota(jnp.int32, (1, LANE), 1) < M2_HEAD_DIM
    new_states, y_pairs = [], []
    for g in range(M2_GROUPS):
        scores = mm_nt(c_g[g], b_g[g])
        y_off = split_cols(mm_nt(c_g[g], states[g]), (LANE, LANE))
        to_end, ends = [], []
        for j in range(2):
            pair = 2 * g + j
            x2 = x_pairs[pair]
            dts, css, decays, end = [], [], [], []
            for h in (2 * pair, 2 * pair + 1):
                dt = jax.nn.softplus(lane_col(small, h) + lane_scalar(dt_bias, h))
                a = dt * (-jnp.exp(lane_scalar(a_log, h)))
                cs, decay = causal_decay(a)
                dts.append(dt)
                css.append(cs)
                decays.append(decay)
                end.append(jnp.sum(a, axis=0, keepdims=True))
            xdt = x2 * jnp.where(lo, dts[0], dts[1])
            y = mm(scores * decays[0], jnp.where(lo, xdt, 0.0)) + mm(scores * decays[1], jnp.where(lo, 0.0, xdt))
            cs2 = jnp.where(lo, css[0], css[1])
            end2 = jnp.where(lo, end[0], end[1])
            y = y + y_off[j] * jnp.exp(cs2)
            y = y + jnp.where(lo, lane_scalar(d_skip, 2 * pair), lane_scalar(d_skip, 2 * pair + 1)) * x2
            y_pairs.append(y)
            to_end.append(xdt * jnp.exp(end2 - cs2))
            ends.append(end2)
        chunk_decay = jnp.exp(to_col(join_cols(ends)))
        new_states.append(states[g] * chunk_decay + mm_tn(join_cols(to_end), b_g[g]))
    return new_states, [join_cols(y_pairs)]


def m2_post_fn(states, toks, consts):
    return [], [rmsnorm(toks[0] * jax.nn.silu(toks[1]), consts[0])]


@jax.custom_vjp
def nilpotent_inverse(n_mat):
    size = n_mat.shape[0]
    eye = lax.broadcasted_iota(jnp.int32, n_mat.shape, 0) == lax.broadcasted_iota(jnp.int32, n_mat.shape, 1)
    inv = jnp.where(eye, 1.0, 0.0) + n_mat
    power = n_mat
    d = 2
    while d < size:
        power = _dot_bf16x3(power, power, ((1,), (0,)))
        inv = inv + _dot_bf16x3(inv, power, ((1,), (0,)))
        d *= 2
    return inv


def _nilpotent_inverse_fwd(n_mat):
    inv = nilpotent_inverse(n_mat)
    return inv, inv


def _nilpotent_inverse_bwd(inv, g):
    return (_dot_bf16x3(inv, _dot_bf16x3(g, inv, ((1,), (1,))), ((0,), (0,))),)


nilpotent_inverse.defvjp(_nilpotent_inverse_fwd, _nilpotent_inverse_bwd)


def gdn_head(state, q, k, v, gate, a_raw, b_raw, dt_bias, a_log, norm_g):
    n = q.shape[0]
    qn = q * lax.rsqrt(jnp.sum(q * q, axis=-1, keepdims=True) + EPS) * (GDN_HEAD_DIM ** -0.5)
    kn = k * lax.rsqrt(jnp.sum(k * k, axis=-1, keepdims=True) + EPS)
    beta = jax.nn.sigmoid(b_raw)
    g = -jnp.exp(a_log) * jax.nn.softplus(a_raw + dt_bias)
    cs, decay = causal_decay(g)
    row = lax.broadcasted_iota(jnp.int32, (n, n), 0)
    col = lax.broadcasted_iota(jnp.int32, (n, n), 1)
    kb = kn * beta
    inv = nilpotent_inverse(-jnp.where(row > col, mm_nt(kb, kn) * decay, 0.0))
    ecs = jnp.exp(cs)
    u = mmh(inv, v * beta)
    w = mmh(inv, kb * ecs)
    qk = mm_nt(qn, kn) * decay
    cs_end = jnp.sum(g, axis=0, keepdims=True)
    v_new = u - mm(w, state)
    o = mm(qn * ecs, state) + mm(qk, v_new)
    new_state = state * jnp.exp(cs_end) + mm_tn(kn * jnp.exp(cs_end - cs), v_new)
    return new_state, rmsnorm(o, norm_g) * jax.nn.silu(gate)


def gdn_fn(states, toks, consts):
    q, k, v, gate, small = toks
    dt_bias, a_log, norm_g = consts
    heads = (GDN_HEAD_DIM,) * GDN_HEADS
    qs, ks, vs, gs = split_cols(q, heads), split_cols(k, heads), split_cols(v, heads), split_cols(gate, heads)
    new_states, ys = [], []
    for h in range(GDN_HEADS):
        st, y = gdn_head(states[h], qs[h], ks[h], vs[h], gs[h], lane_col(small, GDN_HEADS + h), lane_col(small, h),
                         lane_scalar(dt_bias, h), lane_scalar(a_log, h), norm_g)
        new_states.append(st)
        ys.append(y)
    return new_states, [join_cols(ys)]


def adamw_fn(states, toks, consts):
    w, g, m, v = toks
    m = ADAM_B1 * m + (1.0 - ADAM_B1) * g
    v = ADAM_B2 * v + (1.0 - ADAM_B2) * (g * g)
    m_hat = m / (1.0 - ADAM_B1 ** ADAM_STEP)
    v_hat = v / (1.0 - ADAM_B2 ** ADAM_STEP)
    return [], [-ADAM_LR * (m_hat / (jnp.sqrt(v_hat) + ADAM_EPS) + ADAM_WD * w), m, v]


def tok_op(name, fn, t, nblk, tok, consts, outs, states=()):
    nblk = min(nblk, t)
    tok_args = [rows(e[0], nblk, dtype=e[1], grad=e[2] if len(e) > 2 else True) for e in tok]
    const_args = [whole(s) for s in consts]
    out_args = [rows(w, nblk, dtype=dt) for (w, dt) in outs]
    return blocked_op(name, fn, 1, t // nblk, tok_args, const_args, out_args, [(t, w) for (w, _) in outs], states)


def const_op(name, fn, in_shapes, out_shapes):
    return blocked_op(name, fn, 1, 1, [], [whole(s) for s in in_shapes], [whole(s) for s in out_shapes], list(out_shapes))


W = BRANCH_WIDTH
SCAN_ROWS = 128
ROW_BLOCK = 512


def s5_mixer(tag, t, u, p):
    col = (S5_W, 1)
    disc = const_op("s5_disc" + tag, s5_discretise_fn, [col, col, col, (S5_W, 16), (S5_W, 16)], [col, col, (S5_W, 16), (S5_W, 16)])
    ab_re, ab_im, bb_re, bb_im = disc([], [
        jnp.repeat(p["s5_log_step"], S5_STATE).reshape(col), p["s5_a_re"].reshape(col), p["s5_a_im"].reshape(col),
        p["s5_b_re"].reshape(S5_W, S5_GROUP_CH), p["s5_b_im"].reshape(S5_W, S5_GROUP_CH)])
    eye = jnp.eye(S5_GROUPS, dtype=F32)

    def block_in(bb):
        return jnp.einsum("gpc,gh->gchp", bb.reshape(S5_GROUPS, S5_STATE, S5_GROUP_CH), eye).reshape(W, S5_W)

    def block_out(c):
        return jnp.einsum("gcp,gh->gphc", c, eye).reshape(S5_W, W)

    w_b = jnp.concatenate([block_in(bb_re), block_in(bb_im)], axis=1)
    w_c = jnp.concatenate([block_out(p["s5_c_re"]), -block_out(p["s5_c_im"])], axis=0)
    bu = dense("s5_b" + tag, ACT)(u, w_b)
    h = s5_scan("s5_scan" + tag, t, bu, ab_re.reshape(1, S5_W), ab_im.reshape(1, S5_W))
    yc = dense("s5_c" + tag)(h, w_c)
    glu = tok_op("s5_glu" + tag, s5_glu_fn, t, ROW_BLOCK, [(W, F32), (W, F32)], [(1, W), (W, W), (1, W)], [(W, ACT)])
    return glu([yc, u], [p["s5_d"].reshape(1, W), p["s5_w_glu"], p["s5_b_glu"][None]])[0]


def lru_mixer(tag, t, x, gate, p):
    def block_diag(w):
        return jnp.einsum("hij,hk->hikj", w, jnp.eye(LRU_HEADS, dtype=F32)).reshape(W, W)

    op = tok_op("lru" + tag, lru_fn, t, SCAN_ROWS, [(W, F32), (W, F32)],
                [(4, W), (1, W), (W, W), (1, W), (W, W), (1, W), (1, W)], [(W, ACT)], states=[(1, W), (TAIL, W)])
    return op([x, gate], [p["lru_conv_w"], p["lru_conv_b"][None], block_diag(p["lru_w_r"]), p["lru_b_r"][None],
                          block_diag(p["lru_w_i"]), p["lru_b_i"][None], p["lru_lambda"][None]])[0]


def m2_mixer(tag, t, z, xbc, small, p):
    cw = 2 * W
    conv = tok_op("m2_conv" + tag, m2_conv_fn, t, ROW_BLOCK, [(cw, F32)], [(4, cw), (1, cw)],
                  [(W, F32), (W // 2, F32), (W // 2, F32)], states=[(TAIL, cw)])
    xs, bm, cm = conv([xbc], [p["m2_conv_w"], p["m2_conv_b"][None]])
    ssd = tok_op("ssd" + tag, ssd_fn, t, CHUNK, [(W, F32), (W // 2, F32), (W // 2, F32), (LANE, F32)],
                 [(1, M2_HEADS)] * 3, [(W, F32)], states=[(4 * M2_HEAD_DIM, M2_STATE)] * M2_GROUPS)
    y = ssd([xs, bm, cm, small], [p["m2_dt_bias"][None], p["m2_a_log"][None], p["m2_d"][None]])[0]
    post = tok_op("m2_post" + tag, m2_post_fn, t, ROW_BLOCK, [(W, F32), (W, F32)], [(1, W)], [(W, ACT)])
    return post([y, z], [p["m2_norm"][None]])[0]


def gdn_mixer(tag, t, qkv, gate, small, p):
    conv = tok_op("gdn_conv" + tag, gdn_conv_fn, t, ROW_BLOCK, [(3 * W, F32)], [(4, 3 * W)], [(W, F32)] * 3, states=[(TAIL, 3 * W)])
    q, k, v = conv([qkv], [p["gdn_conv_w"]])
    op = tok_op("gdn" + tag, gdn_fn, t, CHUNK, [(W, F32)] * 4 + [(LANE, F32)], [(1, GDN_HEADS), (1, GDN_HEADS), (1, GDN_HEAD_DIM)],
                [(W, ACT)], states=[(GDN_HEAD_DIM, GDN_HEAD_DIM)] * GDN_HEADS)
    return op([q, k, v, gate, small], [p["gdn_dt_bias"][None], p["gdn_a_log"][None], p["gdn_norm"][None]])[0]


WEIGHTS = ["ffn1_norm", "ffn1_w_in", "ffn1_w_out", "mix_norm", "w_in", "w_gate", "b_gate", "s5_log_step", "s5_a_re",
           "s5_a_im", "s5_b_re", "s5_b_im", "s5_c_re", "s5_c_im", "s5_d", "s5_w_glu", "s5_b_glu", "lru_conv_w",
           "lru_conv_b", "lru_w_r", "lru_b_r", "lru_w_i", "lru_b_i", "lru_lambda", "m2_conv_w", "m2_conv_b", "m2_dt_bias",
           "m2_a_log", "m2_d", "m2_norm", "gdn_conv_w", "gdn_dt_bias", "gdn_a_log", "gdn_norm", "w_branch", "w_out",
           "ffn2_norm", "ffn2_w_in", "ffn2_w_out", "ple_norm", "ple_w_gate", "ple_w_proj", "final_norm"]
N_CHIPS = 4
N_DEV = 8
IN_WIDTH = 5136
SHARDED = {
    "ffn1_w_in": ((D_MODEL, 2 * FFN_DIM // N_CHIPS), 1, False),
    "ffn1_w_out": ((FFN_DIM // N_CHIPS, D_MODEL), 0, False),
    "w_in": ((D_MODEL, IN_WIDTH // N_CHIPS), 1, False),
    "w_gate": ((D_MODEL, N_BRANCH * D_MODEL // N_CHIPS), 1, False),
    "s5_w_glu": ((W // N_CHIPS, W), 0, False),
    "lru_conv_w": ((4, W // N_CHIPS), 1, True),
    "m2_conv_w": ((4, 2 * W // N_CHIPS), 1, True),
    "gdn_conv_w": ((4, 3 * W // N_CHIPS), 1, True),
    "w_branch": ((N_BRANCH, W, D_MODEL // N_CHIPS), 2, False),
    "w_out": ((D_MODEL // N_CHIPS, D_MODEL), 0, False),
    "ffn2_w_in": ((D_MODEL, 2 * FFN_DIM // N_CHIPS), 1, False),
    "ffn2_w_out": ((FFN_DIM // N_CHIPS, D_MODEL), 0, False),
    "ple_w_gate": ((D_MODEL // N_CHIPS, D_MODEL), 0, False),
    "ple_w_proj": ((256, D_MODEL // N_CHIPS), 1, False),
}
SMALL = [n for n in WEIGHTS if n not in SHARDED]
ROW = 1024


def _count(shape):
    return math.prod(shape)


def _round_up(n, unit):
    return -(-n // unit) * unit


N_GATHER = sum(DEPTH * _count(s) * (2 if exact else 1) for s, _, exact in SHARDED.values())
N_GRAD = sum(DEPTH * _count(s) for s, _, _ in SHARDED.values())
GATHER_ROWS = _round_up(-(-N_GATHER // ROW), 32)
GRAD_ROWS = _round_up(-(-N_GRAD // ROW), 32)
GRAD_HALF = GRAD_ROWS // 2
IN_PIECES = [(0, 512), (512, 512), (1024, 512), (1536, 512), (2048, 1024), (3072, 8), (3080, 1536), (4616, 512), (5128, 8)]


COL_SHARDED = ("ffn1_w_in", "ffn2_w_in", "w_gate", "ple_w_proj", "w_branch")
IN_SHARD = IN_WIDTH // N_CHIPS


def _w_in_cuts():
    cuts = {0, IN_SHARD}
    for s in range(N_CHIPS):
        for start, _ in IN_PIECES:
            if s * IN_SHARD < start < (s + 1) * IN_SHARD:
                cuts.add(start - s * IN_SHARD)
    return sorted(cuts)


IN_CUTS = _w_in_cuts()
IN_BLOCKS = list(zip(IN_CUTS[:-1], IN_CUTS[1:]))


def _piece_of(col):
    for k, (start, n) in enumerate(IN_PIECES):
        if start <= col < start + n:
            return k, col - start
    raise ValueError(col)


def cut_w_in(w):
    return jnp.concatenate([w[:, :, lo:hi].reshape(-1) for lo, hi in IN_BLOCKS])


def uncut_w_in(flat):
    blocks, off = [], 0
    for lo, hi in IN_BLOCKS:
        cnt = DEPTH * D_MODEL * (hi - lo)
        blocks.append(flat[off:off + cnt].reshape(DEPTH, D_MODEL, hi - lo))
        off += cnt
    return jnp.concatenate(blocks, axis=2)


def pack_for_gather(a):
    parts = []
    for n, (_, _, exact) in SHARDED.items():
        w = cut_w_in(a[n]) if n == "w_in" else a[n]
        parts.append((lax.bitcast_convert_type(w, BF16) if exact else w.astype(BF16)).reshape(-1))
    flat = jnp.concatenate(parts)
    return jnp.pad(flat, (0, GATHER_ROWS * ROW - flat.shape[0])).reshape(GATHER_ROWS, ROW)


def unpack_gathered(buf):
    flat16 = buf.reshape(N_CHIPS, -1)
    flat = buf.astype(ACT).reshape(N_CHIPS, -1)
    out, off = {}, 0
    for n, (shape, ax, exact) in SHARDED.items():
        cnt = DEPTH * _count(shape) * (2 if exact else 1)
        piece = (flat16 if exact else flat)[:, off:off + cnt]
        off += cnt
        if n == "w_in":
            cols = [[[] for _ in IN_PIECES] for _ in range(DEPTH)]
            for s in range(N_CHIPS):
                o = 0
                for lo, hi in IN_BLOCKS:
                    c = DEPTH * D_MODEL * (hi - lo)
                    blk = piece[s, o:o + c].reshape(DEPTH, D_MODEL, hi - lo)
                    o += c
                    k, _ = _piece_of(s * IN_SHARD + lo)
                    for layer in range(DEPTH):
                        cols[layer][k].append(blk[layer])
            out[n] = [[_pad_lanes(jnp.concatenate(c, axis=1)) for c in cols[layer]] for layer in range(DEPTH)]
        elif n in COL_SHARDED:
            out[n] = piece.reshape(N_CHIPS, DEPTH, *shape)
        else:
            if exact:
                w = lax.bitcast_convert_type(piece.reshape(N_CHIPS, DEPTH, *shape, 2), F32)
            else:
                w = piece.reshape(N_CHIPS, DEPTH, *shape)
            full = list(shape)
            full[ax] *= N_CHIPS
            out[n] = jnp.moveaxis(w, 0, ax + 1).reshape(DEPTH, *full)
    return out


def _pad_lanes(w):
    n = w.shape[1]
    return w if n % LANE == 0 else jnp.pad(w, ((0, 0), (0, LANE - n % LANE)))


def shard_w_in_grads(pieces):
    shards = []
    for s in range(N_CHIPS):
        parts = []
        for lo, hi in IN_BLOCKS:
            k, dst = _piece_of(s * IN_SHARD + lo)
            parts.append(jnp.stack([pieces[layer][k][:, dst:dst + hi - lo] for layer in range(DEPTH)]).reshape(-1))
        shards.append(jnp.concatenate(parts))
    return jnp.stack(shards)


def pack_grads(g):
    tail = jnp.zeros((N_CHIPS, GRAD_ROWS * ROW - N_GRAD), F32)
    return jnp.concatenate([g[n] for n in SHARDED] + [tail], axis=1).reshape(N_CHIPS, GRAD_ROWS, ROW)


def shard_full_grads(n, per_layer):
    shape, ax, _ = SHARDED[n]
    parts = []
    for full in per_layer:
        w = full.reshape(*full.shape[:ax], N_CHIPS, shape[ax], *full.shape[ax + 1:])
        parts.append(jnp.moveaxis(w, ax, 0).reshape(N_CHIPS, -1))
    return jnp.concatenate(parts, axis=1)


def unpack_shard(buf):
    flat = buf.reshape(-1)
    out, off = {}, 0
    for n, (shape, _, _) in SHARDED.items():
        cnt = DEPTH * _count(shape)
        piece = flat[off:off + cnt]
        out[n] = uncut_w_in(piece) if n == "w_in" else piece.reshape(DEPTH, *shape)
        off += cnt
    return out


def pack_small(a, prefix, shapes):
    flat = jnp.concatenate([a[prefix + n].reshape(-1) for n in SMALL])
    rows_n = _round_up(-(-flat.shape[0] // ROW), 8)
    return jnp.pad(flat, (0, rows_n * ROW - flat.shape[0])).reshape(rows_n, ROW)


def unpack_small(buf, shapes):
    flat = buf.reshape(-1)
    out, off = {}, 0
    for n in SMALL:
        cnt = _count(shapes[n])
        out[n] = flat[off:off + cnt].reshape(shapes[n])
        off += cnt
    return out


ANY = pl.BlockSpec(memory_space=pl.ANY)


def _position():
    return lax.axis_index("x"), lax.axis_index("y"), lax.axis_index("c")


def _other_chips(x, y):
    return [(1 - x, y), (x, 1 - y), (1 - x, 1 - y)]


PLACE_ROWS = 592


def gather_weights(packed, slot):
    r = packed.shape[0]
    half = r // 2
    nblk = r // PLACE_ROWS

    def place(s_ref, in_ref, o_ref):
        o_ref[...] = in_ref[...]

    slots = pl.pallas_call(
        place, name="place_shard",
        grid_spec=pltpu.PrefetchScalarGridSpec(
            num_scalar_prefetch=1, grid=(nblk,), in_specs=[pl.BlockSpec((PLACE_ROWS, ROW), lambda i, s: (i, 0))],
            out_specs=pl.BlockSpec((None, PLACE_ROWS, ROW), lambda i, s: (s[0], i, 0))),
        out_shape=jax.ShapeDtypeStruct((N_CHIPS, r, ROW), packed.dtype), compiler_params=_params(("arbitrary",)),
    )(slot, packed)

    def body(in_ref, out_ref, send_sems, recv_sems):
        x, y, c = _position()
        sibling = (x, y, 1 - c)
        chips = _other_chips(x, y)

        def half_rows(px, py, pc):
            return out_ref.at[2 * px + py, pl.ds(pl.multiple_of(pc * half, 16), half), :]

        def copy(k, block, to):
            return pltpu.make_async_remote_copy(
                src_ref=half_rows(*block), dst_ref=half_rows(*block),
                send_sem=send_sems.at[k], recv_sem=recv_sems.at[k], device_id=to, device_id_type=MESH)

        first = [copy(j, (x, y, c), (*chip, c)) for j, chip in enumerate(chips)]
        for cp in first:
            cp.start()
        passed = [copy(3 + j, (*chip, c), sibling) for j, chip in enumerate(chips)]
        for j, chip in enumerate(chips):
            copy(j, (*chip, c), (x, y, c)).wait_recv()
            passed[j].start()
        for j, chip in enumerate(chips):
            copy(3 + j, (*chip, 1 - c), (x, y, c)).wait_recv()
        for cp in first + passed:
            cp.wait_send()

    return pl.pallas_call(
        body, name="gather_weights", in_specs=[ANY], out_specs=ANY, input_output_aliases={0: 0},
        out_shape=jax.ShapeDtypeStruct((N_CHIPS, r, ROW), packed.dtype),
        scratch_shapes=[pltpu.SemaphoreType.DMA((6,)), pltpu.SemaphoreType.DMA((6,))],
    )(slots)


def swap_pair_halves(g):
    half = g.shape[1] // 2

    def body(g_ref, land_ref, send_sem, recv_sem):
        x, y, c = _position()
        src = g_ref.at[:, pl.ds(pl.multiple_of((1 - c) * half, 8), half), :]
        cp = pltpu.make_async_remote_copy(src_ref=src, dst_ref=land_ref, send_sem=send_sem, recv_sem=recv_sem,
                                          device_id=(x, y, 1 - c), device_id_type=MESH)
        cp.start()
        cp.wait()

    return pl.pallas_call(
        body, name="swap_pair_halves", in_specs=[ANY], out_specs=ANY,
        out_shape=jax.ShapeDtypeStruct((N_CHIPS, half, ROW), g.dtype),
        scratch_shapes=[pltpu.SemaphoreType.DMA, pltpu.SemaphoreType.DMA],
    )(g)


def exchange_chip_partials(part):
    half = part.shape[1]

    def body(p_ref, land_ref, send_sems, recv_sems):
        x, y, c = _position()
        cps = [pltpu.make_async_remote_copy(src_ref=p_ref.at[2 * px + py], dst_ref=land_ref.at[j], send_sem=send_sems.at[j],
                                            recv_sem=recv_sems.at[j], device_id=(px, py, c), device_id_type=MESH)
               for j, (px, py) in enumerate(_other_chips(x, y))]
        for cp in cps:
            cp.start()
        for cp in cps:
            cp.wait()

    return pl.pallas_call(
        body, name="exchange_chip_partials", in_specs=[ANY], out_specs=ANY,
        out_shape=jax.ShapeDtypeStruct((3, half, ROW), part.dtype),
        scratch_shapes=[pltpu.SemaphoreType.DMA((3,)), pltpu.SemaphoreType.DMA((3,))],
    )(part)


def share_halves(both):
    half = both.shape[0] // 2

    def body(in_ref, out_ref, send_sem, recv_sem):
        x, y, c = _position()
        my_rows = out_ref.at[pl.ds(pl.multiple_of(c * half, 8), half), :]
        cp = pltpu.make_async_remote_copy(src_ref=my_rows, dst_ref=my_rows, send_sem=send_sem, recv_sem=recv_sem,
                                          device_id=(x, y, 1 - c), device_id_type=MESH)
        cp.start()
        cp.wait()

    return pl.pallas_call(
        body, name="share_halves", in_specs=[ANY], out_specs=ANY, input_output_aliases={0: 0},
        out_shape=jax.ShapeDtypeStruct(both.shape, both.dtype),
        scratch_shapes=[pltpu.SemaphoreType.DMA, pltpu.SemaphoreType.DMA],
    )(both)


def gather_all(block):
    m_per = block.shape[0]

    def body(x_ref, out_ref, send_sems, recv_sems, local_sem):
        x, y, c = _position()
        me, sibling = (x, y, c), (x, y, 1 - c)
        chips = _other_chips(x, y)

        def rows_of(px, py, pc):
            return out_ref.at[pl.ds(pl.multiple_of((4 * px + 2 * py + pc) * m_per, 8), m_per), :]

        def copy(k, blk, to, src=None):
            return pltpu.make_async_remote_copy(
                src_ref=rows_of(*blk) if src is None else src, dst_ref=rows_of(*blk),
                send_sem=send_sems.at[k], recv_sem=recv_sems.at[k], device_id=to, device_id_type=MESH)

        mine = pltpu.make_async_copy(x_ref, rows_of(*me), local_sem)
        mine.start()
        first = [copy(0, me, sibling, src=x_ref)]
        first += [copy(1 + j, me, (*chip, c), src=x_ref) for j, chip in enumerate(chips)]
        for cp in first:
            cp.start()
        passed = [copy(4 + j, (*chip, c), sibling) for j, chip in enumerate(chips)]
        for j, chip in enumerate(chips):
            copy(1 + j, (*chip, c), me).wait_recv()
            passed[j].start()
        copy(0, sibling, me).wait_recv()
        for j, chip in enumerate(chips):
            copy(4 + j, (*chip, 1 - c), me).wait_recv()
        for cp in first + passed:
            cp.wait_send()
        mine.wait()

    return pl.pallas_call(
        body, name="gather_all", out_shape=jax.ShapeDtypeStruct((N_DEV * m_per, ROW), block.dtype),
        in_specs=[pl.BlockSpec(memory_space=pltpu.VMEM)], out_specs=pl.BlockSpec(memory_space=pltpu.VMEM),
        scratch_shapes=[pltpu.SemaphoreType.DMA((7,)), pltpu.SemaphoreType.DMA((7,)), pltpu.SemaphoreType.DMA],
        compiler_params=_params(),
    )(block)


SUM_ROWS = 592


def add_pair_halves(grads, landed, core):
    half = landed.shape[1]
    nblk = half // SUM_ROWS

    def body(c_ref, g_ref, l_ref, o_ref, o16_ref):
        acc = g_ref[...] + l_ref[...]
        o_ref[...] = acc
        o16_ref[...] = acc.astype(BF16)

    blk = (None, SUM_ROWS, ROW)
    o_spec = pl.BlockSpec(blk, lambda s, i, c: (s, i, 0))
    return pl.pallas_call(
        body, name="add_pair_halves",
        grid_spec=pltpu.PrefetchScalarGridSpec(
            num_scalar_prefetch=1, grid=(N_CHIPS, nblk),
            in_specs=[pl.BlockSpec(blk, lambda s, i, c: (s, c[0] * nblk + i, 0)), o_spec], out_specs=[o_spec, o_spec]),
        out_shape=[jax.ShapeDtypeStruct(landed.shape, F32), jax.ShapeDtypeStruct(landed.shape, BF16)],
        compiler_params=_params(("arbitrary", "arbitrary")),
    )(core, grads, landed)


def add_chip_partials(part, landed, slot, core):
    half = part.shape[1]
    nblk = half // SUM_ROWS

    def body(s_ref, c_ref, p_ref, l_ref, o_ref):
        o_ref[...] = ((p_ref[...] + l_ref[0].astype(F32)) + l_ref[1].astype(F32)) + l_ref[2].astype(F32)

    return pl.pallas_call(
        body, name="add_chip_partials",
        grid_spec=pltpu.PrefetchScalarGridSpec(
            num_scalar_prefetch=2, grid=(nblk,),
            in_specs=[pl.BlockSpec((None, SUM_ROWS, ROW), lambda i, s, c: (s[0], i, 0)), pl.BlockSpec((3, SUM_ROWS, ROW), lambda i, s, c: (0, i, 0))],
            out_specs=pl.BlockSpec((SUM_ROWS, ROW), lambda i, s, c: (c[0] * nblk + i, 0))),
        out_shape=jax.ShapeDtypeStruct((2 * half, ROW), F32), compiler_params=_params(("arbitrary",)),
    )(slot, core, part, landed)


def sum_devices(stacked):
    m = stacked.shape[1]

    def body(s_ref, o_ref):
        acc = s_ref[0]
        for d in range(1, N_DEV):
            acc = acc + s_ref[d]
        o_ref[...] = acc

    return pl.pallas_call(
        body, name="sum_devices", grid=(m // 8,), in_specs=[pl.BlockSpec((N_DEV, 8, ROW), lambda i: (0, i, 0))],
        out_specs=pl.BlockSpec((8, ROW), lambda i: (i, 0)), out_shape=jax.ShapeDtypeStruct((m, ROW), F32),
        compiler_params=_params(("arbitrary",)),
    )(stacked)


def adamw(name, w, g, m, v):
    width = w.shape[-1]
    n_rows = w.size // width
    nblk = _tile(n_rows, 256, 8)
    arg = rows(width, nblk)
    outs, _ = blocked_forward("adamw_" + name, adamw_fn, 1, n_rows // nblk, [arg] * 4, [t.reshape(n_rows, width) for t in (w, g, m, v)],
                              [], [], [arg] * 3, [(n_rows, width)] * 3, (), False)
    return [o.reshape(w.shape) for o in outs]


def trunk_loss(diff, p_emb, target, wts):
    x, small, gw, hd = diff["x"], diff["small"], diff["gw"], diff["hd"]
    t = x.shape[0]
    d = D_MODEL

    def norm_pair(name, fn, h, o, gain):
        op = tok_op(name, fn, t, 512, [(d, F32), (d, F32)], [(1, d)], [(d, F32), (d, ACT)])
        return op([h, o], [gain[None]])

    def ffn(tag, n, which, i):
        z = linear(f"{which}_in{tag}", ACT, (i,), N_CHIPS)(n, wts[which + "_w_in"], hd[which + "_w_in"][i])
        act = tok_op(f"{which}_act{tag}", swiglu_fn, t, 128, [(2 * FFN_DIM, ACT)], [], [(FFN_DIM, ACT)])([z], [])[0]
        return linear(f"{which}_out{tag}", F32, (i,))(act, wts[which + "_w_out"], hd[which + "_w_out"][i])

    h = x
    n = tok_op("norm_in", norm_fn, t, 512, [(d, F32)], [(1, d)], [(d, ACT)])([x], [small["ffn1_norm"][0][None]])[0]
    loss_rows = None
    for i in range(DEPTH):
        tag = str(i)
        p = {k: v[i] for k, v in small.items() if k != "final_norm"}
        p.update({k: v[i] for k, v in gw.items()})
        o = ffn(tag, n, "ffn1", i)
        h, u = norm_pair("mix_norm" + tag, make_addnorm_fn(0.5), h, o, p["mix_norm"])
        n_in = len(IN_PIECES)
        in_proj = multi_linear("in_proj" + tag, [F32] * n_in + [ACT], [None] * n_in + [(i,)], [0] * n_in + [N_CHIPS])
        proj = in_proj(u, wts["w_in"][i] + [wts["w_gate"]], hd["in_proj"][i])
        s5_u, lru_x, lru_g, m2_z, m2_xbc, m2_dt, gdn_qkv, gdn_g, gdn_ba, gate_logits = proj
        ys = [s5_mixer(tag, t, s5_u, p), lru_mixer(tag, t, lru_x, lru_g, p),
              m2_mixer(tag, t, m2_z, m2_xbc, m2_dt, p), gdn_mixer(tag, t, gdn_qkv, gdn_g, gdn_ba, p)]
        yb = [linear(f"branch{b}_{tag}", ACT, (i, b), N_CHIPS)(y, wts["w_branch"], hd["w_branch"][i][b]) for b, y in enumerate(ys)]
        merge = tok_op("gate_merge" + tag, gate_merge_fn, t, 128, [(N_BRANCH * d, ACT)] + [(d, ACT)] * N_BRANCH,
                       [(1, N_BRANCH * d)], [(d, ACT)])
        mixed = merge([gate_logits] + yb, [p["b_gate"][None]])[0]
        o = linear("w_out" + tag, F32, (i,))(mixed, wts["w_out"], hd["w_out"][i])
        h, n = norm_pair("ffn2_norm" + tag, make_addnorm_fn(1.0), h, o, p["ffn2_norm"])
        o = ffn(tag, n, "ffn2", i)
        h, n = norm_pair("ple_norm" + tag, make_addnorm_fn(0.5), h, o, p["ple_norm"])
        pg = linear("ple_gate" + tag, F32, (i,))(n, wts["ple_w_gate"], hd["ple_w_gate"][i])
        pp = linear("ple_proj" + tag, F32, (i,), N_CHIPS)(p_emb[i], wts["ple_w_proj"], hd["ple_w_proj"][i])
        if i + 1 < DEPTH:
            op = tok_op("ple" + tag, ple_fn, t, 512, [(d, F32)] * 3, [(1, d)], [(d, F32), (d, ACT)])
            h, n = op([h, pg, pp], [small["ffn1_norm"][i + 1][None]])
        else:
            op = tok_op("ple_loss", ple_loss_fn, t, 512, [(d, F32)] * 3 + [(d, F32, False)], [(1, d)], [(1, F32)])
            loss_rows = op([h, pg, pp, target], [small["final_norm"][None]])[0]
    return jnp.sum(loss_rows)


def kernel(x, p, ffn1_norm, ffn1_w_in, ffn1_w_out, mix_norm, w_in, w_gate, b_gate, s5_log_step, s5_a_re, s5_a_im, s5_b_re, s5_b_im, s5_c_re, s5_c_im, s5_d, s5_w_glu, s5_b_glu, lru_conv_w, lru_conv_b, lru_w_r, lru_b_r, lru_w_i, lru_b_i, lru_lambda, m2_conv_w, m2_conv_b, m2_dt_bias, m2_a_log, m2_d, m2_norm, gdn_conv_w, gdn_dt_bias, gdn_a_log, gdn_norm, w_branch, w_out, ffn2_norm, ffn2_w_in, ffn2_w_out, ple_norm, ple_w_gate, ple_w_proj, final_norm, loss_target, m_ffn1_norm, m_ffn1_w_in, m_ffn1_w_out, m_mix_norm, m_w_in, m_w_gate, m_b_gate, m_s5_log_step, m_s5_a_re, m_s5_a_im, m_s5_b_re, m_s5_b_im, m_s5_c_re, m_s5_c_im, m_s5_d, m_s5_w_glu, m_s5_b_glu, m_lru_conv_w, m_lru_conv_b, m_lru_w_r, m_lru_b_r, m_lru_w_i, m_lru_b_i, m_lru_lambda, m_m2_conv_w, m_m2_conv_b, m_m2_dt_bias, m_m2_a_log, m_m2_d, m_m2_norm, m_gdn_conv_w, m_gdn_dt_bias, m_gdn_a_log, m_gdn_norm, m_w_branch, m_w_out, m_ffn2_norm, m_ffn2_w_in, m_ffn2_w_out, m_ple_norm, m_ple_w_gate, m_ple_w_proj, m_final_norm, v_ffn1_norm, v_ffn1_w_in, v_ffn1_w_out, v_mix_norm, v_w_in, v_w_gate, v_b_gate, v_s5_log_step, v_s5_a_re, v_s5_a_im, v_s5_b_re, v_s5_b_im, v_s5_c_re, v_s5_c_im, v_s5_d, v_s5_w_glu, v_s5_b_glu, v_lru_conv_w, v_lru_conv_b, v_lru_w_r, v_lru_b_r, v_lru_w_i, v_lru_b_i, v_lru_lambda, v_m2_conv_w, v_m2_conv_b, v_m2_dt_bias, v_m2_a_log, v_m2_d, v_m2_norm, v_gdn_conv_w, v_gdn_dt_bias, v_gdn_a_log, v_gdn_norm, v_w_branch, v_w_out, v_ffn2_norm, v_ffn2_w_in, v_ffn2_w_out, v_ple_norm, v_ple_w_gate, v_ple_w_proj, v_final_norm):
    a = dict(locals())
    t = x.shape[1]
    core = lax.axis_index("c").astype(jnp.int32).reshape(1)
    slot = (2 * lax.axis_index("x") + lax.axis_index("y")).astype(jnp.int32).reshape(1)

    full = unpack_gathered(gather_weights(pack_for_gather(a), slot))
    exact = [n for n, spec in SHARDED.items() if spec[2]] + ["s5_w_glu"]
    gw = {n: full[n].astype(F32) for n in exact}
    wts = {n: full[n] for n in SHARDED if n not in exact}

    def handle(n):
        shape = SHARDED[n][0]
        return jnp.zeros((N_CHIPS, *shape) if n in COL_SHARDED else (N_CHIPS * shape[0], *shape[1:]), F32)

    hd = {n: [handle(n) for _ in range(DEPTH)] for n in wts if n not in ("w_in", "w_gate", "w_branch")}
    hd["w_branch"] = [[jnp.zeros((N_CHIPS, W, D_MODEL // N_CHIPS), F32) for _ in range(N_BRANCH)] for _ in range(DEPTH)]
    hd["in_proj"] = [[jnp.zeros((D_MODEL, _round_up(n, LANE)), F32) for _, n in IN_PIECES] + [handle("w_gate")] for _ in range(DEPTH)]
    diff = {"x": x.reshape(t, D_MODEL), "small": {n: a[n] for n in SMALL}, "gw": gw, "hd": hd}
    loss_local, vjp = jax.vjp(lambda dd: trunk_loss(dd, p.reshape(DEPTH, t, -1), loss_target.reshape(t, D_MODEL), wts), diff)
    (grads,) = vjp(jnp.ones((), F32))
    loss = lax.psum(loss_local, ("x", "y", "c"))
    grad_x = grads["x"].reshape(x.shape)

    gh = grads["hd"]
    big = {n: jnp.concatenate([g.reshape(N_CHIPS, -1) for g in gh[n]], axis=1) for n in gh if n not in ("in_proj", "w_branch")}
    big["w_in"] = shard_w_in_grads([gh["in_proj"][i][:-1] for i in range(DEPTH)])
    big["w_gate"] = jnp.concatenate([gh["in_proj"][i][-1].reshape(N_CHIPS, -1) for i in range(DEPTH)], axis=1)
    big["w_branch"] = jnp.concatenate([g.reshape(N_CHIPS, -1) for i in range(DEPTH) for g in gh["w_branch"][i]], axis=1)
    for n in exact:
        big[n] = shard_full_grads(n, [grads["gw"][n][i] for i in range(DEPTH)])

    packed = pack_grads(big)
    pair, pair16 = add_pair_halves(packed, swap_pair_halves(packed), core)
    mine = add_chip_partials(pair, exchange_chip_partials(pair16), slot, core)
    g_shard = unpack_shard(share_halves(mine))

    shapes = {n: a[n].shape for n in SMALL}
    gs_local = pack_small({n: grads["small"][n] for n in SMALL}, "", shapes)
    gs = gather_all(gs_local)
    g_small = sum_devices(gs.reshape(N_DEV, gs_local.shape[0], ROW))

    g_all = {**unpack_small(g_small, shapes), **g_shard}
    res = {n: [g_all[n]] + adamw(n, a[n], g_all[n], a["m_" + n], a["v_" + n]) for n in WEIGHTS}
    return (loss, grad_x, *[res[n][0] for n in WEIGHTS], *[res[n][1] for n in WEIGHTS],
            *[res[n][2] for n in WEIGHTS], *[res[n][3] for n in WEIGHTS])
```

```python
import functools
import math

import jax
import jax.numpy as jnp
from jax import lax
from jax.experimental import pallas as pl
from jax.experimental.pallas import tpu as pltpu

F32, BF16 = jnp.float32, jnp.bfloat16
ACT = BF16
EPS = 1e-6
D_MODEL = 1024
DEPTH = 2
FFN_DIM = 2816
BRANCH_WIDTH = 512
N_BRANCH = 4
LRU_C = 8.0
S5_GROUPS, S5_GROUP_CH, S5_STATE = 32, 16, 64
S5_W = S5_GROUPS * S5_STATE
LRU_HEADS, LRU_HEAD_DIM = 8, 64
M2_HEADS, M2_HEAD_DIM, M2_GROUPS, M2_STATE = 8, 64, 2, 128
GDN_HEADS, GDN_HEAD_DIM = 4, 128
CHUNK = 128
ADAM_LR, ADAM_B1, ADAM_B2, ADAM_EPS, ADAM_WD, ADAM_STEP = 0.001, 0.9, 0.999, 1e-08, 0.01, 10
VMEM_LIMIT_BYTES = 56 * 1024 * 1024
MESH = pl.DeviceIdType.MESH


def _params(sem=None):
    return pltpu.CompilerParams(vmem_limit_bytes=VMEM_LIMIT_BYTES, dimension_semantics=sem)


def _dot_bf16(a, b, dims):
    return lax.dot_general(a.astype(BF16), b.astype(BF16), (dims, ((), ())), preferred_element_type=F32)


def _make_mm(dot):
    @jax.custom_vjp
    def nn(a, b):
        return dot(a, b, ((1,), (0,)))

    @jax.custom_vjp
    def nt(a, b):
        return dot(a, b, ((1,), (1,)))

    @jax.custom_vjp
    def tn(a, b):
        return dot(a, b, ((0,), (0,)))

    nn.defvjp(lambda a, b: (nn(a, b), (a, b)), lambda r, g: (nt(g, r[1]), tn(r[0], g)))
    nt.defvjp(lambda a, b: (nt(a, b), (a, b)), lambda r, g: (nn(g, r[1]), tn(g, r[0])))
    tn.defvjp(lambda a, b: (tn(a, b), (a, b)), lambda r, g: (nt(r[1], g), nn(r[0], g)))
    return nn, nt, tn


def _dot_bf16x3(a, b, dims):
    a_hi, b_hi = a.astype(BF16), b.astype(BF16)
    a_lo = (a - a_hi.astype(F32)).astype(BF16)
    b_lo = (b - b_hi.astype(F32)).astype(BF16)

    def dot(p, q):
        return lax.dot_general(p, q, (dims, ((), ())), preferred_element_type=F32)

    return dot(a_hi, b_hi) + (dot(a_hi, b_lo) + dot(a_lo, b_hi))


mm, mm_nt, mm_tn = _make_mm(_dot_bf16)
mmh, mmh_nt, mmh_tn = _make_mm(_dot_bf16x3)


def _row_ids(shape):
    return lax.broadcasted_iota(jnp.int32, shape, 0)


def _shift_down(x, d):
    return jnp.where(_row_ids(x.shape) >= d, pltpu.roll(x, d, 0), 0.0)


def _shift_up(x, d):
    n = x.shape[0]
    return jnp.where(_row_ids(x.shape) < n - d, pltpu.roll(x, n - d, 0), 0.0)


def _first_row(x):
    return jnp.sum(jnp.where(_row_ids(x.shape) == 0, x, 0.0), axis=0, keepdims=True)


def last_row(x):
    return jnp.sum(jnp.where(_row_ids(x.shape) == x.shape[0] - 1, x, 0.0), axis=0, keepdims=True)


def pick_row(x, j):
    return jnp.sum(jnp.where(_row_ids(x.shape) == j, x, 0.0), axis=0, keepdims=True)


@jax.custom_vjp
def lin_scan(a, b, h0):
    n = a.shape[0]
    row = _row_ids(a.shape)
    acc_a = a
    acc_b = b + jnp.where(row == 0, a * h0, 0.0)
    d = 1
    while d < n:
        acc_b = acc_a * _shift_down(acc_b, d) + acc_b
        acc_a = acc_a * jnp.where(row >= d, pltpu.roll(acc_a, d, 0), 1.0)
        d *= 2
    return acc_b


def _lin_scan_fwd(a, b, h0):
    h = lin_scan(a, b, h0)
    return h, (a, h, h0)


def _lin_scan_bwd(res, dh):
    a, h, h0 = res
    n = a.shape[0]
    row = _row_ids(a.shape)
    acc_a = _shift_up(a, 1)
    g = dh
    d = 1
    while d < n:
        g = acc_a * _shift_up(g, d) + g
        acc_a = acc_a * jnp.where(row < n - d, pltpu.roll(acc_a, n - d, 0), 1.0)
        d *= 2
    h_prev = _shift_down(h, 1) + jnp.where(row == 0, h0, 0.0)
    return g * h_prev, g, _first_row(a * g)


lin_scan.defvjp(_lin_scan_fwd, _lin_scan_bwd)


def _cscan(br, bi, ar, ai, up):
    n = br.shape[0]
    shift = _shift_up if up else _shift_down
    hr, hi, pr, pi = br, bi, ar, ai
    d = 1
    while d < n:
        sr, si = shift(hr, d), shift(hi, d)
        hr, hi = hr + pr * sr - pi * si, hi + pr * si + pi * sr
        pr, pi = pr * pr - pi * pi, 2.0 * pr * pi
        d *= 2
    return hr, hi


@jax.custom_vjp
def complex_scan(br, bi, ar, ai, h0r, h0i):
    first = _row_ids(br.shape) == 0
    br = br + jnp.where(first, ar * h0r - ai * h0i, 0.0)
    bi = bi + jnp.where(first, ar * h0i + ai * h0r, 0.0)
    return _cscan(br, bi, ar, ai, False)


def _complex_scan_fwd(br, bi, ar, ai, h0r, h0i):
    hr, hi = complex_scan(br, bi, ar, ai, h0r, h0i)
    return (hr, hi), (ar, ai, hr, hi, h0r, h0i)


def _complex_scan_bwd(res, cts):
    ar, ai, hr, hi, h0r, h0i = res
    gr, gi = _cscan(cts[0], cts[1], ar, -ai, True)
    first = _row_ids(hr.shape) == 0
    pr = _shift_down(hr, 1) + jnp.where(first, h0r, 0.0)
    pi = _shift_down(hi, 1) + jnp.where(first, h0i, 0.0)
    d_ar = jnp.sum(gr * pr + gi * pi, axis=0, keepdims=True)
    d_ai = jnp.sum(gi * pr - gr * pi, axis=0, keepdims=True)
    g0r, g0i = _first_row(gr), _first_row(gi)
    return gr, gi, d_ar, d_ai, ar * g0r + ai * g0i, ar * g0i - ai * g0r


complex_scan.defvjp(_complex_scan_fwd, _complex_scan_bwd)

TAIL = 8


@jax.custom_vjp
def tail_rows(x):
    return x[x.shape[0] - TAIL:, :]


tail_rows.defvjp(
    lambda x: (tail_rows(x), x.shape[0]),
    lambda n, g: (jnp.concatenate([jnp.zeros((n - TAIL, g.shape[1]), g.dtype), g], axis=0),),
)


def _make_shift_tail(d):
    @jax.custom_vjp
    def shifted(x, tail):
        n = x.shape[0]
        tpad = jnp.concatenate([tail, jnp.zeros((n - TAIL, x.shape[1]), x.dtype)], axis=0)
        return jnp.where(_row_ids(x.shape) >= d, pltpu.roll(x, d, 0), pltpu.roll(tpad, n + d - TAIL, 0))

    def fwd(x, tail):
        return shifted(x, tail), None

    def bwd(_, g):
        g8 = g[:TAIL, :]
        dtail = jnp.where(_row_ids(g8.shape) >= TAIL - d, pltpu.roll(g8, TAIL - d, 0), 0.0)
        return _shift_up(g, d), dtail

    shifted.defvjp(fwd, bwd)
    return shifted


_SHIFT_TAIL = {d: _make_shift_tail(d) for d in (1, 2, 3)}


def causal_conv4(x, tail, w):
    y = pick_row(w, 3) * x
    for j in range(3):
        y = y + pick_row(w, j) * _SHIFT_TAIL[3 - j](x, tail)
    return y


def rmsnorm(x, g):
    return x * lax.rsqrt(jnp.mean(x * x, axis=-1, keepdims=True) + EPS) * g


def to_row(col):
    n = col.shape[0]
    eye = lax.broadcasted_iota(jnp.int32, (n, n), 0) == lax.broadcasted_iota(jnp.int32, (n, n), 1)
    return jnp.sum(jnp.where(eye, col, 0.0), axis=0, keepdims=True)


def causal_decay(a_col):
    n = a_col.shape[0]
    causal = lax.broadcasted_iota(jnp.int32, (n, n), 0) >= lax.broadcasted_iota(jnp.int32, (n, n), 1)
    cs = jnp.sum(jnp.where(causal, to_row(a_col), 0.0), axis=1, keepdims=True)
    diff = cs - to_row(cs)
    return cs, jnp.where(causal, jnp.exp(jnp.where(causal, diff, 0.0)), 0.0)


class Arg:
    def __init__(self, block, imap, dtype=F32, grad=True, shared=True):
        self.block, self.imap, self.dtype, self.grad, self.shared = block, imap, dtype, grad, shared


def rows(width, nblk, col=lambda g: 0, dtype=F32, grad=True):
    return Arg((nblk, width), lambda g, c: (c, col(g)), dtype, grad)


def head_rows(width, nblk, head=lambda g: g, dtype=F32, grad=True):
    return Arg((None, nblk, width), lambda g, c: (head(g), c, 0), dtype, grad)


def whole(shape, grad=True):
    return Arg(tuple(shape), lambda g, c: (0,) * len(shape), F32, grad, shared=True)


def per_group(shape, idx=lambda g: g, grad=True):
    return Arg((None,) + tuple(shape), lambda g, c: (idx(g),) + (0,) * len(shape), F32, grad, shared=False)


def _bshape(block):
    return tuple(b for b in block if b is not None)


def _spec(arg, nb=None):
    if nb is None:
        return pl.BlockSpec(arg.block, arg.imap)
    return pl.BlockSpec(arg.block, lambda g, c: arg.imap(g, nb - 1 - c))


def blocked_forward(name, fn, groups, nb, tok_args, tok, const_args, consts, out_args, out_shapes, state_shapes, save,
                    reverse=False):
    n_tok, n_const, n_out, n_state = len(tok), len(consts), len(out_args), len(state_shapes)
    walk = nb if reverse else None

    def body(*refs):
        tok_refs = refs[:n_tok]
        const_refs = refs[n_tok:n_tok + n_const]
        out_refs = refs[n_tok + n_const:n_tok + n_const + n_out]
        pos = n_tok + n_const + n_out
        save_refs = refs[pos:pos + (n_state if save else 0)]
        state_refs = refs[len(refs) - n_state:] if n_state else ()

        @pl.when(pl.program_id(1) == 0)
        def _():
            for s in state_refs:
                s[...] = jnp.zeros_like(s)

        states = [s[...] for s in state_refs]
        for sr, s in zip(save_refs, states):
            sr[...] = s
        new_states, outs = fn(states, [r[...].astype(F32) for r in tok_refs], [r[...] for r in const_refs])
        for o_ref, o in zip(out_refs, outs):
            o_ref[...] = o.astype(o_ref.dtype)
        for s_ref, s in zip(state_refs, new_states):
            s_ref[...] = s

    out_specs = [_spec(a, walk) for a in out_args]
    out_shape = [jax.ShapeDtypeStruct(s, a.dtype) for s, a in zip(out_shapes, out_args)]
    if save:
        assert not reverse
        for s in state_shapes:
            out_specs.append(pl.BlockSpec((None, None) + tuple(s), lambda g, c, k=len(s): (g, c) + (0,) * k))
            out_shape.append(jax.ShapeDtypeStruct((groups, nb) + tuple(s), F32))
    res = pl.pallas_call(
        body, name=name, grid=(groups, nb),
        in_specs=[_spec(a, walk) for a in tok_args] + [_spec(a, walk) for a in const_args],
        out_specs=out_specs, out_shape=out_shape,
        scratch_shapes=[pltpu.VMEM(tuple(s), F32) for s in state_shapes],
        compiler_params=_params(("arbitrary", "arbitrary")),
    )(*tok, *consts)
    return list(res[:n_out]), list(res[n_out:])


def blocked_backward(name, fn, groups, nb, tok_args, tok, const_args, consts, out_args, cts, state_shapes, saved):
    n_tok, n_const, n_out, n_state = len(tok), len(consts), len(out_args), len(state_shapes)
    tok_g = [i for i, a in enumerate(tok_args) if a.grad]
    const_g = [i for i, a in enumerate(const_args) if a.grad]

    def body(*refs):
        tok_refs = refs[:n_tok]
        const_refs = refs[n_tok:n_tok + n_const]
        pos = n_tok + n_const
        saved_refs = refs[pos:pos + n_state]
        ct_refs = refs[pos + n_state:pos + n_state + n_out]
        pos += n_state + n_out
        dtok_refs = refs[pos:pos + len(tok_g)]
        dconst_refs = refs[pos + len(tok_g):pos + len(tok_g) + len(const_g)]
        dstate_refs = refs[len(refs) - n_state:] if n_state else ()
        g_id, c_id = pl.program_id(0), pl.program_id(1)

        @pl.when(c_id == 0)
        def _():
            for s in dstate_refs:
                s[...] = jnp.zeros_like(s)
            for r, i in zip(dconst_refs, const_g):
                if not const_args[i].shared:
                    r[...] = jnp.zeros_like(r)

        @pl.when((c_id == 0) & (g_id == 0))
        def _():
            for r, i in zip(dconst_refs, const_g):
                if const_args[i].shared:
                    r[...] = jnp.zeros_like(r)

        tok_vals = [r[...].astype(F32) for r in tok_refs]
        const_vals = [r[...] for r in const_refs]

        def f(states, tok_d, const_d):
            tv, cv = list(tok_vals), list(const_vals)
            for i, v in zip(tok_g, tok_d):
                tv[i] = v
            for i, v in zip(const_g, const_d):
                cv[i] = v
            return fn(states, tv, cv)

        _, vjp = jax.vjp(f, [r[...] for r in saved_refs], [tok_vals[i] for i in tok_g], [const_vals[i] for i in const_g])
        dstates, dtok, dconst = vjp(([r[...] for r in dstate_refs], [r[...].astype(F32) for r in ct_refs]))
        for r, v in zip(dtok_refs, dtok):
            r[...] = v.astype(r.dtype)
        for r, v in zip(dconst_refs, dconst):
            r[...] += v
        for r, v in zip(dstate_refs, dstates):
            r[...] = v

    in_specs = [_spec(a, nb) for a in tok_args] + [_spec(a, nb) for a in const_args]
    for s in state_shapes:
        in_specs.append(pl.BlockSpec((None, None) + tuple(s), lambda g, c, k=len(s): (g, nb - 1 - c) + (0,) * k))
    in_specs += [_spec(a, nb) for a in out_args]
    out_specs = [_spec(tok_args[i], nb) for i in tok_g] + [_spec(const_args[i], nb) for i in const_g]
    out_shape = [jax.ShapeDtypeStruct(tok[i].shape, tok[i].dtype) for i in tok_g]
    out_shape += [jax.ShapeDtypeStruct(consts[i].shape, F32) for i in const_g]
    res = pl.pallas_call(
        body, name=name, grid=(groups, nb), in_specs=in_specs, out_specs=out_specs, out_shape=out_shape,
        scratch_shapes=[pltpu.VMEM(tuple(s), F32) for s in state_shapes],
        compiler_params=_params(("arbitrary", "arbitrary")),
    )(*tok, *consts, *saved, *cts)
    dtok = [None] * n_tok
    dconst = [None] * n_const
    for i, v in zip(tok_g, res[:len(tok_g)]):
        dtok[i] = v
    for i, v in zip(const_g, res[len(tok_g):]):
        dconst[i] = v
    return dtok, dconst


def blocked_op(name, fn, groups, nb, tok_args, const_args, out_args, out_shapes, state_shapes=()):
    state_shapes = tuple(state_shapes)

    @jax.custom_vjp
    def op(tok, consts):
        outs, _ = blocked_forward(name, fn, groups, nb, tok_args, tok, const_args, consts, out_args, out_shapes, state_shapes, False)
        return outs

    def fwd(tok, consts):
        outs, saved = blocked_forward(name, fn, groups, nb, tok_args, tok, const_args, consts, out_args, out_shapes, state_shapes, True)
        return outs, (tok, consts, saved)

    def bwd(res, cts):
        tok, consts, saved = res
        dtok, dconst = blocked_backward(name + "_bwd", fn, groups, nb, tok_args, tok, const_args, consts, out_args, list(cts), state_shapes, saved)
        dtok = [jnp.zeros_like(t) if d is None else d for t, d in zip(tok, dtok)]
        dconst = [jnp.zeros_like(k) if d is None else d for k, d in zip(consts, dconst)]
        return dtok, dconst

    op.defvjp(fwd, bwd)
    return op


def _make_split(sizes):
    offs = [sum(sizes[:i]) for i in range(len(sizes))]

    @jax.custom_vjp
    def split(x):
        return tuple(x[:, o:o + s] for o, s in zip(offs, sizes))

    split.defvjp(lambda x: (split(x), None), lambda _, g: (jnp.concatenate(list(g), axis=1),))
    return split


def _make_join(sizes):
    offs = [sum(sizes[:i]) for i in range(len(sizes))]

    @jax.custom_vjp
    def join(parts):
        return jnp.concatenate(list(parts), axis=1)

    join.defvjp(lambda parts: (join(parts), None), lambda _, g: (tuple(g[:, o:o + s] for o, s in zip(offs, sizes)),))
    return join


def split_cols(x, sizes):
    return _make_split(tuple(sizes))(x)


def join_cols(parts):
    return _make_join(tuple(p.shape[1] for p in parts))(tuple(parts))


def lane_scalar(row, j):
    lane = lax.broadcasted_iota(jnp.int32, row.shape, 1)
    return jnp.sum(jnp.where(lane == j, row, 0.0), axis=1, keepdims=True)


def lane_col(blk, j):
    lane = lax.broadcasted_iota(jnp.int32, blk.shape, 1)
    return jnp.sum(jnp.where(lane == j, blk, 0.0), axis=1, keepdims=True)


LANE = 128
MM_ROWS = 512
MM_TILE_M, MM_TILE_N = 1024, 1536
MM_TILE_MT = 1408


def _tile(n, cap, unit):
    if n <= cap:
        return n
    best = None
    for t in range(unit, cap + 1, unit):
        if n % t == 0:
            best = t
    assert best is not None, (n, cap, unit)
    return best


def _matmul_resident(a, b, mode, add, out_dtype, name, lead, shards):
    m, k = a.shape
    rows_b, cols_b = b.shape[-2:]
    n = cols_b * max(shards, 1) if mode == "nn" else rows_b
    tm = _tile(m, MM_ROWS, 8)
    nolead = (None,) * len(lead)
    if shards:
        w_spec = pl.BlockSpec((shards,) + nolead + (rows_b, cols_b), lambda i: (0,) + lead + (0, 0))
    else:
        w_spec = pl.BlockSpec(nolead + (rows_b, cols_b), lambda i: lead + (0, 0))
    has_add = add is not None
    dims = ((1,), (0,)) if mode == "nn" else ((1,), (1,))

    def body(*refs):
        a_ref, w_ref, o_ref = refs[0], refs[1], refs[-1]
        if shards and mode == "nn":
            lhs = a_ref[...].astype(BF16)
            for s in range(shards):
                cols = slice(s * cols_b, (s + 1) * cols_b)
                part = _dot_bf16(lhs, w_ref[s], dims)
                if has_add:
                    part = part + refs[2][:, cols].astype(F32)
                o_ref[:, cols] = part.astype(o_ref.dtype)
            return
        if shards:
            acc = _dot_bf16(a_ref[:, 0:cols_b], w_ref[0], dims)
            for s in range(1, shards):
                acc = acc + _dot_bf16(a_ref[:, s * cols_b:(s + 1) * cols_b], w_ref[s], dims)
        else:
            acc = _dot_bf16(a_ref[...], w_ref[...], dims)
        if has_add:
            acc = acc + refs[2][...].astype(F32)
        o_ref[...] = acc.astype(o_ref.dtype)

    o_spec = pl.BlockSpec((tm, n), lambda i: (i, 0))
    return pl.pallas_call(
        body, name=name, grid=(m // tm,),
        in_specs=[pl.BlockSpec((tm, k), lambda i: (i, 0)), w_spec] + ([o_spec] if has_add else []),
        out_specs=o_spec, out_shape=jax.ShapeDtypeStruct((m, n), out_dtype), compiler_params=_params(("parallel",)),
    )(*([a, b] + ([add] if has_add else [])))


def matmul(a, b, mode="nn", add=None, out_dtype=F32, name="matmul", pre=None, col_shards=0):
    if mode != "tn":
        return _matmul_resident(a, b, mode, add, out_dtype, name, () if pre is None else tuple(pre), col_shards)
    assert add is None and pre is None
    (k, m), n = a.shape, b.shape[1]
    shard_n = n // col_shards if col_shards else n
    tm, tn, tk = _tile(m, MM_TILE_MT, LANE), _tile(shard_n, MM_TILE_N, LANE), _tile(k, MM_TILE_M, LANE)
    nk, qn = k // tk, shard_n // tn

    def body(a_ref, b_ref, o_ref, acc_ref):
        l = pl.program_id(2)

        @pl.when(l == 0)
        def _():
            acc_ref[...] = jnp.zeros_like(acc_ref)

        acc_ref[...] += _dot_bf16(a_ref[...], b_ref[...], ((0,), (0,)))

        @pl.when(l == nk - 1)
        def _():
            o_ref[...] = acc_ref[...].astype(o_ref.dtype)

    if col_shards:
        o_spec = pl.BlockSpec((None, tm, tn), lambda i, j, l: (j // qn, i, j % qn))
        out_shape = jax.ShapeDtypeStruct((col_shards, m, shard_n), out_dtype)
    else:
        o_spec = pl.BlockSpec((tm, tn), lambda i, j, l: (i, j))
        out_shape = jax.ShapeDtypeStruct((m, n), out_dtype)
    return pl.pallas_call(
        body, name=name, grid=(m // tm, n // tn, nk),
        in_specs=[pl.BlockSpec((tk, tm), lambda i, j, l: (l, i)), pl.BlockSpec((tk, tn), lambda i, j, l: (l, j))],
        out_specs=o_spec, out_shape=out_shape, scratch_shapes=[pltpu.VMEM((tm, tn), F32)],
        compiler_params=_params(("parallel", "parallel", "arbitrary")),
    )(a, b)


def linear(name, out_dtype=F32, pre=None, col_shards=0):
    @jax.custom_vjp
    def op(a, w, handle):
        return matmul(a, w, "nn", out_dtype=out_dtype, name=name, pre=pre, col_shards=col_shards)

    def fwd(a, w, handle):
        return op(a, w, handle), (a, w)

    def bwd(res, g):
        a, w = res
        da = matmul(g, w, "nt", out_dtype=a.dtype, name=name + "_da", pre=pre, col_shards=col_shards)
        dw = matmul(a, g, "tn", out_dtype=F32, name=name + "_dw", col_shards=col_shards)
        return da, jnp.zeros_like(w), dw

    op.defvjp(fwd, bwd)
    return op


def multi_linear(name, out_dtypes, pres, shards):
    sel = [dict(pre=p, col_shards=s) for p, s in zip(pres, shards)]

    @jax.custom_vjp
    def op(a, ws, handles):
        return [matmul(a, w, "nn", out_dtype=dt, name=f"{name}{i}", **sel[i]) for i, (w, dt) in enumerate(zip(ws, out_dtypes))]

    def fwd(a, ws, handles):
        return op(a, ws, handles), (a, ws)

    def bwd(res, gs):
        a, ws = res
        acc = None
        for i, (g, w) in enumerate(zip(gs, ws)):
            last = i == len(ws) - 1
            acc = matmul(g, w, "nt", add=acc, out_dtype=a.dtype if last else F32, name=f"{name}{i}_da", **sel[i])
        dws = [matmul(a, g, "tn", out_dtype=F32, name=f"{name}{i}_dw", col_shards=shards[i]) for i, g in enumerate(gs)]
        return acc, [jnp.zeros_like(w) for w in ws], dws

    op.defvjp(fwd, bwd)
    return op


def dense(name, out_dtype=F32):
    @jax.custom_vjp
    def op(a, w):
        return matmul(a, w, "nn", out_dtype=out_dtype, name=name)

    def fwd(a, w):
        return op(a, w), (a, w)

    def bwd(res, g):
        a, w = res
        return (matmul(g, w, "nt", out_dtype=a.dtype, name=name + "_da"),
                matmul(a, g, "tn", out_dtype=w.dtype, name=name + "_dw"))

    op.defvjp(fwd, bwd)
    return op


def to_col(row):
    n = row.shape[1]
    eye = lax.broadcasted_iota(jnp.int32, (n, n), 0) == lax.broadcasted_iota(jnp.int32, (n, n), 1)
    return jnp.sum(jnp.where(eye, row, 0.0), axis=1, keepdims=True)


def norm_fn(states, toks, consts):
    return [], [rmsnorm(toks[0], consts[0])]


def make_addnorm_fn(scale):
    def fn(states, toks, consts):
        h = toks[0] + scale * toks[1]
        return [], [h, rmsnorm(h, consts[0])]

    return fn


def swiglu_fn(states, toks, consts):
    gate, up = split_cols(toks[0], (FFN_DIM, FFN_DIM))
    return [], [jax.nn.silu(gate) * up]


def gate_merge_fn(states, toks, consts):
    gates = split_cols(jax.nn.sigmoid(toks[0] + consts[0]), (D_MODEL,) * N_BRANCH)
    mixed = gates[0] * toks[1]
    for n in range(1, N_BRANCH):
        mixed = mixed + gates[n] * toks[1 + n]
    return [], [mixed]


def ple_fn(states, toks, consts):
    h = toks[0] + jax.nn.sigmoid(toks[1]) * toks[2]
    return [], [h, rmsnorm(h, consts[0])]


def ple_loss_fn(states, toks, consts):
    h = toks[0] + jax.nn.sigmoid(toks[1]) * toks[2]
    err = rmsnorm(h, consts[0]) - toks[3]
    return [], [0.5 * jnp.mean(err * err, axis=-1, keepdims=True)]


def s5_discretise_fn(states, toks, consts):
    log_step, a_re, a_im, b_re, b_im = consts
    step = jnp.exp(log_step)
    mag = jnp.exp(a_re * step)
    ab_re, ab_im = mag * jnp.cos(a_im * step), mag * jnp.sin(a_im * step)
    den = a_re * a_re + a_im * a_im
    num_re = ab_re - 1.0
    f_re = (num_re * a_re + ab_im * a_im) / den
    f_im = (ab_im * a_re - num_re * a_im) / den
    return [], [ab_re, ab_im, f_re * b_re - f_im * b_im, f_re * b_im + f_im * b_re]


def s5_scan_fn(states, toks, consts):
    b_re, b_im = split_cols(toks[0], (S5_W, S5_W))
    h_re, h_im = complex_scan(b_re, b_im, consts[0], consts[1], states[0], states[1])
    return [last_row(h_re), last_row(h_im)], [join_cols([h_re, h_im])]


def s5_scan_bwd_fn(states, toks, consts):
    g_re, g_im, acc_re, acc_im = states
    h, ct, h0_re, h0_im = toks
    a_re, a_im = consts
    h_re, h_im = split_cols(h, (S5_W, S5_W))
    c_re, c_im = split_cols(ct, (S5_W, S5_W))
    last = _row_ids(c_re.shape) == c_re.shape[0] - 1
    cts = (c_re + jnp.where(last, g_re, 0.0), c_im + jnp.where(last, g_im, 0.0))
    d_re, d_im, da_re, da_im, d0_re, d0_im = _complex_scan_bwd((a_re, a_im, h_re, h_im, h0_re, h0_im), cts)
    acc_re, acc_im = acc_re + da_re, acc_im + da_im
    return [d0_re, d0_im, acc_re, acc_im], [join_cols([d_re, d_im]), acc_re, acc_im]


def s5_scan(name, t, bu, a_re, a_im):
    nblk = min(SCAN_ROWS, t)
    nb = t // nblk
    wide, row = rows(2 * S5_W, nblk, dtype=ACT), whole((1, S5_W))
    entry = Arg((None, None, 1, S5_W), lambda g, c: (0, c, 0, 0))
    state = [(1, S5_W)] * 2

    def run(bu, a_re, a_im, save):
        return blocked_forward(name, s5_scan_fn, 1, nb, [wide], [bu], [row, row], [a_re, a_im], [wide], [(t, 2 * S5_W)], state, save)

    @jax.custom_vjp
    def op(bu, a_re, a_im):
        return run(bu, a_re, a_im, False)[0][0]

    def fwd(bu, a_re, a_im):
        outs, saved = run(bu, a_re, a_im, True)
        return outs[0], (outs[0], saved, a_re, a_im)

    def bwd(res, ct):
        h, saved, a_re, a_im = res
        outs, _ = blocked_forward(name + "_bwd", s5_scan_bwd_fn, 1, nb, [wide, wide, entry, entry], [h, ct] + saved, [row, row],
                                  [a_re, a_im], [wide, row, row], [(t, 2 * S5_W), (1, S5_W), (1, S5_W)], state * 2, False, reverse=True)
        return tuple(outs)

    op.defvjp(fwd, bwd)
    return op(bu, a_re, a_im)


def s5_glu_fn(states, toks, consts):
    d_skip, w_glu, b_glu = consts
    z = jax.nn.gelu(toks[0] + d_skip * toks[1])
    return [], [z * jax.nn.sigmoid(mm(z, w_glu) + b_glu)]


def lru_fn(states, toks, consts):
    h0, tail = states
    x, gate = toks
    conv_w, conv_b, w_r, b_r, w_i, b_i, lam = consts
    xc = causal_conv4(x, tail, conv_w) + conv_b
    r = jax.nn.sigmoid(mm(xc, w_r) + b_r)
    i_g = jax.nn.sigmoid(mm(xc, w_i) + b_i)
    log_a = -LRU_C * r * jax.nn.softplus(-lam)
    inp = jnp.sqrt(1.0 - jnp.exp(2.0 * log_a)) * (i_g * xc)
    h = lin_scan(jnp.exp(log_a), inp, h0)
    return [last_row(h), tail_rows(x)], [h * jax.nn.gelu(gate)]


def m2_conv_fn(states, toks, consts):
    y = jax.nn.silu(causal_conv4(toks[0], states[0], consts[0]) + consts[1])
    return [tail_rows(toks[0])], list(split_cols(y, (BRANCH_WIDTH, M2_GROUPS * M2_STATE, M2_GROUPS * M2_STATE)))


def gdn_conv_fn(states, toks, consts):
    y = jax.nn.silu(causal_conv4(toks[0], states[0], consts[0]))
    return [tail_rows(toks[0])], list(split_cols(y, (BRANCH_WIDTH,) * 3))


def ssd_fn(states, toks, consts):
    xs, bm, cm, small = toks
    dt_bias, a_log, d_skip = consts
    x_pairs = split_cols(xs, (LANE,) * 4)
    b_g = split_cols(bm, (M2_STATE,) * M2_GROUPS)
    c_g = split_cols(cm, (M2_STATE,) * M2_GROUPS)
    lo = lax.broadcasted_iota(jnp.int32, (1, LANE), 1) < M2_HEAD_DIM
    new_states, y_pairs = [], []
    for g in range(M2_GROUPS):
        scores = mm_nt(c_g[g], b_g[g])
        y_off = split_cols(mm_nt(c_g[g], states[g]), (LANE, LANE))
        to_end, ends = [], []
        for j in range(2):
            pair = 2 * g + j
            x2 = x_pairs[pair]
            dts, css, decays, end = [], [], [], []
            for h in (2 * pair, 2 * pair + 1):
                dt = jax.nn.softplus(lane_col(small, h) + lane_scalar(dt_bias, h))
                a = dt * (-jnp.exp(lane_scalar(a_log, h)))
                cs, decay = causal_decay(a)
                dts.append(dt)
                css.append(cs)
                decays.append(decay)
                end.append(jnp.sum(a, axis=0, keepdims=True))
            xdt = x2 * jnp.where(lo, dts[0], dts[1])
            y = mm(scores * decays[0], jnp.where(lo, xdt, 0.0)) + mm(scores * decays[1], jnp.where(lo, 0.0, xdt))
            cs2 = jnp.where(lo, css[0], css[1])
            end2 = jnp.where(lo, end[0], end[1])
            y = y + y_off[j] * jnp.exp(cs2)
            y = y + jnp.where(lo, lane_scalar(d_skip, 2 * pair), lane_scalar(d_skip, 2 * pair + 1)) * x2
            y_pairs.append(y)
            to_end.append(xdt * jnp.exp(end2 - cs2))
            ends.append(end2)
        chunk_decay = jnp.exp(to_col(join_cols(ends)))
        new_states.append(states[g] * chunk_decay + mm_tn(join_cols(to_end), b_g[g]))
    return new_states, [join_cols(y_pairs)]


def m2_post_fn(states, toks, consts):
    return [], [rmsnorm(toks[0] * jax.nn.silu(toks[1]), consts[0])]


@jax.custom_vjp
def nilpotent_inverses(mats):
    size = mats[0].shape[0]
    eye = lax.broadcasted_iota(jnp.int32, mats[0].shape, 0) == lax.broadcasted_iota(jnp.int32, mats[0].shape, 1)
    invs = [jnp.where(eye, 1.0, 0.0) + m for m in mats]
    powers = list(mats)
    d = 2
    while d < size:
        powers = [_dot_bf16x3(p, p, ((1,), (0,))) for p in powers]
        invs = [i + _dot_bf16x3(i, p, ((1,), (0,))) for i, p in zip(invs, powers)]
        d *= 2
    return tuple(invs)


def _nilpotent_inverses_fwd(mats):
    invs = nilpotent_inverses(mats)
    return invs, invs


def _nilpotent_inverses_bwd(invs, gs):
    right = [_dot_bf16x3(g, i, ((1,), (1,))) for g, i in zip(gs, invs)]
    return (tuple(_dot_bf16x3(i, r, ((0,), (0,))) for i, r in zip(invs, right)),)


nilpotent_inverses.defvjp(_nilpotent_inverses_fwd, _nilpotent_inverses_bwd)


def gdn_fn(states, toks, consts):
    q, k, v, gate, small = toks
    dt_bias, a_log, norm_g = consts
    hs = range(GDN_HEADS)
    heads = (GDN_HEAD_DIM,) * GDN_HEADS
    qs, ks, vs, gs = split_cols(q, heads), split_cols(k, heads), split_cols(v, heads), split_cols(gate, heads)
    n = q.shape[0]
    strict = lax.broadcasted_iota(jnp.int32, (n, n), 0) > lax.broadcasted_iota(jnp.int32, (n, n), 1)
    qn = [qs[h] * lax.rsqrt(jnp.sum(qs[h] * qs[h], axis=-1, keepdims=True) + EPS) * (GDN_HEAD_DIM ** -0.5) for h in hs]
    kn = [ks[h] * lax.rsqrt(jnp.sum(ks[h] * ks[h], axis=-1, keepdims=True) + EPS) for h in hs]
    beta = [jax.nn.sigmoid(lane_col(small, h)) for h in hs]
    g = [-jnp.exp(lane_scalar(a_log, h)) * jax.nn.softplus(lane_col(small, GDN_HEADS + h) + lane_scalar(dt_bias, h)) for h in hs]
    cs_decay = [causal_decay(g[h]) for h in hs]
    cs, decay = [c for c, _ in cs_decay], [d for _, d in cs_decay]
    kb = [kn[h] * beta[h] for h in hs]
    inv = nilpotent_inverses(tuple(-jnp.where(strict, mm_nt(kb[h], kn[h]) * decay[h], 0.0) for h in hs))
    ecs = [jnp.exp(cs[h]) for h in hs]
    u = [mmh(inv[h], vs[h] * beta[h]) for h in hs]
    w = [mmh(inv[h], kb[h] * ecs[h]) for h in hs]
    qk = [mm_nt(qn[h], kn[h]) * decay[h] for h in hs]
    cs_end = [jnp.sum(g[h], axis=0, keepdims=True) for h in hs]
    v_new = [u[h] - mm(w[h], states[h]) for h in hs]
    o = [mm(qn[h] * ecs[h], states[h]) + mm(qk[h], v_new[h]) for h in hs]
    new_states = [states[h] * jnp.exp(cs_end[h]) + mm_tn(kn[h] * jnp.exp(cs_end[h] - cs[h]), v_new[h]) for h in hs]
    return new_states, [join_cols([rmsnorm(o[h], norm_g) * jax.nn.silu(gs[h]) for h in hs])]


def adamw_fn(states, toks, consts):
    w, g, m, v = toks
    m = ADAM_B1 * m + (1.0 - ADAM_B1) * g
    v = ADAM_B2 * v + (1.0 - ADAM_B2) * (g * g)
    m_hat = m / (1.0 - ADAM_B1 ** ADAM_STEP)
    v_hat = v / (1.0 - ADAM_B2 ** ADAM_STEP)
    return [], [-ADAM_LR * (m_hat / (jnp.sqrt(v_hat) + ADAM_EPS) + ADAM_WD * w), m, v]


def tok_op(name, fn, t, nblk, tok, consts, outs, states=()):
    nblk = min(nblk, t)
    tok_args = [rows(e[0], nblk, dtype=e[1], grad=e[2] if len(e) > 2 else True) for e in tok]
    const_args = [whole(s) for s in consts]
    out_args = [rows(w, nblk, dtype=dt) for (w, dt) in outs]
    return blocked_op(name, fn, 1, t // nblk, tok_args, const_args, out_args, [(t, w) for (w, _) in outs], states)


def const_op(name, fn, in_shapes, out_shapes):
    return blocked_op(name, fn, 1, 1, [], [whole(s) for s in in_shapes], [whole(s) for s in out_shapes], list(out_shapes))


W = BRANCH_WIDTH
SCAN_ROWS = 128
ROW_BLOCK = 512


def s5_mixer(tag, t, u, p):
    col = (S5_W, 1)
    disc = const_op("s5_disc" + tag, s5_discretise_fn, [col, col, col, (S5_W, 16), (S5_W, 16)], [col, col, (S5_W, 16), (S5_W, 16)])
    ab_re, ab_im, bb_re, bb_im = disc([], [
        jnp.repeat(p["s5_log_step"], S5_STATE).reshape(col), p["s5_a_re"].reshape(col), p["s5_a_im"].reshape(col),
        p["s5_b_re"].reshape(S5_W, S5_GROUP_CH), p["s5_b_im"].reshape(S5_W, S5_GROUP_CH)])
    eye = jnp.eye(S5_GROUPS, dtype=F32)

    def block_in(bb):
        return jnp.einsum("gpc,gh->gchp", bb.reshape(S5_GROUPS, S5_STATE, S5_GROUP_CH), eye).reshape(W, S5_W)

    def block_out(c):
        return jnp.einsum("gcp,gh->gphc", c, eye).reshape(S5_W, W)

    w_b = jnp.concatenate([block_in(bb_re), block_in(bb_im)], axis=1)
    w_c = jnp.concatenate([block_out(p["s5_c_re"]), -block_out(p["s5_c_im"])], axis=0)
    bu = dense("s5_b" + tag, ACT)(u, w_b)
    h = s5_scan("s5_scan" + tag, t, bu, ab_re.reshape(1, S5_W), ab_im.reshape(1, S5_W))
    yc = dense("s5_c" + tag)(h, w_c)
    glu = tok_op("s5_glu" + tag, s5_glu_fn, t, ROW_BLOCK, [(W, F32), (W, F32)], [(1, W), (W, W), (1, W)], [(W, ACT)])
    return glu([yc, u], [p["s5_d"].reshape(1, W), p["s5_w_glu"], p["s5_b_glu"][None]])[0]


def lru_mixer(tag, t, x, gate, p):
    def block_diag(w):
        return jnp.einsum("hij,hk->hikj", w, jnp.eye(LRU_HEADS, dtype=F32)).reshape(W, W)

    op = tok_op("lru" + tag, lru_fn, t, SCAN_ROWS, [(W, F32), (W, F32)],
                [(4, W), (1, W), (W, W), (1, W), (W, W), (1, W), (1, W)], [(W, ACT)], states=[(1, W), (TAIL, W)])
    return op([x, gate], [p["lru_conv_w"], p["lru_conv_b"][None], block_diag(p["lru_w_r"]), p["lru_b_r"][None],
                          block_diag(p["lru_w_i"]), p["lru_b_i"][None], p["lru_lambda"][None]])[0]


def m2_mixer(tag, t, z, xbc, small, p):
    cw = 2 * W
    conv = tok_op("m2_conv" + tag, m2_conv_fn, t, ROW_BLOCK, [(cw, F32)], [(4, cw), (1, cw)],
                  [(W, F32), (W // 2, F32), (W // 2, F32)], states=[(TAIL, cw)])
    xs, bm, cm = conv([xbc], [p["m2_conv_w"], p["m2_conv_b"][None]])
    ssd = tok_op("ssd" + tag, ssd_fn, t, CHUNK, [(W, F32), (W // 2, F32), (W // 2, F32), (LANE, F32)],
                 [(1, M2_HEADS)] * 3, [(W, F32)], states=[(4 * M2_HEAD_DIM, M2_STATE)] * M2_GROUPS)
    y = ssd([xs, bm, cm, small], [p["m2_dt_bias"][None], p["m2_a_log"][None], p["m2_d"][None]])[0]
    post = tok_op("m2_post" + tag, m2_post_fn, t, ROW_BLOCK, [(W, F32), (W, F32)], [(1, W)], [(W, ACT)])
    return post([y, z], [p["m2_norm"][None]])[0]


def gdn_mixer(tag, t, qkv, gate, small, p):
    conv = tok_op("gdn_conv" + tag, gdn_conv_fn, t, ROW_BLOCK, [(3 * W, F32)], [(4, 3 * W)], [(W, F32)] * 3, states=[(TAIL, 3 * W)])
    q, k, v = conv([qkv], [p["gdn_conv_w"]])
    op = tok_op("gdn" + tag, gdn_fn, t, CHUNK, [(W, F32)] * 4 + [(LANE, F32)], [(1, GDN_HEADS), (1, GDN_HEADS), (1, GDN_HEAD_DIM)],
                [(W, ACT)], states=[(GDN_HEAD_DIM, GDN_HEAD_DIM)] * GDN_HEADS)
    return op([q, k, v, gate, small], [p["gdn_dt_bias"][None], p["gdn_a_log"][None], p["gdn_norm"][None]])[0]


WEIGHTS = ["ffn1_norm", "ffn1_w_in", "ffn1_w_out", "mix_norm", "w_in", "w_gate", "b_gate", "s5_log_step", "s5_a_re",
           "s5_a_im", "s5_b_re", "s5_b_im", "s5_c_re", "s5_c_im", "s5_d", "s5_w_glu", "s5_b_glu", "lru_conv_w",
           "lru_conv_b", "lru_w_r", "lru_b_r", "lru_w_i", "lru_b_i", "lru_lambda", "m2_conv_w", "m2_conv_b", "m2_dt_bias",
           "m2_a_log", "m2_d", "m2_norm", "gdn_conv_w", "gdn_dt_bias", "gdn_a_log", "gdn_norm", "w_branch", "w_out",
           "ffn2_norm", "ffn2_w_in", "ffn2_w_out", "ple_norm", "ple_w_gate", "ple_w_proj", "final_norm"]
N_CHIPS = 4
N_DEV = 8
IN_WIDTH = 5136
SHARDED = {
    "ffn1_w_in": ((D_MODEL, 2 * FFN_DIM // N_CHIPS), 1, False),
    "ffn1_w_out": ((FFN_DIM // N_CHIPS, D_MODEL), 0, False),
    "w_in": ((D_MODEL, IN_WIDTH // N_CHIPS), 1, False),
    "w_gate": ((D_MODEL, N_BRANCH * D_MODEL // N_CHIPS), 1, False),
    "s5_w_glu": ((W // N_CHIPS, W), 0, False),
    "lru_conv_w": ((4, W // N_CHIPS), 1, True),
    "m2_conv_w": ((4, 2 * W // N_CHIPS), 1, True),
    "gdn_conv_w": ((4, 3 * W // N_CHIPS), 1, True),
    "w_branch": ((N_BRANCH, W, D_MODEL // N_CHIPS), 2, False),
    "w_out": ((D_MODEL // N_CHIPS, D_MODEL), 0, False),
    "ffn2_w_in": ((D_MODEL, 2 * FFN_DIM // N_CHIPS), 1, False),
    "ffn2_w_out": ((FFN_DIM // N_CHIPS, D_MODEL), 0, False),
    "ple_w_gate": ((D_MODEL // N_CHIPS, D_MODEL), 0, False),
    "ple_w_proj": ((256, D_MODEL // N_CHIPS), 1, False),
}
SMALL = [n for n in WEIGHTS if n not in SHARDED]
ROW = 1024


def _count(shape):
    return math.prod(shape)


def _round_up(n, unit):
    return -(-n // unit) * unit


N_GATHER = sum(DEPTH * _count(s) * (2 if exact else 1) for s, _, exact in SHARDED.values())
N_GRAD = sum(DEPTH * _count(s) for s, _, _ in SHARDED.values())
GATHER_ROWS = _round_up(-(-N_GATHER // ROW), 32)
GRAD_ROWS = _round_up(-(-N_GRAD // ROW), 32)
GRAD_HALF = GRAD_ROWS // 2
IN_PIECES = [(0, 512), (512, 512), (1024, 512), (1536, 512), (2048, 1024), (3072, 8), (3080, 1536), (4616, 512), (5128, 8)]


COL_SHARDED = ("ffn1_w_in", "ffn2_w_in", "w_gate", "ple_w_proj", "w_branch")
IN_SHARD = IN_WIDTH // N_CHIPS


def _w_in_cuts():
    cuts = {0, IN_SHARD}
    for s in range(N_CHIPS):
        for start, _ in IN_PIECES:
            if s * IN_SHARD < start < (s + 1) * IN_SHARD:
                cuts.add(start - s * IN_SHARD)
    return sorted(cuts)


IN_CUTS = _w_in_cuts()
IN_BLOCKS = list(zip(IN_CUTS[:-1], IN_CUTS[1:]))


def _piece_of(col):
    for k, (start, n) in enumerate(IN_PIECES):
        if start <= col < start + n:
            return k, col - start
    raise ValueError(col)


def cut_w_in(w):
    return jnp.concatenate([w[:, :, lo:hi].reshape(-1) for lo, hi in IN_BLOCKS])


def uncut_w_in(flat):
    blocks, off = [], 0
    for lo, hi in IN_BLOCKS:
        cnt = DEPTH * D_MODEL * (hi - lo)
        blocks.append(flat[off:off + cnt].reshape(DEPTH, D_MODEL, hi - lo))
        off += cnt
    return jnp.concatenate(blocks, axis=2)


def pack_for_gather(a):
    parts = []
    for n, (_, _, exact) in SHARDED.items():
        w = cut_w_in(a[n]) if n == "w_in" else a[n]
        parts.append((lax.bitcast_convert_type(w, BF16) if exact else w.astype(BF16)).reshape(-1))
    flat = jnp.concatenate(parts)
    return jnp.pad(flat, (0, GATHER_ROWS * ROW - flat.shape[0])).reshape(GATHER_ROWS, ROW)


def unpack_gathered(buf):
    flat16 = buf.reshape(N_CHIPS, -1)
    flat = buf.astype(ACT).reshape(N_CHIPS, -1)
    out, off = {}, 0
    for n, (shape, ax, exact) in SHARDED.items():
        cnt = DEPTH * _count(shape) * (2 if exact else 1)
        piece = (flat16 if exact else flat)[:, off:off + cnt]
        off += cnt
        if n == "w_in":
            cols = [[[] for _ in IN_PIECES] for _ in range(DEPTH)]
            for s in range(N_CHIPS):
                o = 0
                for lo, hi in IN_BLOCKS:
                    c = DEPTH * D_MODEL * (hi - lo)
                    blk = piece[s, o:o + c].reshape(DEPTH, D_MODEL, hi - lo)
                    o += c
                    k, _ = _piece_of(s * IN_SHARD + lo)
                    for layer in range(DEPTH):
                        cols[layer][k].append(blk[layer])
            out[n] = [[_pad_lanes(jnp.concatenate(c, axis=1)) for c in cols[layer]] for layer in range(DEPTH)]
        elif n in COL_SHARDED:
            out[n] = piece.reshape(N_CHIPS, DEPTH, *shape)
        else:
            if exact:
                w = lax.bitcast_convert_type(piece.reshape(N_CHIPS, DEPTH, *shape, 2), F32)
            else:
                w = piece.reshape(N_CHIPS, DEPTH, *shape)
            full = list(shape)
            full[ax] *= N_CHIPS
            out[n] = jnp.moveaxis(w, 0, ax + 1).reshape(DEPTH, *full)
    return out


def _pad_lanes(w):
    n = w.shape[1]
    return w if n % LANE == 0 else jnp.pad(w, ((0, 0), (0, LANE - n % LANE)))


def shard_w_in_grads(pieces):
    shards = []
    for s in range(N_CHIPS):
        parts = []
        for lo, hi in IN_BLOCKS:
            k, dst = _piece_of(s * IN_SHARD + lo)
            parts.append(jnp.stack([pieces[layer][k][:, dst:dst + hi - lo] for layer in range(DEPTH)]).reshape(-1))
        shards.append(jnp.concatenate(parts))
    return jnp.stack(shards)


def pack_grads(g):
    tail = jnp.zeros((N_CHIPS, GRAD_ROWS * ROW - N_GRAD), F32)
    return jnp.concatenate([g[n] for n in SHARDED] + [tail], axis=1).reshape(N_CHIPS, GRAD_ROWS, ROW)


def shard_full_grads(n, per_layer):
    shape, ax, _ = SHARDED[n]
    parts = []
    for full in per_layer:
        w = full.reshape(*full.shape[:ax], N_CHIPS, shape[ax], *full.shape[ax + 1:])
        parts.append(jnp.moveaxis(w, ax, 0).reshape(N_CHIPS, -1))
    return jnp.concatenate(parts, axis=1)


def unpack_shard(buf):
    flat = buf.reshape(-1)
    out, off = {}, 0
    for n, (shape, _, _) in SHARDED.items():
        cnt = DEPTH * _count(shape)
        piece = flat[off:off + cnt]
        out[n] = uncut_w_in(piece) if n == "w_in" else piece.reshape(DEPTH, *shape)
        off += cnt
    return out


def _rows_of(shape):
    return -(-_count(shape) // ROW)


def pack_small(vals, shapes):
    used = sum(_rows_of(shapes[n]) for n in SMALL)
    parts = []
    for n in SMALL:
        flat = vals[n].reshape(-1)
        tail = (_round_up(used, 8) - used) * ROW if n == SMALL[-1] else 0
        parts.append(jnp.pad(flat, (0, _rows_of(shapes[n]) * ROW - flat.shape[0] + tail)))
    return jnp.concatenate(parts).reshape(-1, ROW)


def unpack_small(buf, shapes):
    out, row = {}, 0
    for n in SMALL:
        r = _rows_of(shapes[n])
        out[n] = buf[row:row + r].reshape(-1)[:_count(shapes[n])].reshape(shapes[n])
        row += r
    return out


ANY = pl.BlockSpec(memory_space=pl.ANY)


def _position():
    return lax.axis_index("x"), lax.axis_index("y"), lax.axis_index("c")


def _other_chips(x, y):
    return [(1 - x, y), (x, 1 - y), (1 - x, 1 - y)]


PLACE_ROWS = 592


def gather_weights(packed, slot):
    r = packed.shape[0]
    half = r // 2
    nblk = r // PLACE_ROWS

    def place(s_ref, in_ref, o_ref):
        o_ref[...] = in_ref[...]

    slots = pl.pallas_call(
        place, name="place_shard",
        grid_spec=pltpu.PrefetchScalarGridSpec(
            num_scalar_prefetch=1, grid=(nblk,), in_specs=[pl.BlockSpec((PLACE_ROWS, ROW), lambda i, s: (i, 0))],
            out_specs=pl.BlockSpec((None, PLACE_ROWS, ROW), lambda i, s: (s[0], i, 0))),
        out_shape=jax.ShapeDtypeStruct((N_CHIPS, r, ROW), packed.dtype), compiler_params=_params(("arbitrary",)),
    )(slot, packed)

    def body(in_ref, out_ref, send_sems, recv_sems):
        x, y, c = _position()
        sibling = (x, y, 1 - c)
        chips = _other_chips(x, y)

        def half_rows(px, py, pc):
            return out_ref.at[2 * px + py, pl.ds(pl.multiple_of(pc * half, 16), half), :]

        def copy(k, block, to):
            return pltpu.make_async_remote_copy(
                src_ref=half_rows(*block), dst_ref=half_rows(*block),
                send_sem=send_sems.at[k], recv_sem=recv_sems.at[k], device_id=to, device_id_type=MESH)

        first = [copy(j, (x, y, c), (*chip, c)) for j, chip in enumerate(chips)]
        for cp in first:
            cp.start()
        passed = [copy(3 + j, (*chip, c), sibling) for j, chip in enumerate(chips)]
        for j, chip in enumerate(chips):
            copy(j, (*chip, c), (x, y, c)).wait_recv()
            passed[j].start()
        for j, chip in enumerate(chips):
            copy(3 + j, (*chip, 1 - c), (x, y, c)).wait_recv()
        for cp in first + passed:
            cp.wait_send()

    return pl.pallas_call(
        body, name="gather_weights", in_specs=[ANY], out_specs=ANY, input_output_aliases={0: 0},
        out_shape=jax.ShapeDtypeStruct((N_CHIPS, r, ROW), packed.dtype),
        scratch_shapes=[pltpu.SemaphoreType.DMA((6,)), pltpu.SemaphoreType.DMA((6,))],
    )(slots)


def swap_pair_halves(g):
    half = g.shape[1] // 2

    def body(g_ref, land_ref, send_sem, recv_sem):
        x, y, c = _position()
        src = g_ref.at[:, pl.ds(pl.multiple_of((1 - c) * half, 8), half), :]
        cp = pltpu.make_async_remote_copy(src_ref=src, dst_ref=land_ref, send_sem=send_sem, recv_sem=recv_sem,
                                          device_id=(x, y, 1 - c), device_id_type=MESH)
        cp.start()
        cp.wait()

    return pl.pallas_call(
        body, name="swap_pair_halves", in_specs=[ANY], out_specs=ANY,
        out_shape=jax.ShapeDtypeStruct((N_CHIPS, half, ROW), g.dtype),
        scratch_shapes=[pltpu.SemaphoreType.DMA, pltpu.SemaphoreType.DMA],
    )(g)


def exchange_chip_partials(part):
    half = part.shape[1]

    def body(p_ref, land_ref, send_sems, recv_sems):
        x, y, c = _position()
        cps = [pltpu.make_async_remote_copy(src_ref=p_ref.at[2 * px + py], dst_ref=land_ref.at[j], send_sem=send_sems.at[j],
                                            recv_sem=recv_sems.at[j], device_id=(px, py, c), device_id_type=MESH)
               for j, (px, py) in enumerate(_other_chips(x, y))]
        for cp in cps:
            cp.start()
        for cp in cps:
            cp.wait()

    return pl.pallas_call(
        body, name="exchange_chip_partials", in_specs=[ANY], out_specs=ANY,
        out_shape=jax.ShapeDtypeStruct((3, half, ROW), part.dtype),
        scratch_shapes=[pltpu.SemaphoreType.DMA((3,)), pltpu.SemaphoreType.DMA((3,))],
    )(part)


def share_halves(both):
    half = both.shape[0] // 2

    def body(in_ref, out_ref, send_sem, recv_sem):
        x, y, c = _position()
        my_rows = out_ref.at[pl.ds(pl.multiple_of(c * half, 8), half), :]
        cp = pltpu.make_async_remote_copy(src_ref=my_rows, dst_ref=my_rows, send_sem=send_sem, recv_sem=recv_sem,
                                          device_id=(x, y, 1 - c), device_id_type=MESH)
        cp.start()
        cp.wait()

    return pl.pallas_call(
        body, name="share_halves", in_specs=[ANY], out_specs=ANY, input_output_aliases={0: 0},
        out_shape=jax.ShapeDtypeStruct(both.shape, both.dtype),
        scratch_shapes=[pltpu.SemaphoreType.DMA, pltpu.SemaphoreType.DMA],
    )(both)


def gather_all(block):
    m_per = block.shape[0]

    def body(x_ref, out_ref, send_sems, recv_sems, local_sem):
        x, y, c = _position()
        me, sibling = (x, y, c), (x, y, 1 - c)
        chips = _other_chips(x, y)

        def rows_of(px, py, pc):
            return out_ref.at[pl.ds(pl.multiple_of((4 * px + 2 * py + pc) * m_per, 8), m_per), :]

        def copy(k, blk, to, src=None):
            return pltpu.make_async_remote_copy(
                src_ref=rows_of(*blk) if src is None else src, dst_ref=rows_of(*blk),
                send_sem=send_sems.at[k], recv_sem=recv_sems.at[k], device_id=to, device_id_type=MESH)

        mine = pltpu.make_async_copy(x_ref, rows_of(*me), local_sem)
        mine.start()
        first = [copy(0, me, sibling, src=x_ref)]
        first += [copy(1 + j, me, (*chip, c), src=x_ref) for j, chip in enumerate(chips)]
        for cp in first:
            cp.start()
        passed = [copy(4 + j, (*chip, c), sibling) for j, chip in enumerate(chips)]
        for j, chip in enumerate(chips):
            copy(1 + j, (*chip, c), me).wait_recv()
            passed[j].start()
        copy(0, sibling, me).wait_recv()
        for j, chip in enumerate(chips):
            copy(4 + j, (*chip, 1 - c), me).wait_recv()
        for cp in first + passed:
            cp.wait_send()
        mine.wait()

    return pl.pallas_call(
        body, name="gather_all", out_shape=jax.ShapeDtypeStruct((N_DEV * m_per, ROW), block.dtype),
        in_specs=[pl.BlockSpec(memory_space=pltpu.VMEM)], out_specs=pl.BlockSpec(memory_space=pltpu.VMEM),
        scratch_shapes=[pltpu.SemaphoreType.DMA((7,)), pltpu.SemaphoreType.DMA((7,)), pltpu.SemaphoreType.DMA],
        compiler_params=_params(),
    )(block)


SUM_ROWS = 592


def add_pair_halves(grads, landed, core):
    half = landed.shape[1]
    nblk = half // SUM_ROWS

    def body(c_ref, g_ref, l_ref, o_ref, o16_ref):
        acc = g_ref[...] + l_ref[...]
        o_ref[...] = acc
        o16_ref[...] = acc.astype(BF16)

    blk = (None, SUM_ROWS, ROW)
    o_spec = pl.BlockSpec(blk, lambda s, i, c: (s, i, 0))
    return pl.pallas_call(
        body, name="add_pair_halves",
        grid_spec=pltpu.PrefetchScalarGridSpec(
            num_scalar_prefetch=1, grid=(N_CHIPS, nblk),
            in_specs=[pl.BlockSpec(blk, lambda s, i, c: (s, c[0] * nblk + i, 0)), o_spec], out_specs=[o_spec, o_spec]),
        out_shape=[jax.ShapeDtypeStruct(landed.shape, F32), jax.ShapeDtypeStruct(landed.shape, BF16)],
        compiler_params=_params(("arbitrary", "arbitrary")),
    )(core, grads, landed)


def add_chip_partials(part, landed, slot, core):
    half = part.shape[1]
    nblk = half // SUM_ROWS

    def body(s_ref, c_ref, p_ref, l_ref, o_ref):
        o_ref[...] = ((p_ref[...] + l_ref[0].astype(F32)) + l_ref[1].astype(F32)) + l_ref[2].astype(F32)

    return pl.pallas_call(
        body, name="add_chip_partials",
        grid_spec=pltpu.PrefetchScalarGridSpec(
            num_scalar_prefetch=2, grid=(nblk,),
            in_specs=[pl.BlockSpec((None, SUM_ROWS, ROW), lambda i, s, c: (s[0], i, 0)), pl.BlockSpec((3, SUM_ROWS, ROW), lambda i, s, c: (0, i, 0))],
            out_specs=pl.BlockSpec((SUM_ROWS, ROW), lambda i, s, c: (c[0] * nblk + i, 0))),
        out_shape=jax.ShapeDtypeStruct((2 * half, ROW), F32), compiler_params=_params(("arbitrary",)),
    )(slot, core, part, landed)


def sum_devices(stacked):
    m = stacked.shape[1]

    def body(s_ref, o_ref):
        acc = s_ref[0]
        for d in range(1, N_DEV):
            acc = acc + s_ref[d]
        o_ref[...] = acc

    return pl.pallas_call(
        body, name="sum_devices", grid=(m // 8,), in_specs=[pl.BlockSpec((N_DEV, 8, ROW), lambda i: (0, i, 0))],
        out_specs=pl.BlockSpec((8, ROW), lambda i: (i, 0)), out_shape=jax.ShapeDtypeStruct((m, ROW), F32),
        compiler_params=_params(("arbitrary",)),
    )(stacked)


def adamw(name, w, g, m, v):
    width = w.shape[-1]
    n_rows = w.size // width
    nblk = _tile(n_rows, 256, 8)
    arg = rows(width, nblk)
    outs, _ = blocked_forward("adamw_" + name, adamw_fn, 1, n_rows // nblk, [arg] * 4, [t.reshape(n_rows, width) for t in (w, g, m, v)],
                              [], [], [arg] * 3, [(n_rows, width)] * 3, (), False)
    return [o.reshape(w.shape) for o in outs]


def trunk_loss(diff, p_emb, target, wts):
    x, small, gw, hd = diff["x"], diff["small"], diff["gw"], diff["hd"]
    t = x.shape[0]
    d = D_MODEL

    def norm_pair(name, fn, h, o, gain):
        op = tok_op(name, fn, t, 512, [(d, F32), (d, F32)], [(1, d)], [(d, F32), (d, ACT)])
        return op([h, o], [gain[None]])

    def ffn(tag, n, which, i):
        z = linear(f"{which}_in{tag}", ACT, (i,), N_CHIPS)(n, wts[which + "_w_in"], hd[which + "_w_in"][i])
        act = tok_op(f"{which}_act{tag}", swiglu_fn, t, 128, [(2 * FFN_DIM, ACT)], [], [(FFN_DIM, ACT)])([z], [])[0]
        return linear(f"{which}_out{tag}", F32, (i,))(act, wts[which + "_w_out"], hd[which + "_w_out"][i])

    h = x
    n = tok_op("norm_in", norm_fn, t, 512, [(d, F32)], [(1, d)], [(d, ACT)])([x], [small["ffn1_norm"][0][None]])[0]
    loss_rows = None
    for i in range(DEPTH):
        tag = str(i)
        p = {k: v[i] for k, v in small.items() if k != "final_norm"}
        p.update({k: v[i] for k, v in gw.items()})
        o = ffn(tag, n, "ffn1", i)
        h, u = norm_pair("mix_norm" + tag, make_addnorm_fn(0.5), h, o, p["mix_norm"])
        n_in = len(IN_PIECES)
        in_proj = multi_linear("in_proj" + tag, [F32] * n_in + [ACT], [None] * n_in + [(i,)], [0] * n_in + [N_CHIPS])
        proj = in_proj(u, wts["w_in"][i] + [wts["w_gate"]], hd["in_proj"][i])
        s5_u, lru_x, lru_g, m2_z, m2_xbc, m2_dt, gdn_qkv, gdn_g, gdn_ba, gate_logits = proj
        ys = [s5_mixer(tag, t, s5_u, p), lru_mixer(tag, t, lru_x, lru_g, p),
              m2_mixer(tag, t, m2_z, m2_xbc, m2_dt, p), gdn_mixer(tag, t, gdn_qkv, gdn_g, gdn_ba, p)]
        yb = [linear(f"branch{b}_{tag}", ACT, (i, b), N_CHIPS)(y, wts["w_branch"], hd["w_branch"][i][b]) for b, y in enumerate(ys)]
        merge = tok_op("gate_merge" + tag, gate_merge_fn, t, 128, [(N_BRANCH * d, ACT)] + [(d, ACT)] * N_BRANCH,
                       [(1, N_BRANCH * d)], [(d, ACT)])
        mixed = merge([gate_logits] + yb, [p["b_gate"][None]])[0]
        o = linear("w_out" + tag, F32, (i,))(mixed, wts["w_out"], hd["w_out"][i])
        h, n = norm_pair("ffn2_norm" + tag, make_addnorm_fn(1.0), h, o, p["ffn2_norm"])
        o = ffn(tag, n, "ffn2", i)
        h, n = norm_pair("ple_norm" + tag, make_addnorm_fn(0.5), h, o, p["ple_norm"])
        pg = linear("ple_gate" + tag, F32, (i,))(n, wts["ple_w_gate"], hd["ple_w_gate"][i])
        pp = linear("ple_proj" + tag, F32, (i,), N_CHIPS)(p_emb[i], wts["ple_w_proj"], hd["ple_w_proj"][i])
        if i + 1 < DEPTH:
            op = tok_op("ple" + tag, ple_fn, t, 512, [(d, F32)] * 3, [(1, d)], [(d, F32), (d, ACT)])
            h, n = op([h, pg, pp], [small["ffn1_norm"][i + 1][None]])
        else:
            op = tok_op("ple_loss", ple_loss_fn, t, 512, [(d, F32)] * 3 + [(d, F32, False)], [(1, d)], [(1, F32)])
            loss_rows = op([h, pg, pp, target], [small["final_norm"][None]])[0]
    return jnp.sum(loss_rows)


def kernel(x, p, ffn1_norm, ffn1_w_in, ffn1_w_out, mix_norm, w_in, w_gate, b_gate, s5_log_step, s5_a_re, s5_a_im, s5_b_re, s5_b_im, s5_c_re, s5_c_im, s5_d, s5_w_glu, s5_b_glu, lru_conv_w, lru_conv_b, lru_w_r, lru_b_r, lru_w_i, lru_b_i, lru_lambda, m2_conv_w, m2_conv_b, m2_dt_bias, m2_a_log, m2_d, m2_norm, gdn_conv_w, gdn_dt_bias, gdn_a_log, gdn_norm, w_branch, w_out, ffn2_norm, ffn2_w_in, ffn2_w_out, ple_norm, ple_w_gate, ple_w_proj, final_norm, loss_target, m_ffn1_norm, m_ffn1_w_in, m_ffn1_w_out, m_mix_norm, m_w_in, m_w_gate, m_b_gate, m_s5_log_step, m_s5_a_re, m_s5_a_im, m_s5_b_re, m_s5_b_im, m_s5_c_re, m_s5_c_im, m_s5_d, m_s5_w_glu, m_s5_b_glu, m_lru_conv_w, m_lru_conv_b, m_lru_w_r, m_lru_b_r, m_lru_w_i, m_lru_b_i, m_lru_lambda, m_m2_conv_w, m_m2_conv_b, m_m2_dt_bias, m_m2_a_log, m_m2_d, m_m2_norm, m_gdn_conv_w, m_gdn_dt_bias, m_gdn_a_log, m_gdn_norm, m_w_branch, m_w_out, m_ffn2_norm, m_ffn2_w_in, m_ffn2_w_out, m_ple_norm, m_ple_w_gate, m_ple_w_proj, m_final_norm, v_ffn1_norm, v_ffn1_w_in, v_ffn1_w_out, v_mix_norm, v_w_in, v_w_gate, v_b_gate, v_s5_log_step, v_s5_a_re, v_s5_a_im, v_s5_b_re, v_s5_b_im, v_s5_c_re, v_s5_c_im, v_s5_d, v_s5_w_glu, v_s5_b_glu, v_lru_conv_w, v_lru_conv_b, v_lru_w_r, v_lru_b_r, v_lru_w_i, v_lru_b_i, v_lru_lambda, v_m2_conv_w, v_m2_conv_b, v_m2_dt_bias, v_m2_a_log, v_m2_d, v_m2_norm, v_gdn_conv_w, v_gdn_dt_bias, v_gdn_a_log, v_gdn_norm, v_w_branch, v_w_out, v_ffn2_norm, v_ffn2_w_in, v_ffn2_w_out, v_ple_norm, v_ple_w_gate, v_ple_w_proj, v_final_norm):
    a = dict(locals())
    t = x.shape[1]
    core = lax.axis_index("c").astype(jnp.int32).reshape(1)
    slot = (2 * lax.axis_index("x") + lax.axis_index("y")).astype(jnp.int32).reshape(1)

    full = unpack_gathered(gather_weights(pack_for_gather(a), slot))
    exact = [n for n, spec in SHARDED.items() if spec[2]] + ["s5_w_glu"]
    gw = {n: full[n].astype(F32) for n in exact}
    wts = {n: full[n] for n in SHARDED if n not in exact}

    def handle(n):
        shape = SHARDED[n][0]
        return jnp.zeros((N_CHIPS, *shape) if n in COL_SHARDED else (N_CHIPS * shape[0], *shape[1:]), F32)

    hd = {n: [handle(n) for _ in range(DEPTH)] for n in wts if n not in ("w_in", "w_gate", "w_branch")}
    hd["w_branch"] = [[jnp.zeros((N_CHIPS, W, D_MODEL // N_CHIPS), F32) for _ in range(N_BRANCH)] for _ in range(DEPTH)]
    hd["in_proj"] = [[jnp.zeros((D_MODEL, _round_up(n, LANE)), F32) for _, n in IN_PIECES] + [handle("w_gate")] for _ in range(DEPTH)]
    per_layer = {n: a[n] if n == "final_norm" else [a[n][layer] for layer in range(DEPTH)] for n in SMALL}
    diff = {"x": x.reshape(t, D_MODEL), "small": per_layer, "gw": gw, "hd": hd}
    loss_local, vjp = jax.vjp(lambda dd: trunk_loss(dd, p.reshape(DEPTH, t, -1), loss_target.reshape(t, D_MODEL), wts), diff)
    (grads,) = vjp(jnp.ones((), F32))
    loss = lax.psum(loss_local, ("x", "y", "c"))
    grad_x = grads["x"].reshape(x.shape)

    gh = grads["hd"]
    big = {n: jnp.concatenate([g.reshape(N_CHIPS, -1) for g in gh[n]], axis=1) for n in gh if n not in ("in_proj", "w_branch")}
    big["w_in"] = shard_w_in_grads([gh["in_proj"][i][:-1] for i in range(DEPTH)])
    big["w_gate"] = jnp.concatenate([gh["in_proj"][i][-1].reshape(N_CHIPS, -1) for i in range(DEPTH)], axis=1)
    big["w_branch"] = jnp.concatenate([g.reshape(N_CHIPS, -1) for i in range(DEPTH) for g in gh["w_branch"][i]], axis=1)
    for n in exact:
        big[n] = shard_full_grads(n, [grads["gw"][n][i] for i in range(DEPTH)])

    packed = pack_grads(big)
    pair, pair16 = add_pair_halves(packed, swap_pair_halves(packed), core)
    mine = add_chip_partials(pair, exchange_chip_partials(pair16), slot, core)
    g_shard = unpack_shard(share_halves(mine))

    shapes = {n: a[n].shape for n in SMALL}
    gs_local = pack_small({n: g if n == "final_norm" else jnp.stack(g) for n, g in grads["small"].items()}, shapes)
    gs = gather_all(gs_local)
    g_small = sum_devices(gs.reshape(N_DEV, gs_local.shape[0], ROW))

    g_all = {**unpack_small(g_small, shapes), **g_shard}
    res = {n: [g_all[n]] + adamw(n, a[n], g_all[n], a["m_" + n], a["v_" + n]) for n in WEIGHTS}
    return (loss, grad_x, *[res[n][0] for n in WEIGHTS], *[res[n][1] for n in WEIGHTS],
            *[res[n][2] for n in WEIGHTS], *[res[n][3] for n in WEIGHTS])
```

```python
import functools
import math

import jax
import jax.numpy as jnp
from jax import lax
from jax.experimental import pallas as pl
from jax.experimental.pallas import tpu as pltpu

F32, BF16 = jnp.float32, jnp.bfloat16
ACT = BF16
EPS = 1e-6
D_MODEL = 1024
DEPTH = 2
FFN_DIM = 2816
BRANCH_WIDTH = 512
N_BRANCH = 4
LRU_C = 8.0
S5_GROUPS, S5_GROUP_CH, S5_STATE = 32, 16, 64
S5_W = S5_GROUPS * S5_STATE
LRU_HEADS, LRU_HEAD_DIM = 8, 64
M2_HEADS, M2_HEAD_DIM, M2_GROUPS, M2_STATE = 8, 64, 2, 128
GDN_HEADS, GDN_HEAD_DIM = 4, 128
CHUNK = 128
ADAM_LR, ADAM_B1, ADAM_B2, ADAM_EPS, ADAM_WD, ADAM_STEP = 0.001, 0.9, 0.999, 1e-08, 0.01, 10
VMEM_LIMIT_BYTES = 56 * 1024 * 1024
MESH = pl.DeviceIdType.MESH


def _params(sem=None):
    return pltpu.CompilerParams(vmem_limit_bytes=VMEM_LIMIT_BYTES, dimension_semantics=sem)


def _dot_bf16(a, b, dims):
    return lax.dot_general(a.astype(BF16), b.astype(BF16), (dims, ((), ())), preferred_element_type=F32)


def _make_mm(dot):
    @jax.custom_vjp
    def nn(a, b):
        return dot(a, b, ((1,), (0,)))

    @jax.custom_vjp
    def nt(a, b):
        return dot(a, b, ((1,), (1,)))

    @jax.custom_vjp
    def tn(a, b):
        return dot(a, b, ((0,), (0,)))

    nn.defvjp(lambda a, b: (nn(a, b), (a, b)), lambda r, g: (nt(g, r[1]), tn(r[0], g)))
    nt.defvjp(lambda a, b: (nt(a, b), (a, b)), lambda r, g: (nn(g, r[1]), tn(g, r[0])))
    tn.defvjp(lambda a, b: (tn(a, b), (a, b)), lambda r, g: (nt(r[1], g), nn(r[0], g)))
    return nn, nt, tn


def _dot_bf16x3(a, b, dims):
    a_hi, b_hi = a.astype(BF16), b.astype(BF16)
    a_lo = (a - a_hi.astype(F32)).astype(BF16)
    b_lo = (b - b_hi.astype(F32)).astype(BF16)

    def dot(p, q):
        return lax.dot_general(p, q, (dims, ((), ())), preferred_element_type=F32)

    return dot(a_hi, b_hi) + (dot(a_hi, b_lo) + dot(a_lo, b_hi))


mm, mm_nt, mm_tn = _make_mm(_dot_bf16)
mmh, mmh_nt, mmh_tn = _make_mm(_dot_bf16x3)


def _row_ids(shape):
    return lax.broadcasted_iota(jnp.int32, shape, 0)


def _shift_down(x, d):
    return jnp.where(_row_ids(x.shape) >= d, pltpu.roll(x, d, 0), 0.0)


def _shift_up(x, d):
    n = x.shape[0]
    return jnp.where(_row_ids(x.shape) < n - d, pltpu.roll(x, n - d, 0), 0.0)


def _first_row(x):
    return jnp.sum(jnp.where(_row_ids(x.shape) == 0, x, 0.0), axis=0, keepdims=True)


def last_row(x):
    return jnp.sum(jnp.where(_row_ids(x.shape) == x.shape[0] - 1, x, 0.0), axis=0, keepdims=True)


def pick_row(x, j):
    return jnp.sum(jnp.where(_row_ids(x.shape) == j, x, 0.0), axis=0, keepdims=True)


@jax.custom_vjp
def lin_scan(a, b, h0):
    n = a.shape[0]
    row = _row_ids(a.shape)
    acc_a = a
    acc_b = b + jnp.where(row == 0, a * h0, 0.0)
    d = 1
    while d < n:
        acc_b = acc_a * _shift_down(acc_b, d) + acc_b
        acc_a = acc_a * jnp.where(row >= d, pltpu.roll(acc_a, d, 0), 1.0)
        d *= 2
    return acc_b


def _lin_scan_fwd(a, b, h0):
    h = lin_scan(a, b, h0)
    return h, (a, h, h0)


def _lin_scan_bwd(res, dh):
    a, h, h0 = res
    n = a.shape[0]
    row = _row_ids(a.shape)
    acc_a = _shift_up(a, 1)
    g = dh
    d = 1
    while d < n:
        g = acc_a * _shift_up(g, d) + g
        acc_a = acc_a * jnp.where(row < n - d, pltpu.roll(acc_a, n - d, 0), 1.0)
        d *= 2
    h_prev = _shift_down(h, 1) + jnp.where(row == 0, h0, 0.0)
    return g * h_prev, g, _first_row(a * g)


lin_scan.defvjp(_lin_scan_fwd, _lin_scan_bwd)


def _cscan(br, bi, ar, ai, up):
    n = br.shape[0]
    shift = _shift_up if up else _shift_down
    hr, hi, pr, pi = br, bi, ar, ai
    d = 1
    while d < n:
        sr, si = shift(hr, d), shift(hi, d)
        hr, hi = hr + pr * sr - pi * si, hi + pr * si + pi * sr
        pr, pi = pr * pr - pi * pi, 2.0 * pr * pi
        d *= 2
    return hr, hi


@jax.custom_vjp
def complex_scan(br, bi, ar, ai, h0r, h0i):
    first = _row_ids(br.shape) == 0
    br = br + jnp.where(first, ar * h0r - ai * h0i, 0.0)
    bi = bi + jnp.where(first, ar * h0i + ai * h0r, 0.0)
    return _cscan(br, bi, ar, ai, False)


def _complex_scan_fwd(br, bi, ar, ai, h0r, h0i):
    hr, hi = complex_scan(br, bi, ar, ai, h0r, h0i)
    return (hr, hi), (ar, ai, hr, hi, h0r, h0i)


def _complex_scan_bwd(res, cts):
    ar, ai, hr, hi, h0r, h0i = res
    gr, gi = _cscan(cts[0], cts[1], ar, -ai, True)
    first = _row_ids(hr.shape) == 0
    pr = _shift_down(hr, 1) + jnp.where(first, h0r, 0.0)
    pi = _shift_down(hi, 1) + jnp.where(first, h0i, 0.0)
    d_ar = jnp.sum(gr * pr + gi * pi, axis=0, keepdims=True)
    d_ai = jnp.sum(gi * pr - gr * pi, axis=0, keepdims=True)
    g0r, g0i = _first_row(gr), _first_row(gi)
    return gr, gi, d_ar, d_ai, ar * g0r + ai * g0i, ar * g0i - ai * g0r


complex_scan.defvjp(_complex_scan_fwd, _complex_scan_bwd)

TAIL = 8


@jax.custom_vjp
def tail_rows(x):
    return x[x.shape[0] - TAIL:, :]


tail_rows.defvjp(
    lambda x: (tail_rows(x), x.shape[0]),
    lambda n, g: (jnp.concatenate([jnp.zeros((n - TAIL, g.shape[1]), g.dtype), g], axis=0),),
)


def _make_shift_tail(d):
    @jax.custom_vjp
    def shifted(x, tail):
        n = x.shape[0]
        tpad = jnp.concatenate([tail, jnp.zeros((n - TAIL, x.shape[1]), x.dtype)], axis=0)
        return jnp.where(_row_ids(x.shape) >= d, pltpu.roll(x, d, 0), pltpu.roll(tpad, n + d - TAIL, 0))

    def fwd(x, tail):
        return shifted(x, tail), None

    def bwd(_, g):
        g8 = g[:TAIL, :]
        dtail = jnp.where(_row_ids(g8.shape) >= TAIL - d, pltpu.roll(g8, TAIL - d, 0), 0.0)
        return _shift_up(g, d), dtail

    shifted.defvjp(fwd, bwd)
    return shifted


_SHIFT_TAIL = {d: _make_shift_tail(d) for d in (1, 2, 3)}


def causal_conv4(x, tail, w):
    y = pick_row(w, 3) * x
    for j in range(3):
        y = y + pick_row(w, j) * _SHIFT_TAIL[3 - j](x, tail)
    return y


def rmsnorm(x, g):
    return x * lax.rsqrt(jnp.mean(x * x, axis=-1, keepdims=True) + EPS) * g


def to_row(col):
    n = col.shape[0]
    eye = lax.broadcasted_iota(jnp.int32, (n, n), 0) == lax.broadcasted_iota(jnp.int32, (n, n), 1)
    return jnp.sum(jnp.where(eye, col, 0.0), axis=0, keepdims=True)


def causal_decay(a_col):
    n = a_col.shape[0]
    causal = lax.broadcasted_iota(jnp.int32, (n, n), 0) >= lax.broadcasted_iota(jnp.int32, (n, n), 1)
    cs = jnp.sum(jnp.where(causal, to_row(a_col), 0.0), axis=1, keepdims=True)
    diff = cs - to_row(cs)
    return cs, jnp.where(causal, jnp.exp(jnp.where(causal, diff, 0.0)), 0.0)


class Arg:
    def __init__(self, block, imap, dtype=F32, grad=True, shared=True):
        self.block, self.imap, self.dtype, self.grad, self.shared = block, imap, dtype, grad, shared


def rows(width, nblk, col=lambda g: 0, dtype=F32, grad=True):
    return Arg((nblk, width), lambda g, c: (c, col(g)), dtype, grad)


def head_rows(width, nblk, head=lambda g: g, dtype=F32, grad=True):
    return Arg((None, nblk, width), lambda g, c: (head(g), c, 0), dtype, grad)


def whole(shape, grad=True):
    return Arg(tuple(shape), lambda g, c: (0,) * len(shape), F32, grad, shared=True)


def per_group(shape, idx=lambda g: g, grad=True):
    return Arg((None,) + tuple(shape), lambda g, c: (idx(g),) + (0,) * len(shape), F32, grad, shared=False)


def _bshape(block):
    return tuple(b for b in block if b is not None)


def _spec(arg, nb=None):
    if nb is None:
        return pl.BlockSpec(arg.block, arg.imap)
    return pl.BlockSpec(arg.block, lambda g, c: arg.imap(g, nb - 1 - c))


def blocked_forward(name, fn, groups, nb, tok_args, tok, const_args, consts, out_args, out_shapes, state_shapes, save,
                    reverse=False):
    n_tok, n_const, n_out, n_state = len(tok), len(consts), len(out_args), len(state_shapes)
    walk = nb if reverse else None

    def body(*refs):
        tok_refs = refs[:n_tok]
        const_refs = refs[n_tok:n_tok + n_const]
        out_refs = refs[n_tok + n_const:n_tok + n_const + n_out]
        pos = n_tok + n_const + n_out
        save_refs = refs[pos:pos + (n_state if save else 0)]
        state_refs = refs[len(refs) - n_state:] if n_state else ()

        @pl.when(pl.program_id(1) == 0)
        def _():
            for s in state_refs:
                s[...] = jnp.zeros_like(s)

        states = [s[...] for s in state_refs]
        for sr, s in zip(save_refs, states):
            sr[...] = s
        new_states, outs = fn(states, [r[...].astype(F32) for r in tok_refs], [r[...] for r in const_refs])
        for o_ref, o in zip(out_refs, outs):
            o_ref[...] = o.astype(o_ref.dtype)
        for s_ref, s in zip(state_refs, new_states):
            s_ref[...] = s

    out_specs = [_spec(a, walk) for a in out_args]
    out_shape = [jax.ShapeDtypeStruct(s, a.dtype) for s, a in zip(out_shapes, out_args)]
    if save:
        assert not reverse
        for s in state_shapes:
            out_specs.append(pl.BlockSpec((None, None) + tuple(s), lambda g, c, k=len(s): (g, c) + (0,) * k))
            out_shape.append(jax.ShapeDtypeStruct((groups, nb) + tuple(s), F32))
    res = pl.pallas_call(
        body, name=name, grid=(groups, nb),
        in_specs=[_spec(a, walk) for a in tok_args] + [_spec(a, walk) for a in const_args],
        out_specs=out_specs, out_shape=out_shape,
        scratch_shapes=[pltpu.VMEM(tuple(s), F32) for s in state_shapes],
        compiler_params=_params(("arbitrary", "arbitrary")),
    )(*tok, *consts)
    return list(res[:n_out]), list(res[n_out:])


def blocked_backward(name, fn, groups, nb, tok_args, tok, const_args, consts, out_args, cts, state_shapes, saved):
    n_tok, n_const, n_out, n_state = len(tok), len(consts), len(out_args), len(state_shapes)
    tok_g = [i for i, a in enumerate(tok_args) if a.grad]
    const_g = [i for i, a in enumerate(const_args) if a.grad]

    def body(*refs):
        tok_refs = refs[:n_tok]
        const_refs = refs[n_tok:n_tok + n_const]
        pos = n_tok + n_const
        saved_refs = refs[pos:pos + n_state]
        ct_refs = refs[pos + n_state:pos + n_state + n_out]
        pos += n_state + n_out
        dtok_refs = refs[pos:pos + len(tok_g)]
        dconst_refs = refs[pos + len(tok_g):pos + len(tok_g) + len(const_g)]
        dstate_refs = refs[len(refs) - n_state:] if n_state else ()
        g_id, c_id = pl.program_id(0), pl.program_id(1)

        @pl.when(c_id == 0)
        def _():
            for s in dstate_refs:
                s[...] = jnp.zeros_like(s)
            for r, i in zip(dconst_refs, const_g):
                if not const_args[i].shared:
                    r[...] = jnp.zeros_like(r)

        @pl.when((c_id == 0) & (g_id == 0))
        def _():
            for r, i in zip(dconst_refs, const_g):
                if const_args[i].shared:
                    r[...] = jnp.zeros_like(r)

        tok_vals = [r[...].astype(F32) for r in tok_refs]
        const_vals = [r[...] for r in const_refs]

        def f(states, tok_d, const_d):
            tv, cv = list(tok_vals), list(const_vals)
            for i, v in zip(tok_g, tok_d):
                tv[i] = v
            for i, v in zip(const_g, const_d):
                cv[i] = v
            return fn(states, tv, cv)

        _, vjp = jax.vjp(f, [r[...] for r in saved_refs], [tok_vals[i] for i in tok_g], [const_vals[i] for i in const_g])
        dstates, dtok, dconst = vjp(([r[...] for r in dstate_refs], [r[...].astype(F32) for r in ct_refs]))
        for r, v in zip(dtok_refs, dtok):
            r[...] = v.astype(r.dtype)
        for r, v in zip(dconst_refs, dconst):
            r[...] += v
        for r, v in zip(dstate_refs, dstates):
            r[...] = v

    in_specs = [_spec(a, nb) for a in tok_args] + [_spec(a, nb) for a in const_args]
    for s in state_shapes:
        in_specs.append(pl.BlockSpec((None, None) + tuple(s), lambda g, c, k=len(s): (g, nb - 1 - c) + (0,) * k))
    in_specs += [_spec(a, nb) for a in out_args]
    out_specs = [_spec(tok_args[i], nb) for i in tok_g] + [_spec(const_args[i], nb) for i in const_g]
    out_shape = [jax.ShapeDtypeStruct(tok[i].shape, tok[i].dtype) for i in tok_g]
    out_shape += [jax.ShapeDtypeStruct(consts[i].shape, F32) for i in const_g]
    res = pl.pallas_call(
        body, name=name, grid=(groups, nb), in_specs=in_specs, out_specs=out_specs, out_shape=out_shape,
        scratch_shapes=[pltpu.VMEM(tuple(s), F32) for s in state_shapes],
        compiler_params=_params(("arbitrary", "arbitrary")),
    )(*tok, *consts, *saved, *cts)
    dtok = [None] * n_tok
    dconst = [None] * n_const
    for i, v in zip(tok_g, res[:len(tok_g)]):
        dtok[i] = v
    for i, v in zip(const_g, res[len(tok_g):]):
        dconst[i] = v
    return dtok, dconst


def blocked_op(name, fn, groups, nb, tok_args, const_args, out_args, out_shapes, state_shapes=()):
    state_shapes = tuple(state_shapes)

    @jax.custom_vjp
    def op(tok, consts):
        outs, _ = blocked_forward(name, fn, groups, nb, tok_args, tok, const_args, consts, out_args, out_shapes, state_shapes, False)
        return outs

    def fwd(tok, consts):
        outs, saved = blocked_forward(name, fn, groups, nb, tok_args, tok, const_args, consts, out_args, out_shapes, state_shapes, True)
        return outs, (tok, consts, saved)

    def bwd(res, cts):
        tok, consts, saved = res
        dtok, dconst = blocked_backward(name + "_bwd", fn, groups, nb, tok_args, tok, const_args, consts, out_args, list(cts), state_shapes, saved)
        dtok = [jnp.zeros_like(t) if d is None else d for t, d in zip(tok, dtok)]
        dconst = [jnp.zeros_like(k) if d is None else d for k, d in zip(consts, dconst)]
        return dtok, dconst

    op.defvjp(fwd, bwd)
    return op


def _make_split(sizes):
    offs = [sum(sizes[:i]) for i in range(len(sizes))]

    @jax.custom_vjp
    def split(x):
        return tuple(x[:, o:o + s] for o, s in zip(offs, sizes))

    split.defvjp(lambda x: (split(x), None), lambda _, g: (jnp.concatenate(list(g), axis=1),))
    return split


def _make_join(sizes):
    offs = [sum(sizes[:i]) for i in range(len(sizes))]

    @jax.custom_vjp
    def join(parts):
        return jnp.concatenate(list(parts), axis=1)

    join.defvjp(lambda parts: (join(parts), None), lambda _, g: (tuple(g[:, o:o + s] for o, s in zip(offs, sizes)),))
    return join


def split_cols(x, sizes):
    return _make_split(tuple(sizes))(x)


def join_cols(parts):
    return _make_join(tuple(p.shape[1] for p in parts))(tuple(parts))


def lane_scalar(row, j):
    lane = lax.broadcasted_iota(jnp.int32, row.shape, 1)
    return jnp.sum(jnp.where(lane == j, row, 0.0), axis=1, keepdims=True)


def lane_col(blk, j):
    lane = lax.broadcasted_iota(jnp.int32, blk.shape, 1)
    return jnp.sum(jnp.where(lane == j, blk, 0.0), axis=1, keepdims=True)


LANE = 128
MM_ROWS = 512
MM_TILE_M, MM_TILE_N = 1024, 1536
MM_TILE_MT = 1408


def _tile(n, cap, unit):
    if n <= cap:
        return n
    best = None
    for t in range(unit, cap + 1, unit):
        if n % t == 0:
            best = t
    assert best is not None, (n, cap, unit)
    return best


def _matmul_resident(a, b, mode, add, out_dtype, name, lead, shards):
    m, k = a.shape
    rows_b, cols_b = b.shape[-2:]
    n = cols_b * max(shards, 1) if mode == "nn" else rows_b
    tm = _tile(m, MM_ROWS, 8)
    nolead = (None,) * len(lead)
    if shards:
        w_spec = pl.BlockSpec((shards,) + nolead + (rows_b, cols_b), lambda i: (0,) + lead + (0, 0))
    else:
        w_spec = pl.BlockSpec(nolead + (rows_b, cols_b), lambda i: lead + (0, 0))
    has_add = add is not None
    dims = ((1,), (0,)) if mode == "nn" else ((1,), (1,))

    def body(*refs):
        a_ref, w_ref, o_ref = refs[0], refs[1], refs[-1]
        if shards and mode == "nn":
            lhs = a_ref[...].astype(BF16)
            for s in range(shards):
                cols = slice(s * cols_b, (s + 1) * cols_b)
                part = _dot_bf16(lhs, w_ref[s], dims)
                if has_add:
                    part = part + refs[2][:, cols].astype(F32)
                o_ref[:, cols] = part.astype(o_ref.dtype)
            return
        if shards:
            acc = _dot_bf16(a_ref[:, 0:cols_b], w_ref[0], dims)
            for s in range(1, shards):
                acc = acc + _dot_bf16(a_ref[:, s * cols_b:(s + 1) * cols_b], w_ref[s], dims)
        else:
            acc = _dot_bf16(a_ref[...], w_ref[...], dims)
        if has_add:
            acc = acc + refs[2][...].astype(F32)
        o_ref[...] = acc.astype(o_ref.dtype)

    o_spec = pl.BlockSpec((tm, n), lambda i: (i, 0))
    return pl.pallas_call(
        body, name=name, grid=(m // tm,),
        in_specs=[pl.BlockSpec((tm, k), lambda i: (i, 0)), w_spec] + ([o_spec] if has_add else []),
        out_specs=o_spec, out_shape=jax.ShapeDtypeStruct((m, n), out_dtype), compiler_params=_params(("parallel",)),
    )(*([a, b] + ([add] if has_add else [])))


def matmul(a, b, mode="nn", add=None, out_dtype=F32, name="matmul", pre=None, col_shards=0):
    if mode != "tn":
        return _matmul_resident(a, b, mode, add, out_dtype, name, () if pre is None else tuple(pre), col_shards)
    assert add is None and pre is None
    (k, m), n = a.shape, b.shape[1]
    shard_n = n // col_shards if col_shards else n
    tm, tn, tk = _tile(m, MM_TILE_MT, LANE), _tile(shard_n, MM_TILE_N, LANE), _tile(k, MM_TILE_M, LANE)
    nk, qn = k // tk, shard_n // tn

    def body(a_ref, b_ref, o_ref, acc_ref):
        l = pl.program_id(2)

        @pl.when(l == 0)
        def _():
            acc_ref[...] = jnp.zeros_like(acc_ref)

        acc_ref[...] += _dot_bf16(a_ref[...], b_ref[...], ((0,), (0,)))

        @pl.when(l == nk - 1)
        def _():
            o_ref[...] = acc_ref[...].astype(o_ref.dtype)

    if col_shards:
        o_spec = pl.BlockSpec((None, tm, tn), lambda i, j, l: (j // qn, i, j % qn))
        out_shape = jax.ShapeDtypeStruct((col_shards, m, shard_n), out_dtype)
    else:
        o_spec = pl.BlockSpec((tm, tn), lambda i, j, l: (i, j))
        out_shape = jax.ShapeDtypeStruct((m, n), out_dtype)
    return pl.pallas_call(
        body, name=name, grid=(m // tm, n // tn, nk),
        in_specs=[pl.BlockSpec((tk, tm), lambda i, j, l: (l, i)), pl.BlockSpec((tk, tn), lambda i, j, l: (l, j))],
        out_specs=o_spec, out_shape=out_shape, scratch_shapes=[pltpu.VMEM((tm, tn), F32)],
        compiler_params=_params(("parallel", "parallel", "arbitrary")),
    )(a, b)


def matmul_sum_nt(gs, ws, out_dtype, name):
    m, count = gs[0].shape[0], len(gs)
    tm = _tile(m, MM_ROWS // 2, 8)

    def body(*refs):
        acc = _dot_bf16(refs[0][...], refs[count][...], ((1,), (1,)))
        for i in range(1, count):
            acc = acc + _dot_bf16(refs[i][...], refs[count + i][...], ((1,), (1,)))
        refs[-1][...] = acc.astype(refs[-1].dtype)

    width = ws[0].shape[0]
    return pl.pallas_call(
        body, name=name, grid=(m // tm,),
        in_specs=[pl.BlockSpec((tm, g.shape[1]), lambda i: (i, 0)) for g in gs] + [pl.BlockSpec(w.shape, lambda i: (0, 0)) for w in ws],
        out_specs=pl.BlockSpec((tm, width), lambda i: (i, 0)), out_shape=jax.ShapeDtypeStruct((m, width), out_dtype),
        compiler_params=_params(("parallel",)),
    )(*gs, *ws)


def matmul_multi_nn(a, ws, out_dtypes, name):
    m, k = a.shape
    tm = _tile(m, MM_ROWS // 2, 8)
    count = len(ws)

    def body(*refs):
        lhs = refs[0][...].astype(BF16)
        for i in range(count):
            out = refs[1 + count + i]
            out[...] = _dot_bf16(lhs, refs[1 + i][...], ((1,), (0,))).astype(out.dtype)

    return pl.pallas_call(
        body, name=name, grid=(m // tm,),
        in_specs=[pl.BlockSpec((tm, k), lambda i: (i, 0))] + [pl.BlockSpec(w.shape, lambda i: (0, 0)) for w in ws],
        out_specs=[pl.BlockSpec((tm, w.shape[1]), lambda i: (i, 0)) for w in ws],
        out_shape=[jax.ShapeDtypeStruct((m, w.shape[1]), dt) for w, dt in zip(ws, out_dtypes)],
        compiler_params=_params(("parallel",)),
    )(a, *ws)


def linear(name, out_dtype=F32, pre=None, col_shards=0):
    @jax.custom_vjp
    def op(a, w, handle):
        return matmul(a, w, "nn", out_dtype=out_dtype, name=name, pre=pre, col_shards=col_shards)

    def fwd(a, w, handle):
        return op(a, w, handle), (a, w)

    def bwd(res, g):
        a, w = res
        da = matmul(g, w, "nt", out_dtype=a.dtype, name=name + "_da", pre=pre, col_shards=col_shards)
        dw = matmul(a, g, "tn", out_dtype=F32, name=name + "_dw", col_shards=col_shards)
        return da, jnp.zeros_like(w), dw

    op.defvjp(fwd, bwd)
    return op


def multi_linear(name, out_dtypes, pres, shards):
    sel = [dict(pre=p, col_shards=s) for p, s in zip(pres, shards)]

    plain = [i for i in range(len(pres)) if pres[i] is None]
    rest = [i for i in range(len(pres)) if pres[i] is not None]

    @jax.custom_vjp
    def op(a, ws, handles):
        outs = dict(zip(plain, matmul_multi_nn(a, [ws[i] for i in plain], [out_dtypes[i] for i in plain], name)))
        for i in rest:
            outs[i] = matmul(a, ws[i], "nn", out_dtype=out_dtypes[i], name=f"{name}{i}", **sel[i])
        return [outs[i] for i in range(len(ws))]

    def fwd(a, ws, handles):
        return op(a, ws, handles), (a, ws)

    def bwd(res, gs):
        a, ws = res
        acc = matmul_sum_nt([gs[i] for i in plain], [ws[i] for i in plain], a.dtype if not rest else F32, name + "_da")
        for i in rest:
            acc = matmul(gs[i], ws[i], "nt", add=acc, out_dtype=a.dtype if i == rest[-1] else F32, name=f"{name}{i}_da", **sel[i])
        dws = [matmul(a, g, "tn", out_dtype=F32, name=f"{name}{i}_dw", col_shards=shards[i]) for i, g in enumerate(gs)]
        return acc, [jnp.zeros_like(w) for w in ws], dws

    op.defvjp(fwd, bwd)
    return op


def dense(name, out_dtype=F32):
    @jax.custom_vjp
    def op(a, w):
        return matmul(a, w, "nn", out_dtype=out_dtype, name=name)

    def fwd(a, w):
        return op(a, w), (a, w)

    def bwd(res, g):
        a, w = res
        return (matmul(g, w, "nt", out_dtype=a.dtype, name=name + "_da"),
                matmul(a, g, "tn", out_dtype=w.dtype, name=name + "_dw"))

    op.defvjp(fwd, bwd)
    return op


def to_col(row):
    n = row.shape[1]
    eye = lax.broadcasted_iota(jnp.int32, (n, n), 0) == lax.broadcasted_iota(jnp.int32, (n, n), 1)
    return jnp.sum(jnp.where(eye, row, 0.0), axis=1, keepdims=True)


def norm_fn(states, toks, consts):
    return [], [rmsnorm(toks[0], consts[0])]


def make_addnorm_fn(scale):
    def fn(states, toks, consts):
        h = toks[0] + scale * toks[1]
        return [], [h, rmsnorm(h, consts[0])]

    return fn


def swiglu_fn(states, toks, consts):
    gate, up = split_cols(toks[0], (FFN_DIM, FFN_DIM))
    return [], [jax.nn.silu(gate) * up]


def gate_merge_fn(states, toks, consts):
    gates = split_cols(jax.nn.sigmoid(toks[0] + consts[0]), (D_MODEL,) * N_BRANCH)
    mixed = gates[0] * toks[1]
    for n in range(1, N_BRANCH):
        mixed = mixed + gates[n] * toks[1 + n]
    return [], [mixed]


def ple_fn(states, toks, consts):
    h = toks[0] + jax.nn.sigmoid(toks[1]) * toks[2]
    return [], [h, rmsnorm(h, consts[0])]


def ple_loss_fn(states, toks, consts):
    h = toks[0] + jax.nn.sigmoid(toks[1]) * toks[2]
    err = rmsnorm(h, consts[0]) - toks[3]
    return [], [0.5 * jnp.mean(err * err, axis=-1, keepdims=True)]


def s5_discretise_fn(states, toks, consts):
    log_step, a_re, a_im, b_re, b_im = consts
    step = jnp.exp(log_step)
    mag = jnp.exp(a_re * step)
    ab_re, ab_im = mag * jnp.cos(a_im * step), mag * jnp.sin(a_im * step)
    den = a_re * a_re + a_im * a_im
    num_re = ab_re - 1.0
    f_re = (num_re * a_re + ab_im * a_im) / den
    f_im = (ab_im * a_re - num_re * a_im) / den
    return [], [ab_re, ab_im, f_re * b_re - f_im * b_im, f_re * b_im + f_im * b_re]


def s5_scan_fn(states, toks, consts):
    b_re, b_im = split_cols(toks[0], (S5_W, S5_W))
    h_re, h_im = complex_scan(b_re, b_im, consts[0], consts[1], states[0], states[1])
    return [last_row(h_re), last_row(h_im)], [join_cols([h_re, h_im])]


def s5_scan_bwd_fn(states, toks, consts):
    g_re, g_im, acc_re, acc_im = states
    h, ct, h0_re, h0_im = toks
    a_re, a_im = consts
    h_re, h_im = split_cols(h, (S5_W, S5_W))
    c_re, c_im = split_cols(ct, (S5_W, S5_W))
    last = _row_ids(c_re.shape) == c_re.shape[0] - 1
    cts = (c_re + jnp.where(last, g_re, 0.0), c_im + jnp.where(last, g_im, 0.0))
    d_re, d_im, da_re, da_im, d0_re, d0_im = _complex_scan_bwd((a_re, a_im, h_re, h_im, h0_re, h0_im), cts)
    acc_re, acc_im = acc_re + da_re, acc_im + da_im
    return [d0_re, d0_im, acc_re, acc_im], [join_cols([d_re, d_im]), acc_re, acc_im]


def s5_scan(name, t, bu, a_re, a_im):
    nblk = min(SCAN_ROWS, t)
    nb = t // nblk
    wide, row = rows(2 * S5_W, nblk, dtype=ACT), whole((1, S5_W))
    entry = Arg((None, None, 1, S5_W), lambda g, c: (0, c, 0, 0))
    state = [(1, S5_W)] * 2

    def run(bu, a_re, a_im, save):
        return blocked_forward(name, s5_scan_fn, 1, nb, [wide], [bu], [row, row], [a_re, a_im], [wide], [(t, 2 * S5_W)], state, save)

    @jax.custom_vjp
    def op(bu, a_re, a_im):
        return run(bu, a_re, a_im, False)[0][0]

    def fwd(bu, a_re, a_im):
        outs, saved = run(bu, a_re, a_im, True)
        return outs[0], (outs[0], saved, a_re, a_im)

    def bwd(res, ct):
        h, saved, a_re, a_im = res
        outs, _ = blocked_forward(name + "_bwd", s5_scan_bwd_fn, 1, nb, [wide, wide, entry, entry], [h, ct] + saved, [row, row],
                                  [a_re, a_im], [wide, row, row], [(t, 2 * S5_W), (1, S5_W), (1, S5_W)], state * 2, False, reverse=True)
        return tuple(outs)

    op.defvjp(fwd, bwd)
    return op(bu, a_re, a_im)


def s5_glu_fn(states, toks, consts):
    d_skip, w_glu, b_glu = consts
    z = jax.nn.gelu(toks[0] + d_skip * toks[1])
    return [], [z * jax.nn.sigmoid(mm(z, w_glu) + b_glu)]


def lru_fn(states, toks, consts):
    h0, tail = states
    x, gate = toks
    conv_w, conv_b, w_r, b_r, w_i, b_i, lam = consts
    xc = causal_conv4(x, tail, conv_w) + conv_b
    r = jax.nn.sigmoid(mm(xc, w_r) + b_r)
    i_g = jax.nn.sigmoid(mm(xc, w_i) + b_i)
    log_a = -LRU_C * r * jax.nn.softplus(-lam)
    inp = jnp.sqrt(1.0 - jnp.exp(2.0 * log_a)) * (i_g * xc)
    h = lin_scan(jnp.exp(log_a), inp, h0)
    return [last_row(h), tail_rows(x)], [h * jax.nn.gelu(gate)]


def m2_conv_fn(states, toks, consts):
    y = jax.nn.silu(causal_conv4(toks[0], states[0], consts[0]) + consts[1])
    return [tail_rows(toks[0])], list(split_cols(y, (BRANCH_WIDTH, M2_GROUPS * M2_STATE, M2_GROUPS * M2_STATE)))


def gdn_conv_fn(states, toks, consts):
    y = jax.nn.silu(causal_conv4(toks[0], states[0], consts[0]))
    return [tail_rows(toks[0])], list(split_cols(y, (BRANCH_WIDTH,) * 3))


def ssd_fn(states, toks, consts):
    xs, bm, cm, small = toks
    dt_bias, a_log, d_skip = consts
    x_pairs = split_cols(xs, (LANE,) * 4)
    b_g = split_cols(bm, (M2_STATE,) * M2_GROUPS)
    c_g = split_cols(cm, (M2_STATE,) * M2_GROUPS)
    lo = lax.broadcasted_iota(jnp.int32, (1, LANE), 1) < M2_HEAD_DIM
    new_states, y_pairs = [], []
    for g in range(M2_GROUPS):
        scores = mm_nt(c_g[g], b_g[g])
        y_off = split_cols(mm_nt(c_g[g], states[g]), (LANE, LANE))
        to_end, ends = [], []
        for j in range(2):
            pair = 2 * g + j
            x2 = x_pairs[pair]
            dts, css, decays, end = [], [], [], []
            for h in (2 * pair, 2 * pair + 1):
                dt = jax.nn.softplus(lane_col(small, h) + lane_scalar(dt_bias, h))
                a = dt * (-jnp.exp(lane_scalar(a_log, h)))
                cs, decay = causal_decay(a)
                dts.append(dt)
                css.append(cs)
                decays.append(decay)
                end.append(jnp.sum(a, axis=0, keepdims=True))
            xdt = x2 * jnp.where(lo, dts[0], dts[1])
            y = mm(scores * decays[0], jnp.where(lo, xdt, 0.0)) + mm(scores * decays[1], jnp.where(lo, 0.0, xdt))
            cs2 = jnp.where(lo, css[0], css[1])
            end2 = jnp.where(lo, end[0], end[1])
            y = y + y_off[j] * jnp.exp(cs2)
            y = y + jnp.where(lo, lane_scalar(d_skip, 2 * pair), lane_scalar(d_skip, 2 * pair + 1)) * x2
            y_pairs.append(y)
            to_end.append(xdt * jnp.exp(end2 - cs2))
            ends.append(end2)
        chunk_decay = jnp.exp(to_col(join_cols(ends)))
        new_states.append(states[g] * chunk_decay + mm_tn(join_cols(to_end), b_g[g]))
    return new_states, [join_cols(y_pairs)]


def m2_post_fn(states, toks, consts):
    return [], [rmsnorm(toks[0] * jax.nn.silu(toks[1]), consts[0])]


@jax.custom_vjp
def nilpotent_inverses(mats):
    size = mats[0].shape[0]
    eye = lax.broadcasted_iota(jnp.int32, mats[0].shape, 0) == lax.broadcasted_iota(jnp.int32, mats[0].shape, 1)
    invs = [jnp.where(eye, 1.0, 0.0) + m for m in mats]
    powers = list(mats)
    d = 2
    while d < size:
        powers = [_dot_bf16x3(p, p, ((1,), (0,))) for p in powers]
        invs = [i + _dot_bf16x3(i, p, ((1,), (0,))) for i, p in zip(invs, powers)]
        d *= 2
    return tuple(invs)


def _nilpotent_inverses_fwd(mats):
    invs = nilpotent_inverses(mats)
    return invs, invs


def _nilpotent_inverses_bwd(invs, gs):
    right = [_dot_bf16x3(g, i, ((1,), (1,))) for g, i in zip(gs, invs)]
    return (tuple(_dot_bf16x3(i, r, ((0,), (0,))) for i, r in zip(invs, right)),)


nilpotent_inverses.defvjp(_nilpotent_inverses_fwd, _nilpotent_inverses_bwd)


def gdn_fn(states, toks, consts):
    q, k, v, gate, small = toks
    dt_bias, a_log, norm_g = consts
    hs = range(GDN_HEADS)
    heads = (GDN_HEAD_DIM,) * GDN_HEADS
    qs, ks, vs, gs = split_cols(q, heads), split_cols(k, heads), split_cols(v, heads), split_cols(gate, heads)
    n = q.shape[0]
    strict = lax.broadcasted_iota(jnp.int32, (n, n), 0) > lax.broadcasted_iota(jnp.int32, (n, n), 1)
    qn = [qs[h] * lax.rsqrt(jnp.sum(qs[h] * qs[h], axis=-1, keepdims=True) + EPS) * (GDN_HEAD_DIM ** -0.5) for h in hs]
    kn = [ks[h] * lax.rsqrt(jnp.sum(ks[h] * ks[h], axis=-1, keepdims=True) + EPS) for h in hs]
    beta = [jax.nn.sigmoid(lane_col(small, h)) for h in hs]
    g = [-jnp.exp(lane_scalar(a_log, h)) * jax.nn.softplus(lane_col(small, GDN_HEADS + h) + lane_scalar(dt_bias, h)) for h in hs]
    cs_decay = [causal_decay(g[h]) for h in hs]
    cs, decay = [c for c, _ in cs_decay], [d for _, d in cs_decay]
    kb = [kn[h] * beta[h] for h in hs]
    inv = nilpotent_inverses(tuple(-jnp.where(strict, mm_nt(kb[h], kn[h]) * decay[h], 0.0) for h in hs))
    ecs = [jnp.exp(cs[h]) for h in hs]
    u = [mmh(inv[h], vs[h] * beta[h]) for h in hs]
    w = [mmh(inv[h], kb[h] * ecs[h]) for h in hs]
    qk = [mm_nt(qn[h], kn[h]) * decay[h] for h in hs]
    cs_end = [jnp.sum(g[h], axis=0, keepdims=True) for h in hs]
    v_new = [u[h] - mm(w[h], states[h]) for h in hs]
    o = [mm(qn[h] * ecs[h], states[h]) + mm(qk[h], v_new[h]) for h in hs]
    new_states = [states[h] * jnp.exp(cs_end[h]) + mm_tn(kn[h] * jnp.exp(cs_end[h] - cs[h]), v_new[h]) for h in hs]
    return new_states, [join_cols([rmsnorm(o[h], norm_g) * jax.nn.silu(gs[h]) for h in hs])]


def adamw_fn(states, toks, consts):
    w, g, m, v = toks
    m = ADAM_B1 * m + (1.0 - ADAM_B1) * g
    v = ADAM_B2 * v + (1.0 - ADAM_B2) * (g * g)
    m_hat = m / (1.0 - ADAM_B1 ** ADAM_STEP)
    v_hat = v / (1.0 - ADAM_B2 ** ADAM_STEP)
    return [], [-ADAM_LR * (m_hat / (jnp.sqrt(v_hat) + ADAM_EPS) + ADAM_WD * w), m, v]


def tok_op(name, fn, t, nblk, tok, consts, outs, states=()):
    nblk = min(nblk, t)
    tok_args = [rows(e[0], nblk, dtype=e[1], grad=e[2] if len(e) > 2 else True) for e in tok]
    const_args = [whole(s) for s in consts]
    out_args = [rows(w, nblk, dtype=dt) for (w, dt) in outs]
    return blocked_op(name, fn, 1, t // nblk, tok_args, const_args, out_args, [(t, w) for (w, _) in outs], states)


def const_op(name, fn, in_shapes, out_shapes):
    return blocked_op(name, fn, 1, 1, [], [whole(s) for s in in_shapes], [whole(s) for s in out_shapes], list(out_shapes))


W = BRANCH_WIDTH
SCAN_ROWS = 128
ROW_BLOCK = 512


def s5_mixer(tag, t, u, p):
    col = (S5_W, 1)
    disc = const_op("s5_disc" + tag, s5_discretise_fn, [col, col, col, (S5_W, 16), (S5_W, 16)], [col, col, (S5_W, 16), (S5_W, 16)])
    ab_re, ab_im, bb_re, bb_im = disc([], [
        jnp.repeat(p["s5_log_step"], S5_STATE).reshape(col), p["s5_a_re"].reshape(col), p["s5_a_im"].reshape(col),
        p["s5_b_re"].reshape(S5_W, S5_GROUP_CH), p["s5_b_im"].reshape(S5_W, S5_GROUP_CH)])
    eye = jnp.eye(S5_GROUPS, dtype=F32)

    def block_in(bb):
        return jnp.einsum("gpc,gh->gchp", bb.reshape(S5_GROUPS, S5_STATE, S5_GROUP_CH), eye).reshape(W, S5_W)

    def block_out(c):
        return jnp.einsum("gcp,gh->gphc", c, eye).reshape(S5_W, W)

    w_b = jnp.concatenate([block_in(bb_re), block_in(bb_im)], axis=1)
    w_c = jnp.concatenate([block_out(p["s5_c_re"]), -block_out(p["s5_c_im"])], axis=0)
    bu = dense("s5_b" + tag, ACT)(u, w_b)
    h = s5_scan("s5_scan" + tag, t, bu, ab_re.reshape(1, S5_W), ab_im.reshape(1, S5_W))
    yc = dense("s5_c" + tag)(h, w_c)
    glu = tok_op("s5_glu" + tag, s5_glu_fn, t, ROW_BLOCK, [(W, F32), (W, F32)], [(1, W), (W, W), (1, W)], [(W, ACT)])
    return glu([yc, u], [p["s5_d"].reshape(1, W), p["s5_w_glu"], p["s5_b_glu"][None]])[0]


def lru_mixer(tag, t, x, gate, p):
    def block_diag(w):
        return jnp.einsum("hij,hk->hikj", w, jnp.eye(LRU_HEADS, dtype=F32)).reshape(W, W)

    op = tok_op("lru" + tag, lru_fn, t, SCAN_ROWS, [(W, F32), (W, F32)],
                [(4, W), (1, W), (W, W), (1, W), (W, W), (1, W), (1, W)], [(W, ACT)], states=[(1, W), (TAIL, W)])
    return op([x, gate], [p["lru_conv_w"], p["lru_conv_b"][None], block_diag(p["lru_w_r"]), p["lru_b_r"][None],
                          block_diag(p["lru_w_i"]), p["lru_b_i"][None], p["lru_lambda"][None]])[0]


def m2_mixer(tag, t, z, xbc, small, p):
    cw = 2 * W
    conv = tok_op("m2_conv" + tag, m2_conv_fn, t, ROW_BLOCK, [(cw, F32)], [(4, cw), (1, cw)],
                  [(W, F32), (W // 2, F32), (W // 2, F32)], states=[(TAIL, cw)])
    xs, bm, cm = conv([xbc], [p["m2_conv_w"], p["m2_conv_b"][None]])
    ssd = tok_op("ssd" + tag, ssd_fn, t, CHUNK, [(W, F32), (W // 2, F32), (W // 2, F32), (LANE, F32)],
                 [(1, M2_HEADS)] * 3, [(W, F32)], states=[(4 * M2_HEAD_DIM, M2_STATE)] * M2_GROUPS)
    y = ssd([xs, bm, cm, small], [p["m2_dt_bias"][None], p["m2_a_log"][None], p["m2_d"][None]])[0]
    post = tok_op("m2_post" + tag, m2_post_fn, t, ROW_BLOCK, [(W, F32), (W, F32)], [(1, W)], [(W, ACT)])
    return post([y, z], [p["m2_norm"][None]])[0]


def gdn_mixer(tag, t, qkv, gate, small, p):
    conv = tok_op("gdn_conv" + tag, gdn_conv_fn, t, ROW_BLOCK, [(3 * W, F32)], [(4, 3 * W)], [(W, F32)] * 3, states=[(TAIL, 3 * W)])
    q, k, v = conv([qkv], [p["gdn_conv_w"]])
    op = tok_op("gdn" + tag, gdn_fn, t, CHUNK, [(W, F32)] * 4 + [(LANE, F32)], [(1, GDN_HEADS), (1, GDN_HEADS), (1, GDN_HEAD_DIM)],
                [(W, ACT)], states=[(GDN_HEAD_DIM, GDN_HEAD_DIM)] * GDN_HEADS)
    return op([q, k, v, gate, small], [p["gdn_dt_bias"][None], p["gdn_a_log"][None], p["gdn_norm"][None]])[0]


WEIGHTS = ["ffn1_norm", "ffn1_w_in", "ffn1_w_out", "mix_norm", "w_in", "w_gate", "b_gate", "s5_log_step", "s5_a_re",
           "s5_a_im", "s5_b_re", "s5_b_im", "s5_c_re", "s5_c_im", "s5_d", "s5_w_glu", "s5_b_glu", "lru_conv_w",
           "lru_conv_b", "lru_w_r", "lru_b_r", "lru_w_i", "lru_b_i", "lru_lambda", "m2_conv_w", "m2_conv_b", "m2_dt_bias",
           "m2_a_log", "m2_d", "m2_norm", "gdn_conv_w", "gdn_dt_bias", "gdn_a_log", "gdn_norm", "w_branch", "w_out",
           "ffn2_norm", "ffn2_w_in", "ffn2_w_out", "ple_norm", "ple_w_gate", "ple_w_proj", "final_norm"]
N_CHIPS = 4
N_DEV = 8
IN_WIDTH = 5136
SHARDED = {
    "ffn1_w_in": ((D_MODEL, 2 * FFN_DIM // N_CHIPS), 1, False),
    "ffn1_w_out": ((FFN_DIM // N_CHIPS, D_MODEL), 0, False),
    "w_in": ((D_MODEL, IN_WIDTH // N_CHIPS), 1, False),
    "w_gate": ((D_MODEL, N_BRANCH * D_MODEL // N_CHIPS), 1, False),
    "s5_w_glu": ((W // N_CHIPS, W), 0, False),
    "lru_conv_w": ((4, W // N_CHIPS), 1, True),
    "m2_conv_w": ((4, 2 * W // N_CHIPS), 1, True),
    "gdn_conv_w": ((4, 3 * W // N_CHIPS), 1, True),
    "w_branch": ((N_BRANCH, W, D_MODEL // N_CHIPS), 2, False),
    "w_out": ((D_MODEL // N_CHIPS, D_MODEL), 0, False),
    "ffn2_w_in": ((D_MODEL, 2 * FFN_DIM // N_CHIPS), 1, False),
    "ffn2_w_out": ((FFN_DIM // N_CHIPS, D_MODEL), 0, False),
    "ple_w_gate": ((D_MODEL // N_CHIPS, D_MODEL), 0, False),
    "ple_w_proj": ((256, D_MODEL // N_CHIPS), 1, False),
}
SMALL = [n for n in WEIGHTS if n not in SHARDED]
ROW = 1024


def _count(shape):
    return math.prod(shape)


def _round_up(n, unit):
    return -(-n // unit) * unit


N_GATHER = sum(DEPTH * _count(s) * (2 if exact else 1) for s, _, exact in SHARDED.values())
N_GRAD = sum(DEPTH * _count(s) for s, _, _ in SHARDED.values())
GATHER_ROWS = _round_up(-(-N_GATHER // ROW), 32)
GRAD_ROWS = _round_up(-(-N_GRAD // ROW), 32)
GRAD_HALF = GRAD_ROWS // 2
IN_PIECES = [(0, 512), (512, 512), (1024, 512), (1536, 512), (2048, 1024), (3072, 8), (3080, 1536), (4616, 512), (5128, 8)]


COL_SHARDED = ("ffn1_w_in", "ffn2_w_in", "w_gate", "ple_w_proj", "w_branch")
IN_SHARD = IN_WIDTH // N_CHIPS


def _w_in_cuts():
    cuts = {0, IN_SHARD}
    for s in range(N_CHIPS):
        for start, _ in IN_PIECES:
            if s * IN_SHARD < start < (s + 1) * IN_SHARD:
                cuts.add(start - s * IN_SHARD)
    return sorted(cuts)


IN_CUTS = _w_in_cuts()
IN_BLOCKS = list(zip(IN_CUTS[:-1], IN_CUTS[1:]))


def _piece_of(col):
    for k, (start, n) in enumerate(IN_PIECES):
        if start <= col < start + n:
            return k, col - start
    raise ValueError(col)


def cut_w_in(w):
    return jnp.concatenate([w[:, :, lo:hi].reshape(-1) for lo, hi in IN_BLOCKS])


def uncut_w_in(flat):
    blocks, off = [], 0
    for lo, hi in IN_BLOCKS:
        cnt = DEPTH * D_MODEL * (hi - lo)
        blocks.append(flat[off:off + cnt].reshape(DEPTH, D_MODEL, hi - lo))
        off += cnt
    return jnp.concatenate(blocks, axis=2)


def pack_for_gather(a):
    parts = []
    for n, (_, _, exact) in SHARDED.items():
        w = cut_w_in(a[n]) if n == "w_in" else a[n]
        parts.append((lax.bitcast_convert_type(w, BF16) if exact else w.astype(BF16)).reshape(-1))
    flat = jnp.concatenate(parts)
    return jnp.pad(flat, (0, GATHER_ROWS * ROW - flat.shape[0])).reshape(GATHER_ROWS, ROW)


def unpack_gathered(buf):
    flat16 = buf.reshape(N_CHIPS, -1)
    flat = buf.astype(ACT).reshape(N_CHIPS, -1)
    out, off = {}, 0
    for n, (shape, ax, exact) in SHARDED.items():
        cnt = DEPTH * _count(shape) * (2 if exact else 1)
        piece = (flat16 if exact else flat)[:, off:off + cnt]
        off += cnt
        if n == "w_in":
            cols = [[[] for _ in IN_PIECES] for _ in range(DEPTH)]
            for s in range(N_CHIPS):
                o = 0
                for lo, hi in IN_BLOCKS:
                    c = DEPTH * D_MODEL * (hi - lo)
                    blk = piece[s, o:o + c].reshape(DEPTH, D_MODEL, hi - lo)
                    o += c
                    k, _ = _piece_of(s * IN_SHARD + lo)
                    for layer in range(DEPTH):
                        cols[layer][k].append(blk[layer])
            out[n] = [[_pad_lanes(jnp.concatenate(c, axis=1)) for c in cols[layer]] for layer in range(DEPTH)]
        elif n in COL_SHARDED:
            out[n] = piece.reshape(N_CHIPS, DEPTH, *shape)
        else:
            if exact:
                w = lax.bitcast_convert_type(piece.reshape(N_CHIPS, DEPTH, *shape, 2), F32)
            else:
                w = piece.reshape(N_CHIPS, DEPTH, *shape)
            full = list(shape)
            full[ax] *= N_CHIPS
            out[n] = jnp.moveaxis(w, 0, ax + 1).reshape(DEPTH, *full)
    return out


def _pad_lanes(w):
    n = w.shape[1]
    return w if n % LANE == 0 else jnp.pad(w, ((0, 0), (0, LANE - n % LANE)))


def shard_w_in_grads(pieces):
    shards = []
    for s in range(N_CHIPS):
        parts = []
        for lo, hi in IN_BLOCKS:
            k, dst = _piece_of(s * IN_SHARD + lo)
            parts.append(jnp.stack([pieces[layer][k][:, dst:dst + hi - lo] for layer in range(DEPTH)]).reshape(-1))
        shards.append(jnp.concatenate(parts))
    return jnp.stack(shards)


def pack_grads(g):
    tail = jnp.zeros((N_CHIPS, GRAD_ROWS * ROW - N_GRAD), F32)
    return jnp.concatenate([g[n] for n in SHARDED] + [tail], axis=1).reshape(N_CHIPS, GRAD_ROWS, ROW)


def shard_full_grads(n, per_layer):
    shape, ax, _ = SHARDED[n]
    parts = []
    for full in per_layer:
        w = full.reshape(*full.shape[:ax], N_CHIPS, shape[ax], *full.shape[ax + 1:])
        parts.append(jnp.moveaxis(w, ax, 0).reshape(N_CHIPS, -1))
    return jnp.concatenate(parts, axis=1)


def unpack_shard(buf):
    flat = buf.reshape(-1)
    out, off = {}, 0
    for n, (shape, _, _) in SHARDED.items():
        cnt = DEPTH * _count(shape)
        piece = flat[off:off + cnt]
        out[n] = uncut_w_in(piece) if n == "w_in" else piece.reshape(DEPTH, *shape)
        off += cnt
    return out


def _rows_of(shape):
    return -(-_count(shape) // ROW)


def pack_small(vals, shapes):
    used = sum(_rows_of(shapes[n]) for n in SMALL)
    parts = []
    for n in SMALL:
        flat = vals[n].reshape(-1)
        tail = (_round_up(used, 8) - used) * ROW if n == SMALL[-1] else 0
        parts.append(jnp.pad(flat, (0, _rows_of(shapes[n]) * ROW - flat.shape[0] + tail)))
    return jnp.concatenate(parts).reshape(-1, ROW)


def unpack_small(buf, shapes):
    out, row = {}, 0
    for n in SMALL:
        r = _rows_of(shapes[n])
        out[n] = buf[row:row + r].reshape(-1)[:_count(shapes[n])].reshape(shapes[n])
        row += r
    return out


ANY = pl.BlockSpec(memory_space=pl.ANY)


def _position():
    return lax.axis_index("x"), lax.axis_index("y"), lax.axis_index("c")


def _other_chips(x, y):
    return [(1 - x, y), (x, 1 - y), (1 - x, 1 - y)]


PLACE_ROWS = 592


def gather_weights(packed, slot):
    r = packed.shape[0]
    half = r // 2
    nblk = r // PLACE_ROWS

    def place(s_ref, in_ref, o_ref):
        o_ref[...] = in_ref[...]

    slots = pl.pallas_call(
        place, name="place_shard",
        grid_spec=pltpu.PrefetchScalarGridSpec(
            num_scalar_prefetch=1, grid=(nblk,), in_specs=[pl.BlockSpec((PLACE_ROWS, ROW), lambda i, s: (i, 0))],
            out_specs=pl.BlockSpec((None, PLACE_ROWS, ROW), lambda i, s: (s[0], i, 0))),
        out_shape=jax.ShapeDtypeStruct((N_CHIPS, r, ROW), packed.dtype), compiler_params=_params(("arbitrary",)),
    )(slot, packed)

    def body(in_ref, out_ref, send_sems, recv_sems):
        x, y, c = _position()
        sibling = (x, y, 1 - c)
        chips = _other_chips(x, y)

        def half_rows(px, py, pc):
            return out_ref.at[2 * px + py, pl.ds(pl.multiple_of(pc * half, 16), half), :]

        def copy(k, block, to):
            return pltpu.make_async_remote_copy(
                src_ref=half_rows(*block), dst_ref=half_rows(*block),
                send_sem=send_sems.at[k], recv_sem=recv_sems.at[k], device_id=to, device_id_type=MESH)

        first = [copy(j, (x, y, c), (*chip, c)) for j, chip in enumerate(chips)]
        for cp in first:
            cp.start()
        passed = [copy(3 + j, (*chip, c), sibling) for j, chip in enumerate(chips)]
        for j, chip in enumerate(chips):
            copy(j, (*chip, c), (x, y, c)).wait_recv()
            passed[j].start()
        for j, chip in enumerate(chips):
            copy(3 + j, (*chip, 1 - c), (x, y, c)).wait_recv()
        for cp in first + passed:
            cp.wait_send()

    return pl.pallas_call(
        body, name="gather_weights", in_specs=[ANY], out_specs=ANY, input_output_aliases={0: 0},
        out_shape=jax.ShapeDtypeStruct((N_CHIPS, r, ROW), packed.dtype),
        scratch_shapes=[pltpu.SemaphoreType.DMA((6,)), pltpu.SemaphoreType.DMA((6,))],
    )(slots)


def swap_pair_halves(g):
    half = g.shape[1] // 2

    def body(g_ref, land_ref, send_sem, recv_sem):
        x, y, c = _position()
        src = g_ref.at[:, pl.ds(pl.multiple_of((1 - c) * half, 8), half), :]
        cp = pltpu.make_async_remote_copy(src_ref=src, dst_ref=land_ref, send_sem=send_sem, recv_sem=recv_sem,
                                          device_id=(x, y, 1 - c), device_id_type=MESH)
        cp.start()
        cp.wait()

    return pl.pallas_call(
        body, name="swap_pair_halves", in_specs=[ANY], out_specs=ANY,
        out_shape=jax.ShapeDtypeStruct((N_CHIPS, half, ROW), g.dtype),
        scratch_shapes=[pltpu.SemaphoreType.DMA, pltpu.SemaphoreType.DMA],
    )(g)


def exchange_chip_partials(part):
    half = part.shape[1]

    def body(p_ref, land_ref, send_sems, recv_sems):
        x, y, c = _position()
        cps = [pltpu.make_async_remote_copy(src_ref=p_ref.at[2 * px + py], dst_ref=land_ref.at[j], send_sem=send_sems.at[j],
                                            recv_sem=recv_sems.at[j], device_id=(px, py, c), device_id_type=MESH)
               for j, (px, py) in enumerate(_other_chips(x, y))]
        for cp in cps:
            cp.start()
        for cp in cps:
            cp.wait()

    return pl.pallas_call(
        body, name="exchange_chip_partials", in_specs=[ANY], out_specs=ANY,
        out_shape=jax.ShapeDtypeStruct((3, half, ROW), part.dtype),
        scratch_shapes=[pltpu.SemaphoreType.DMA((3,)), pltpu.SemaphoreType.DMA((3,))],
    )(part)


def share_halves(both):
    half = both.shape[0] // 2

    def body(in_ref, out_ref, send_sem, recv_sem):
        x, y, c = _position()
        my_rows = out_ref.at[pl.ds(pl.multiple_of(c * half, 8), half), :]
        cp = pltpu.make_async_remote_copy(src_ref=my_rows, dst_ref=my_rows, send_sem=send_sem, recv_sem=recv_sem,
                                          device_id=(x, y, 1 - c), device_id_type=MESH)
        cp.start()
        cp.wait()

    return pl.pallas_call(
        body, name="share_halves", in_specs=[ANY], out_specs=ANY, input_output_aliases={0: 0},
        out_shape=jax.ShapeDtypeStruct(both.shape, both.dtype),
        scratch_shapes=[pltpu.SemaphoreType.DMA, pltpu.SemaphoreType.DMA],
    )(both)


def gather_all(block):
    m_per = block.shape[0]

    def body(x_ref, out_ref, send_sems, recv_sems, local_sem):
        x, y, c = _position()
        me, sibling = (x, y, c), (x, y, 1 - c)
        chips = _other_chips(x, y)

        def rows_of(px, py, pc):
            return out_ref.at[pl.ds(pl.multiple_of((4 * px + 2 * py + pc) * m_per, 8), m_per), :]

        def copy(k, blk, to, src=None):
            return pltpu.make_async_remote_copy(
                src_ref=rows_of(*blk) if src is None else src, dst_ref=rows_of(*blk),
                send_sem=send_sems.at[k], recv_sem=recv_sems.at[k], device_id=to, device_id_type=MESH)

        mine = pltpu.make_async_copy(x_ref, rows_of(*me), local_sem)
        mine.start()
        first = [copy(0, me, sibling, src=x_ref)]
        first += [copy(1 + j, me, (*chip, c), src=x_ref) for j, chip in enumerate(chips)]
        for cp in first:
            cp.start()
        passed = [copy(4 + j, (*chip, c), sibling) for j, chip in enumerate(chips)]
        for j, chip in enumerate(chips):
            copy(1 + j, (*chip, c), me).wait_recv()
            passed[j].start()
        copy(0, sibling, me).wait_recv()
        for j, chip in enumerate(chips):
            copy(4 + j, (*chip, 1 - c), me).wait_recv()
        for cp in first + passed:
            cp.wait_send()
        mine.wait()

    return pl.pallas_call(
        body, name="gather_all", out_shape=jax.ShapeDtypeStruct((N_DEV * m_per, ROW), block.dtype),
        in_specs=[pl.BlockSpec(memory_space=pltpu.VMEM)], out_specs=pl.BlockSpec(memory_space=pltpu.VMEM),
        scratch_shapes=[pltpu.SemaphoreType.DMA((7,)), pltpu.SemaphoreType.DMA((7,)), pltpu.SemaphoreType.DMA],
        compiler_params=_params(),
    )(block)


SUM_ROWS = 592


def add_pair_halves(grads, landed, core):
    half = landed.shape[1]
    nblk = half // SUM_ROWS

    def body(c_ref, g_ref, l_ref, o_ref, o16_ref):
        acc = g_ref[...] + l_ref[...]
        o_ref[...] = acc
        o16_ref[...] = acc.astype(BF16)

    blk = (None, SUM_ROWS, ROW)
    o_spec = pl.BlockSpec(blk, lambda s, i, c: (s, i, 0))
    return pl.pallas_call(
        body, name="add_pair_halves",
        grid_spec=pltpu.PrefetchScalarGridSpec(
            num_scalar_prefetch=1, grid=(N_CHIPS, nblk),
            in_specs=[pl.BlockSpec(blk, lambda s, i, c: (s, c[0] * nblk + i, 0)), o_spec], out_specs=[o_spec, o_spec]),
        out_shape=[jax.ShapeDtypeStruct(landed.shape, F32), jax.ShapeDtypeStruct(landed.shape, BF16)],
        compiler_params=_params(("arbitrary", "arbitrary")),
    )(core, grads, landed)


def add_chip_partials(part, landed, slot, core):
    half = part.shape[1]
    nblk = half // SUM_ROWS

    def body(s_ref, c_ref, p_ref, l_ref, o_ref):
        o_ref[...] = ((p_ref[...] + l_ref[0].astype(F32)) + l_ref[1].astype(F32)) + l_ref[2].astype(F32)

    return pl.pallas_call(
        body, name="add_chip_partials",
        grid_spec=pltpu.PrefetchScalarGridSpec(
            num_scalar_prefetch=2, grid=(nblk,),
            in_specs=[pl.BlockSpec((None, SUM_ROWS, ROW), lambda i, s, c: (s[0], i, 0)), pl.BlockSpec((3, SUM_ROWS, ROW), lambda i, s, c: (0, i, 0))],
            out_specs=pl.BlockSpec((SUM_ROWS, ROW), lambda i, s, c: (c[0] * nblk + i, 0))),
        out_shape=jax.ShapeDtypeStruct((2 * half, ROW), F32), compiler_params=_params(("arbitrary",)),
    )(slot, core, part, landed)


def sum_devices(stacked):
    m = stacked.shape[1]

    def body(s_ref, o_ref):
        acc = s_ref[0]
        for d in range(1, N_DEV):
            acc = acc + s_ref[d]
        o_ref[...] = acc

    return pl.pallas_call(
        body, name="sum_devices", grid=(m // 8,), in_specs=[pl.BlockSpec((N_DEV, 8, ROW), lambda i: (0, i, 0))],
        out_specs=pl.BlockSpec((8, ROW), lambda i: (i, 0)), out_shape=jax.ShapeDtypeStruct((m, ROW), F32),
        compiler_params=_params(("arbitrary",)),
    )(stacked)


def adamw(name, w, g, m, v):
    width = w.shape[-1]
    n_rows = w.size // width
    nblk = _tile(n_rows, 256, 8)
    arg = rows(width, nblk)
    outs, _ = blocked_forward("adamw_" + name, adamw_fn, 1, n_rows // nblk, [arg] * 4, [t.reshape(n_rows, width) for t in (w, g, m, v)],
                              [], [], [arg] * 3, [(n_rows, width)] * 3, (), False)
    return [o.reshape(w.shape) for o in outs]


def trunk_loss(diff, p_emb, target, wts):
    x, small, gw, hd = diff["x"], diff["small"], diff["gw"], diff["hd"]
    t = x.shape[0]
    d = D_MODEL

    def norm_pair(name, fn, h, o, gain):
        op = tok_op(name, fn, t, 512, [(d, F32), (d, F32)], [(1, d)], [(d, F32), (d, ACT)])
        return op([h, o], [gain[None]])

    def ffn(tag, n, which, i):
        z = linear(f"{which}_in{tag}", ACT, (i,), N_CHIPS)(n, wts[which + "_w_in"], hd[which + "_w_in"][i])
        act = tok_op(f"{which}_act{tag}", swiglu_fn, t, 128, [(2 * FFN_DIM, ACT)], [], [(FFN_DIM, ACT)])([z], [])[0]
        return linear(f"{which}_out{tag}", F32, (i,))(act, wts[which + "_w_out"], hd[which + "_w_out"][i])

    h = x
    n = tok_op("norm_in", norm_fn, t, 512, [(d, F32)], [(1, d)], [(d, ACT)])([x], [small["ffn1_norm"][0][None]])[0]
    loss_rows = None
    for i in range(DEPTH):
        tag = str(i)
        p = {k: v[i] for k, v in small.items() if k != "final_norm"}
        p.update({k: v[i] for k, v in gw.items()})
        o = ffn(tag, n, "ffn1", i)
        h, u = norm_pair("mix_norm" + tag, make_addnorm_fn(0.5), h, o, p["mix_norm"])
        n_in = len(IN_PIECES)
        in_proj = multi_linear("in_proj" + tag, [F32] * n_in + [ACT], [None] * n_in + [(i,)], [0] * n_in + [N_CHIPS])
        proj = in_proj(u, wts["w_in"][i] + [wts["w_gate"]], hd["in_proj"][i])
        s5_u, lru_x, lru_g, m2_z, m2_xbc, m2_dt, gdn_qkv, gdn_g, gdn_ba, gate_logits = proj
        ys = [s5_mixer(tag, t, s5_u, p), lru_mixer(tag, t, lru_x, lru_g, p),
              m2_mixer(tag, t, m2_z, m2_xbc, m2_dt, p), gdn_mixer(tag, t, gdn_qkv, gdn_g, gdn_ba, p)]
        yb = [linear(f"branch{b}_{tag}", ACT, (i, b), N_CHIPS)(y, wts["w_branch"], hd["w_branch"][i][b]) for b, y in enumerate(ys)]
        merge = tok_op("gate_merge" + tag, gate_merge_fn, t, 128, [(N_BRANCH * d, ACT)] + [(d, ACT)] * N_BRANCH,
                       [(1, N_BRANCH * d)], [(d, ACT)])
        mixed = merge([gate_logits] + yb, [p["b_gate"][None]])[0]
        o = linear("w_out" + tag, F32, (i,))(mixed, wts["w_out"], hd["w_out"][i])
        h, n = norm_pair("ffn2_norm" + tag, make_addnorm_fn(1.0), h, o, p["ffn2_norm"])
        o = ffn(tag, n, "ffn2", i)
        h, n = norm_pair("ple_norm" + tag, make_addnorm_fn(0.5), h, o, p["ple_norm"])
        pg = linear("ple_gate" + tag, F32, (i,))(n, wts["ple_w_gate"], hd["ple_w_gate"][i])
        pp = linear("ple_proj" + tag, F32, (i,), N_CHIPS)(p_emb[i], wts["ple_w_proj"], hd["ple_w_proj"][i])
        if i + 1 < DEPTH:
            op = tok_op("ple" + tag, ple_fn, t, 512, [(d, F32)] * 3, [(1, d)], [(d, F32), (d, ACT)])
            h, n = op([h, pg, pp], [small["ffn1_norm"][i + 1][None]])
        else:
            op = tok_op("ple_loss", ple_loss_fn, t, 512, [(d, F32)] * 3 + [(d, F32, False)], [(1, d)], [(1, F32)])
            loss_rows = op([h, pg, pp, target], [small["final_norm"][None]])[0]
    return jnp.sum(loss_rows)


def kernel(x, p, ffn1_norm, ffn1_w_in, ffn1_w_out, mix_norm, w_in, w_gate, b_gate, s5_log_step, s5_a_re, s5_a_im, s5_b_re, s5_b_im, s5_c_re, s5_c_im, s5_d, s5_w_glu, s5_b_glu, lru_conv_w, lru_conv_b, lru_w_r, lru_b_r, lru_w_i, lru_b_i, lru_lambda, m2_conv_w, m2_conv_b, m2_dt_bias, m2_a_log, m2_d, m2_norm, gdn_conv_w, gdn_dt_bias, gdn_a_log, gdn_norm, w_branch, w_out, ffn2_norm, ffn2_w_in, ffn2_w_out, ple_norm, ple_w_gate, ple_w_proj, final_norm, loss_target, m_ffn1_norm, m_ffn1_w_in, m_ffn1_w_out, m_mix_norm, m_w_in, m_w_gate, m_b_gate, m_s5_log_step, m_s5_a_re, m_s5_a_im, m_s5_b_re, m_s5_b_im, m_s5_c_re, m_s5_c_im, m_s5_d, m_s5_w_glu, m_s5_b_glu, m_lru_conv_w, m_lru_conv_b, m_lru_w_r, m_lru_b_r, m_lru_w_i, m_lru_b_i, m_lru_lambda, m_m2_conv_w, m_m2_conv_b, m_m2_dt_bias, m_m2_a_log, m_m2_d, m_m2_norm, m_gdn_conv_w, m_gdn_dt_bias, m_gdn_a_log, m_gdn_norm, m_w_branch, m_w_out, m_ffn2_norm, m_ffn2_w_in, m_ffn2_w_out, m_ple_norm, m_ple_w_gate, m_ple_w_proj, m_final_norm, v_ffn1_norm, v_ffn1_w_in, v_ffn1_w_out, v_mix_norm, v_w_in, v_w_gate, v_b_gate, v_s5_log_step, v_s5_a_re, v_s5_a_im, v_s5_b_re, v_s5_b_im, v_s5_c_re, v_s5_c_im, v_s5_d, v_s5_w_glu, v_s5_b_glu, v_lru_conv_w, v_lru_conv_b, v_lru_w_r, v_lru_b_r, v_lru_w_i, v_lru_b_i, v_lru_lambda, v_m2_conv_w, v_m2_conv_b, v_m2_dt_bias, v_m2_a_log, v_m2_d, v_m2_norm, v_gdn_conv_w, v_gdn_dt_bias, v_gdn_a_log, v_gdn_norm, v_w_branch, v_w_out, v_ffn2_norm, v_ffn2_w_in, v_ffn2_w_out, v_ple_norm, v_ple_w_gate, v_ple_w_proj, v_final_norm):
    a = dict(locals())
    t = x.shape[1]
    core = lax.axis_index("c").astype(jnp.int32).reshape(1)
    slot = (2 * lax.axis_index("x") + lax.axis_index("y")).astype(jnp.int32).reshape(1)

    full = unpack_gathered(gather_weights(pack_for_gather(a), slot))
    exact = [n for n, spec in SHARDED.items() if spec[2]] + ["s5_w_glu"]
    gw = {n: full[n].astype(F32) for n in exact}
    wts = {n: full[n] for n in SHARDED if n not in exact}

    def handle(n):
        shape = SHARDED[n][0]
        return jnp.zeros((N_CHIPS, *shape) if n in COL_SHARDED else (N_CHIPS * shape[0], *shape[1:]), F32)

    hd = {n: [handle(n) for _ in range(DEPTH)] for n in wts if n not in ("w_in", "w_gate", "w_branch")}
    hd["w_branch"] = [[jnp.zeros((N_CHIPS, W, D_MODEL // N_CHIPS), F32) for _ in range(N_BRANCH)] for _ in range(DEPTH)]
    hd["in_proj"] = [[jnp.zeros((D_MODEL, _round_up(n, LANE)), F32) for _, n in IN_PIECES] + [handle("w_gate")] for _ in range(DEPTH)]
    per_layer = {n: a[n] if n == "final_norm" else [a[n][layer] for layer in range(DEPTH)] for n in SMALL}
    diff = {"x": x.reshape(t, D_MODEL), "small": per_layer, "gw": gw, "hd": hd}
    loss_local, vjp = jax.vjp(lambda dd: trunk_loss(dd, p.reshape(DEPTH, t, -1), loss_target.reshape(t, D_MODEL), wts), diff)
    (grads,) = vjp(jnp.ones((), F32))
    loss = lax.psum(loss_local, ("x", "y", "c"))
    grad_x = grads["x"].reshape(x.shape)

    gh = grads["hd"]
    big = {n: jnp.concatenate([g.reshape(N_CHIPS, -1) for g in gh[n]], axis=1) for n in gh if n not in ("in_proj", "w_branch")}
    big["w_in"] = shard_w_in_grads([gh["in_proj"][i][:-1] for i in range(DEPTH)])
    big["w_gate"] = jnp.concatenate([gh["in_proj"][i][-1].reshape(N_CHIPS, -1) for i in range(DEPTH)], axis=1)
    big["w_branch"] = jnp.concatenate([g.reshape(N_CHIPS, -1) for i in range(DEPTH) for g in gh["w_branch"][i]], axis=1)
    for n in exact:
        big[n] = shard_full_grads(n, [grads["gw"][n][i] for i in range(DEPTH)])

    packed = pack_grads(big)
    pair, pair16 = add_pair_halves(packed, swap_pair_halves(packed), core)
    mine = add_chip_partials(pair, exchange_chip_partials(pair16), slot, core)
    g_shard = unpack_shard(share_halves(mine))

    shapes = {n: a[n].shape for n in SMALL}
    gs_local = pack_small({n: g if n == "final_norm" else jnp.stack(g) for n, g in grads["small"].items()}, shapes)
    gs = gather_all(gs_local)
    g_small = sum_devices(gs.reshape(N_DEV, gs_local.shape[0], ROW))

    g_all = {**unpack_small(g_small, shapes), **g_shard}
    res = {n: [g_all[n]] + adamw(n, a[n], g_all[n], a["m_" + n], a["v_" + n]) for n in WEIGHTS}
    return (loss, grad_x, *[res[n][0] for n in WEIGHTS], *[res[n][1] for n in WEIGHTS],
            *[res[n][2] for n in WEIGHTS], *[res[n][3] for n in WEIGHTS])
```

```python
import functools
import math

import jax
import jax.numpy as jnp
from jax import lax
from jax.experimental import pallas as pl
from jax.experimental.pallas import tpu as pltpu

F32, BF16 = jnp.float32, jnp.bfloat16
ACT = BF16
EPS = 1e-6
D_MODEL = 1024
DEPTH = 2
FFN_DIM = 2816
BRANCH_WIDTH = 512
N_BRANCH = 4
LRU_C = 8.0
S5_GROUPS, S5_GROUP_CH, S5_STATE = 32, 16, 64
S5_W = S5_GROUPS * S5_STATE
LRU_HEADS, LRU_HEAD_DIM = 8, 64
M2_HEADS, M2_HEAD_DIM, M2_GROUPS, M2_STATE = 8, 64, 2, 128
GDN_HEADS, GDN_HEAD_DIM = 4, 128
CHUNK = 128
ADAM_LR, ADAM_B1, ADAM_B2, ADAM_EPS, ADAM_WD, ADAM_STEP = 0.001, 0.9, 0.999, 1e-08, 0.01, 10
VMEM_LIMIT_BYTES = 56 * 1024 * 1024
MESH = pl.DeviceIdType.MESH


def _params(sem=None):
    return pltpu.CompilerParams(vmem_limit_bytes=VMEM_LIMIT_BYTES, dimension_semantics=sem)


def _dot_bf16(a, b, dims):
    return lax.dot_general(a.astype(BF16), b.astype(BF16), (dims, ((), ())), preferred_element_type=F32)


def _make_mm(dot):
    @jax.custom_vjp
    def nn(a, b):
        return dot(a, b, ((1,), (0,)))

    @jax.custom_vjp
    def nt(a, b):
        return dot(a, b, ((1,), (1,)))

    @jax.custom_vjp
    def tn(a, b):
        return dot(a, b, ((0,), (0,)))

    nn.defvjp(lambda a, b: (nn(a, b), (a, b)), lambda r, g: (nt(g, r[1]), tn(r[0], g)))
    nt.defvjp(lambda a, b: (nt(a, b), (a, b)), lambda r, g: (nn(g, r[1]), tn(g, r[0])))
    tn.defvjp(lambda a, b: (tn(a, b), (a, b)), lambda r, g: (nt(r[1], g), nn(r[0], g)))
    return nn, nt, tn


def _dot_bf16x3(a, b, dims):
    a_hi, b_hi = a.astype(BF16), b.astype(BF16)
    a_lo = (a - a_hi.astype(F32)).astype(BF16)
    b_lo = (b - b_hi.astype(F32)).astype(BF16)

    def dot(p, q):
        return lax.dot_general(p, q, (dims, ((), ())), preferred_element_type=F32)

    return dot(a_hi, b_hi) + (dot(a_hi, b_lo) + dot(a_lo, b_hi))


mm, mm_nt, mm_tn = _make_mm(_dot_bf16)
mmh, mmh_nt, mmh_tn = _make_mm(_dot_bf16x3)


def _row_ids(shape):
    return lax.broadcasted_iota(jnp.int32, shape, 0)


def _shift_down(x, d):
    return jnp.where(_row_ids(x.shape) >= d, pltpu.roll(x, d, 0), 0.0)


def _shift_up(x, d):
    n = x.shape[0]
    return jnp.where(_row_ids(x.shape) < n - d, pltpu.roll(x, n - d, 0), 0.0)


def _first_row(x):
    return jnp.sum(jnp.where(_row_ids(x.shape) == 0, x, 0.0), axis=0, keepdims=True)


def last_row(x):
    return jnp.sum(jnp.where(_row_ids(x.shape) == x.shape[0] - 1, x, 0.0), axis=0, keepdims=True)


def pick_row(x, j):
    return jnp.sum(jnp.where(_row_ids(x.shape) == j, x, 0.0), axis=0, keepdims=True)


@jax.custom_vjp
def lin_scan(a, b, h0):
    n = a.shape[0]
    row = _row_ids(a.shape)
    acc_a = a
    acc_b = b + jnp.where(row == 0, a * h0, 0.0)
    d = 1
    while d < n:
        acc_b = acc_a * _shift_down(acc_b, d) + acc_b
        acc_a = acc_a * jnp.where(row >= d, pltpu.roll(acc_a, d, 0), 1.0)
        d *= 2
    return acc_b


def _lin_scan_fwd(a, b, h0):
    h = lin_scan(a, b, h0)
    return h, (a, h, h0)


def _lin_scan_bwd(res, dh):
    a, h, h0 = res
    n = a.shape[0]
    row = _row_ids(a.shape)
    acc_a = _shift_up(a, 1)
    g = dh
    d = 1
    while d < n:
        g = acc_a * _shift_up(g, d) + g
        acc_a = acc_a * jnp.where(row < n - d, pltpu.roll(acc_a, n - d, 0), 1.0)
        d *= 2
    h_prev = _shift_down(h, 1) + jnp.where(row == 0, h0, 0.0)
    return g * h_prev, g, _first_row(a * g)


lin_scan.defvjp(_lin_scan_fwd, _lin_scan_bwd)


def _cscan(br, bi, ar, ai, up):
    n = br.shape[0]
    shift = _shift_up if up else _shift_down
    hr, hi, pr, pi = br, bi, ar, ai
    d = 1
    while d < n:
        sr, si = shift(hr, d), shift(hi, d)
        hr, hi = hr + pr * sr - pi * si, hi + pr * si + pi * sr
        pr, pi = pr * pr - pi * pi, 2.0 * pr * pi
        d *= 2
    return hr, hi


@jax.custom_vjp
def complex_scan(br, bi, ar, ai, h0r, h0i):
    first = _row_ids(br.shape) == 0
    br = br + jnp.where(first, ar * h0r - ai * h0i, 0.0)
    bi = bi + jnp.where(first, ar * h0i + ai * h0r, 0.0)
    return _cscan(br, bi, ar, ai, False)


def _complex_scan_fwd(br, bi, ar, ai, h0r, h0i):
    hr, hi = complex_scan(br, bi, ar, ai, h0r, h0i)
    return (hr, hi), (ar, ai, hr, hi, h0r, h0i)


def _complex_scan_bwd(res, cts):
    ar, ai, hr, hi, h0r, h0i = res
    gr, gi = _cscan(cts[0], cts[1], ar, -ai, True)
    first = _row_ids(hr.shape) == 0
    pr = _shift_down(hr, 1) + jnp.where(first, h0r, 0.0)
    pi = _shift_down(hi, 1) + jnp.where(first, h0i, 0.0)
    d_ar = jnp.sum(gr * pr + gi * pi, axis=0, keepdims=True)
    d_ai = jnp.sum(gi * pr - gr * pi, axis=0, keepdims=True)
    g0r, g0i = _first_row(gr), _first_row(gi)
    return gr, gi, d_ar, d_ai, ar * g0r + ai * g0i, ar * g0i - ai * g0r


complex_scan.defvjp(_complex_scan_fwd, _complex_scan_bwd)

TAIL = 8


@jax.custom_vjp
def tail_rows(x):
    return x[x.shape[0] - TAIL:, :]


tail_rows.defvjp(
    lambda x: (tail_rows(x), x.shape[0]),
    lambda n, g: (jnp.concatenate([jnp.zeros((n - TAIL, g.shape[1]), g.dtype), g], axis=0),),
)


def _make_shift_tail(d):
    @jax.custom_vjp
    def shifted(x, tail):
        n = x.shape[0]
        tpad = jnp.concatenate([tail, jnp.zeros((n - TAIL, x.shape[1]), x.dtype)], axis=0)
        return jnp.where(_row_ids(x.shape) >= d, pltpu.roll(x, d, 0), pltpu.roll(tpad, n + d - TAIL, 0))

    def fwd(x, tail):
        return shifted(x, tail), None

    def bwd(_, g):
        g8 = g[:TAIL, :]
        dtail = jnp.where(_row_ids(g8.shape) >= TAIL - d, pltpu.roll(g8, TAIL - d, 0), 0.0)
        return _shift_up(g, d), dtail

    shifted.defvjp(fwd, bwd)
    return shifted


_SHIFT_TAIL = {d: _make_shift_tail(d) for d in (1, 2, 3)}


def causal_conv4(x, tail, w):
    y = pick_row(w, 3) * x
    for j in range(3):
        y = y + pick_row(w, j) * _SHIFT_TAIL[3 - j](x, tail)
    return y


def rmsnorm(x, g):
    return x * lax.rsqrt(jnp.mean(x * x, axis=-1, keepdims=True) + EPS) * g


def to_row(col):
    n = col.shape[0]
    eye = lax.broadcasted_iota(jnp.int32, (n, n), 0) == lax.broadcasted_iota(jnp.int32, (n, n), 1)
    return jnp.sum(jnp.where(eye, col, 0.0), axis=0, keepdims=True)


def causal_decay(a_col):
    n = a_col.shape[0]
    causal = lax.broadcasted_iota(jnp.int32, (n, n), 0) >= lax.broadcasted_iota(jnp.int32, (n, n), 1)
    cs = jnp.sum(jnp.where(causal, to_row(a_col), 0.0), axis=1, keepdims=True)
    diff = cs - to_row(cs)
    return cs, jnp.where(causal, jnp.exp(jnp.where(causal, diff, 0.0)), 0.0)


class Arg:
    def __init__(self, block, imap, dtype=F32, grad=True, shared=True):
        self.block, self.imap, self.dtype, self.grad, self.shared = block, imap, dtype, grad, shared


def rows(width, nblk, col=lambda g: 0, dtype=F32, grad=True):
    return Arg((nblk, width), lambda g, c: (c, col(g)), dtype, grad)


def head_rows(width, nblk, head=lambda g: g, dtype=F32, grad=True):
    return Arg((None, nblk, width), lambda g, c: (head(g), c, 0), dtype, grad)


def whole(shape, grad=True):
    return Arg(tuple(shape), lambda g, c: (0,) * len(shape), F32, grad, shared=True)


def per_group(shape, idx=lambda g: g, grad=True):
    return Arg((None,) + tuple(shape), lambda g, c: (idx(g),) + (0,) * len(shape), F32, grad, shared=False)


def _bshape(block):
    return tuple(b for b in block if b is not None)


def _spec(arg, nb=None):
    if nb is None:
        return pl.BlockSpec(arg.block, arg.imap)
    return pl.BlockSpec(arg.block, lambda g, c: arg.imap(g, nb - 1 - c))


def blocked_forward(name, fn, groups, nb, tok_args, tok, const_args, consts, out_args, out_shapes, state_shapes, save,
                    reverse=False):
    n_tok, n_const, n_out, n_state = len(tok), len(consts), len(out_args), len(state_shapes)
    walk = nb if reverse else None

    def body(*refs):
        tok_refs = refs[:n_tok]
        const_refs = refs[n_tok:n_tok + n_const]
        out_refs = refs[n_tok + n_const:n_tok + n_const + n_out]
        pos = n_tok + n_const + n_out
        save_refs = refs[pos:pos + (n_state if save else 0)]
        state_refs = refs[len(refs) - n_state:] if n_state else ()

        @pl.when(pl.program_id(1) == 0)
        def _():
            for s in state_refs:
                s[...] = jnp.zeros_like(s)

        states = [s[...] for s in state_refs]
        for sr, s in zip(save_refs, states):
            sr[...] = s
        new_states, outs = fn(states, [r[...].astype(F32) for r in tok_refs], [r[...] for r in const_refs])
        for o_ref, o in zip(out_refs, outs):
            o_ref[...] = o.astype(o_ref.dtype)
        for s_ref, s in zip(state_refs, new_states):
            s_ref[...] = s

    out_specs = [_spec(a, walk) for a in out_args]
    out_shape = [jax.ShapeDtypeStruct(s, a.dtype) for s, a in zip(out_shapes, out_args)]
    if save:
        assert not reverse
        for s in state_shapes:
            out_specs.append(pl.BlockSpec((None, None) + tuple(s), lambda g, c, k=len(s): (g, c) + (0,) * k))
            out_shape.append(jax.ShapeDtypeStruct((groups, nb) + tuple(s), F32))
    res = pl.pallas_call(
        body, name=name, grid=(groups, nb),
        in_specs=[_spec(a, walk) for a in tok_args] + [_spec(a, walk) for a in const_args],
        out_specs=out_specs, out_shape=out_shape,
        scratch_shapes=[pltpu.VMEM(tuple(s), F32) for s in state_shapes],
        compiler_params=_params(("arbitrary", "arbitrary")),
    )(*tok, *consts)
    return list(res[:n_out]), list(res[n_out:])


def blocked_backward(name, fn, groups, nb, tok_args, tok, const_args, consts, out_args, cts, state_shapes, saved):
    n_tok, n_const, n_out, n_state = len(tok), len(consts), len(out_args), len(state_shapes)
    tok_g = [i for i, a in enumerate(tok_args) if a.grad]
    const_g = [i for i, a in enumerate(const_args) if a.grad]

    def body(*refs):
        tok_refs = refs[:n_tok]
        const_refs = refs[n_tok:n_tok + n_const]
        pos = n_tok + n_const
        saved_refs = refs[pos:pos + n_state]
        ct_refs = refs[pos + n_state:pos + n_state + n_out]
        pos += n_state + n_out
        dtok_refs = refs[pos:pos + len(tok_g)]
        dconst_refs = refs[pos + len(tok_g):pos + len(tok_g) + len(const_g)]
        dstate_refs = refs[len(refs) - n_state:] if n_state else ()
        g_id, c_id = pl.program_id(0), pl.program_id(1)

        @pl.when(c_id == 0)
        def _():
            for s in dstate_refs:
                s[...] = jnp.zeros_like(s)
            for r, i in zip(dconst_refs, const_g):
                if not const_args[i].shared:
                    r[...] = jnp.zeros_like(r)

        @pl.when((c_id == 0) & (g_id == 0))
        def _():
            for r, i in zip(dconst_refs, const_g):
                if const_args[i].shared:
                    r[...] = jnp.zeros_like(r)

        tok_vals = [r[...].astype(F32) for r in tok_refs]
        const_vals = [r[...] for r in const_refs]

        def f(states, tok_d, const_d):
            tv, cv = list(tok_vals), list(const_vals)
            for i, v in zip(tok_g, tok_d):
                tv[i] = v
            for i, v in zip(const_g, const_d):
                cv[i] = v
            return fn(states, tv, cv)

        _, vjp = jax.vjp(f, [r[...] for r in saved_refs], [tok_vals[i] for i in tok_g], [const_vals[i] for i in const_g])
        dstates, dtok, dconst = vjp(([r[...] for r in dstate_refs], [r[...].astype(F32) for r in ct_refs]))
        for r, v in zip(dtok_refs, dtok):
            r[...] = v.astype(r.dtype)
        for r, v in zip(dconst_refs, dconst):
            r[...] += v
        for r, v in zip(dstate_refs, dstates):
            r[...] = v

    in_specs = [_spec(a, nb) for a in tok_args] + [_spec(a, nb) for a in const_args]
    for s in state_shapes:
        in_specs.append(pl.BlockSpec((None, None) + tuple(s), lambda g, c, k=len(s): (g, nb - 1 - c) + (0,) * k))
    in_specs += [_spec(a, nb) for a in out_args]
    out_specs = [_spec(tok_args[i], nb) for i in tok_g] + [_spec(const_args[i], nb) for i in const_g]
    out_shape = [jax.ShapeDtypeStruct(tok[i].shape, tok[i].dtype) for i in tok_g]
    out_shape += [jax.ShapeDtypeStruct(consts[i].shape, F32) for i in const_g]
    res = pl.pallas_call(
        body, name=name, grid=(groups, nb), in_specs=in_specs, out_specs=out_specs, out_shape=out_shape,
        scratch_shapes=[pltpu.VMEM(tuple(s), F32) for s in state_shapes],
        compiler_params=_params(("arbitrary", "arbitrary")),
    )(*tok, *consts, *saved, *cts)
    dtok = [None] * n_tok
    dconst = [None] * n_const
    for i, v in zip(tok_g, res[:len(tok_g)]):
        dtok[i] = v
    for i, v in zip(const_g, res[len(tok_g):]):
        dconst[i] = v
    return dtok, dconst


def blocked_op(name, fn, groups, nb, tok_args, const_args, out_args, out_shapes, state_shapes=()):
    state_shapes = tuple(state_shapes)

    @jax.custom_vjp
    def op(tok, consts):
        outs, _ = blocked_forward(name, fn, groups, nb, tok_args, tok, const_args, consts, out_args, out_shapes, state_shapes, False)
        return outs

    def fwd(tok, consts):
        outs, saved = blocked_forward(name, fn, groups, nb, tok_args, tok, const_args, consts, out_args, out_shapes, state_shapes, True)
        return outs, (tok, consts, saved)

    def bwd(res, cts):
        tok, consts, saved = res
        dtok, dconst = blocked_backward(name + "_bwd", fn, groups, nb, tok_args, tok, const_args, consts, out_args, list(cts), state_shapes, saved)
        dtok = [jnp.zeros_like(t) if d is None else d for t, d in zip(tok, dtok)]
        dconst = [jnp.zeros_like(k) if d is None else d for k, d in zip(consts, dconst)]
        return dtok, dconst

    op.defvjp(fwd, bwd)
    return op


def _make_split(sizes):
    offs = [sum(sizes[:i]) for i in range(len(sizes))]

    @jax.custom_vjp
    def split(x):
        return tuple(x[:, o:o + s] for o, s in zip(offs, sizes))

    split.defvjp(lambda x: (split(x), None), lambda _, g: (jnp.concatenate(list(g), axis=1),))
    return split


def _make_join(sizes):
    offs = [sum(sizes[:i]) for i in range(len(sizes))]

    @jax.custom_vjp
    def join(parts):
        return jnp.concatenate(list(parts), axis=1)

    join.defvjp(lambda parts: (join(parts), None), lambda _, g: (tuple(g[:, o:o + s] for o, s in zip(offs, sizes)),))
    return join


def split_cols(x, sizes):
    return _make_split(tuple(sizes))(x)


def join_cols(parts):
    return _make_join(tuple(p.shape[1] for p in parts))(tuple(parts))


def lane_scalar(row, j):
    lane = lax.broadcasted_iota(jnp.int32, row.shape, 1)
    return jnp.sum(jnp.where(lane == j, row, 0.0), axis=1, keepdims=True)


def lane_col(blk, j):
    lane = lax.broadcasted_iota(jnp.int32, blk.shape, 1)
    return jnp.sum(jnp.where(lane == j, blk, 0.0), axis=1, keepdims=True)


LANE = 128
MM_ROWS = 512
MM_TILE_M, MM_TILE_N = 1024, 1536
MM_TILE_MT = 1408


def _tile(n, cap, unit):
    if n <= cap:
        return n
    best = None
    for t in range(unit, cap + 1, unit):
        if n % t == 0:
            best = t
    assert best is not None, (n, cap, unit)
    return best


def _matmul_resident(a, b, mode, add, out_dtype, name, lead, shards):
    m, k = a.shape
    rows_b, cols_b = b.shape[-2:]
    n = cols_b * max(shards, 1) if mode == "nn" else rows_b
    tm = _tile(m, MM_ROWS, 8)
    nolead = (None,) * len(lead)
    if shards:
        w_spec = pl.BlockSpec((shards,) + nolead + (rows_b, cols_b), lambda i: (0,) + lead + (0, 0))
    else:
        w_spec = pl.BlockSpec(nolead + (rows_b, cols_b), lambda i: lead + (0, 0))
    has_add = add is not None
    dims = ((1,), (0,)) if mode == "nn" else ((1,), (1,))

    def body(*refs):
        a_ref, w_ref, o_ref = refs[0], refs[1], refs[-1]
        if shards and mode == "nn":
            lhs = a_ref[...].astype(BF16)
            for s in range(shards):
                cols = slice(s * cols_b, (s + 1) * cols_b)
                part = _dot_bf16(lhs, w_ref[s], dims)
                if has_add:
                    part = part + refs[2][:, cols].astype(F32)
                o_ref[:, cols] = part.astype(o_ref.dtype)
            return
        if shards:
            acc = _dot_bf16(a_ref[:, 0:cols_b], w_ref[0], dims)
            for s in range(1, shards):
                acc = acc + _dot_bf16(a_ref[:, s * cols_b:(s + 1) * cols_b], w_ref[s], dims)
        else:
            acc = _dot_bf16(a_ref[...], w_ref[...], dims)
        if has_add:
            acc = acc + refs[2][...].astype(F32)
        o_ref[...] = acc.astype(o_ref.dtype)

    o_spec = pl.BlockSpec((tm, n), lambda i: (i, 0))
    return pl.pallas_call(
        body, name=name, grid=(m // tm,),
        in_specs=[pl.BlockSpec((tm, k), lambda i: (i, 0)), w_spec] + ([o_spec] if has_add else []),
        out_specs=o_spec, out_shape=jax.ShapeDtypeStruct((m, n), out_dtype), compiler_params=_params(("parallel",)),
    )(*([a, b] + ([add] if has_add else [])))


def matmul(a, b, mode="nn", add=None, out_dtype=F32, name="matmul", pre=None, col_shards=0):
    if mode != "tn":
        return _matmul_resident(a, b, mode, add, out_dtype, name, () if pre is None else tuple(pre), col_shards)
    assert add is None and pre is None
    (k, m), n = a.shape, b.shape[1]
    shard_n = n // col_shards if col_shards else n
    tm, tn, tk = _tile(m, MM_TILE_MT, LANE), _tile(shard_n, MM_TILE_N, LANE), _tile(k, MM_TILE_M, LANE)
    nk, qn = k // tk, shard_n // tn

    def body(a_ref, b_ref, o_ref, acc_ref):
        l = pl.program_id(2)

        @pl.when(l == 0)
        def _():
            acc_ref[...] = jnp.zeros_like(acc_ref)

        acc_ref[...] += _dot_bf16(a_ref[...], b_ref[...], ((0,), (0,)))

        @pl.when(l == nk - 1)
        def _():
            o_ref[...] = acc_ref[...].astype(o_ref.dtype)

    if col_shards:
        o_spec = pl.BlockSpec((None, tm, tn), lambda i, j, l: (j // qn, i, j % qn))
        out_shape = jax.ShapeDtypeStruct((col_shards, m, shard_n), out_dtype)
    else:
        o_spec = pl.BlockSpec((tm, tn), lambda i, j, l: (i, j))
        out_shape = jax.ShapeDtypeStruct((m, n), out_dtype)
    return pl.pallas_call(
        body, name=name, grid=(m // tm, n // tn, nk),
        in_specs=[pl.BlockSpec((tk, tm), lambda i, j, l: (l, i)), pl.BlockSpec((tk, tn), lambda i, j, l: (l, j))],
        out_specs=o_spec, out_shape=out_shape, scratch_shapes=[pltpu.VMEM((tm, tn), F32)],
        compiler_params=_params(("parallel", "parallel", "arbitrary")),
    )(a, b)


def matmul_sum_nt(gs, ws, out_dtype, name):
    m, count = gs[0].shape[0], len(gs)
    tm = _tile(m, MM_ROWS // 2, 8)

    def body(*refs):
        acc = _dot_bf16(refs[0][...], refs[count][...], ((1,), (1,)))
        for i in range(1, count):
            acc = acc + _dot_bf16(refs[i][...], refs[count + i][...], ((1,), (1,)))
        refs[-1][...] = acc.astype(refs[-1].dtype)

    width = ws[0].shape[0]
    return pl.pallas_call(
        body, name=name, grid=(m // tm,),
        in_specs=[pl.BlockSpec((tm, g.shape[1]), lambda i: (i, 0)) for g in gs] + [pl.BlockSpec(w.shape, lambda i: (0, 0)) for w in ws],
        out_specs=pl.BlockSpec((tm, width), lambda i: (i, 0)), out_shape=jax.ShapeDtypeStruct((m, width), out_dtype),
        compiler_params=_params(("parallel",)),
    )(*gs, *ws)


def matmul_multi_nn(a, ws, out_dtypes, name):
    m, k = a.shape
    tm = _tile(m, MM_ROWS // 2, 8)
    count = len(ws)

    def body(*refs):
        lhs = refs[0][...].astype(BF16)
        for i in range(count):
            out = refs[1 + count + i]
            out[...] = _dot_bf16(lhs, refs[1 + i][...], ((1,), (0,))).astype(out.dtype)

    return pl.pallas_call(
        body, name=name, grid=(m // tm,),
        in_specs=[pl.BlockSpec((tm, k), lambda i: (i, 0))] + [pl.BlockSpec(w.shape, lambda i: (0, 0)) for w in ws],
        out_specs=[pl.BlockSpec((tm, w.shape[1]), lambda i: (i, 0)) for w in ws],
        out_shape=[jax.ShapeDtypeStruct((m, w.shape[1]), dt) for w, dt in zip(ws, out_dtypes)],
        compiler_params=_params(("parallel",)),
    )(a, *ws)


def matmul_swiglu(a, w, lead, name):
    m, k = a.shape
    shards, (rows_b, cols_b) = w.shape[0], w.shape[-2:]
    half = shards // 2
    tm = _tile(m, MM_ROWS // 2, 8)

    def body(a_ref, w_ref, z_ref, act_ref):
        lhs = a_ref[...].astype(BF16)
        for s in range(half):
            gate = _dot_bf16(lhs, w_ref[s], ((1,), (0,)))
            up = _dot_bf16(lhs, w_ref[s + half], ((1,), (0,)))
            z_ref[:, s * cols_b:(s + 1) * cols_b] = gate.astype(z_ref.dtype)
            z_ref[:, (s + half) * cols_b:(s + half + 1) * cols_b] = up.astype(z_ref.dtype)
            act_ref[:, s * cols_b:(s + 1) * cols_b] = (jax.nn.silu(gate) * up).astype(act_ref.dtype)

    w_spec = pl.BlockSpec((shards,) + (None,) * len(lead) + (rows_b, cols_b), lambda i: (0,) + lead + (0, 0))
    return pl.pallas_call(
        body, name=name, grid=(m // tm,), in_specs=[pl.BlockSpec((tm, k), lambda i: (i, 0)), w_spec],
        out_specs=[pl.BlockSpec((tm, shards * cols_b), lambda i: (i, 0)), pl.BlockSpec((tm, half * cols_b), lambda i: (i, 0))],
        out_shape=[jax.ShapeDtypeStruct((m, shards * cols_b), ACT), jax.ShapeDtypeStruct((m, half * cols_b), ACT)],
        compiler_params=_params(("parallel",)),
    )(a, w)


def linear_swiglu(name, pre, col_shards, act_rows):
    lead = tuple(pre)

    @jax.custom_vjp
    def op(a, w, handle):
        return matmul_swiglu(a, w, lead, name)[1]

    def fwd(a, w, handle):
        z, act = matmul_swiglu(a, w, lead, name)
        return act, (a, w, z)

    def bwd(res, d_act):
        a, w, z = res
        nblk = min(act_rows, z.shape[0])
        dz = blocked_backward(name + "_act_bwd", swiglu_fn, 1, z.shape[0] // nblk, [rows(z.shape[1], nblk, dtype=z.dtype)], [z], [], [],
                              [rows(d_act.shape[1], nblk, dtype=d_act.dtype)], [d_act], (), [])[0][0]
        da = matmul(dz, w, "nt", out_dtype=a.dtype, name=name + "_da", pre=pre, col_shards=col_shards)
        dw = matmul(a, dz, "tn", out_dtype=F32, name=name + "_dw", col_shards=col_shards)
        return da, jnp.zeros_like(w), dw

    op.defvjp(fwd, bwd)
    return op


def linear(name, out_dtype=F32, pre=None, col_shards=0):
    @jax.custom_vjp
    def op(a, w, handle):
        return matmul(a, w, "nn", out_dtype=out_dtype, name=name, pre=pre, col_shards=col_shards)

    def fwd(a, w, handle):
        return op(a, w, handle), (a, w)

    def bwd(res, g):
        a, w = res
        da = matmul(g, w, "nt", out_dtype=a.dtype, name=name + "_da", pre=pre, col_shards=col_shards)
        dw = matmul(a, g, "tn", out_dtype=F32, name=name + "_dw", col_shards=col_shards)
        return da, jnp.zeros_like(w), dw

    op.defvjp(fwd, bwd)
    return op


def multi_linear(name, out_dtypes, pres, shards):
    sel = [dict(pre=p, col_shards=s) for p, s in zip(pres, shards)]

    plain = [i for i in range(len(pres)) if pres[i] is None]
    rest = [i for i in range(len(pres)) if pres[i] is not None]

    @jax.custom_vjp
    def op(a, ws, handles):
        outs = dict(zip(plain, matmul_multi_nn(a, [ws[i] for i in plain], [out_dtypes[i] for i in plain], name)))
        for i in rest:
            outs[i] = matmul(a, ws[i], "nn", out_dtype=out_dtypes[i], name=f"{name}{i}", **sel[i])
        return [outs[i] for i in range(len(ws))]

    def fwd(a, ws, handles):
        return op(a, ws, handles), (a, ws)

    def bwd(res, gs):
        a, ws = res
        acc = matmul_sum_nt([gs[i] for i in plain], [ws[i] for i in plain], a.dtype if not rest else F32, name + "_da")
        for i in rest:
            acc = matmul(gs[i], ws[i], "nt", add=acc, out_dtype=a.dtype if i == rest[-1] else F32, name=f"{name}{i}_da", **sel[i])
        dws = [matmul(a, g, "tn", out_dtype=F32, name=f"{name}{i}_dw", col_shards=shards[i]) for i, g in enumerate(gs)]
        return acc, [jnp.zeros_like(w) for w in ws], dws

    op.defvjp(fwd, bwd)
    return op


def dense(name, out_dtype=F32):
    @jax.custom_vjp
    def op(a, w):
        return matmul(a, w, "nn", out_dtype=out_dtype, name=name)

    def fwd(a, w):
        return op(a, w), (a, w)

    def bwd(res, g):
        a, w = res
        return (matmul(g, w, "nt", out_dtype=a.dtype, name=name + "_da"),
                matmul(a, g, "tn", out_dtype=w.dtype, name=name + "_dw"))

    op.defvjp(fwd, bwd)
    return op


def to_col(row):
    n = row.shape[1]
    eye = lax.broadcasted_iota(jnp.int32, (n, n), 0) == lax.broadcasted_iota(jnp.int32, (n, n), 1)
    return jnp.sum(jnp.where(eye, row, 0.0), axis=1, keepdims=True)


def norm_fn(states, toks, consts):
    return [], [rmsnorm(toks[0], consts[0])]


def make_addnorm_fn(scale):
    def fn(states, toks, consts):
        h = toks[0] + scale * toks[1]
        return [], [h, rmsnorm(h, consts[0])]

    return fn


def swiglu_fn(states, toks, consts):
    gate, up = split_cols(toks[0], (FFN_DIM, FFN_DIM))
    return [], [jax.nn.silu(gate) * up]


def gate_merge_fn(states, toks, consts):
    gates = split_cols(jax.nn.sigmoid(toks[0] + consts[0]), (D_MODEL,) * N_BRANCH)
    mixed = gates[0] * toks[1]
    for n in range(1, N_BRANCH):
        mixed = mixed + gates[n] * toks[1 + n]
    return [], [mixed]


def ple_fn(states, toks, consts):
    h = toks[0] + jax.nn.sigmoid(toks[1]) * toks[2]
    return [], [h, rmsnorm(h, consts[0])]


def ple_loss_fn(states, toks, consts):
    h = toks[0] + jax.nn.sigmoid(toks[1]) * toks[2]
    err = rmsnorm(h, consts[0]) - toks[3]
    return [], [0.5 * jnp.mean(err * err, axis=-1, keepdims=True)]


def s5_discretise_fn(states, toks, consts):
    log_step, a_re, a_im, b_re, b_im = consts
    step = jnp.exp(log_step)
    mag = jnp.exp(a_re * step)
    ab_re, ab_im = mag * jnp.cos(a_im * step), mag * jnp.sin(a_im * step)
    den = a_re * a_re + a_im * a_im
    num_re = ab_re - 1.0
    f_re = (num_re * a_re + ab_im * a_im) / den
    f_im = (ab_im * a_re - num_re * a_im) / den
    return [], [ab_re, ab_im, f_re * b_re - f_im * b_im, f_re * b_im + f_im * b_re]


def s5_scan_fn(states, toks, consts):
    b_re, b_im = split_cols(toks[0], (S5_W, S5_W))
    h_re, h_im = complex_scan(b_re, b_im, consts[0], consts[1], states[0], states[1])
    return [last_row(h_re), last_row(h_im)], [join_cols([h_re, h_im])]


def s5_scan_bwd_fn(states, toks, consts):
    g_re, g_im, acc_re, acc_im = states
    h, ct, h0_re, h0_im = toks
    a_re, a_im = consts
    h_re, h_im = split_cols(h, (S5_W, S5_W))
    c_re, c_im = split_cols(ct, (S5_W, S5_W))
    last = _row_ids(c_re.shape) == c_re.shape[0] - 1
    cts = (c_re + jnp.where(last, g_re, 0.0), c_im + jnp.where(last, g_im, 0.0))
    d_re, d_im, da_re, da_im, d0_re, d0_im = _complex_scan_bwd((a_re, a_im, h_re, h_im, h0_re, h0_im), cts)
    acc_re, acc_im = acc_re + da_re, acc_im + da_im
    return [d0_re, d0_im, acc_re, acc_im], [join_cols([d_re, d_im]), acc_re, acc_im]


def s5_scan(name, t, bu, a_re, a_im):
    nblk = min(SCAN_ROWS, t)
    nb = t // nblk
    wide, row = rows(2 * S5_W, nblk, dtype=ACT), whole((1, S5_W))
    entry = Arg((None, None, 1, S5_W), lambda g, c: (0, c, 0, 0))
    state = [(1, S5_W)] * 2

    def run(bu, a_re, a_im, save):
        return blocked_forward(name, s5_scan_fn, 1, nb, [wide], [bu], [row, row], [a_re, a_im], [wide], [(t, 2 * S5_W)], state, save)

    @jax.custom_vjp
    def op(bu, a_re, a_im):
        return run(bu, a_re, a_im, False)[0][0]

    def fwd(bu, a_re, a_im):
        outs, saved = run(bu, a_re, a_im, True)
        return outs[0], (outs[0], saved, a_re, a_im)

    def bwd(res, ct):
        h, saved, a_re, a_im = res
        outs, _ = blocked_forward(name + "_bwd", s5_scan_bwd_fn, 1, nb, [wide, wide, entry, entry], [h, ct] + saved, [row, row],
                                  [a_re, a_im], [wide, row, row], [(t, 2 * S5_W), (1, S5_W), (1, S5_W)], state * 2, False, reverse=True)
        return tuple(outs)

    op.defvjp(fwd, bwd)
    return op(bu, a_re, a_im)


def s5_glu_fn(states, toks, consts):
    d_skip, w_glu, b_glu = consts
    z = jax.nn.gelu(toks[0] + d_skip * toks[1])
    return [], [z * jax.nn.sigmoid(mm(z, w_glu) + b_glu)]


def lru_fn(states, toks, consts):
    h0, tail = states
    x, gate = toks
    conv_w, conv_b, w_r, b_r, w_i, b_i, lam = consts
    xc = causal_conv4(x, tail, conv_w) + conv_b
    r = jax.nn.sigmoid(mm(xc, w_r) + b_r)
    i_g = jax.nn.sigmoid(mm(xc, w_i) + b_i)
    log_a = -LRU_C * r * jax.nn.softplus(-lam)
    inp = jnp.sqrt(1.0 - jnp.exp(2.0 * log_a)) * (i_g * xc)
    h = lin_scan(jnp.exp(log_a), inp, h0)
    return [last_row(h), tail_rows(x)], [h * jax.nn.gelu(gate)]


def m2_conv_fn(states, toks, consts):
    y = jax.nn.silu(causal_conv4(toks[0], states[0], consts[0]) + consts[1])
    return [tail_rows(toks[0])], list(split_cols(y, (BRANCH_WIDTH, M2_GROUPS * M2_STATE, M2_GROUPS * M2_STATE)))


def gdn_conv_fn(states, toks, consts):
    y = jax.nn.silu(causal_conv4(toks[0], states[0], consts[0]))
    return [tail_rows(toks[0])], list(split_cols(y, (BRANCH_WIDTH,) * 3))


def ssd_fn(states, toks, consts):
    xs, bm, cm, small = toks
    dt_bias, a_log, d_skip = consts
    x_pairs = split_cols(xs, (LANE,) * 4)
    b_g = split_cols(bm, (M2_STATE,) * M2_GROUPS)
    c_g = split_cols(cm, (M2_STATE,) * M2_GROUPS)
    lo = lax.broadcasted_iota(jnp.int32, (1, LANE), 1) < M2_HEAD_DIM
    new_states, y_pairs = [], []
    for g in range(M2_GROUPS):
        scores = mm_nt(c_g[g], b_g[g])
        y_off = split_cols(mm_nt(c_g[g], states[g]), (LANE, LANE))
        to_end, ends = [], []
        for j in range(2):
            pair = 2 * g + j
            x2 = x_pairs[pair]
            dts, css, decays, end = [], [], [], []
            for h in (2 * pair, 2 * pair + 1):
                dt = jax.nn.softplus(lane_col(small, h) + lane_scalar(dt_bias, h))
                a = dt * (-jnp.exp(lane_scalar(a_log, h)))
                cs, decay = causal_decay(a)
                dts.append(dt)
                css.append(cs)
                decays.append(decay)
                end.append(jnp.sum(a, axis=0, keepdims=True))
            xdt = x2 * jnp.where(lo, dts[0], dts[1])
            y = mm(scores * decays[0], jnp.where(lo, xdt, 0.0)) + mm(scores * decays[1], jnp.where(lo, 0.0, xdt))
            cs2 = jnp.where(lo, css[0], css[1])
            end2 = jnp.where(lo, end[0], end[1])
            y = y + y_off[j] * jnp.exp(cs2)
            y = y + jnp.where(lo, lane_scalar(d_skip, 2 * pair), lane_scalar(d_skip, 2 * pair + 1)) * x2
            y_pairs.append(y)
            to_end.append(xdt * jnp.exp(end2 - cs2))
            ends.append(end2)
        chunk_decay = jnp.exp(to_col(join_cols(ends)))
        new_states.append(states[g] * chunk_decay + mm_tn(join_cols(to_end), b_g[g]))
    return new_states, [join_cols(y_pairs)]


def m2_post_fn(states, toks, consts):
    return [], [rmsnorm(toks[0] * jax.nn.silu(toks[1]), consts[0])]


@jax.custom_vjp
def nilpotent_inverses(mats):
    size = mats[0].shape[0]
    eye = lax.broadcasted_iota(jnp.int32, mats[0].shape, 0) == lax.broadcasted_iota(jnp.int32, mats[0].shape, 1)
    invs = [jnp.where(eye, 1.0, 0.0) + m for m in mats]
    powers = list(mats)
    d = 2
    while d < size:
        powers = [_dot_bf16x3(p, p, ((1,), (0,))) for p in powers]
        invs = [i + _dot_bf16x3(i, p, ((1,), (0,))) for i, p in zip(invs, powers)]
        d *= 2
    return tuple(invs)


def _nilpotent_inverses_fwd(mats):
    invs = nilpotent_inverses(mats)
    return invs, invs


def _nilpotent_inverses_bwd(invs, gs):
    right = [_dot_bf16x3(g, i, ((1,), (1,))) for g, i in zip(gs, invs)]
    return (tuple(_dot_bf16x3(i, r, ((0,), (0,))) for i, r in zip(invs, right)),)


nilpotent_inverses.defvjp(_nilpotent_inverses_fwd, _nilpotent_inverses_bwd)


def gdn_fn(states, toks, consts):
    q, k, v, gate, small = toks
    dt_bias, a_log, norm_g = consts
    hs = range(GDN_HEADS)
    heads = (GDN_HEAD_DIM,) * GDN_HEADS
    qs, ks, vs, gs = split_cols(q, heads), split_cols(k, heads), split_cols(v, heads), split_cols(gate, heads)
    n = q.shape[0]
    strict = lax.broadcasted_iota(jnp.int32, (n, n), 0) > lax.broadcasted_iota(jnp.int32, (n, n), 1)
    qn = [qs[h] * lax.rsqrt(jnp.sum(qs[h] * qs[h], axis=-1, keepdims=True) + EPS) * (GDN_HEAD_DIM ** -0.5) for h in hs]
    kn = [ks[h] * lax.rsqrt(jnp.sum(ks[h] * ks[h], axis=-1, keepdims=True) + EPS) for h in hs]
    beta = [jax.nn.sigmoid(lane_col(small, h)) for h in hs]
    g = [-jnp.exp(lane_scalar(a_log, h)) * jax.nn.softplus(lane_col(small, GDN_HEADS + h) + lane_scalar(dt_bias, h)) for h in hs]
    cs_decay = [causal_decay(g[h]) for h in hs]
    cs, decay = [c for c, _ in cs_decay], [d for _, d in cs_decay]
    kb = [kn[h] * beta[h] for h in hs]
    inv = nilpotent_inverses(tuple(-jnp.where(strict, mm_nt(kb[h], kn[h]) * decay[h], 0.0) for h in hs))
    ecs = [jnp.exp(cs[h]) for h in hs]
    u = [mmh(inv[h], vs[h] * beta[h]) for h in hs]
    w = [mmh(inv[h], kb[h] * ecs[h]) for h in hs]
    qk = [mm_nt(qn[h], kn[h]) * decay[h] for h in hs]
    cs_end = [jnp.sum(g[h], axis=0, keepdims=True) for h in hs]
    v_new = [u[h] - mm(w[h], states[h]) for h in hs]
    o = [mm(qn[h] * ecs[h], states[h]) + mm(qk[h], v_new[h]) for h in hs]
    new_states = [states[h] * jnp.exp(cs_end[h]) + mm_tn(kn[h] * jnp.exp(cs_end[h] - cs[h]), v_new[h]) for h in hs]
    return new_states, [join_cols([rmsnorm(o[h], norm_g) * jax.nn.silu(gs[h]) for h in hs])]


def adamw_fn(states, toks, consts):
    w, g, m, v = toks
    m = ADAM_B1 * m + (1.0 - ADAM_B1) * g
    v = ADAM_B2 * v + (1.0 - ADAM_B2) * (g * g)
    m_hat = m / (1.0 - ADAM_B1 ** ADAM_STEP)
    v_hat = v / (1.0 - ADAM_B2 ** ADAM_STEP)
    return [], [-ADAM_LR * (m_hat / (jnp.sqrt(v_hat) + ADAM_EPS) + ADAM_WD * w), m, v]


def tok_op(name, fn, t, nblk, tok, consts, outs, states=()):
    nblk = min(nblk, t)
    tok_args = [rows(e[0], nblk, dtype=e[1], grad=e[2] if len(e) > 2 else True) for e in tok]
    const_args = [whole(s) for s in consts]
    out_args = [rows(w, nblk, dtype=dt) for (w, dt) in outs]
    return blocked_op(name, fn, 1, t // nblk, tok_args, const_args, out_args, [(t, w) for (w, _) in outs], states)


def const_op(name, fn, in_shapes, out_shapes):
    return blocked_op(name, fn, 1, 1, [], [whole(s) for s in in_shapes], [whole(s) for s in out_shapes], list(out_shapes))


W = BRANCH_WIDTH
SCAN_ROWS = 128
ROW_BLOCK = 512


def s5_mixer(tag, t, u, p):
    col = (S5_W, 1)
    disc = const_op("s5_disc" + tag, s5_discretise_fn, [col, col, col, (S5_W, 16), (S5_W, 16)], [col, col, (S5_W, 16), (S5_W, 16)])
    ab_re, ab_im, bb_re, bb_im = disc([], [
        jnp.repeat(p["s5_log_step"], S5_STATE).reshape(col), p["s5_a_re"].reshape(col), p["s5_a_im"].reshape(col),
        p["s5_b_re"].reshape(S5_W, S5_GROUP_CH), p["s5_b_im"].reshape(S5_W, S5_GROUP_CH)])
    eye = jnp.eye(S5_GROUPS, dtype=F32)

    def block_in(bb):
        return jnp.einsum("gpc,gh->gchp", bb.reshape(S5_GROUPS, S5_STATE, S5_GROUP_CH), eye).reshape(W, S5_W)

    def block_out(c):
        return jnp.einsum("gcp,gh->gphc", c, eye).reshape(S5_W, W)

    w_b = jnp.concatenate([block_in(bb_re), block_in(bb_im)], axis=1)
    w_c = jnp.concatenate([block_out(p["s5_c_re"]), -block_out(p["s5_c_im"])], axis=0)
    bu = dense("s5_b" + tag, ACT)(u, w_b)
    h = s5_scan("s5_scan" + tag, t, bu, ab_re.reshape(1, S5_W), ab_im.reshape(1, S5_W))
    yc = dense("s5_c" + tag)(h, w_c)
    glu = tok_op("s5_glu" + tag, s5_glu_fn, t, ROW_BLOCK, [(W, F32), (W, F32)], [(1, W), (W, W), (1, W)], [(W, ACT)])
    return glu([yc, u], [p["s5_d"].reshape(1, W), p["s5_w_glu"], p["s5_b_glu"][None]])[0]


def lru_mixer(tag, t, x, gate, p):
    def block_diag(w):
        return jnp.einsum("hij,hk->hikj", w, jnp.eye(LRU_HEADS, dtype=F32)).reshape(W, W)

    op = tok_op("lru" + tag, lru_fn, t, SCAN_ROWS, [(W, F32), (W, F32)],
                [(4, W), (1, W), (W, W), (1, W), (W, W), (1, W), (1, W)], [(W, ACT)], states=[(1, W), (TAIL, W)])
    return op([x, gate], [p["lru_conv_w"], p["lru_conv_b"][None], block_diag(p["lru_w_r"]), p["lru_b_r"][None],
                          block_diag(p["lru_w_i"]), p["lru_b_i"][None], p["lru_lambda"][None]])[0]


def m2_mixer(tag, t, z, xbc, small, p):
    cw = 2 * W
    conv = tok_op("m2_conv" + tag, m2_conv_fn, t, ROW_BLOCK, [(cw, F32)], [(4, cw), (1, cw)],
                  [(W, F32), (W // 2, F32), (W // 2, F32)], states=[(TAIL, cw)])
    xs, bm, cm = conv([xbc], [p["m2_conv_w"], p["m2_conv_b"][None]])
    ssd = tok_op("ssd" + tag, ssd_fn, t, CHUNK, [(W, F32), (W // 2, F32), (W // 2, F32), (LANE, F32)],
                 [(1, M2_HEADS)] * 3, [(W, F32)], states=[(4 * M2_HEAD_DIM, M2_STATE)] * M2_GROUPS)
    y = ssd([xs, bm, cm, small], [p["m2_dt_bias"][None], p["m2_a_log"][None], p["m2_d"][None]])[0]
    post = tok_op("m2_post" + tag, m2_post_fn, t, ROW_BLOCK, [(W, F32), (W, F32)], [(1, W)], [(W, ACT)])
    return post([y, z], [p["m2_norm"][None]])[0]


def gdn_mixer(tag, t, qkv, gate, small, p):
    conv = tok_op("gdn_conv" + tag, gdn_conv_fn, t, ROW_BLOCK, [(3 * W, F32)], [(4, 3 * W)], [(W, F32)] * 3, states=[(TAIL, 3 * W)])
    q, k, v = conv([qkv], [p["gdn_conv_w"]])
    op = tok_op("gdn" + tag, gdn_fn, t, CHUNK, [(W, F32)] * 4 + [(LANE, F32)], [(1, GDN_HEADS), (1, GDN_HEADS), (1, GDN_HEAD_DIM)],
                [(W, ACT)], states=[(GDN_HEAD_DIM, GDN_HEAD_DIM)] * GDN_HEADS)
    return op([q, k, v, gate, small], [p["gdn_dt_bias"][None], p["gdn_a_log"][None], p["gdn_norm"][None]])[0]


WEIGHTS = ["ffn1_norm", "ffn1_w_in", "ffn1_w_out", "mix_norm", "w_in", "w_gate", "b_gate", "s5_log_step", "s5_a_re",
           "s5_a_im", "s5_b_re", "s5_b_im", "s5_c_re", "s5_c_im", "s5_d", "s5_w_glu", "s5_b_glu", "lru_conv_w",
           "lru_conv_b", "lru_w_r", "lru_b_r", "lru_w_i", "lru_b_i", "lru_lambda", "m2_conv_w", "m2_conv_b", "m2_dt_bias",
           "m2_a_log", "m2_d", "m2_norm", "gdn_conv_w", "gdn_dt_bias", "gdn_a_log", "gdn_norm", "w_branch", "w_out",
           "ffn2_norm", "ffn2_w_in", "ffn2_w_out", "ple_norm", "ple_w_gate", "ple_w_proj", "final_norm"]
N_CHIPS = 4
N_DEV = 8
IN_WIDTH = 5136
SHARDED = {
    "ffn1_w_in": ((D_MODEL, 2 * FFN_DIM // N_CHIPS), 1, False),
    "ffn1_w_out": ((FFN_DIM // N_CHIPS, D_MODEL), 0, False),
    "w_in": ((D_MODEL, IN_WIDTH // N_CHIPS), 1, False),
    "w_gate": ((D_MODEL, N_BRANCH * D_MODEL // N_CHIPS), 1, False),
    "s5_w_glu": ((W // N_CHIPS, W), 0, False),
    "lru_conv_w": ((4, W // N_CHIPS), 1, True),
    "m2_conv_w": ((4, 2 * W // N_CHIPS), 1, True),
    "gdn_conv_w": ((4, 3 * W // N_CHIPS), 1, True),
    "w_branch": ((N_BRANCH, W, D_MODEL // N_CHIPS), 2, False),
    "w_out": ((D_MODEL // N_CHIPS, D_MODEL), 0, False),
    "ffn2_w_in": ((D_MODEL, 2 * FFN_DIM // N_CHIPS), 1, False),
    "ffn2_w_out": ((FFN_DIM // N_CHIPS, D_MODEL), 0, False),
    "ple_w_gate": ((D_MODEL // N_CHIPS, D_MODEL), 0, False),
    "ple_w_proj": ((256, D_MODEL // N_CHIPS), 1, False),
}
SMALL = [n for n in WEIGHTS if n not in SHARDED]
ROW = 1024


def _count(shape):
    return math.prod(shape)


def _round_up(n, unit):
    return -(-n // unit) * unit


N_GATHER = sum(DEPTH * _count(s) * (2 if exact else 1) for s, _, exact in SHARDED.values())
N_GRAD = sum(DEPTH * _count(s) for s, _, _ in SHARDED.values())
GATHER_ROWS = _round_up(-(-N_GATHER // ROW), 32)
GRAD_ROWS = _round_up(-(-N_GRAD // ROW), 32)
GRAD_HALF = GRAD_ROWS // 2
IN_PIECES = [(0, 512), (512, 512), (1024, 512), (1536, 512), (2048, 1024), (3072, 8), (3080, 1536), (4616, 512), (5128, 8)]


COL_SHARDED = ("ffn1_w_in", "ffn2_w_in", "w_gate", "ple_w_proj", "w_branch")
IN_SHARD = IN_WIDTH // N_CHIPS


def _w_in_cuts():
    cuts = {0, IN_SHARD}
    for s in range(N_CHIPS):
        for start, _ in IN_PIECES:
            if s * IN_SHARD < start < (s + 1) * IN_SHARD:
                cuts.add(start - s * IN_SHARD)
    return sorted(cuts)


IN_CUTS = _w_in_cuts()
IN_BLOCKS = list(zip(IN_CUTS[:-1], IN_CUTS[1:]))


def _piece_of(col):
    for k, (start, n) in enumerate(IN_PIECES):
        if start <= col < start + n:
            return k, col - start
    raise ValueError(col)


def cut_w_in(w):
    return jnp.concatenate([w[:, :, lo:hi].reshape(-1) for lo, hi in IN_BLOCKS])


def uncut_w_in(flat):
    blocks, off = [], 0
    for lo, hi in IN_BLOCKS:
        cnt = DEPTH * D_MODEL * (hi - lo)
        blocks.append(flat[off:off + cnt].reshape(DEPTH, D_MODEL, hi - lo))
        off += cnt
    return jnp.concatenate(blocks, axis=2)


def pack_for_gather(a):
    parts = []
    for n, (_, _, exact) in SHARDED.items():
        w = cut_w_in(a[n]) if n == "w_in" else a[n]
        parts.append((lax.bitcast_convert_type(w, BF16) if exact else w.astype(BF16)).reshape(-1))
    flat = jnp.concatenate(parts)
    return jnp.pad(flat, (0, GATHER_ROWS * ROW - flat.shape[0])).reshape(GATHER_ROWS, ROW)


def unpack_gathered(buf):
    flat16 = buf.reshape(N_CHIPS, -1)
    flat = buf.astype(ACT).reshape(N_CHIPS, -1)
    out, off = {}, 0
    for n, (shape, ax, exact) in SHARDED.items():
        cnt = DEPTH * _count(shape) * (2 if exact else 1)
        piece = (flat16 if exact else flat)[:, off:off + cnt]
        off += cnt
        if n == "w_in":
            cols = [[[] for _ in IN_PIECES] for _ in range(DEPTH)]
            for s in range(N_CHIPS):
                o = 0
                for lo, hi in IN_BLOCKS:
                    c = DEPTH * D_MODEL * (hi - lo)
                    blk = piece[s, o:o + c].reshape(DEPTH, D_MODEL, hi - lo)
                    o += c
                    k, _ = _piece_of(s * IN_SHARD + lo)
                    for layer in range(DEPTH):
                        cols[layer][k].append(blk[layer])
            out[n] = [[_pad_lanes(jnp.concatenate(c, axis=1)) for c in cols[layer]] for layer in range(DEPTH)]
        elif n in COL_SHARDED:
            out[n] = piece.reshape(N_CHIPS, DEPTH, *shape)
        else:
            if exact:
                w = lax.bitcast_convert_type(piece.reshape(N_CHIPS, DEPTH, *shape, 2), F32)
            else:
                w = piece.reshape(N_CHIPS, DEPTH, *shape)
            full = list(shape)
            full[ax] *= N_CHIPS
            out[n] = jnp.moveaxis(w, 0, ax + 1).reshape(DEPTH, *full)
    return out


def _pad_lanes(w):
    n = w.shape[1]
    return w if n % LANE == 0 else jnp.pad(w, ((0, 0), (0, LANE - n % LANE)))


def shard_w_in_grads(pieces):
    shards = []
    for s in range(N_CHIPS):
        parts = []
        for lo, hi in IN_BLOCKS:
            k, dst = _piece_of(s * IN_SHARD + lo)
            parts.append(jnp.stack([pieces[layer][k][:, dst:dst + hi - lo] for layer in range(DEPTH)]).reshape(-1))
        shards.append(jnp.concatenate(parts))
    return jnp.stack(shards)


def pack_grads(g):
    tail = jnp.zeros((N_CHIPS, GRAD_ROWS * ROW - N_GRAD), F32)
    return jnp.concatenate([g[n] for n in SHARDED] + [tail], axis=1).reshape(N_CHIPS, GRAD_ROWS, ROW)


def shard_full_grads(n, per_layer):
    shape, ax, _ = SHARDED[n]
    parts = []
    for full in per_layer:
        w = full.reshape(*full.shape[:ax], N_CHIPS, shape[ax], *full.shape[ax + 1:])
        parts.append(jnp.moveaxis(w, ax, 0).reshape(N_CHIPS, -1))
    return jnp.concatenate(parts, axis=1)


def unpack_shard(buf):
    flat = buf.reshape(-1)
    out, off = {}, 0
    for n, (shape, _, _) in SHARDED.items():
        cnt = DEPTH * _count(shape)
        piece = flat[off:off + cnt]
        out[n] = uncut_w_in(piece) if n == "w_in" else piece.reshape(DEPTH, *shape)
        off += cnt
    return out


def _rows_of(shape):
    return -(-_count(shape) // ROW)


def pack_small(vals, shapes):
    used = sum(_rows_of(shapes[n]) for n in SMALL)
    parts = []
    for n in SMALL:
        flat = vals[n].reshape(-1)
        tail = (_round_up(used, 8) - used) * ROW if n == SMALL[-1] else 0
        parts.append(jnp.pad(flat, (0, _rows_of(shapes[n]) * ROW - flat.shape[0] + tail)))
    return jnp.concatenate(parts).reshape(-1, ROW)


def unpack_small(buf, shapes):
    out, row = {}, 0
    for n in SMALL:
        r = _rows_of(shapes[n])
        out[n] = buf[row:row + r].reshape(-1)[:_count(shapes[n])].reshape(shapes[n])
        row += r
    return out


ANY = pl.BlockSpec(memory_space=pl.ANY)


def _position():
    return lax.axis_index("x"), lax.axis_index("y"), lax.axis_index("c")


def _other_chips(x, y):
    return [(1 - x, y), (x, 1 - y), (1 - x, 1 - y)]


PLACE_ROWS = 592


def gather_weights(packed, slot):
    r = packed.shape[0]
    half = r // 2
    nblk = r // PLACE_ROWS

    def place(s_ref, in_ref, o_ref):
        o_ref[...] = in_ref[...]

    slots = pl.pallas_call(
        place, name="place_shard",
        grid_spec=pltpu.PrefetchScalarGridSpec(
            num_scalar_prefetch=1, grid=(nblk,), in_specs=[pl.BlockSpec((PLACE_ROWS, ROW), lambda i, s: (i, 0))],
            out_specs=pl.BlockSpec((None, PLACE_ROWS, ROW), lambda i, s: (s[0], i, 0))),
        out_shape=jax.ShapeDtypeStruct((N_CHIPS, r, ROW), packed.dtype), compiler_params=_params(("arbitrary",)),
    )(slot, packed)

    def body(in_ref, out_ref, send_sems, recv_sems):
        x, y, c = _position()
        sibling = (x, y, 1 - c)
        chips = _other_chips(x, y)

        def half_rows(px, py, pc):
            return out_ref.at[2 * px + py, pl.ds(pl.multiple_of(pc * half, 16), half), :]

        def copy(k, block, to):
            return pltpu.make_async_remote_copy(
                src_ref=half_rows(*block), dst_ref=half_rows(*block),
                send_sem=send_sems.at[k], recv_sem=recv_sems.at[k], device_id=to, device_id_type=MESH)

        first = [copy(j, (x, y, c), (*chip, c)) for j, chip in enumerate(chips)]
        for cp in first:
            cp.start()
        passed = [copy(3 + j, (*chip, c), sibling) for j, chip in enumerate(chips)]
        for j, chip in enumerate(chips):
            copy(j, (*chip, c), (x, y, c)).wait_recv()
            passed[j].start()
        for j, chip in enumerate(chips):
            copy(3 + j, (*chip, 1 - c), (x, y, c)).wait_recv()
        for cp in first + passed:
            cp.wait_send()

    return pl.pallas_call(
        body, name="gather_weights", in_specs=[ANY], out_specs=ANY, input_output_aliases={0: 0},
        out_shape=jax.ShapeDtypeStruct((N_CHIPS, r, ROW), packed.dtype),
        scratch_shapes=[pltpu.SemaphoreType.DMA((6,)), pltpu.SemaphoreType.DMA((6,))],
    )(slots)


def swap_pair_halves(g):
    half = g.shape[1] // 2

    def body(g_ref, land_ref, send_sem, recv_sem):
        x, y, c = _position()
        src = g_ref.at[:, pl.ds(pl.multiple_of((1 - c) * half, 8), half), :]
        cp = pltpu.make_async_remote_copy(src_ref=src, dst_ref=land_ref, send_sem=send_sem, recv_sem=recv_sem,
                                          device_id=(x, y, 1 - c), device_id_type=MESH)
        cp.start()
        cp.wait()

    return pl.pallas_call(
        body, name="swap_pair_halves", in_specs=[ANY], out_specs=ANY,
        out_shape=jax.ShapeDtypeStruct((N_CHIPS, half, ROW), g.dtype),
        scratch_shapes=[pltpu.SemaphoreType.DMA, pltpu.SemaphoreType.DMA],
    )(g)


def exchange_chip_partials(part):
    half = part.shape[1]

    def body(p_ref, land_ref, send_sems, recv_sems):
        x, y, c = _position()
        cps = [pltpu.make_async_remote_copy(src_ref=p_ref.at[2 * px + py], dst_ref=land_ref.at[j], send_sem=send_sems.at[j],
                                            recv_sem=recv_sems.at[j], device_id=(px, py, c), device_id_type=MESH)
               for j, (px, py) in enumerate(_other_chips(x, y))]
        for cp in cps:
            cp.start()
        for cp in cps:
            cp.wait()

    return pl.pallas_call(
        body, name="exchange_chip_partials", in_specs=[ANY], out_specs=ANY,
        out_shape=jax.ShapeDtypeStruct((3, half, ROW), part.dtype),
        scratch_shapes=[pltpu.SemaphoreType.DMA((3,)), pltpu.SemaphoreType.DMA((3,))],
    )(part)


def share_halves(both):
    half = both.shape[0] // 2

    def body(in_ref, out_ref, send_sem, recv_sem):
        x, y, c = _position()
        my_rows = out_ref.at[pl.ds(pl.multiple_of(c * half, 8), half), :]
        cp = pltpu.make_async_remote_copy(src_ref=my_rows, dst_ref=my_rows, send_sem=send_sem, recv_sem=recv_sem,
                                          device_id=(x, y, 1 - c), device_id_type=MESH)
        cp.start()
        cp.wait()

    return pl.pallas_call(
        body, name="share_halves", in_specs=[ANY], out_specs=ANY, input_output_aliases={0: 0},
        out_shape=jax.ShapeDtypeStruct(both.shape, both.dtype),
        scratch_shapes=[pltpu.SemaphoreType.DMA, pltpu.SemaphoreType.DMA],
    )(both)


def gather_all(block):
    m_per = block.shape[0]

    def body(x_ref, out_ref, send_sems, recv_sems, local_sem):
        x, y, c = _position()
        me, sibling = (x, y, c), (x, y, 1 - c)
        chips = _other_chips(x, y)

        def rows_of(px, py, pc):
            return out_ref.at[pl.ds(pl.multiple_of((4 * px + 2 * py + pc) * m_per, 8), m_per), :]

        def copy(k, blk, to, src=None):
            return pltpu.make_async_remote_copy(
                src_ref=rows_of(*blk) if src is None else src, dst_ref=rows_of(*blk),
                send_sem=send_sems.at[k], recv_sem=recv_sems.at[k], device_id=to, device_id_type=MESH)

        mine = pltpu.make_async_copy(x_ref, rows_of(*me), local_sem)
        mine.start()
        first = [copy(0, me, sibling, src=x_ref)]
        first += [copy(1 + j, me, (*chip, c), src=x_ref) for j, chip in enumerate(chips)]
        for cp in first:
            cp.start()
        passed = [copy(4 + j, (*chip, c), sibling) for j, chip in enumerate(chips)]
        for j, chip in enumerate(chips):
            copy(1 + j, (*chip, c), me).wait_recv()
            passed[j].start()
        copy(0, sibling, me).wait_recv()
        for j, chip in enumerate(chips):
            copy(4 + j, (*chip, 1 - c), me).wait_recv()
        for cp in first + passed:
            cp.wait_send()
        mine.wait()

    return pl.pallas_call(
        body, name="gather_all", out_shape=jax.ShapeDtypeStruct((N_DEV * m_per, ROW), block.dtype),
        in_specs=[pl.BlockSpec(memory_space=pltpu.VMEM)], out_specs=pl.BlockSpec(memory_space=pltpu.VMEM),
        scratch_shapes=[pltpu.SemaphoreType.DMA((7,)), pltpu.SemaphoreType.DMA((7,)), pltpu.SemaphoreType.DMA],
        compiler_params=_params(),
    )(block)


SUM_ROWS = 592


def add_pair_halves(grads, landed, core):
    half = landed.shape[1]
    nblk = half // SUM_ROWS

    def body(c_ref, g_ref, l_ref, o_ref, o16_ref):
        acc = g_ref[...] + l_ref[...]
        o_ref[...] = acc
        o16_ref[...] = acc.astype(BF16)

    blk = (None, SUM_ROWS, ROW)
    o_spec = pl.BlockSpec(blk, lambda s, i, c: (s, i, 0))
    return pl.pallas_call(
        body, name="add_pair_halves",
        grid_spec=pltpu.PrefetchScalarGridSpec(
            num_scalar_prefetch=1, grid=(N_CHIPS, nblk),
            in_specs=[pl.BlockSpec(blk, lambda s, i, c: (s, c[0] * nblk + i, 0)), o_spec], out_specs=[o_spec, o_spec]),
        out_shape=[jax.ShapeDtypeStruct(landed.shape, F32), jax.ShapeDtypeStruct(landed.shape, BF16)],
        compiler_params=_params(("arbitrary", "arbitrary")),
    )(core, grads, landed)


def add_chip_partials(part, landed, slot, core):
    half = part.shape[1]
    nblk = half // SUM_ROWS

    def body(s_ref, c_ref, p_ref, l_ref, o_ref):
        o_ref[...] = ((p_ref[...] + l_ref[0].astype(F32)) + l_ref[1].astype(F32)) + l_ref[2].astype(F32)

    return pl.pallas_call(
        body, name="add_chip_partials",
        grid_spec=pltpu.PrefetchScalarGridSpec(
            num_scalar_prefetch=2, grid=(nblk,),
            in_specs=[pl.BlockSpec((None, SUM_ROWS, ROW), lambda i, s, c: (s[0], i, 0)), pl.BlockSpec((3, SUM_ROWS, ROW), lambda i, s, c: (0, i, 0))],
            out_specs=pl.BlockSpec((SUM_ROWS, ROW), lambda i, s, c: (c[0] * nblk + i, 0))),
        out_shape=jax.ShapeDtypeStruct((2 * half, ROW), F32), compiler_params=_params(("arbitrary",)),
    )(slot, core, part, landed)


def sum_devices(stacked):
    m = stacked.shape[1]

    def body(s_ref, o_ref):
        acc = s_ref[0]
        for d in range(1, N_DEV):
            acc = acc + s_ref[d]
        o_ref[...] = acc

    return pl.pallas_call(
        body, name="sum_devices", grid=(m // 8,), in_specs=[pl.BlockSpec((N_DEV, 8, ROW), lambda i: (0, i, 0))],
        out_specs=pl.BlockSpec((8, ROW), lambda i: (i, 0)), out_shape=jax.ShapeDtypeStruct((m, ROW), F32),
        compiler_params=_params(("arbitrary",)),
    )(stacked)


def adamw(name, w, g, m, v):
    width = w.shape[-1]
    n_rows = w.size // width
    nblk = _tile(n_rows, 256, 8)
    arg = rows(width, nblk)
    outs, _ = blocked_forward("adamw_" + name, adamw_fn, 1, n_rows // nblk, [arg] * 4, [t.reshape(n_rows, width) for t in (w, g, m, v)],
                              [], [], [arg] * 3, [(n_rows, width)] * 3, (), False)
    return [o.reshape(w.shape) for o in outs]


def trunk_loss(diff, p_emb, target, wts):
    x, small, gw, hd = diff["x"], diff["small"], diff["gw"], diff["hd"]
    t = x.shape[0]
    d = D_MODEL

    def norm_pair(name, fn, h, o, gain):
        op = tok_op(name, fn, t, 512, [(d, F32), (d, F32)], [(1, d)], [(d, F32), (d, ACT)])
        return op([h, o], [gain[None]])

    def ffn(tag, n, which, i):
        act = linear_swiglu(f"{which}_in{tag}", (i,), N_CHIPS, 128)(n, wts[which + "_w_in"], hd[which + "_w_in"][i])
        return linear(f"{which}_out{tag}", F32, (i,))(act, wts[which + "_w_out"], hd[which + "_w_out"][i])

    h = x
    n = tok_op("norm_in", norm_fn, t, 512, [(d, F32)], [(1, d)], [(d, ACT)])([x], [small["ffn1_norm"][0][None]])[0]
    loss_rows = None
    for i in range(DEPTH):
        tag = str(i)
        p = {k: v[i] for k, v in small.items() if k != "final_norm"}
        p.update({k: v[i] for k, v in gw.items()})
        o = ffn(tag, n, "ffn1", i)
        h, u = norm_pair("mix_norm" + tag, make_addnorm_fn(0.5), h, o, p["mix_norm"])
        n_in = len(IN_PIECES)
        in_proj = multi_linear("in_proj" + tag, [F32] * n_in + [ACT], [None] * n_in + [(i,)], [0] * n_in + [N_CHIPS])
        proj = in_proj(u, wts["w_in"][i] + [wts["w_gate"]], hd["in_proj"][i])
        s5_u, lru_x, lru_g, m2_z, m2_xbc, m2_dt, gdn_qkv, gdn_g, gdn_ba, gate_logits = proj
        ys = [s5_mixer(tag, t, s5_u, p), lru_mixer(tag, t, lru_x, lru_g, p),
              m2_mixer(tag, t, m2_z, m2_xbc, m2_dt, p), gdn_mixer(tag, t, gdn_qkv, gdn_g, gdn_ba, p)]
        yb = [linear(f"branch{b}_{tag}", ACT, (i, b), N_CHIPS)(y, wts["w_branch"], hd["w_branch"][i][b]) for b, y in enumerate(ys)]
        merge = tok_op("gate_merge" + tag, gate_merge_fn, t, 128, [(N_BRANCH * d, ACT)] + [(d, ACT)] * N_BRANCH,
                       [(1, N_BRANCH * d)], [(d, ACT)])
        mixed = merge([gate_logits] + yb, [p["b_gate"][None]])[0]
        o = linear("w_out" + tag, F32, (i,))(mixed, wts["w_out"], hd["w_out"][i])
        h, n = norm_pair("ffn2_norm" + tag, make_addnorm_fn(1.0), h, o, p["ffn2_norm"])
        o = ffn(tag, n, "ffn2", i)
        h, n = norm_pair("ple_norm" + tag, make_addnorm_fn(0.5), h, o, p["ple_norm"])
        pg = linear("ple_gate" + tag, F32, (i,))(n, wts["ple_w_gate"], hd["ple_w_gate"][i])
        pp = linear("ple_proj" + tag, F32, (i,), N_CHIPS)(p_emb[i], wts["ple_w_proj"], hd["ple_w_proj"][i])
        if i + 1 < DEPTH:
            op = tok_op("ple" + tag, ple_fn, t, 512, [(d, F32)] * 3, [(1, d)], [(d, F32), (d, ACT)])
            h, n = op([h, pg, pp], [small["ffn1_norm"][i + 1][None]])
        else:
            op = tok_op("ple_loss", ple_loss_fn, t, 512, [(d, F32)] * 3 + [(d, F32, False)], [(1, d)], [(1, F32)])
            loss_rows = op([h, pg, pp, target], [small["final_norm"][None]])[0]
    return jnp.sum(loss_rows)


def kernel(x, p, ffn1_norm, ffn1_w_in, ffn1_w_out, mix_norm, w_in, w_gate, b_gate, s5_log_step, s5_a_re, s5_a_im, s5_b_re, s5_b_im, s5_c_re, s5_c_im, s5_d, s5_w_glu, s5_b_glu, lru_conv_w, lru_conv_b, lru_w_r, lru_b_r, lru_w_i, lru_b_i, lru_lambda, m2_conv_w, m2_conv_b, m2_dt_bias, m2_a_log, m2_d, m2_norm, gdn_conv_w, gdn_dt_bias, gdn_a_log, gdn_norm, w_branch, w_out, ffn2_norm, ffn2_w_in, ffn2_w_out, ple_norm, ple_w_gate, ple_w_proj, final_norm, loss_target, m_ffn1_norm, m_ffn1_w_in, m_ffn1_w_out, m_mix_norm, m_w_in, m_w_gate, m_b_gate, m_s5_log_step, m_s5_a_re, m_s5_a_im, m_s5_b_re, m_s5_b_im, m_s5_c_re, m_s5_c_im, m_s5_d, m_s5_w_glu, m_s5_b_glu, m_lru_conv_w, m_lru_conv_b, m_lru_w_r, m_lru_b_r, m_lru_w_i, m_lru_b_i, m_lru_lambda, m_m2_conv_w, m_m2_conv_b, m_m2_dt_bias, m_m2_a_log, m_m2_d, m_m2_norm, m_gdn_conv_w, m_gdn_dt_bias, m_gdn_a_log, m_gdn_norm, m_w_branch, m_w_out, m_ffn2_norm, m_ffn2_w_in, m_ffn2_w_out, m_ple_norm, m_ple_w_gate, m_ple_w_proj, m_final_norm, v_ffn1_norm, v_ffn1_w_in, v_ffn1_w_out, v_mix_norm, v_w_in, v_w_gate, v_b_gate, v_s5_log_step, v_s5_a_re, v_s5_a_im, v_s5_b_re, v_s5_b_im, v_s5_c_re, v_s5_c_im, v_s5_d, v_s5_w_glu, v_s5_b_glu, v_lru_conv_w, v_lru_conv_b, v_lru_w_r, v_lru_b_r, v_lru_w_i, v_lru_b_i, v_lru_lambda, v_m2_conv_w, v_m2_conv_b, v_m2_dt_bias, v_m2_a_log, v_m2_d, v_m2_norm, v_gdn_conv_w, v_gdn_dt_bias, v_gdn_a_log, v_gdn_norm, v_w_branch, v_w_out, v_ffn2_norm, v_ffn2_w_in, v_ffn2_w_out, v_ple_norm, v_ple_w_gate, v_ple_w_proj, v_final_norm):
    a = dict(locals())
    t = x.shape[1]
    core = lax.axis_index("c").astype(jnp.int32).reshape(1)
    slot = (2 * lax.axis_index("x") + lax.axis_index("y")).astype(jnp.int32).reshape(1)

    full = unpack_gathered(gather_weights(pack_for_gather(a), slot))
    exact = [n for n, spec in SHARDED.items() if spec[2]] + ["s5_w_glu"]
    gw = {n: full[n].astype(F32) for n in exact}
    wts = {n: full[n] for n in SHARDED if n not in exact}

    def handle(n):
        shape = SHARDED[n][0]
        return jnp.zeros((N_CHIPS, *shape) if n in COL_SHARDED else (N_CHIPS * shape[0], *shape[1:]), F32)

    hd = {n: [handle(n) for _ in range(DEPTH)] for n in wts if n not in ("w_in", "w_gate", "w_branch")}
    hd["w_branch"] = [[jnp.zeros((N_CHIPS, W, D_MODEL // N_CHIPS), F32) for _ in range(N_BRANCH)] for _ in range(DEPTH)]
    hd["in_proj"] = [[jnp.zeros((D_MODEL, _round_up(n, LANE)), F32) for _, n in IN_PIECES] + [handle("w_gate")] for _ in range(DEPTH)]
    per_layer = {n: a[n] if n == "final_norm" else [a[n][layer] for layer in range(DEPTH)] for n in SMALL}
    diff = {"x": x.reshape(t, D_MODEL), "small": per_layer, "gw": gw, "hd": hd}
    loss_local, vjp = jax.vjp(lambda dd: trunk_loss(dd, p.reshape(DEPTH, t, -1), loss_target.reshape(t, D_MODEL), wts), diff)
    (grads,) = vjp(jnp.ones((), F32))
    loss = lax.psum(loss_local, ("x", "y", "c"))
    grad_x = grads["x"].reshape(x.shape)

    gh = grads["hd"]
    big = {n: jnp.concatenate([g.reshape(N_CHIPS, -1) for g in gh[n]], axis=1) for n in gh if n not in ("in_proj", "w_branch")}
    big["w_in"] = shard_w_in_grads([gh["in_proj"][i][:-1] for i in range(DEPTH)])
    big["w_gate"] = jnp.concatenate([gh["in_proj"][i][-1].reshape(N_CHIPS, -1) for i in range(DEPTH)], axis=1)
    big["w_branch"] = jnp.concatenate([g.reshape(N_CHIPS, -1) for i in range(DEPTH) for g in gh["w_branch"][i]], axis=1)
    for n in exact:
        big[n] = shard_full_grads(n, [grads["gw"][n][i] for i in range(DEPTH)])

    packed = pack_grads(big)
    pair, pair16 = add_pair_halves(packed, swap_pair_halves(packed), core)
    mine = add_chip_partials(pair, exchange_chip_partials(pair16), slot, core)
    g_shard = unpack_shard(share_halves(mine))

    shapes = {n: a[n].shape for n in SMALL}
    gs_local = pack_small({n: g if n == "final_norm" else jnp.stack(g) for n, g in grads["small"].items()}, shapes)
    gs = gather_all(gs_local)
    g_small = sum_devices(gs.reshape(N_DEV, gs_local.shape[0], ROW))

    g_all = {**unpack_small(g_small, shapes), **g_shard}
    res = {n: [g_all[n]] + adamw(n, a[n], g_all[n], a["m_" + n], a["v_" + n]) for n in WEIGHTS}
    return (loss, grad_x, *[res[n][0] for n in WEIGHTS], *[res[n][1] for n in WEIGHTS],
            *[res[n][2] for n in WEIGHTS], *[res[n][3] for n in WEIGHTS])
```

```python
import functools
import math

import jax
import jax.numpy as jnp
from jax import lax
from jax.experimental import pallas as pl
from jax.experimental.pallas import tpu as pltpu

F32, BF16 = jnp.float32, jnp.bfloat16
ACT = BF16
EPS = 1e-6
D_MODEL = 1024
DEPTH = 2
FFN_DIM = 2816
BRANCH_WIDTH = 512
N_BRANCH = 4
LRU_C = 8.0
S5_GROUPS, S5_GROUP_CH, S5_STATE = 32, 16, 64
S5_W = S5_GROUPS * S5_STATE
LRU_HEADS, LRU_HEAD_DIM = 8, 64
M2_HEADS, M2_HEAD_DIM, M2_GROUPS, M2_STATE = 8, 64, 2, 128
GDN_HEADS, GDN_HEAD_DIM = 4, 128
CHUNK = 128
ADAM_LR, ADAM_B1, ADAM_B2, ADAM_EPS, ADAM_WD, ADAM_STEP = 0.001, 0.9, 0.999, 1e-08, 0.01, 10
VMEM_LIMIT_BYTES = 56 * 1024 * 1024
MESH = pl.DeviceIdType.MESH


def _params(sem=None):
    return pltpu.CompilerParams(vmem_limit_bytes=VMEM_LIMIT_BYTES, dimension_semantics=sem)


def _dot_bf16(a, b, dims):
    return lax.dot_general(a.astype(BF16), b.astype(BF16), (dims, ((), ())), preferred_element_type=F32)


def _make_mm(dot):
    @jax.custom_vjp
    def nn(a, b):
        return dot(a, b, ((1,), (0,)))

    @jax.custom_vjp
    def nt(a, b):
        return dot(a, b, ((1,), (1,)))

    @jax.custom_vjp
    def tn(a, b):
        return dot(a, b, ((0,), (0,)))

    nn.defvjp(lambda a, b: (nn(a, b), (a, b)), lambda r, g: (nt(g, r[1]), tn(r[0], g)))
    nt.defvjp(lambda a, b: (nt(a, b), (a, b)), lambda r, g: (nn(g, r[1]), tn(g, r[0])))
    tn.defvjp(lambda a, b: (tn(a, b), (a, b)), lambda r, g: (nt(r[1], g), nn(r[0], g)))
    return nn, nt, tn


def _dot_bf16x3(a, b, dims):
    a_hi, b_hi = a.astype(BF16), b.astype(BF16)
    a_lo = (a - a_hi.astype(F32)).astype(BF16)
    b_lo = (b - b_hi.astype(F32)).astype(BF16)

    def dot(p, q):
        return lax.dot_general(p, q, (dims, ((), ())), preferred_element_type=F32)

    return dot(a_hi, b_hi) + (dot(a_hi, b_lo) + dot(a_lo, b_hi))


mm, mm_nt, mm_tn = _make_mm(_dot_bf16)
mmh, mmh_nt, mmh_tn = _make_mm(_dot_bf16x3)


def _row_ids(shape):
    return lax.broadcasted_iota(jnp.int32, shape, 0)


def _shift_down(x, d):
    return jnp.where(_row_ids(x.shape) >= d, pltpu.roll(x, d, 0), 0.0)


def _shift_up(x, d):
    n = x.shape[0]
    return jnp.where(_row_ids(x.shape) < n - d, pltpu.roll(x, n - d, 0), 0.0)


def _first_row(x):
    return jnp.sum(jnp.where(_row_ids(x.shape) == 0, x, 0.0), axis=0, keepdims=True)


def last_row(x):
    return jnp.sum(jnp.where(_row_ids(x.shape) == x.shape[0] - 1, x, 0.0), axis=0, keepdims=True)


def pick_row(x, j):
    return jnp.sum(jnp.where(_row_ids(x.shape) == j, x, 0.0), axis=0, keepdims=True)


@jax.custom_vjp
def lin_scan(a, b, h0):
    n = a.shape[0]
    row = _row_ids(a.shape)
    acc_a = a
    acc_b = b + jnp.where(row == 0, a * h0, 0.0)
    d = 1
    while d < n:
        acc_b = acc_a * _shift_down(acc_b, d) + acc_b
        acc_a = acc_a * jnp.where(row >= d, pltpu.roll(acc_a, d, 0), 1.0)
        d *= 2
    return acc_b


def _lin_scan_fwd(a, b, h0):
    h = lin_scan(a, b, h0)
    return h, (a, h, h0)


def _lin_scan_bwd(res, dh):
    a, h, h0 = res
    n = a.shape[0]
    row = _row_ids(a.shape)
    acc_a = _shift_up(a, 1)
    g = dh
    d = 1
    while d < n:
        g = acc_a * _shift_up(g, d) + g
        acc_a = acc_a * jnp.where(row < n - d, pltpu.roll(acc_a, n - d, 0), 1.0)
        d *= 2
    h_prev = _shift_down(h, 1) + jnp.where(row == 0, h0, 0.0)
    return g * h_prev, g, _first_row(a * g)


lin_scan.defvjp(_lin_scan_fwd, _lin_scan_bwd)


def _cscan(br, bi, ar, ai, up):
    n = br.shape[0]
    shift = _shift_up if up else _shift_down
    hr, hi, pr, pi = br, bi, ar, ai
    d = 1
    while d < n:
        sr, si = shift(hr, d), shift(hi, d)
        hr, hi = hr + pr * sr - pi * si, hi + pr * si + pi * sr
        pr, pi = pr * pr - pi * pi, 2.0 * pr * pi
        d *= 2
    return hr, hi


@jax.custom_vjp
def complex_scan(br, bi, ar, ai, h0r, h0i):
    first = _row_ids(br.shape) == 0
    br = br + jnp.where(first, ar * h0r - ai * h0i, 0.0)
    bi = bi + jnp.where(first, ar * h0i + ai * h0r, 0.0)
    return _cscan(br, bi, ar, ai, False)


def _complex_scan_fwd(br, bi, ar, ai, h0r, h0i):
    hr, hi = complex_scan(br, bi, ar, ai, h0r, h0i)
    return (hr, hi), (ar, ai, hr, hi, h0r, h0i)


def _complex_scan_bwd(res, cts):
    ar, ai, hr, hi, h0r, h0i = res
    gr, gi = _cscan(cts[0], cts[1], ar, -ai, True)
    first = _row_ids(hr.shape) == 0
    pr = _shift_down(hr, 1) + jnp.where(first, h0r, 0.0)
    pi = _shift_down(hi, 1) + jnp.where(first, h0i, 0.0)
    d_ar = jnp.sum(gr * pr + gi * pi, axis=0, keepdims=True)
    d_ai = jnp.sum(gi * pr - gr * pi, axis=0, keepdims=True)
    g0r, g0i = _first_row(gr), _first_row(gi)
    return gr, gi, d_ar, d_ai, ar * g0r + ai * g0i, ar * g0i - ai * g0r


complex_scan.defvjp(_complex_scan_fwd, _complex_scan_bwd)

TAIL = 8


@jax.custom_vjp
def tail_rows(x):
    return x[x.shape[0] - TAIL:, :]


tail_rows.defvjp(
    lambda x: (tail_rows(x), x.shape[0]),
    lambda n, g: (jnp.concatenate([jnp.zeros((n - TAIL, g.shape[1]), g.dtype), g], axis=0),),
)


def _make_shift_tail(d):
    @jax.custom_vjp
    def shifted(x, tail):
        n = x.shape[0]
        tpad = jnp.concatenate([tail, jnp.zeros((n - TAIL, x.shape[1]), x.dtype)], axis=0)
        return jnp.where(_row_ids(x.shape) >= d, pltpu.roll(x, d, 0), pltpu.roll(tpad, n + d - TAIL, 0))

    def fwd(x, tail):
        return shifted(x, tail), None

    def bwd(_, g):
        g8 = g[:TAIL, :]
        dtail = jnp.where(_row_ids(g8.shape) >= TAIL - d, pltpu.roll(g8, TAIL - d, 0), 0.0)
        return _shift_up(g, d), dtail

    shifted.defvjp(fwd, bwd)
    return shifted


_SHIFT_TAIL = {d: _make_shift_tail(d) for d in (1, 2, 3)}


def causal_conv4(x, tail, w):
    y = pick_row(w, 3) * x
    for j in range(3):
        y = y + pick_row(w, j) * _SHIFT_TAIL[3 - j](x, tail)
    return y


def rmsnorm(x, g):
    return x * lax.rsqrt(jnp.mean(x * x, axis=-1, keepdims=True) + EPS) * g


def to_row(col):
    n = col.shape[0]
    eye = lax.broadcasted_iota(jnp.int32, (n, n), 0) == lax.broadcasted_iota(jnp.int32, (n, n), 1)
    return jnp.sum(jnp.where(eye, col, 0.0), axis=0, keepdims=True)


def causal_decay(a_col):
    n = a_col.shape[0]
    causal = lax.broadcasted_iota(jnp.int32, (n, n), 0) >= lax.broadcasted_iota(jnp.int32, (n, n), 1)
    cs = jnp.sum(jnp.where(causal, to_row(a_col), 0.0), axis=1, keepdims=True)
    diff = cs - to_row(cs)
    return cs, jnp.where(causal, jnp.exp(jnp.where(causal, diff, 0.0)), 0.0)


class Arg:
    def __init__(self, block, imap, dtype=F32, grad=True, shared=True):
        self.block, self.imap, self.dtype, self.grad, self.shared = block, imap, dtype, grad, shared


def rows(width, nblk, col=lambda g: 0, dtype=F32, grad=True):
    return Arg((nblk, width), lambda g, c: (c, col(g)), dtype, grad)


def head_rows(width, nblk, head=lambda g: g, dtype=F32, grad=True):
    return Arg((None, nblk, width), lambda g, c: (head(g), c, 0), dtype, grad)


def whole(shape, grad=True):
    return Arg(tuple(shape), lambda g, c: (0,) * len(shape), F32, grad, shared=True)


def per_group(shape, idx=lambda g: g, grad=True):
    return Arg((None,) + tuple(shape), lambda g, c: (idx(g),) + (0,) * len(shape), F32, grad, shared=False)


def _bshape(block):
    return tuple(b for b in block if b is not None)


def _spec(arg, nb=None):
    if nb is None:
        return pl.BlockSpec(arg.block, arg.imap)
    return pl.BlockSpec(arg.block, lambda g, c: arg.imap(g, nb - 1 - c))


def blocked_forward(name, fn, groups, nb, tok_args, tok, const_args, consts, out_args, out_shapes, state_shapes, save,
                    reverse=False):
    n_tok, n_const, n_out, n_state = len(tok), len(consts), len(out_args), len(state_shapes)
    walk = nb if reverse else None

    def body(*refs):
        tok_refs = refs[:n_tok]
        const_refs = refs[n_tok:n_tok + n_const]
        out_refs = refs[n_tok + n_const:n_tok + n_const + n_out]
        pos = n_tok + n_const + n_out
        save_refs = refs[pos:pos + (n_state if save else 0)]
        state_refs = refs[len(refs) - n_state:] if n_state else ()

        @pl.when(pl.program_id(1) == 0)
        def _():
            for s in state_refs:
                s[...] = jnp.zeros_like(s)

        states = [s[...] for s in state_refs]
        for sr, s in zip(save_refs, states):
            sr[...] = s
        new_states, outs = fn(states, [r[...].astype(F32) for r in tok_refs], [r[...] for r in const_refs])
        for o_ref, o in zip(out_refs, outs):
            o_ref[...] = o.astype(o_ref.dtype)
        for s_ref, s in zip(state_refs, new_states):
            s_ref[...] = s

    out_specs = [_spec(a, walk) for a in out_args]
    out_shape = [jax.ShapeDtypeStruct(s, a.dtype) for s, a in zip(out_shapes, out_args)]
    if save:
        assert not reverse
        for s in state_shapes:
            out_specs.append(pl.BlockSpec((None, None) + tuple(s), lambda g, c, k=len(s): (g, c) + (0,) * k))
            out_shape.append(jax.ShapeDtypeStruct((groups, nb) + tuple(s), F32))
    res = pl.pallas_call(
        body, name=name, grid=(groups, nb),
        in_specs=[_spec(a, walk) for a in tok_args] + [_spec(a, walk) for a in const_args],
        out_specs=out_specs, out_shape=out_shape,
        scratch_shapes=[pltpu.VMEM(tuple(s), F32) for s in state_shapes],
        compiler_params=_params(("arbitrary", "arbitrary")),
    )(*tok, *consts)
    return list(res[:n_out]), list(res[n_out:])


def blocked_backward(name, fn, groups, nb, tok_args, tok, const_args, consts, out_args, cts, state_shapes, saved):
    n_tok, n_const, n_out, n_state = len(tok), len(consts), len(out_args), len(state_shapes)
    tok_g = [i for i, a in enumerate(tok_args) if a.grad]
    const_g = [i for i, a in enumerate(const_args) if a.grad]

    def body(*refs):
        tok_refs = refs[:n_tok]
        const_refs = refs[n_tok:n_tok + n_const]
        pos = n_tok + n_const
        saved_refs = refs[pos:pos + n_state]
        ct_refs = refs[pos + n_state:pos + n_state + n_out]
        pos += n_state + n_out
        dtok_refs = refs[pos:pos + len(tok_g)]
        dconst_refs = refs[pos + len(tok_g):pos + len(tok_g) + len(const_g)]
        dstate_refs = refs[len(refs) - n_state:] if n_state else ()
        g_id, c_id = pl.program_id(0), pl.program_id(1)

        @pl.when(c_id == 0)
        def _():
            for s in dstate_refs:
                s[...] = jnp.zeros_like(s)
            for r, i in zip(dconst_refs, const_g):
                if not const_args[i].shared:
                    r[...] = jnp.zeros_like(r)

        @pl.when((c_id == 0) & (g_id == 0))
        def _():
            for r, i in zip(dconst_refs, const_g):
                if const_args[i].shared:
                    r[...] = jnp.zeros_like(r)

        tok_vals = [r[...].astype(F32) for r in tok_refs]
        const_vals = [r[...] for r in const_refs]

        def f(states, tok_d, const_d):
            tv, cv = list(tok_vals), list(const_vals)
            for i, v in zip(tok_g, tok_d):
                tv[i] = v
            for i, v in zip(const_g, const_d):
                cv[i] = v
            return fn(states, tv, cv)

        _, vjp = jax.vjp(f, [r[...] for r in saved_refs], [tok_vals[i] for i in tok_g], [const_vals[i] for i in const_g])
        dstates, dtok, dconst = vjp(([r[...] for r in dstate_refs], [r[...].astype(F32) for r in ct_refs]))
        for r, v in zip(dtok_refs, dtok):
            r[...] = v.astype(r.dtype)
        for r, v in zip(dconst_refs, dconst):
            r[...] += v
        for r, v in zip(dstate_refs, dstates):
            r[...] = v

    in_specs = [_spec(a, nb) for a in tok_args] + [_spec(a, nb) for a in const_args]
    for s in state_shapes:
        in_specs.append(pl.BlockSpec((None, None) + tuple(s), lambda g, c, k=len(s): (g, nb - 1 - c) + (0,) * k))
    in_specs += [_spec(a, nb) for a in out_args]
    out_specs = [_spec(tok_args[i], nb) for i in tok_g] + [_spec(const_args[i], nb) for i in const_g]
    out_shape = [jax.ShapeDtypeStruct(tok[i].shape, tok[i].dtype) for i in tok_g]
    out_shape += [jax.ShapeDtypeStruct(consts[i].shape, F32) for i in const_g]
    res = pl.pallas_call(
        body, name=name, grid=(groups, nb), in_specs=in_specs, out_specs=out_specs, out_shape=out_shape,
        scratch_shapes=[pltpu.VMEM(tuple(s), F32) for s in state_shapes],
        compiler_params=_params(("arbitrary", "arbitrary")),
    )(*tok, *consts, *saved, *cts)
    dtok = [None] * n_tok
    dconst = [None] * n_const
    for i, v in zip(tok_g, res[:len(tok_g)]):
        dtok[i] = v
    for i, v in zip(const_g, res[len(tok_g):]):
        dconst[i] = v
    return dtok, dconst


def blocked_op(name, fn, groups, nb, tok_args, const_args, out_args, out_shapes, state_shapes=()):
    state_shapes = tuple(state_shapes)

    @jax.custom_vjp
    def op(tok, consts):
        outs, _ = blocked_forward(name, fn, groups, nb, tok_args, tok, const_args, consts, out_args, out_shapes, state_shapes, False)
        return outs

    def fwd(tok, consts):
        outs, saved = blocked_forward(name, fn, groups, nb, tok_args, tok, const_args, consts, out_args, out_shapes, state_shapes, True)
        return outs, (tok, consts, saved)

    def bwd(res, cts):
        tok, consts, saved = res
        dtok, dconst = blocked_backward(name + "_bwd", fn, groups, nb, tok_args, tok, const_args, consts, out_args, list(cts), state_shapes, saved)
        dtok = [jnp.zeros_like(t) if d is None else d for t, d in zip(tok, dtok)]
        dconst = [jnp.zeros_like(k) if d is None else d for k, d in zip(consts, dconst)]
        return dtok, dconst

    op.defvjp(fwd, bwd)
    return op


def _make_split(sizes):
    offs = [sum(sizes[:i]) for i in range(len(sizes))]

    @jax.custom_vjp
    def split(x):
        return tuple(x[:, o:o + s] for o, s in zip(offs, sizes))

    split.defvjp(lambda x: (split(x), None), lambda _, g: (jnp.concatenate(list(g), axis=1),))
    return split


def _make_join(sizes):
    offs = [sum(sizes[:i]) for i in range(len(sizes))]

    @jax.custom_vjp
    def join(parts):
        return jnp.concatenate(list(parts), axis=1)

    join.defvjp(lambda parts: (join(parts), None), lambda _, g: (tuple(g[:, o:o + s] for o, s in zip(offs, sizes)),))
    return join


def split_cols(x, sizes):
    return _make_split(tuple(sizes))(x)


def join_cols(parts):
    return _make_join(tuple(p.shape[1] for p in parts))(tuple(parts))


def lane_scalar(row, j):
    lane = lax.broadcasted_iota(jnp.int32, row.shape, 1)
    return jnp.sum(jnp.where(lane == j, row, 0.0), axis=1, keepdims=True)


def lane_col(blk, j):
    lane = lax.broadcasted_iota(jnp.int32, blk.shape, 1)
    return jnp.sum(jnp.where(lane == j, blk, 0.0), axis=1, keepdims=True)


LANE = 128
MM_ROWS = 512
MM_TILE_M, MM_TILE_N = 1024, 1536
MM_TILE_MT = 1408


def _tile(n, cap, unit):
    if n <= cap:
        return n
    best = None
    for t in range(unit, cap + 1, unit):
        if n % t == 0:
            best = t
    assert best is not None, (n, cap, unit)
    return best


def _matmul_resident(a, b, mode, add, out_dtype, name, lead, shards):
    m, k = a.shape
    rows_b, cols_b = b.shape[-2:]
    n = cols_b * max(shards, 1) if mode == "nn" else rows_b
    tm = _tile(m, MM_ROWS, 8)
    nolead = (None,) * len(lead)
    if shards:
        w_spec = pl.BlockSpec((shards,) + nolead + (rows_b, cols_b), lambda i: (0,) + lead + (0, 0))
    else:
        w_spec = pl.BlockSpec(nolead + (rows_b, cols_b), lambda i: lead + (0, 0))
    has_add = add is not None
    dims = ((1,), (0,)) if mode == "nn" else ((1,), (1,))

    def body(*refs):
        a_ref, w_ref, o_ref = refs[0], refs[1], refs[-1]
        if shards and mode == "nn":
            lhs = a_ref[...].astype(BF16)
            for s in range(shards):
                cols = slice(s * cols_b, (s + 1) * cols_b)
                part = _dot_bf16(lhs, w_ref[s], dims)
                if has_add:
                    part = part + refs[2][:, cols].astype(F32)
                o_ref[:, cols] = part.astype(o_ref.dtype)
            return
        if shards:
            acc = _dot_bf16(a_ref[:, 0:cols_b], w_ref[0], dims)
            for s in range(1, shards):
                acc = acc + _dot_bf16(a_ref[:, s * cols_b:(s + 1) * cols_b], w_ref[s], dims)
        else:
            acc = _dot_bf16(a_ref[...], w_ref[...], dims)
        if has_add:
            acc = acc + refs[2][...].astype(F32)
        o_ref[...] = acc.astype(o_ref.dtype)

    o_spec = pl.BlockSpec((tm, n), lambda i: (i, 0))
    return pl.pallas_call(
        body, name=name, grid=(m // tm,),
        in_specs=[pl.BlockSpec((tm, k), lambda i: (i, 0)), w_spec] + ([o_spec] if has_add else []),
        out_specs=o_spec, out_shape=jax.ShapeDtypeStruct((m, n), out_dtype), compiler_params=_params(("parallel",)),
    )(*([a, b] + ([add] if has_add else [])))


def matmul(a, b, mode="nn", add=None, out_dtype=F32, name="matmul", pre=None, col_shards=0):
    if mode != "tn":
        return _matmul_resident(a, b, mode, add, out_dtype, name, () if pre is None else tuple(pre), col_shards)
    assert add is None and pre is None
    (k, m), n = a.shape, b.shape[1]
    shard_n = n // col_shards if col_shards else n
    tm, tn, tk = _tile(m, MM_TILE_MT, LANE), _tile(shard_n, MM_TILE_N, LANE), _tile(k, MM_TILE_M, LANE)
    nk, qn = k // tk, shard_n // tn

    def body(a_ref, b_ref, o_ref, acc_ref):
        l = pl.program_id(2)

        @pl.when(l == 0)
        def _():
            acc_ref[...] = jnp.zeros_like(acc_ref)

        acc_ref[...] += _dot_bf16(a_ref[...], b_ref[...], ((0,), (0,)))

        @pl.when(l == nk - 1)
        def _():
            o_ref[...] = acc_ref[...].astype(o_ref.dtype)

    if col_shards:
        o_spec = pl.BlockSpec((None, tm, tn), lambda i, j, l: (j // qn, i, j % qn))
        out_shape = jax.ShapeDtypeStruct((col_shards, m, shard_n), out_dtype)
    else:
        o_spec = pl.BlockSpec((tm, tn), lambda i, j, l: (i, j))
        out_shape = jax.ShapeDtypeStruct((m, n), out_dtype)
    return pl.pallas_call(
        body, name=name, grid=(m // tm, n // tn, nk),
        in_specs=[pl.BlockSpec((tk, tm), lambda i, j, l: (l, i)), pl.BlockSpec((tk, tn), lambda i, j, l: (l, j))],
        out_specs=o_spec, out_shape=out_shape, scratch_shapes=[pltpu.VMEM((tm, tn), F32)],
        compiler_params=_params(("parallel", "parallel", "arbitrary")),
    )(a, b)


def matmul_sum_nt(gs, ws, out_dtype, name):
    m, count = gs[0].shape[0], len(gs)
    tm = _tile(m, MM_ROWS // 2, 8)

    def body(*refs):
        acc = _dot_bf16(refs[0][...], refs[count][...], ((1,), (1,)))
        for i in range(1, count):
            acc = acc + _dot_bf16(refs[i][...], refs[count + i][...], ((1,), (1,)))
        refs[-1][...] = acc.astype(refs[-1].dtype)

    width = ws[0].shape[0]
    return pl.pallas_call(
        body, name=name, grid=(m // tm,),
        in_specs=[pl.BlockSpec((tm, g.shape[1]), lambda i: (i, 0)) for g in gs] + [pl.BlockSpec(w.shape, lambda i: (0, 0)) for w in ws],
        out_specs=pl.BlockSpec((tm, width), lambda i: (i, 0)), out_shape=jax.ShapeDtypeStruct((m, width), out_dtype),
        compiler_params=_params(("parallel",)),
    )(*gs, *ws)


def matmul_multi_nn(a, ws, out_dtypes, name):
    m, k = a.shape
    tm = _tile(m, MM_ROWS // 2, 8)
    count = len(ws)

    def body(*refs):
        lhs = refs[0][...].astype(BF16)
        for i in range(count):
            out = refs[1 + count + i]
            out[...] = _dot_bf16(lhs, refs[1 + i][...], ((1,), (0,))).astype(out.dtype)

    return pl.pallas_call(
        body, name=name, grid=(m // tm,),
        in_specs=[pl.BlockSpec((tm, k), lambda i: (i, 0))] + [pl.BlockSpec(w.shape, lambda i: (0, 0)) for w in ws],
        out_specs=[pl.BlockSpec((tm, w.shape[1]), lambda i: (i, 0)) for w in ws],
        out_shape=[jax.ShapeDtypeStruct((m, w.shape[1]), dt) for w, dt in zip(ws, out_dtypes)],
        compiler_params=_params(("parallel",)),
    )(a, *ws)


def matmul_swiglu(a, w, lead, name):
    m, k = a.shape
    shards, (rows_b, cols_b) = w.shape[0], w.shape[-2:]
    half = shards // 2
    tm = _tile(m, MM_ROWS // 2, 8)

    def body(a_ref, w_ref, z_ref, act_ref):
        lhs = a_ref[...].astype(BF16)
        for s in range(half):
            gate = _dot_bf16(lhs, w_ref[s], ((1,), (0,)))
            up = _dot_bf16(lhs, w_ref[s + half], ((1,), (0,)))
            z_ref[:, s * cols_b:(s + 1) * cols_b] = gate.astype(z_ref.dtype)
            z_ref[:, (s + half) * cols_b:(s + half + 1) * cols_b] = up.astype(z_ref.dtype)
            act_ref[:, s * cols_b:(s + 1) * cols_b] = (jax.nn.silu(gate) * up).astype(act_ref.dtype)

    w_spec = pl.BlockSpec((shards,) + (None,) * len(lead) + (rows_b, cols_b), lambda i: (0,) + lead + (0, 0))
    return pl.pallas_call(
        body, name=name, grid=(m // tm,), in_specs=[pl.BlockSpec((tm, k), lambda i: (i, 0)), w_spec],
        out_specs=[pl.BlockSpec((tm, shards * cols_b), lambda i: (i, 0)), pl.BlockSpec((tm, half * cols_b), lambda i: (i, 0))],
        out_shape=[jax.ShapeDtypeStruct((m, shards * cols_b), ACT), jax.ShapeDtypeStruct((m, half * cols_b), ACT)],
        compiler_params=_params(("parallel",)),
    )(a, w)


def linear_swiglu(name, pre, col_shards, act_rows):
    lead = tuple(pre)

    @jax.custom_vjp
    def op(a, w, handle):
        return matmul_swiglu(a, w, lead, name)[1]

    def fwd(a, w, handle):
        z, act = matmul_swiglu(a, w, lead, name)
        return act, (a, w, z)

    def bwd(res, d_act):
        a, w, z = res
        nblk = min(act_rows, z.shape[0])
        dz = blocked_backward(name + "_act_bwd", swiglu_fn, 1, z.shape[0] // nblk, [rows(z.shape[1], nblk, dtype=z.dtype)], [z], [], [],
                              [rows(d_act.shape[1], nblk, dtype=d_act.dtype)], [d_act], (), [])[0][0]
        da = matmul(dz, w, "nt", out_dtype=a.dtype, name=name + "_da", pre=pre, col_shards=col_shards)
        dw = matmul(a, dz, "tn", out_dtype=F32, name=name + "_dw", col_shards=col_shards)
        return da, jnp.zeros_like(w), dw

    op.defvjp(fwd, bwd)
    return op


def linear(name, out_dtype=F32, pre=None, col_shards=0):
    @jax.custom_vjp
    def op(a, w, handle):
        return matmul(a, w, "nn", out_dtype=out_dtype, name=name, pre=pre, col_shards=col_shards)

    def fwd(a, w, handle):
        return op(a, w, handle), (a, w)

    def bwd(res, g):
        a, w = res
        da = matmul(g, w, "nt", out_dtype=a.dtype, name=name + "_da", pre=pre, col_shards=col_shards)
        dw = matmul(a, g, "tn", out_dtype=F32, name=name + "_dw", col_shards=col_shards)
        return da, jnp.zeros_like(w), dw

    op.defvjp(fwd, bwd)
    return op


def multi_linear(name, out_dtypes, pres, shards):
    sel = [dict(pre=p, col_shards=s) for p, s in zip(pres, shards)]

    plain = [i for i in range(len(pres)) if pres[i] is None]
    rest = [i for i in range(len(pres)) if pres[i] is not None]

    @jax.custom_vjp
    def op(a, ws, handles):
        outs = dict(zip(plain, matmul_multi_nn(a, [ws[i] for i in plain], [out_dtypes[i] for i in plain], name)))
        for i in rest:
            outs[i] = matmul(a, ws[i], "nn", out_dtype=out_dtypes[i], name=f"{name}{i}", **sel[i])
        return [outs[i] for i in range(len(ws))]

    def fwd(a, ws, handles):
        return op(a, ws, handles), (a, ws)

    def bwd(res, gs):
        a, ws = res
        acc = matmul_sum_nt([gs[i] for i in plain], [ws[i] for i in plain], a.dtype if not rest else F32, name + "_da")
        for i in rest:
            acc = matmul(gs[i], ws[i], "nt", add=acc, out_dtype=a.dtype if i == rest[-1] else F32, name=f"{name}{i}_da", **sel[i])
        dws = [matmul(a, g, "tn", out_dtype=F32, name=f"{name}{i}_dw", col_shards=shards[i]) for i, g in enumerate(gs)]
        return acc, [jnp.zeros_like(w) for w in ws], dws

    op.defvjp(fwd, bwd)
    return op


def block_matmul(a, b, mode, blocks, out_dtype, name):
    m = a.shape[0]
    if mode == "tn":
        kin, kout, tk = a.shape[1] // blocks, b.shape[1] // blocks, _tile(m, MM_TILE_M, 8)

        def acc_body(a_ref, g_ref, o_ref):
            @pl.when(pl.program_id(1) == 0)
            def _():
                o_ref[...] = jnp.zeros_like(o_ref)

            o_ref[...] += _dot_bf16(a_ref[...], g_ref[...], ((0,), (0,)))

        return pl.pallas_call(
            acc_body, name=name, grid=(blocks, m // tk),
            in_specs=[pl.BlockSpec((tk, kin), lambda j, l: (l, j)), pl.BlockSpec((tk, kout), lambda j, l: (l, j))],
            out_specs=pl.BlockSpec((None, kin, kout), lambda j, l: (j, 0, 0)),
            out_shape=jax.ShapeDtypeStruct((blocks, kin, kout), out_dtype), compiler_params=_params(("parallel", "arbitrary")),
        )(a, b)
    kin, kout = b.shape[1:]
    win, wout, dims = (kin, kout, ((1,), (0,))) if mode == "nn" else (kout, kin, ((1,), (1,)))
    tm = _tile(m, MM_TILE_M, 8)

    def body(a_ref, w_ref, o_ref):
        o_ref[...] = _dot_bf16(a_ref[...], w_ref[...], dims).astype(o_ref.dtype)

    return pl.pallas_call(
        body, name=name, grid=(m // tm, blocks),
        in_specs=[pl.BlockSpec((tm, win), lambda i, j: (i, j)), pl.BlockSpec((None, kin, kout), lambda i, j: (j, 0, 0))],
        out_specs=pl.BlockSpec((tm, wout), lambda i, j: (i, j)), out_shape=jax.ShapeDtypeStruct((m, blocks * wout), out_dtype),
        compiler_params=_params(("parallel", "parallel")),
    )(a, b)


def block_dense(name, out_dtype=F32):
    @jax.custom_vjp
    def op(a, w):
        return block_matmul(a, w, "nn", w.shape[0], out_dtype, name)

    def fwd(a, w):
        return op(a, w), (a, w)

    def bwd(res, g):
        a, w = res
        return (block_matmul(g, w, "nt", w.shape[0], a.dtype, name + "_da"),
                block_matmul(a, g, "tn", w.shape[0], w.dtype, name + "_dw"))

    op.defvjp(fwd, bwd)
    return op


def dense(name, out_dtype=F32):
    @jax.custom_vjp
    def op(a, w):
        return matmul(a, w, "nn", out_dtype=out_dtype, name=name)

    def fwd(a, w):
        return op(a, w), (a, w)

    def bwd(res, g):
        a, w = res
        return (matmul(g, w, "nt", out_dtype=a.dtype, name=name + "_da"),
                matmul(a, g, "tn", out_dtype=w.dtype, name=name + "_dw"))

    op.defvjp(fwd, bwd)
    return op


def to_col(row):
    n = row.shape[1]
    eye = lax.broadcasted_iota(jnp.int32, (n, n), 0) == lax.broadcasted_iota(jnp.int32, (n, n), 1)
    return jnp.sum(jnp.where(eye, row, 0.0), axis=1, keepdims=True)


def norm_fn(states, toks, consts):
    return [], [rmsnorm(toks[0], consts[0])]


def make_addnorm_fn(scale):
    def fn(states, toks, consts):
        h = toks[0] + scale * toks[1]
        return [], [h, rmsnorm(h, consts[0])]

    return fn


def swiglu_fn(states, toks, consts):
    gate, up = split_cols(toks[0], (FFN_DIM, FFN_DIM))
    return [], [jax.nn.silu(gate) * up]


def gate_merge_fn(states, toks, consts):
    gates = split_cols(jax.nn.sigmoid(toks[0] + consts[0]), (D_MODEL,) * N_BRANCH)
    mixed = gates[0] * toks[1]
    for n in range(1, N_BRANCH):
        mixed = mixed + gates[n] * toks[1 + n]
    return [], [mixed]


def ple_fn(states, toks, consts):
    h = toks[0] + jax.nn.sigmoid(toks[1]) * toks[2]
    return [], [h, rmsnorm(h, consts[0])]


def ple_loss_fn(states, toks, consts):
    h = toks[0] + jax.nn.sigmoid(toks[1]) * toks[2]
    err = rmsnorm(h, consts[0]) - toks[3]
    return [], [0.5 * jnp.mean(err * err, axis=-1, keepdims=True)]


def s5_discretise_fn(states, toks, consts):
    log_step, a_re, a_im, b_re, b_im = consts
    step = jnp.exp(log_step)
    mag = jnp.exp(a_re * step)
    ab_re, ab_im = mag * jnp.cos(a_im * step), mag * jnp.sin(a_im * step)
    den = a_re * a_re + a_im * a_im
    num_re = ab_re - 1.0
    f_re = (num_re * a_re + ab_im * a_im) / den
    f_im = (ab_im * a_re - num_re * a_im) / den
    return [], [ab_re, ab_im, f_re * b_re - f_im * b_im, f_re * b_im + f_im * b_re]


S5_BLOCKS = 4
S5_SLAB = S5_W // S5_BLOCKS


def s5_split(x):
    parts = split_cols(x, (S5_SLAB,) * (2 * S5_BLOCKS))
    return join_cols(list(parts[0::2])), join_cols(list(parts[1::2]))


def s5_join(re, im):
    r, i = split_cols(re, (S5_SLAB,) * S5_BLOCKS), split_cols(im, (S5_SLAB,) * S5_BLOCKS)
    return join_cols([part for pair in zip(r, i) for part in pair])


def s5_scan_fn(states, toks, consts):
    b_re, b_im = s5_split(toks[0])
    h_re, h_im = complex_scan(b_re, b_im, consts[0], consts[1], states[0], states[1])
    return [last_row(h_re), last_row(h_im)], [s5_join(h_re, h_im)]


def s5_scan_bwd_fn(states, toks, consts):
    g_re, g_im, acc_re, acc_im = states
    h, ct, h0_re, h0_im = toks
    a_re, a_im = consts
    h_re, h_im = s5_split(h)
    c_re, c_im = s5_split(ct)
    last = _row_ids(c_re.shape) == c_re.shape[0] - 1
    cts = (c_re + jnp.where(last, g_re, 0.0), c_im + jnp.where(last, g_im, 0.0))
    d_re, d_im, da_re, da_im, d0_re, d0_im = _complex_scan_bwd((a_re, a_im, h_re, h_im, h0_re, h0_im), cts)
    acc_re, acc_im = acc_re + da_re, acc_im + da_im
    return [d0_re, d0_im, acc_re, acc_im], [s5_join(d_re, d_im), acc_re, acc_im]


def s5_scan(name, t, bu, a_re, a_im):
    nblk = min(SCAN_ROWS, t)
    nb = t // nblk
    wide, row = rows(2 * S5_W, nblk, dtype=ACT), whole((1, S5_W))
    entry = Arg((None, None, 1, S5_W), lambda g, c: (0, c, 0, 0))
    state = [(1, S5_W)] * 2

    def run(bu, a_re, a_im, save):
        return blocked_forward(name, s5_scan_fn, 1, nb, [wide], [bu], [row, row], [a_re, a_im], [wide], [(t, 2 * S5_W)], state, save)

    @jax.custom_vjp
    def op(bu, a_re, a_im):
        return run(bu, a_re, a_im, False)[0][0]

    def fwd(bu, a_re, a_im):
        outs, saved = run(bu, a_re, a_im, True)
        return outs[0], (outs[0], saved, a_re, a_im)

    def bwd(res, ct):
        h, saved, a_re, a_im = res
        outs, _ = blocked_forward(name + "_bwd", s5_scan_bwd_fn, 1, nb, [wide, wide, entry, entry], [h, ct] + saved, [row, row],
                                  [a_re, a_im], [wide, row, row], [(t, 2 * S5_W), (1, S5_W), (1, S5_W)], state * 2, False, reverse=True)
        return tuple(outs)

    op.defvjp(fwd, bwd)
    return op(bu, a_re, a_im)


def s5_glu_fn(states, toks, consts):
    d_skip, w_glu, b_glu = consts
    z = jax.nn.gelu(toks[0] + d_skip * toks[1])
    return [], [z * jax.nn.sigmoid(mm(z, w_glu) + b_glu)]


def lru_fn(states, toks, consts):
    h0, tail = states
    x, gate = toks
    conv_w, conv_b, w_r, b_r, w_i, b_i, lam = consts
    xc = causal_conv4(x, tail, conv_w) + conv_b
    r = jax.nn.sigmoid(mm(xc, w_r) + b_r)
    i_g = jax.nn.sigmoid(mm(xc, w_i) + b_i)
    log_a = -LRU_C * r * jax.nn.softplus(-lam)
    inp = jnp.sqrt(1.0 - jnp.exp(2.0 * log_a)) * (i_g * xc)
    h = lin_scan(jnp.exp(log_a), inp, h0)
    return [last_row(h), tail_rows(x)], [h * jax.nn.gelu(gate)]


def m2_conv_fn(states, toks, consts):
    y = jax.nn.silu(causal_conv4(toks[0], states[0], consts[0]) + consts[1])
    return [tail_rows(toks[0])], list(split_cols(y, (BRANCH_WIDTH, M2_GROUPS * M2_STATE, M2_GROUPS * M2_STATE)))


def gdn_conv_fn(states, toks, consts):
    y = jax.nn.silu(causal_conv4(toks[0], states[0], consts[0]))
    return [tail_rows(toks[0])], list(split_cols(y, (BRANCH_WIDTH,) * 3))


def ssd_fn(states, toks, consts):
    xs, bm, cm, small = toks
    dt_bias, a_log, d_skip = consts
    x_pairs = split_cols(xs, (LANE,) * 4)
    b_g = split_cols(bm, (M2_STATE,) * M2_GROUPS)
    c_g = split_cols(cm, (M2_STATE,) * M2_GROUPS)
    lo = lax.broadcasted_iota(jnp.int32, (1, LANE), 1) < M2_HEAD_DIM
    new_states, y_pairs = [], []
    for g in range(M2_GROUPS):
        scores = mm_nt(c_g[g], b_g[g])
        y_off = split_cols(mm_nt(c_g[g], states[g]), (LANE, LANE))
        to_end, ends = [], []
        for j in range(2):
            pair = 2 * g + j
            x2 = x_pairs[pair]
            dts, css, decays, end = [], [], [], []
            for h in (2 * pair, 2 * pair + 1):
                dt = jax.nn.softplus(lane_col(small, h) + lane_scalar(dt_bias, h))
                a = dt * (-jnp.exp(lane_scalar(a_log, h)))
                cs, decay = causal_decay(a)
                dts.append(dt)
                css.append(cs)
                decays.append(decay)
                end.append(jnp.sum(a, axis=0, keepdims=True))
            xdt = x2 * jnp.where(lo, dts[0], dts[1])
            y = mm(scores * decays[0], jnp.where(lo, xdt, 0.0)) + mm(scores * decays[1], jnp.where(lo, 0.0, xdt))
            cs2 = jnp.where(lo, css[0], css[1])
            end2 = jnp.where(lo, end[0], end[1])
            y = y + y_off[j] * jnp.exp(cs2)
            y = y + jnp.where(lo, lane_scalar(d_skip, 2 * pair), lane_scalar(d_skip, 2 * pair + 1)) * x2
            y_pairs.append(y)
            to_end.append(xdt * jnp.exp(end2 - cs2))
            ends.append(end2)
        chunk_decay = jnp.exp(to_col(join_cols(ends)))
        new_states.append(states[g] * chunk_decay + mm_tn(join_cols(to_end), b_g[g]))
    return new_states, [join_cols(y_pairs)]


def m2_post_fn(states, toks, consts):
    return [], [rmsnorm(toks[0] * jax.nn.silu(toks[1]), consts[0])]


@jax.custom_vjp
def nilpotent_inverses(mats):
    size = mats[0].shape[0]
    eye = lax.broadcasted_iota(jnp.int32, mats[0].shape, 0) == lax.broadcasted_iota(jnp.int32, mats[0].shape, 1)
    invs = [jnp.where(eye, 1.0, 0.0) + m for m in mats]
    powers = list(mats)
    d = 2
    while d < size:
        powers = [_dot_bf16x3(p, p, ((1,), (0,))) for p in powers]
        invs = [i + _dot_bf16x3(i, p, ((1,), (0,))) for i, p in zip(invs, powers)]
        d *= 2
    return tuple(invs)


def _nilpotent_inverses_fwd(mats):
    invs = nilpotent_inverses(mats)
    return invs, invs


def _nilpotent_inverses_bwd(invs, gs):
    right = [_dot_bf16x3(g, i, ((1,), (1,))) for g, i in zip(gs, invs)]
    return (tuple(_dot_bf16x3(i, r, ((0,), (0,))) for i, r in zip(invs, right)),)


nilpotent_inverses.defvjp(_nilpotent_inverses_fwd, _nilpotent_inverses_bwd)


def gdn_fn(states, toks, consts):
    q, k, v, gate, small = toks
    dt_bias, a_log, norm_g = consts
    hs = range(GDN_HEADS)
    heads = (GDN_HEAD_DIM,) * GDN_HEADS
    qs, ks, vs, gs = split_cols(q, heads), split_cols(k, heads), split_cols(v, heads), split_cols(gate, heads)
    n = q.shape[0]
    strict = lax.broadcasted_iota(jnp.int32, (n, n), 0) > lax.broadcasted_iota(jnp.int32, (n, n), 1)
    qn = [qs[h] * lax.rsqrt(jnp.sum(qs[h] * qs[h], axis=-1, keepdims=True) + EPS) * (GDN_HEAD_DIM ** -0.5) for h in hs]
    kn = [ks[h] * lax.rsqrt(jnp.sum(ks[h] * ks[h], axis=-1, keepdims=True) + EPS) for h in hs]
    beta = [jax.nn.sigmoid(lane_col(small, h)) for h in hs]
    g = [-jnp.exp(lane_scalar(a_log, h)) * jax.nn.softplus(lane_col(small, GDN_HEADS + h) + lane_scalar(dt_bias, h)) for h in hs]
    cs_decay = [causal_decay(g[h]) for h in hs]
    cs, decay = [c for c, _ in cs_decay], [d for _, d in cs_decay]
    kb = [kn[h] * beta[h] for h in hs]
    inv = nilpotent_inverses(tuple(-jnp.where(strict, mm_nt(kb[h], kn[h]) * decay[h], 0.0) for h in hs))
    ecs = [jnp.exp(cs[h]) for h in hs]
    u = [mmh(inv[h], vs[h] * beta[h]) for h in hs]
    w = [mmh(inv[h], kb[h] * ecs[h]) for h in hs]
    qk = [mm_nt(qn[h], kn[h]) * decay[h] for h in hs]
    cs_end = [jnp.sum(g[h], axis=0, keepdims=True) for h in hs]
    v_new = [u[h] - mm(w[h], states[h]) for h in hs]
    o = [mm(qn[h] * ecs[h], states[h]) + mm(qk[h], v_new[h]) for h in hs]
    new_states = [states[h] * jnp.exp(cs_end[h]) + mm_tn(kn[h] * jnp.exp(cs_end[h] - cs[h]), v_new[h]) for h in hs]
    return new_states, [join_cols([rmsnorm(o[h], norm_g) * jax.nn.silu(gs[h]) for h in hs])]


def adamw_fn(states, toks, consts):
    w, g, m, v = toks
    m = ADAM_B1 * m + (1.0 - ADAM_B1) * g
    v = ADAM_B2 * v + (1.0 - ADAM_B2) * (g * g)
    m_hat = m / (1.0 - ADAM_B1 ** ADAM_STEP)
    v_hat = v / (1.0 - ADAM_B2 ** ADAM_STEP)
    return [], [-ADAM_LR * (m_hat / (jnp.sqrt(v_hat) + ADAM_EPS) + ADAM_WD * w), m, v]


def tok_op(name, fn, t, nblk, tok, consts, outs, states=()):
    nblk = min(nblk, t)
    tok_args = [rows(e[0], nblk, dtype=e[1], grad=e[2] if len(e) > 2 else True) for e in tok]
    const_args = [whole(s) for s in consts]
    out_args = [rows(w, nblk, dtype=dt) for (w, dt) in outs]
    return blocked_op(name, fn, 1, t // nblk, tok_args, const_args, out_args, [(t, w) for (w, _) in outs], states)


def const_op(name, fn, in_shapes, out_shapes):
    return blocked_op(name, fn, 1, 1, [], [whole(s) for s in in_shapes], [whole(s) for s in out_shapes], list(out_shapes))


W = BRANCH_WIDTH
SCAN_ROWS = 128
ROW_BLOCK = 512


def s5_mixer(tag, t, u, p):
    col = (S5_W, 1)
    disc = const_op("s5_disc" + tag, s5_discretise_fn, [col, col, col, (S5_W, 16), (S5_W, 16)], [col, col, (S5_W, 16), (S5_W, 16)])
    ab_re, ab_im, bb_re, bb_im = disc([], [
        jnp.repeat(p["s5_log_step"], S5_STATE).reshape(col), p["s5_a_re"].reshape(col), p["s5_a_im"].reshape(col),
        p["s5_b_re"].reshape(S5_W, S5_GROUP_CH), p["s5_b_im"].reshape(S5_W, S5_GROUP_CH)])
    per = S5_GROUPS // S5_BLOCKS
    eye = jnp.eye(per, dtype=F32)

    def block_in(bb):
        blk = bb.reshape(S5_BLOCKS, per, S5_STATE, S5_GROUP_CH)
        return jnp.einsum("jgpc,gh->jgchp", blk, eye).reshape(S5_BLOCKS, LANE, per * S5_STATE)

    def block_out(c):
        blk = c.reshape(S5_BLOCKS, per, S5_GROUP_CH, S5_STATE)
        return jnp.einsum("jgcp,gh->jgphc", blk, eye).reshape(S5_BLOCKS, per * S5_STATE, LANE)

    w_b = jnp.concatenate([block_in(bb_re), block_in(bb_im)], axis=2)
    w_c = jnp.concatenate([block_out(p["s5_c_re"]), -block_out(p["s5_c_im"])], axis=1)
    bu = block_dense("s5_b" + tag, ACT)(u, w_b)
    h = s5_scan("s5_scan" + tag, t, bu, ab_re.reshape(1, S5_W), ab_im.reshape(1, S5_W))
    yc = block_dense("s5_c" + tag)(h, w_c)
    glu = tok_op("s5_glu" + tag, s5_glu_fn, t, ROW_BLOCK, [(W, F32), (W, F32)], [(1, W), (W, W), (1, W)], [(W, ACT)])
    return glu([yc, u], [p["s5_d"].reshape(1, W), p["s5_w_glu"], p["s5_b_glu"][None]])[0]


def lru_mixer(tag, t, x, gate, p):
    def block_diag(w):
        return jnp.einsum("hij,hk->hikj", w, jnp.eye(LRU_HEADS, dtype=F32)).reshape(W, W)

    op = tok_op("lru" + tag, lru_fn, t, SCAN_ROWS, [(W, F32), (W, F32)],
                [(4, W), (1, W), (W, W), (1, W), (W, W), (1, W), (1, W)], [(W, ACT)], states=[(1, W), (TAIL, W)])
    return op([x, gate], [p["lru_conv_w"], p["lru_conv_b"][None], block_diag(p["lru_w_r"]), p["lru_b_r"][None],
                          block_diag(p["lru_w_i"]), p["lru_b_i"][None], p["lru_lambda"][None]])[0]


def m2_mixer(tag, t, z, xbc, small, p):
    cw = 2 * W
    conv = tok_op("m2_conv" + tag, m2_conv_fn, t, ROW_BLOCK, [(cw, F32)], [(4, cw), (1, cw)],
                  [(W, F32), (W // 2, F32), (W // 2, F32)], states=[(TAIL, cw)])
    xs, bm, cm = conv([xbc], [p["m2_conv_w"], p["m2_conv_b"][None]])
    ssd = tok_op("ssd" + tag, ssd_fn, t, CHUNK, [(W, F32), (W // 2, F32), (W // 2, F32), (LANE, F32)],
                 [(1, M2_HEADS)] * 3, [(W, F32)], states=[(4 * M2_HEAD_DIM, M2_STATE)] * M2_GROUPS)
    y = ssd([xs, bm, cm, small], [p["m2_dt_bias"][None], p["m2_a_log"][None], p["m2_d"][None]])[0]
    post = tok_op("m2_post" + tag, m2_post_fn, t, ROW_BLOCK, [(W, F32), (W, F32)], [(1, W)], [(W, ACT)])
    return post([y, z], [p["m2_norm"][None]])[0]


def gdn_mixer(tag, t, qkv, gate, small, p):
    conv = tok_op("gdn_conv" + tag, gdn_conv_fn, t, ROW_BLOCK, [(3 * W, F32)], [(4, 3 * W)], [(W, F32)] * 3, states=[(TAIL, 3 * W)])
    q, k, v = conv([qkv], [p["gdn_conv_w"]])
    op = tok_op("gdn" + tag, gdn_fn, t, CHUNK, [(W, F32)] * 4 + [(LANE, F32)], [(1, GDN_HEADS), (1, GDN_HEADS), (1, GDN_HEAD_DIM)],
                [(W, ACT)], states=[(GDN_HEAD_DIM, GDN_HEAD_DIM)] * GDN_HEADS)
    return op([q, k, v, gate, small], [p["gdn_dt_bias"][None], p["gdn_a_log"][None], p["gdn_norm"][None]])[0]


WEIGHTS = ["ffn1_norm", "ffn1_w_in", "ffn1_w_out", "mix_norm", "w_in", "w_gate", "b_gate", "s5_log_step", "s5_a_re",
           "s5_a_im", "s5_b_re", "s5_b_im", "s5_c_re", "s5_c_im", "s5_d", "s5_w_glu", "s5_b_glu", "lru_conv_w",
           "lru_conv_b", "lru_w_r", "lru_b_r", "lru_w_i", "lru_b_i", "lru_lambda", "m2_conv_w", "m2_conv_b", "m2_dt_bias",
           "m2_a_log", "m2_d", "m2_norm", "gdn_conv_w", "gdn_dt_bias", "gdn_a_log", "gdn_norm", "w_branch", "w_out",
           "ffn2_norm", "ffn2_w_in", "ffn2_w_out", "ple_norm", "ple_w_gate", "ple_w_proj", "final_norm"]
N_CHIPS = 4
N_DEV = 8
IN_WIDTH = 5136
SHARDED = {
    "ffn1_w_in": ((D_MODEL, 2 * FFN_DIM // N_CHIPS), 1, False),
    "ffn1_w_out": ((FFN_DIM // N_CHIPS, D_MODEL), 0, False),
    "w_in": ((D_MODEL, IN_WIDTH // N_CHIPS), 1, False),
    "w_gate": ((D_MODEL, N_BRANCH * D_MODEL // N_CHIPS), 1, False),
    "s5_w_glu": ((W // N_CHIPS, W), 0, False),
    "lru_conv_w": ((4, W // N_CHIPS), 1, True),
    "m2_conv_w": ((4, 2 * W // N_CHIPS), 1, True),
    "gdn_conv_w": ((4, 3 * W // N_CHIPS), 1, True),
    "w_branch": ((N_BRANCH, W, D_MODEL // N_CHIPS), 2, False),
    "w_out": ((D_MODEL // N_CHIPS, D_MODEL), 0, False),
    "ffn2_w_in": ((D_MODEL, 2 * FFN_DIM // N_CHIPS), 1, False),
    "ffn2_w_out": ((FFN_DIM // N_CHIPS, D_MODEL), 0, False),
    "ple_w_gate": ((D_MODEL // N_CHIPS, D_MODEL), 0, False),
    "ple_w_proj": ((256, D_MODEL // N_CHIPS), 1, False),
}
SMALL = [n for n in WEIGHTS if n not in SHARDED]
ROW = 1024


def _count(shape):
    return math.prod(shape)


def _round_up(n, unit):
    return -(-n // unit) * unit


N_GATHER = sum(DEPTH * _count(s) * (2 if exact else 1) for s, _, exact in SHARDED.values())
N_GRAD = sum(DEPTH * _count(s) for s, _, _ in SHARDED.values())
GATHER_ROWS = _round_up(-(-N_GATHER // ROW), 32)
GRAD_ROWS = _round_up(-(-N_GRAD // ROW), 32)
GRAD_HALF = GRAD_ROWS // 2
IN_PIECES = [(0, 512), (512, 512), (1024, 512), (1536, 512), (2048, 1024), (3072, 8), (3080, 1536), (4616, 512), (5128, 8)]


COL_SHARDED = ("ffn1_w_in", "ffn2_w_in", "w_gate", "ple_w_proj", "w_branch")
IN_SHARD = IN_WIDTH // N_CHIPS


def _w_in_cuts():
    cuts = {0, IN_SHARD}
    for s in range(N_CHIPS):
        for start, _ in IN_PIECES:
            if s * IN_SHARD < start < (s + 1) * IN_SHARD:
                cuts.add(start - s * IN_SHARD)
    return sorted(cuts)


IN_CUTS = _w_in_cuts()
IN_BLOCKS = list(zip(IN_CUTS[:-1], IN_CUTS[1:]))


def _piece_of(col):
    for k, (start, n) in enumerate(IN_PIECES):
        if start <= col < start + n:
            return k, col - start
    raise ValueError(col)


def cut_w_in(w):
    return jnp.concatenate([w[:, :, lo:hi].reshape(-1) for lo, hi in IN_BLOCKS])


def uncut_w_in(flat):
    blocks, off = [], 0
    for lo, hi in IN_BLOCKS:
        cnt = DEPTH * D_MODEL * (hi - lo)
        blocks.append(flat[off:off + cnt].reshape(DEPTH, D_MODEL, hi - lo))
        off += cnt
    return jnp.concatenate(blocks, axis=2)


def pack_for_gather(a):
    parts = []
    for n, (_, _, exact) in SHARDED.items():
        w = cut_w_in(a[n]) if n == "w_in" else a[n]
        parts.append((lax.bitcast_convert_type(w, BF16) if exact else w.astype(BF16)).reshape(-1))
    flat = jnp.concatenate(parts)
    return jnp.pad(flat, (0, GATHER_ROWS * ROW - flat.shape[0])).reshape(GATHER_ROWS, ROW)


def unpack_gathered(buf):
    flat16 = buf.reshape(N_CHIPS, -1)
    flat = buf.astype(ACT).reshape(N_CHIPS, -1)
    out, off = {}, 0
    for n, (shape, ax, exact) in SHARDED.items():
        cnt = DEPTH * _count(shape) * (2 if exact else 1)
        piece = (flat16 if exact else flat)[:, off:off + cnt]
        off += cnt
        if n == "w_in":
            cols = [[[] for _ in IN_PIECES] for _ in range(DEPTH)]
            for s in range(N_CHIPS):
                o = 0
                for lo, hi in IN_BLOCKS:
                    c = DEPTH * D_MODEL * (hi - lo)
                    blk = piece[s, o:o + c].reshape(DEPTH, D_MODEL, hi - lo)
                    o += c
                    k, _ = _piece_of(s * IN_SHARD + lo)
                    for layer in range(DEPTH):
                        cols[layer][k].append(blk[layer])
            out[n] = [[_pad_lanes(jnp.concatenate(c, axis=1)) for c in cols[layer]] for layer in range(DEPTH)]
        elif n in COL_SHARDED:
            out[n] = piece.reshape(N_CHIPS, DEPTH, *shape)
        else:
            if exact:
                w = lax.bitcast_convert_type(piece.reshape(N_CHIPS, DEPTH, *shape, 2), F32)
            else:
                w = piece.reshape(N_CHIPS, DEPTH, *shape)
            full = list(shape)
            full[ax] *= N_CHIPS
            out[n] = jnp.moveaxis(w, 0, ax + 1).reshape(DEPTH, *full)
    return out


def _pad_lanes(w):
    n = w.shape[1]
    return w if n % LANE == 0 else jnp.pad(w, ((0, 0), (0, LANE - n % LANE)))


def shard_w_in_grads(pieces):
    shards = []
    for s in range(N_CHIPS):
        parts = []
        for lo, hi in IN_BLOCKS:
            k, dst = _piece_of(s * IN_SHARD + lo)
            parts.append(jnp.stack([pieces[layer][k][:, dst:dst + hi - lo] for layer in range(DEPTH)]).reshape(-1))
        shards.append(jnp.concatenate(parts))
    return jnp.stack(shards)


def pack_grads(g):
    tail = jnp.zeros((N_CHIPS, GRAD_ROWS * ROW - N_GRAD), F32)
    return jnp.concatenate([g[n] for n in SHARDED] + [tail], axis=1).reshape(N_CHIPS, GRAD_ROWS, ROW)


def shard_full_grads(n, per_layer):
    shape, ax, _ = SHARDED[n]
    parts = []
    for full in per_layer:
        w = full.reshape(*full.shape[:ax], N_CHIPS, shape[ax], *full.shape[ax + 1:])
        parts.append(jnp.moveaxis(w, ax, 0).reshape(N_CHIPS, -1))
    return jnp.concatenate(parts, axis=1)


def unpack_shard(buf):
    flat = buf.reshape(-1)
    out, off = {}, 0
    for n, (shape, _, _) in SHARDED.items():
        cnt = DEPTH * _count(shape)
        piece = flat[off:off + cnt]
        out[n] = uncut_w_in(piece) if n == "w_in" else piece.reshape(DEPTH, *shape)
        off += cnt
    return out


def _rows_of(shape):
    return -(-_count(shape) // ROW)


def pack_small(vals, shapes):
    used = sum(_rows_of(shapes[n]) for n in SMALL)
    parts = []
    for n in SMALL:
        flat = vals[n].reshape(-1)
        tail = (_round_up(used, 8) - used) * ROW if n == SMALL[-1] else 0
        parts.append(jnp.pad(flat, (0, _rows_of(shapes[n]) * ROW - flat.shape[0] + tail)))
    return jnp.concatenate(parts).reshape(-1, ROW)


def unpack_small(buf, shapes):
    out, row = {}, 0
    for n in SMALL:
        r = _rows_of(shapes[n])
        out[n] = buf[row:row + r].reshape(-1)[:_count(shapes[n])].reshape(shapes[n])
        row += r
    return out


ANY = pl.BlockSpec(memory_space=pl.ANY)


def _position():
    return lax.axis_index("x"), lax.axis_index("y"), lax.axis_index("c")


def _other_chips(x, y):
    return [(1 - x, y), (x, 1 - y), (1 - x, 1 - y)]


PLACE_ROWS = 592


def gather_weights(packed, slot):
    r = packed.shape[0]
    half = r // 2
    nblk = r // PLACE_ROWS

    def place(s_ref, in_ref, o_ref):
        o_ref[...] = in_ref[...]

    slots = pl.pallas_call(
        place, name="place_shard",
        grid_spec=pltpu.PrefetchScalarGridSpec(
            num_scalar_prefetch=1, grid=(nblk,), in_specs=[pl.BlockSpec((PLACE_ROWS, ROW), lambda i, s: (i, 0))],
            out_specs=pl.BlockSpec((None, PLACE_ROWS, ROW), lambda i, s: (s[0], i, 0))),
        out_shape=jax.ShapeDtypeStruct((N_CHIPS, r, ROW), packed.dtype), compiler_params=_params(("arbitrary",)),
    )(slot, packed)

    def body(in_ref, out_ref, send_sems, recv_sems):
        x, y, c = _position()
        sibling = (x, y, 1 - c)
        chips = _other_chips(x, y)

        def half_rows(px, py, pc):
            return out_ref.at[2 * px + py, pl.ds(pl.multiple_of(pc * half, 16), half), :]

        def copy(k, block, to):
            return pltpu.make_async_remote_copy(
                src_ref=half_rows(*block), dst_ref=half_rows(*block),
                send_sem=send_sems.at[k], recv_sem=recv_sems.at[k], device_id=to, device_id_type=MESH)

        first = [copy(j, (x, y, c), (*chip, c)) for j, chip in enumerate(chips)]
        for cp in first:
            cp.start()
        passed = [copy(3 + j, (*chip, c), sibling) for j, chip in enumerate(chips)]
        for j, chip in enumerate(chips):
            copy(j, (*chip, c), (x, y, c)).wait_recv()
            passed[j].start()
        for j, chip in enumerate(chips):
            copy(3 + j, (*chip, 1 - c), (x, y, c)).wait_recv()
        for cp in first + passed:
            cp.wait_send()

    return pl.pallas_call(
        body, name="gather_weights", in_specs=[ANY], out_specs=ANY, input_output_aliases={0: 0},
        out_shape=jax.ShapeDtypeStruct((N_CHIPS, r, ROW), packed.dtype),
        scratch_shapes=[pltpu.SemaphoreType.DMA((6,)), pltpu.SemaphoreType.DMA((6,))],
    )(slots)


def swap_pair_halves(g):
    half = g.shape[1] // 2

    def body(g_ref, land_ref, send_sem, recv_sem):
        x, y, c = _position()
        src = g_ref.at[:, pl.ds(pl.multiple_of((1 - c) * half, 8), half), :]
        cp = pltpu.make_async_remote_copy(src_ref=src, dst_ref=land_ref, send_sem=send_sem, recv_sem=recv_sem,
                                          device_id=(x, y, 1 - c), device_id_type=MESH)
        cp.start()
        cp.wait()

    return pl.pallas_call(
        body, name="swap_pair_halves", in_specs=[ANY], out_specs=ANY,
        out_shape=jax.ShapeDtypeStruct((N_CHIPS, half, ROW), g.dtype),
        scratch_shapes=[pltpu.SemaphoreType.DMA, pltpu.SemaphoreType.DMA],
    )(g)


def exchange_chip_partials(part):
    half = part.shape[1]

    def body(p_ref, land_ref, send_sems, recv_sems):
        x, y, c = _position()
        cps = [pltpu.make_async_remote_copy(src_ref=p_ref.at[2 * px + py], dst_ref=land_ref.at[j], send_sem=send_sems.at[j],
                                            recv_sem=recv_sems.at[j], device_id=(px, py, c), device_id_type=MESH)
               for j, (px, py) in enumerate(_other_chips(x, y))]
        for cp in cps:
            cp.start()
        for cp in cps:
            cp.wait()

    return pl.pallas_call(
        body, name="exchange_chip_partials", in_specs=[ANY], out_specs=ANY,
        out_shape=jax.ShapeDtypeStruct((3, half, ROW), part.dtype),
        scratch_shapes=[pltpu.SemaphoreType.DMA((3,)), pltpu.SemaphoreType.DMA((3,))],
    )(part)


def share_halves(both):
    half = both.shape[0] // 2

    def body(in_ref, out_ref, send_sem, recv_sem):
        x, y, c = _position()
        my_rows = out_ref.at[pl.ds(pl.multiple_of(c * half, 8), half), :]
        cp = pltpu.make_async_remote_copy(src_ref=my_rows, dst_ref=my_rows, send_sem=send_sem, recv_sem=recv_sem,
                                          device_id=(x, y, 1 - c), device_id_type=MESH)
        cp.start()
        cp.wait()

    return pl.pallas_call(
        body, name="share_halves", in_specs=[ANY], out_specs=ANY, input_output_aliases={0: 0},
        out_shape=jax.ShapeDtypeStruct(both.shape, both.dtype),
        scratch_shapes=[pltpu.SemaphoreType.DMA, pltpu.SemaphoreType.DMA],
    )(both)


def gather_all(block):
    m_per = block.shape[0]

    def body(x_ref, out_ref, send_sems, recv_sems, local_sem):
        x, y, c = _position()
        me, sibling = (x, y, c), (x, y, 1 - c)
        chips = _other_chips(x, y)

        def rows_of(px, py, pc):
            return out_ref.at[pl.ds(pl.multiple_of((4 * px + 2 * py + pc) * m_per, 8), m_per), :]

        def copy(k, blk, to, src=None):
            return pltpu.make_async_remote_copy(
                src_ref=rows_of(*blk) if src is None else src, dst_ref=rows_of(*blk),
                send_sem=send_sems.at[k], recv_sem=recv_sems.at[k], device_id=to, device_id_type=MESH)

        mine = pltpu.make_async_copy(x_ref, rows_of(*me), local_sem)
        mine.start()
        first = [copy(0, me, sibling, src=x_ref)]
        first += [copy(1 + j, me, (*chip, c), src=x_ref) for j, chip in enumerate(chips)]
        for cp in first:
            cp.start()
        passed = [copy(4 + j, (*chip, c), sibling) for j, chip in enumerate(chips)]
        for j, chip in enumerate(chips):
            copy(1 + j, (*chip, c), me).wait_recv()
            passed[j].start()
        copy(0, sibling, me).wait_recv()
        for j, chip in enumerate(chips):
            copy(4 + j, (*chip, 1 - c), me).wait_recv()
        for cp in first + passed:
            cp.wait_send()
        mine.wait()

    return pl.pallas_call(
        body, name="gather_all", out_shape=jax.ShapeDtypeStruct((N_DEV * m_per, ROW), block.dtype),
        in_specs=[pl.BlockSpec(memory_space=pltpu.VMEM)], out_specs=pl.BlockSpec(memory_space=pltpu.VMEM),
        scratch_shapes=[pltpu.SemaphoreType.DMA((7,)), pltpu.SemaphoreType.DMA((7,)), pltpu.SemaphoreType.DMA],
        compiler_params=_params(),
    )(block)


SUM_ROWS = 592


def add_pair_halves(grads, landed, core):
    half = landed.shape[1]
    nblk = half // SUM_ROWS

    def body(c_ref, g_ref, l_ref, o_ref, o16_ref):
        acc = g_ref[...] + l_ref[...]
        o_ref[...] = acc
        o16_ref[...] = acc.astype(BF16)

    blk = (None, SUM_ROWS, ROW)
    o_spec = pl.BlockSpec(blk, lambda s, i, c: (s, i, 0))
    return pl.pallas_call(
        body, name="add_pair_halves",
        grid_spec=pltpu.PrefetchScalarGridSpec(
            num_scalar_prefetch=1, grid=(N_CHIPS, nblk),
            in_specs=[pl.BlockSpec(blk, lambda s, i, c: (s, c[0] * nblk + i, 0)), o_spec], out_specs=[o_spec, o_spec]),
        out_shape=[jax.ShapeDtypeStruct(landed.shape, F32), jax.ShapeDtypeStruct(landed.shape, BF16)],
        compiler_params=_params(("arbitrary", "arbitrary")),
    )(core, grads, landed)


def add_chip_partials(part, landed, slot, core):
    half = part.shape[1]
    nblk = half // SUM_ROWS

    def body(s_ref, c_ref, p_ref, l_ref, o_ref):
        o_ref[...] = ((p_ref[...] + l_ref[0].astype(F32)) + l_ref[1].astype(F32)) + l_ref[2].astype(F32)

    return pl.pallas_call(
        body, name="add_chip_partials",
        grid_spec=pltpu.PrefetchScalarGridSpec(
            num_scalar_prefetch=2, grid=(nblk,),
            in_specs=[pl.BlockSpec((None, SUM_ROWS, ROW), lambda i, s, c: (s[0], i, 0)), pl.BlockSpec((3, SUM_ROWS, ROW), lambda i, s, c: (0, i, 0))],
            out_specs=pl.BlockSpec((SUM_ROWS, ROW), lambda i, s, c: (c[0] * nblk + i, 0))),
        out_shape=jax.ShapeDtypeStruct((2 * half, ROW), F32), compiler_params=_params(("arbitrary",)),
    )(slot, core, part, landed)


def sum_devices(stacked):
    m = stacked.shape[1]

    def body(s_ref, o_ref):
        acc = s_ref[0]
        for d in range(1, N_DEV):
            acc = acc + s_ref[d]
        o_ref[...] = acc

    return pl.pallas_call(
        body, name="sum_devices", grid=(m // 8,), in_specs=[pl.BlockSpec((N_DEV, 8, ROW), lambda i: (0, i, 0))],
        out_specs=pl.BlockSpec((8, ROW), lambda i: (i, 0)), out_shape=jax.ShapeDtypeStruct((m, ROW), F32),
        compiler_params=_params(("arbitrary",)),
    )(stacked)


def adamw(name, w, g, m, v):
    width = w.shape[-1]
    n_rows = w.size // width
    nblk = _tile(n_rows, 256, 8)
    arg = rows(width, nblk)
    outs, _ = blocked_forward("adamw_" + name, adamw_fn, 1, n_rows // nblk, [arg] * 4, [t.reshape(n_rows, width) for t in (w, g, m, v)],
                              [], [], [arg] * 3, [(n_rows, width)] * 3, (), False)
    return [o.reshape(w.shape) for o in outs]


def trunk_loss(diff, p_emb, target, wts):
    x, small, gw, hd = diff["x"], diff["small"], diff["gw"], diff["hd"]
    t = x.shape[0]
    d = D_MODEL

    def norm_pair(name, fn, h, o, gain):
        op = tok_op(name, fn, t, 512, [(d, F32), (d, F32)], [(1, d)], [(d, F32), (d, ACT)])
        return op([h, o], [gain[None]])

    def ffn(tag, n, which, i):
        act = linear_swiglu(f"{which}_in{tag}", (i,), N_CHIPS, 128)(n, wts[which + "_w_in"], hd[which + "_w_in"][i])
        return linear(f"{which}_out{tag}", F32, (i,))(act, wts[which + "_w_out"], hd[which + "_w_out"][i])

    h = x
    n = tok_op("norm_in", norm_fn, t, 512, [(d, F32)], [(1, d)], [(d, ACT)])([x], [small["ffn1_norm"][0][None]])[0]
    loss_rows = None
    for i in range(DEPTH):
        tag = str(i)
        p = {k: v[i] for k, v in small.items() if k != "final_norm"}
        p.update({k: v[i] for k, v in gw.items()})
        o = ffn(tag, n, "ffn1", i)
        h, u = norm_pair("mix_norm" + tag, make_addnorm_fn(0.5), h, o, p["mix_norm"])
        n_in = len(IN_PIECES)
        in_proj = multi_linear("in_proj" + tag, [F32] * n_in + [ACT], [None] * n_in + [(i,)], [0] * n_in + [N_CHIPS])
        proj = in_proj(u, wts["w_in"][i] + [wts["w_gate"]], hd["in_proj"][i])
        s5_u, lru_x, lru_g, m2_z, m2_xbc, m2_dt, gdn_qkv, gdn_g, gdn_ba, gate_logits = proj
        ys = [s5_mixer(tag, t, s5_u, p), lru_mixer(tag, t, lru_x, lru_g, p),
              m2_mixer(tag, t, m2_z, m2_xbc, m2_dt, p), gdn_mixer(tag, t, gdn_qkv, gdn_g, gdn_ba, p)]
        yb = [linear(f"branch{b}_{tag}", ACT, (i, b), N_CHIPS)(y, wts["w_branch"], hd["w_branch"][i][b]) for b, y in enumerate(ys)]
        merge = tok_op("gate_merge" + tag, gate_merge_fn, t, 128, [(N_BRANCH * d, ACT)] + [(d, ACT)] * N_BRANCH,
                       [(1, N_BRANCH * d)], [(d, ACT)])
        mixed = merge([gate_logits] + yb, [p["b_gate"][None]])[0]
        o = linear("w_out" + tag, F32, (i,))(mixed, wts["w_out"], hd["w_out"][i])
        h, n = norm_pair("ffn2_norm" + tag, make_addnorm_fn(1.0), h, o, p["ffn2_norm"])
        o = ffn(tag, n, "ffn2", i)
        h, n = norm_pair("ple_norm" + tag, make_addnorm_fn(0.5), h, o, p["ple_norm"])
        pg = linear("ple_gate" + tag, F32, (i,))(n, wts["ple_w_gate"], hd["ple_w_gate"][i])
        pp = linear("ple_proj" + tag, F32, (i,), N_CHIPS)(p_emb[i], wts["ple_w_proj"], hd["ple_w_proj"][i])
        if i + 1 < DEPTH:
            op = tok_op("ple" + tag, ple_fn, t, 512, [(d, F32)] * 3, [(1, d)], [(d, F32), (d, ACT)])
            h, n = op([h, pg, pp], [small["ffn1_norm"][i + 1][None]])
        else:
            op = tok_op("ple_loss", ple_loss_fn, t, 512, [(d, F32)] * 3 + [(d, F32, False)], [(1, d)], [(1, F32)])
            loss_rows = op([h, pg, pp, target], [small["final_norm"][None]])[0]
    return jnp.sum(loss_rows)


def kernel(x, p, ffn1_norm, ffn1_w_in, ffn1_w_out, mix_norm, w_in, w_gate, b_gate, s5_log_step, s5_a_re, s5_a_im, s5_b_re, s5_b_im, s5_c_re, s5_c_im, s5_d, s5_w_glu, s5_b_glu, lru_conv_w, lru_conv_b, lru_w_r, lru_b_r, lru_w_i, lru_b_i, lru_lambda, m2_conv_w, m2_conv_b, m2_dt_bias, m2_a_log, m2_d, m2_norm, gdn_conv_w, gdn_dt_bias, gdn_a_log, gdn_norm, w_branch, w_out, ffn2_norm, ffn2_w_in, ffn2_w_out, ple_norm, ple_w_gate, ple_w_proj, final_norm, loss_target, m_ffn1_norm, m_ffn1_w_in, m_ffn1_w_out, m_mix_norm, m_w_in, m_w_gate, m_b_gate, m_s5_log_step, m_s5_a_re, m_s5_a_im, m_s5_b_re, m_s5_b_im, m_s5_c_re, m_s5_c_im, m_s5_d, m_s5_w_glu, m_s5_b_glu, m_lru_conv_w, m_lru_conv_b, m_lru_w_r, m_lru_b_r, m_lru_w_i, m_lru_b_i, m_lru_lambda, m_m2_conv_w, m_m2_conv_b, m_m2_dt_bias, m_m2_a_log, m_m2_d, m_m2_norm, m_gdn_conv_w, m_gdn_dt_bias, m_gdn_a_log, m_gdn_norm, m_w_branch, m_w_out, m_ffn2_norm, m_ffn2_w_in, m_ffn2_w_out, m_ple_norm, m_ple_w_gate, m_ple_w_proj, m_final_norm, v_ffn1_norm, v_ffn1_w_in, v_ffn1_w_out, v_mix_norm, v_w_in, v_w_gate, v_b_gate, v_s5_log_step, v_s5_a_re, v_s5_a_im, v_s5_b_re, v_s5_b_im, v_s5_c_re, v_s5_c_im, v_s5_d, v_s5_w_glu, v_s5_b_glu, v_lru_conv_w, v_lru_conv_b, v_lru_w_r, v_lru_b_r, v_lru_w_i, v_lru_b_i, v_lru_lambda, v_m2_conv_w, v_m2_conv_b, v_m2_dt_bias, v_m2_a_log, v_m2_d, v_m2_norm, v_gdn_conv_w, v_gdn_dt_bias, v_gdn_a_log, v_gdn_norm, v_w_branch, v_w_out, v_ffn2_norm, v_ffn2_w_in, v_ffn2_w_out, v_ple_norm, v_ple_w_gate, v_ple_w_proj, v_final_norm):
    a = dict(locals())
    t = x.shape[1]
    core = lax.axis_index("c").astype(jnp.int32).reshape(1)
    slot = (2 * lax.axis_index("x") + lax.axis_index("y")).astype(jnp.int32).reshape(1)

    full = unpack_gathered(gather_weights(pack_for_gather(a), slot))
    exact = [n for n, spec in SHARDED.items() if spec[2]] + ["s5_w_glu"]
    gw = {n: full[n].astype(F32) for n in exact}
    wts = {n: full[n] for n in SHARDED if n not in exact}

    def handle(n):
        shape = SHARDED[n][0]
        return jnp.zeros((N_CHIPS, *shape) if n in COL_SHARDED else (N_CHIPS * shape[0], *shape[1:]), F32)

    hd = {n: [handle(n) for _ in range(DEPTH)] for n in wts if n not in ("w_in", "w_gate", "w_branch")}
    hd["w_branch"] = [[jnp.zeros((N_CHIPS, W, D_MODEL // N_CHIPS), F32) for _ in range(N_BRANCH)] for _ in range(DEPTH)]
    hd["in_proj"] = [[jnp.zeros((D_MODEL, _round_up(n, LANE)), F32) for _, n in IN_PIECES] + [handle("w_gate")] for _ in range(DEPTH)]
    per_layer = {n: a[n] if n == "final_norm" else [a[n][layer] for layer in range(DEPTH)] for n in SMALL}
    diff = {"x": x.reshape(t, D_MODEL), "small": per_layer, "gw": gw, "hd": hd}
    loss_local, vjp = jax.vjp(lambda dd: trunk_loss(dd, p.reshape(DEPTH, t, -1), loss_target.reshape(t, D_MODEL), wts), diff)
    (grads,) = vjp(jnp.ones((), F32))
    loss = lax.psum(loss_local, ("x", "y", "c"))
    grad_x = grads["x"].reshape(x.shape)

    gh = grads["hd"]
    big = {n: jnp.concatenate([g.reshape(N_CHIPS, -1) for g in gh[n]], axis=1) for n in gh if n not in ("in_proj", "w_branch")}
    big["w_in"] = shard_w_in_grads([gh["in_proj"][i][:-1] for i in range(DEPTH)])
    big["w_gate"] = jnp.concatenate([gh["in_proj"][i][-1].reshape(N_CHIPS, -1) for i in range(DEPTH)], axis=1)
    big["w_branch"] = jnp.concatenate([g.reshape(N_CHIPS, -1) for i in range(DEPTH) for g in gh["w_branch"][i]], axis=1)
    for n in exact:
        big[n] = shard_full_grads(n, [grads["gw"][n][i] for i in range(DEPTH)])

    packed = pack_grads(big)
    pair, pair16 = add_pair_halves(packed, swap_pair_halves(packed), core)
    mine = add_chip_partials(pair, exchange_chip_partials(pair16), slot, core)
    g_shard = unpack_shard(share_halves(mine))

    shapes = {n: a[n].shape for n in SMALL}
    gs_local = pack_small({n: g if n == "final_norm" else jnp.stack(g) for n, g in grads["small"].items()}, shapes)
    gs = gather_all(gs_local)
    g_small = sum_devices(gs.reshape(N_DEV, gs_local.shape[0], ROW))

    g_all = {**unpack_small(g_small, shapes), **g_shard}
    res = {n: [g_all[n]] + adamw(n, a[n], g_all[n], a["m_" + n], a["v_" + n]) for n in WEIGHTS}
    return (loss, grad_x, *[res[n][0] for n in WEIGHTS], *[res[n][1] for n in WEIGHTS],
            *[res[n][2] for n in WEIGHTS], *[res[n][3] for n in WEIGHTS])
```
